```python
import math
import jax, jax.numpy as jnp
from jax import lax
import numpy as np

D_MODEL = 1024
BATCH = 4
SEQ = 4096
DEPTH = 2

NORM_EPS = 1e-6
D_FF = 4 * D_MODEL
QBLK = 128
NEG_INF = -1e30
FORCE_SCORE = 1e4

NUM_BUCKETS = 32
MAX_DISTANCE = 1024

LRU_WIDTH = D_MODEL // 2
LRU_BLOCKS = 8
LRU_BLOCK_DIM = LRU_WIDTH // LRU_BLOCKS
CONV_WIDTH = 4
LRU_C = 8.0

MLA_HEADS = 8
MLA_NOPE = 64
MLA_ROPE = 32
MLA_QK = MLA_NOPE + MLA_ROPE
MLA_V = 64
MLA_Q_RANK = 384
MLA_KV_RANK = 256
ROPE_THETA = 10000.0

HY_SPLITS = (LRU_WIDTH, LRU_WIDTH, MLA_Q_RANK, MLA_KV_RANK, MLA_ROPE)
HY_IN = sum(HY_SPLITS)
HY_OUT = LRU_WIDTH + MLA_HEADS * MLA_V

NSA_HEADS = 16
NSA_GROUPS = 2
NSA_HPG = NSA_HEADS // NSA_GROUPS
NSA_DK = 64
NSA_DV = 64
NSA_KV = NSA_GROUPS * NSA_DK
CMP_BLOCK = 32
CMP_STRIDE = 16
CMP_HIDDEN = 256
SLC_BLOCK = 64
SLC_TOPN = 8
WINDOW = 512
NSA_SPLITS = (NSA_HEADS * NSA_DK,) + (NSA_KV,) * 6 + (3 * NSA_HEADS,)
NSA_IN = sum(NSA_SPLITS)
NSA_OUT = NSA_HEADS * NSA_DV

N_EVEN = (DEPTH + 1) // 2
N_ODD = DEPTH // 2

kernel_name = 'hybrid_rglru_mla_nsa_trunk'


def rms_norm(x, g):
    xf = x.astype(jnp.float32)
    y = xf * lax.rsqrt(jnp.mean(xf * xf, axis=-1, keepdims=True) + NORM_EPS)
    return (y * g.astype(jnp.float32)).astype(x.dtype)


def split_cols(z, sizes):
    out, off = [], 0
    for s in sizes:
        out.append(z[..., off:off + s])
        off += s
    return out


def t5_bucket(dist):
    n = jnp.maximum(dist, 0)
    max_exact = NUM_BUCKETS // 2
    nf = jnp.maximum(n, 1).astype(jnp.float32)
    large = max_exact + (jnp.log(nf / max_exact) / math.log(MAX_DISTANCE / max_exact)
                         * (NUM_BUCKETS - max_exact)).astype(jnp.int32)
    large = jnp.minimum(large, NUM_BUCKETS - 1)
    return jnp.where(n < max_exact, n, large)


def apply_rope(x, pos):
    half = x.shape[-1] // 2
    freqs = ROPE_THETA ** (-jnp.arange(half, dtype=jnp.float32) / half)
    ang = pos.astype(jnp.float32)[:, None] * freqs[None, :]
    cos = jnp.cos(ang)[None, :, None, :]
    sin = jnp.sin(ang)[None, :, None, :]
    xf = x.astype(jnp.float32)
    x1, x2 = xf[..., :half], xf[..., half:]
    return jnp.concatenate([x1 * cos - x2 * sin, x2 * cos + x1 * sin], axis=-1).astype(x.dtype)


def causal_depthwise_conv(x, w, b):
    S = x.shape[1]
    xp = jnp.pad(x, ((0, 0), (CONV_WIDTH - 1, 0), (0, 0)))
    y = xp[:, 0:S] * w[0]
    for k in range(1, CONV_WIDTH):
        y = y + xp[:, k:k + S] * w[k]
    return y + b


def rg_lru(x, wa, ba, wx, bx, lam):
    B, S, C = x.shape
    xb = x.reshape(B, S, LRU_BLOCKS, LRU_BLOCK_DIM)
    r = jax.nn.sigmoid(jnp.einsum('bsnd,nde->bsne', xb, wa) + ba).reshape(B, S, C)
    i = jax.nn.sigmoid(jnp.einsum('bsnd,nde->bsne', xb, wx) + bx).reshape(B, S, C)
    log_a = -LRU_C * r.astype(jnp.float32) * jax.nn.softplus(-lam.astype(jnp.float32))
    a = jnp.exp(log_a)
    mult = jnp.sqrt(-jnp.expm1(2.0 * log_a))
    mult = mult.at[:, 0].set(1.0)
    b_in = mult * (i * x).astype(jnp.float32)

    def combine(lhs, rhs):
        a1, b1 = lhs
        a2, b2 = rhs
        return a1 * a2, a2 * b1 + b2

    _, h = lax.associative_scan(combine, (a, b_in), axis=1)
    return h.astype(x.dtype)


def causal_block_attention(q, k, v):
    B, H, S, dk = q.shape
    dv = v.shape[-1]
    nb = S // QBLK
    scale = dk ** -0.5
    qb = q.reshape(B, H, nb, QBLK, dk).transpose(2, 0, 1, 3, 4)
    kpos = jnp.arange(S)

    def body(args):
        qi, blk = args
        qpos = blk * QBLK + jnp.arange(QBLK)
        s = jnp.einsum('bhqd,bhkd->bhqk', qi, k).astype(jnp.float32) * scale
        s = jnp.where(kpos[None, :] <= qpos[:, None], s, NEG_INF)
        p = jax.nn.softmax(s, axis=-1).astype(v.dtype)
        return jnp.einsum('bhqk,bhkd->bhqd', p, v)

    o = lax.map(body, (qb, jnp.arange(nb)))
    return o.transpose(1, 0, 3, 2, 4).reshape(B, S, H * dv)


def hybrid_mixer(u, w_in, conv_w, conv_b, wa, ba, wx, bx, lam,
                 q_norm, kv_norm, w_uq, w_ukv, qn_q, qn_k, w_out):
    B, S, _ = u.shape
    pos = jnp.arange(S)
    z = u @ w_in
    xr, xg, q_lat, kv_lat, k_rope = split_cols(z, HY_SPLITS)
    lru = rg_lru(causal_depthwise_conv(xr, conv_w, conv_b), wa, ba, wx, bx, lam)
    lru = lru * jax.nn.gelu(xg)
    q = (rms_norm(q_lat, q_norm) @ w_uq).reshape(B, S, MLA_HEADS, MLA_QK)
    kv = (rms_norm(kv_lat, kv_norm) @ w_ukv).reshape(B, S, MLA_HEADS, MLA_NOPE + MLA_V)
    k_nope, v = kv[..., :MLA_NOPE], kv[..., MLA_NOPE:]
    k_r = jnp.broadcast_to(k_rope[:, :, None, :], (B, S, MLA_HEADS, MLA_ROPE))
    k = jnp.concatenate([k_nope, k_r], axis=-1)
    q = rms_norm(q, qn_q)
    k = rms_norm(k, qn_k)
    q = jnp.concatenate([q[..., :MLA_NOPE], apply_rope(q[..., MLA_NOPE:], pos)], axis=-1)
    k = jnp.concatenate([k[..., :MLA_NOPE], apply_rope(k[..., MLA_NOPE:], pos)], axis=-1)
    mla = causal_block_attention(q.transpose(0, 2, 1, 3), k.transpose(0, 2, 1, 3),
                                 v.transpose(0, 2, 1, 3))
    return jnp.concatenate([lru, mla], axis=-1) @ w_out


def nsa_mixer(u, w_in, pos_k, w1_k, w2_k, pos_v, w1_v, w2_v, q_norm, k_norm, rel_bias, w_out):
    B, S, _ = u.shape
    G, HPG, DK, DV = NSA_GROUPS, NSA_HPG, NSA_DK, NSA_DV
    z = u @ w_in
    q, kc, vc, ks, vs, kw, vw, gate = split_cols(z, NSA_SPLITS)
    q = rms_norm(q.reshape(B, S, NSA_HEADS, DK), q_norm).reshape(B, S, G, HPG, DK)
    gate = jax.nn.sigmoid(gate).reshape(B, S, NSA_HEADS, 3)

    nc = (S - CMP_BLOCK) // CMP_STRIDE + 1
    cstart = jnp.arange(nc) * CMP_STRIDE
    cend = cstart + CMP_BLOCK - 1
    cidx = cstart[:, None] + jnp.arange(CMP_BLOCK)[None, :]

    def compress(t, pos_emb, w1, w2):
        t = t.reshape(B, S, G, DK)[:, cidx]
        t = t + pos_emb[None, None, :, None, :]
        t = t.transpose(0, 1, 3, 2, 4).reshape(B, nc, G, CMP_BLOCK * DK)
        return jax.nn.gelu(t @ w1) @ w2

    ck = rms_norm(compress(kc, pos_k, w1_k, w2_k), k_norm[0])
    cv = compress(vc, pos_v, w1_v, w2_v)

    nsel = S // SLC_BLOCK
    n_top = min(SLC_TOPN, nsel)
    kb = rms_norm(ks.reshape(B, S, G, DK), k_norm[1]).reshape(B, nsel, SLC_BLOCK, G, DK).transpose(0, 3, 1, 2, 4)
    vb = vs.reshape(B, nsel, SLC_BLOCK, G, DV).transpose(0, 3, 1, 2, 4)
    sstart = jnp.arange(nsel) * SLC_BLOCK
    ov = jnp.clip(jnp.minimum(cstart[:, None] + CMP_BLOCK, sstart[None, :] + SLC_BLOCK)
                  - jnp.maximum(cstart[:, None], sstart[None, :]), 0, None).astype(jnp.float32) / CMP_BLOCK

    kw_pad = jnp.pad(rms_norm(kw.reshape(B, S, G, DK), k_norm[2]), ((0, 0), (WINDOW, 0), (0, 0), (0, 0)))
    vw_pad = jnp.pad(vw.reshape(B, S, G, DV), ((0, 0), (WINDOW, 0), (0, 0), (0, 0)))

    tab = rel_bias.reshape(NUM_BUCKETS, G, HPG)
    tab_g = tab.transpose(1, 0, 2)
    scale = DK ** -0.5
    nb = S // QBLK
    qb = q.reshape(B, nb, QBLK, G, HPG, DK).transpose(1, 0, 2, 3, 4, 5)
    bi = jnp.arange(B)[:, None, None, None]
    gi = jnp.arange(G)[None, :, None, None]
    blk_id = jnp.arange(nsel)

    def block(args):
        qi, blk = args
        q0 = blk * QBLK
        qpos = q0 + jnp.arange(QBLK)
        s = jnp.einsum('bqghd,bngd->bghqn', qi, ck).astype(jnp.float32) * scale
        s = s + tab[t5_bucket(qpos[:, None] - cend[None, :])].transpose(2, 3, 0, 1)
        valid_c = cend[None, :] <= qpos[:, None]
        s = jnp.where(valid_c, s, NEG_INF)
        p_c = jax.nn.softmax(s, axis=-1) * jnp.any(valid_c, axis=-1)[:, None]
        o_c = jnp.einsum('bghqn,bngd->bqghd', p_c.astype(cv.dtype), cv)
        imp = jnp.einsum('bghqn,nm->bgqm', p_c, ov)
        cur = qpos // SLC_BLOCK
        valid_s = sstart[None, :] <= qpos[:, None]
        forced = (blk_id[None, :] == 0) | (blk_id[None, :] == cur[:, None]) | (blk_id[None, :] == cur[:, None] - 1)
        score = jnp.where(valid_s & forced, FORCE_SCORE, jnp.where(valid_s, imp, -1.0))
        _, idx = lax.top_k(score, n_top)
        k_sel = kb[bi, gi, idx].reshape(B, G, QBLK, n_top * SLC_BLOCK, DK)
        v_sel = vb[bi, gi, idx].reshape(B, G, QBLK, n_top * SLC_BLOCK, DV)
        kpos = (idx[..., None] * SLC_BLOCK + jnp.arange(SLC_BLOCK)).reshape(B, G, QBLK, n_top * SLC_BLOCK)
        dist = qpos[None, None, :, None] - kpos
        s = jnp.einsum('bqghd,bgqkd->bghqk', qi, k_sel).astype(jnp.float32) * scale
        s = s + tab_g[gi, t5_bucket(dist)].transpose(0, 1, 4, 2, 3)
        s = jnp.where((dist >= 0)[:, :, None], s, NEG_INF)
        p_s = jax.nn.softmax(s, axis=-1).astype(v_sel.dtype)
        o_s = jnp.einsum('bghqk,bgqkd->bqghd', p_s, v_sel)
        k_win = lax.dynamic_slice_in_dim(kw_pad, q0, WINDOW + QBLK, axis=1)
        v_win = lax.dynamic_slice_in_dim(vw_pad, q0, WINDOW + QBLK, axis=1)
        wpos = q0 - WINDOW + jnp.arange(WINDOW + QBLK)
        wdist = qpos[:, None] - wpos[None, :]
        wmask = (wdist >= 0) & (wdist < WINDOW) & (wpos[None, :] >= 0)
        s = jnp.einsum('bqghd,bkgd->bghqk', qi, k_win).astype(jnp.float32) * scale
        s = s + tab[t5_bucket(wdist)].transpose(2, 3, 0, 1)
        s = jnp.where(wmask, s, NEG_INF)
        p_w = jax.nn.softmax(s, axis=-1).astype(v_win.dtype)
        o_w = jnp.einsum('bghqk,bkgd->bqghd', p_w, v_win)
        return o_c, o_s, o_w

    o_c, o_s, o_w = lax.map(block, (qb, jnp.arange(nb)))

    def unblock(o):
        return o.transpose(1, 0, 2, 3, 4, 5).reshape(B, S, NSA_HEADS, DV)

    o = (gate[..., 0:1] * unblock(o_c) + gate[..., 1:2] * unblock(o_s)
         + gate[..., 2:3] * unblock(o_w))
    return o.reshape(B, S, NSA_OUT) @ w_out


def squared_relu_mlp(u, w1, w2):
    return jnp.square(jax.nn.relu(u @ w1)) @ w2


def setup_inputs(seed: int = 0) -> dict:
    key = jax.random.key(seed)
    keys = iter(jax.random.split(key, 40))
    f32 = jnp.float32

    def nrm(shape, fan_in):
        return jax.random.normal(next(keys), shape, f32) * fan_in ** -0.5

    def gain(shape):
        return 1.0 + 0.05 * jax.random.normal(next(keys), shape, f32)

    def small(shape, s=0.1):
        return s * jax.random.normal(next(keys), shape, f32)

    x = jax.random.normal(next(keys), (BATCH, SEQ, D_MODEL), f32)
    rel_bias = small((NUM_BUCKETS, NSA_HEADS), 0.5)
    norm_mix = gain((DEPTH, D_MODEL))
    norm_mlp = gain((DEPTH, D_MODEL))
    mlp_w1 = nrm((DEPTH, D_MODEL, D_FF), D_MODEL)
    mlp_w2 = nrm((DEPTH, D_FF, D_MODEL), D_FF)
    hy_w_in = nrm((N_EVEN, D_MODEL, HY_IN), D_MODEL)
    lru_conv_w = nrm((N_EVEN, CONV_WIDTH, LRU_WIDTH), CONV_WIDTH)
    lru_conv_b = small((N_EVEN, LRU_WIDTH))
    lru_wa = nrm((N_EVEN, LRU_BLOCKS, LRU_BLOCK_DIM, LRU_BLOCK_DIM), LRU_BLOCK_DIM)
    lru_ba = small((N_EVEN, LRU_BLOCKS, LRU_BLOCK_DIM))
    lru_wx = nrm((N_EVEN, LRU_BLOCKS, LRU_BLOCK_DIM, LRU_BLOCK_DIM), LRU_BLOCK_DIM)
    lru_bx = small((N_EVEN, LRU_BLOCKS, LRU_BLOCK_DIM))
    a_c = jax.random.uniform(next(keys), (N_EVEN, LRU_WIDTH), f32, minval=0.9, maxval=0.999)
    a0 = a_c ** (1.0 / LRU_C)
    lru_lambda = jnp.log(a0 / (1.0 - a0))
    mla_q_norm = gain((N_EVEN, MLA_Q_RANK))
    mla_kv_norm = gain((N_EVEN, MLA_KV_RANK))
    mla_w_uq = nrm((N_EVEN, MLA_Q_RANK, MLA_HEADS * MLA_QK), MLA_Q_RANK)
    mla_w_ukv = nrm((N_EVEN, MLA_KV_RANK, MLA_HEADS * (MLA_NOPE + MLA_V)), MLA_KV_RANK)
    mla_qn_q = gain((N_EVEN, MLA_QK))
    mla_qn_k = gain((N_EVEN, MLA_QK))
    hy_w_out = nrm((N_EVEN, HY_OUT, D_MODEL), HY_OUT)
    nsa_w_in = nrm((N_ODD, D_MODEL, NSA_IN), D_MODEL)
    nsa_cmp_pos_k = small((N_ODD, CMP_BLOCK, NSA_DK))
    nsa_cmp_w1_k = nrm((N_ODD, CMP_BLOCK * NSA_DK, CMP_HIDDEN), CMP_BLOCK * NSA_DK)
    nsa_cmp_w2_k = nrm((N_ODD, CMP_HIDDEN, NSA_DK), CMP_HIDDEN)
    nsa_cmp_pos_v = small((N_ODD, CMP_BLOCK, NSA_DV))
    nsa_cmp_w1_v = nrm((N_ODD, CMP_BLOCK * NSA_DV, CMP_HIDDEN), CMP_BLOCK * NSA_DV)
    nsa_cmp_w2_v = nrm((N_ODD, CMP_HIDDEN, NSA_DV), CMP_HIDDEN)
    nsa_q_norm = gain((N_ODD, NSA_DK))
    nsa_k_norm = gain((N_ODD, 3, NSA_DK))
    nsa_w_out = nrm((N_ODD, NSA_OUT, D_MODEL), NSA_OUT)
    return {'x': x, 'rel_bias': rel_bias, 'norm_mix': norm_mix, 'norm_mlp': norm_mlp,
            'mlp_w1': mlp_w1, 'mlp_w2': mlp_w2, 'hy_w_in': hy_w_in,
            'lru_conv_w': lru_conv_w, 'lru_conv_b': lru_conv_b, 'lru_wa': lru_wa, 'lru_ba': lru_ba,
            'lru_wx': lru_wx, 'lru_bx': lru_bx, 'lru_lambda': lru_lambda,
            'mla_q_norm': mla_q_norm, 'mla_kv_norm': mla_kv_norm, 'mla_w_uq': mla_w_uq,
            'mla_w_ukv': mla_w_ukv, 'mla_qn_q': mla_qn_q, 'mla_qn_k': mla_qn_k, 'hy_w_out': hy_w_out,
            'nsa_w_in': nsa_w_in, 'nsa_cmp_pos_k': nsa_cmp_pos_k, 'nsa_cmp_w1_k': nsa_cmp_w1_k,
            'nsa_cmp_w2_k': nsa_cmp_w2_k, 'nsa_cmp_pos_v': nsa_cmp_pos_v, 'nsa_cmp_w1_v': nsa_cmp_w1_v,
            'nsa_cmp_w2_v': nsa_cmp_w2_v, 'nsa_q_norm': nsa_q_norm, 'nsa_k_norm': nsa_k_norm,
            'nsa_w_out': nsa_w_out}


def reference(x, rel_bias, norm_mix, norm_mlp, mlp_w1, mlp_w2, hy_w_in,
              lru_conv_w, lru_conv_b, lru_wa, lru_ba, lru_wx, lru_bx, lru_lambda,
              mla_q_norm, mla_kv_norm, mla_w_uq, mla_w_ukv, mla_qn_q, mla_qn_k, hy_w_out,
              nsa_w_in, nsa_cmp_pos_k, nsa_cmp_w1_k, nsa_cmp_w2_k, nsa_cmp_pos_v, nsa_cmp_w1_v,
              nsa_cmp_w2_v, nsa_q_norm, nsa_k_norm, nsa_w_out):
    h = x
    for layer in range(DEPTH):
        u = rms_norm(h, norm_mix[layer])
        if layer % 2 == 0:
            e = layer // 2
            mix = hybrid_mixer(u, hy_w_in[e], lru_conv_w[e], lru_conv_b[e], lru_wa[e], lru_ba[e],
                               lru_wx[e], lru_bx[e], lru_lambda[e], mla_q_norm[e], mla_kv_norm[e],
                               mla_w_uq[e], mla_w_ukv[e], mla_qn_q[e], mla_qn_k[e], hy_w_out[e])
        else:
            o = layer // 2
            mix = nsa_mixer(u, nsa_w_in[o], nsa_cmp_pos_k[o], nsa_cmp_w1_k[o], nsa_cmp_w2_k[o],
                            nsa_cmp_pos_v[o], nsa_cmp_w1_v[o], nsa_cmp_w2_v[o], nsa_q_norm[o],
                            nsa_k_norm[o], rel_bias, nsa_w_out[o])
        h = h + mix
        h = h + squared_relu_mlp(rms_norm(h, norm_mlp[layer]), mlp_w1[layer], mlp_w2[layer])
    return h
```

```python
import functools
import math

import numpy as np
import jax
import jax.numpy as jnp
from jax import lax
from jax.experimental import pallas as pl
from jax.experimental.pallas import tpu as pltpu

F32 = jnp.float32
BF16 = jnp.bfloat16

NORM_EPS = 1e-6
MASK_VALUE = -1e30
FORCE_SCORE = 1e4
REMOVED_SCORE = -3e38

NUM_BUCKETS = 32
MAX_DISTANCE = 1024

LRU_WIDTH = 512
LRU_BLOCKS = 8
CONV_WIDTH = 4
LRU_C = 8.0

MLA_HEADS = 8
MLA_NOPE = 64
MLA_ROPE = 32
MLA_QK = MLA_NOPE + MLA_ROPE
MLA_V = 64
MLA_Q_RANK = 384
MLA_KV_RANK = 256
ROPE_THETA = 10000.0

NSA_HEADS = 16
NSA_GROUPS = 2
NSA_HPG = NSA_HEADS // NSA_GROUPS
NSA_DK = 64
CMP_BLOCK = 32
CMP_STRIDE = 16
CMP_HIDDEN = 256
SLC_BLOCK = 64
SLC_TOPN = 8
WINDOW = 512

LANES = 128
HALF = LANES // 2
NSA_TQ = 128
NSA_TK = 256
HEAD_ORDER = (0, 2, 4, 6, 1, 3, 5, 7)
VMEM_LIMIT = 56 * 1024 * 1024


def _cparams(sem):
    return pltpu.CompilerParams(dimension_semantics=sem, vmem_limit_bytes=VMEM_LIMIT)


def _rms(x, g, n=None):
    n = x.shape[-1] if n is None else n
    ms = jnp.sum(x * x, axis=-1, keepdims=True) * (1.0 / n)
    return x * lax.rsqrt(ms + NORM_EPS) * g


def _gelu_tanh(x):
    return 0.5 * x * (1.0 + jnp.tanh(math.sqrt(2.0 / math.pi) * (x + 0.044715 * (x * x * x))))


def _sigmoid(x):
    return 1.0 / (1.0 + jnp.exp(-x))


def _dot(a, b):
    return jnp.dot(a, b, preferred_element_type=F32)


def _dot_t(a, b):
    return lax.dot_general(a, b, (((1,), (1,)), ((), ())), preferred_element_type=F32)


def _norm_proj_kernel(x_ref, g_ref, w_ref, *out_refs, splits):
    xn = _rms(x_ref[...], g_ref[...]).astype(BF16)
    off = 0
    for o_ref, n in zip(out_refs, splits):
        o_ref[...] = _dot(xn, w_ref[:, off:off + n]).astype(o_ref.dtype)
        off += n


def _norm_proj(x, g, w, splits, tm=512):
    t, d = x.shape
    n = w.shape[1]
    assert sum(splits) == n and t % tm == 0
    return pl.pallas_call(
        functools.partial(_norm_proj_kernel, splits=splits),
        grid=(t // tm,),
        in_specs=[pl.BlockSpec((tm, d), lambda i: (i, 0)),
                  pl.BlockSpec((1, d), lambda i: (0, 0)),
                  pl.BlockSpec((d, n), lambda i: (0, 0))],
        out_specs=[pl.BlockSpec((tm, s), lambda i: (i, 0)) for s in splits],
        out_shape=[jax.ShapeDtypeStruct((t, s), F32) for s in splits],
        compiler_params=_cparams(("parallel",)),
        name="norm_proj",
    )(x, g.reshape(1, d), w)


def _mix_mlp_kernel(*refs, group_sizes):
    n_a = sum(group_sizes)
    n_g = len(group_sizes)
    h_ref = refs[0]
    a_refs = refs[1:1 + n_a]
    wo_refs = refs[1 + n_a:1 + n_a + n_g]
    g_ref, w1_ref, w2_ref, out_ref, hres, xn, acc = refs[1 + n_a + n_g:]
    f = pl.program_id(1)

    @pl.when(f == 0)
    def _():
        h1 = h_ref[...]
        k = 0
        for gi, gs in enumerate(group_sizes):
            a = a_refs[k][...].astype(F32)
            for j in range(1, gs):
                a = a + a_refs[k + j][...].astype(F32)
            k += gs
            h1 = h1 + _dot(a.astype(BF16), wo_refs[gi][...])
        hres[...] = h1
        xn[...] = _rms(h1, g_ref[...]).astype(BF16)
        acc[...] = jnp.zeros_like(acc)

    u = jnp.maximum(_dot(xn[...], w1_ref[...]), 0.0)
    acc[...] += _dot((u * u).astype(BF16), w2_ref[...])

    @pl.when(f == pl.num_programs(1) - 1)
    def _():
        out_ref[...] = hres[...] + acc[...]


def _mix_mlp(h, groups, wos, g, w1, w2, tm=512, tf=1024):
    t, d = h.shape
    ff = w1.shape[1]
    group_sizes = tuple(len(gr) for gr in groups)
    a_list = [a for gr in groups for a in gr]
    in_specs = [pl.BlockSpec((tm, d), lambda i, f: (i, 0))]
    in_specs += [pl.BlockSpec((tm, a.shape[1]), lambda i, f: (i, 0)) for a in a_list]
    in_specs += [pl.BlockSpec(w.shape, lambda i, f: (0, 0)) for w in wos]
    in_specs += [pl.BlockSpec((1, d), lambda i, f: (0, 0)),
                 pl.BlockSpec((d, tf), lambda i, f: (0, f)),
                 pl.BlockSpec((tf, d), lambda i, f: (f, 0))]
    return pl.pallas_call(
        functools.partial(_mix_mlp_kernel, group_sizes=group_sizes),
        grid=(t // tm, ff // tf),
        in_specs=in_specs,
        out_specs=pl.BlockSpec((tm, d), lambda i, f: (i, 0)),
        out_shape=jax.ShapeDtypeStruct((t, d), F32),
        scratch_shapes=[pltpu.VMEM((tm, d), F32), pltpu.VMEM((tm, d), BF16),
                        pltpu.VMEM((tm, d), F32)],
        compiler_params=_cparams(("parallel", "arbitrary")),
        name="mix_mlp",
    )(h, *a_list, *wos, g.reshape(1, d), w1, w2)


def _lru_kernel(xr_ref, xg_ref, cw_ref, cb_ref, wa_ref, ba_ref, wx_ref, bx_ref, lam_ref,
                o_ref, tail_ref, hc_ref, *, tc):
    c = pl.program_id(1)

    @pl.when(c == 0)
    def _():
        tail_ref[...] = jnp.zeros_like(tail_ref)
        hc_ref[...] = jnp.zeros_like(hc_ref)

    x = xr_ref[0]
    width = x.shape[1]
    row = lax.broadcasted_iota(jnp.int32, (tc, width), 0)
    row8 = lax.broadcasted_iota(jnp.int32, (8, width), 0)
    tail = tail_ref[...]
    cw = cw_ref[...]
    y = x * cw[CONV_WIDTH - 1:CONV_WIDTH]
    for s in range(1, CONV_WIDTH):
        sh = pltpu.roll(x, s, 0)
        head = jnp.where(row8 < s, pltpu.roll(tail, s, 0), sh[0:8])
        sh = jnp.concatenate([head, sh[8:]], axis=0)
        y = y + sh * cw[CONV_WIDTH - 1 - s:CONV_WIDTH - s]
    y = y + cb_ref[...]
    tail_ref[...] = x[tc - 8:tc]

    yb = y.astype(BF16)
    r = _sigmoid(_dot(yb, wa_ref[...]) + ba_ref[...])
    i = _sigmoid(_dot(yb, wx_ref[...]) + bx_ref[...])
    nl = -lam_ref[...]
    softplus = jnp.maximum(nl, 0.0) + jnp.log(1.0 + jnp.exp(-jnp.abs(nl)))
    log_a = (-LRU_C) * r * softplus
    a = jnp.exp(log_a)
    mult = jnp.sqrt(1.0 - jnp.exp(2.0 * log_a))
    mult = jnp.where((row == 0) & (c == 0), 1.0, mult)
    b = mult * (i * y)

    d = 1
    while d < tc:
        keep = row >= d
        a_sh = jnp.where(keep, pltpu.roll(a, d, 0), 1.0)
        b_sh = jnp.where(keep, pltpu.roll(b, d, 0), 0.0)
        b = a * b_sh + b
        a = a * a_sh
        d *= 2
    h = b + a * hc_ref[...]
    hc_ref[...] = h[tc - 1:tc]
    o_ref[0] = (h * _gelu_tanh(xg_ref[0])).astype(o_ref.dtype)


def _block_diag(w):
    n, d, e = w.shape
    eye = jnp.eye(n, dtype=w.dtype)
    return (w[:, :, None, :] * eye[:, None, :, None]).reshape(n * d, n * e)


def _rg_lru(z_lru, conv_w, conv_b, wa, ba, wx, bx, lam, tc=256):
    bsz, s, _ = z_lru.shape
    c = LRU_WIDTH
    tc = min(tc, s)
    vec = lambda v: pl.BlockSpec(v, lambda b, i: (0, 0))
    return pl.pallas_call(
        functools.partial(_lru_kernel, tc=tc),
        grid=(bsz, s // tc),
        in_specs=[pl.BlockSpec((1, tc, c), lambda b, i: (b, i, 0)),
                  pl.BlockSpec((1, tc, c), lambda b, i: (b, i, 1)),
                  vec((CONV_WIDTH, c)), vec((1, c)), vec((c, c)), vec((1, c)),
                  vec((c, c)), vec((1, c)), vec((1, c))],
        out_specs=pl.BlockSpec((1, tc, c), lambda b, i: (b, i, 0)),
        out_shape=jax.ShapeDtypeStruct((bsz, s, c), BF16),
        scratch_shapes=[pltpu.VMEM((8, c), F32), pltpu.VMEM((1, c), F32)],
        compiler_params=_cparams(("parallel", "arbitrary")),
        name="rg_lru",
    )(z_lru, z_lru, conv_w, conv_b.reshape(1, c), _block_diag(wa).astype(BF16), ba.reshape(1, c),
      _block_diag(wx).astype(BF16), bx.reshape(1, c), lam.reshape(1, c))


def _rope(x, cos, sin_lo, sin_hi):
    return (x * cos + pltpu.roll(x, LANES - MLA_ROPE // 2, 1) * sin_lo
            + pltpu.roll(x, MLA_ROPE // 2, 1) * sin_hi)


def _mla_prep_kernel(ql_ref, kvl_ref, kr_ref, gq_ref, gkv_ref, wq_ref, wk_ref, wv_ref,
                     nq_ref, nk_ref, cos_ref, s1_ref, s2_ref, q_out, k_out, v_out):
    qn = _rms(ql_ref[...], gq_ref[...]).astype(BF16)
    kvn = _rms(kvl_ref[...], gkv_ref[...]).astype(BF16)
    kr = kr_ref[...]
    cos, s1, s2 = cos_ref[...], s1_ref[...], s2_ref[...]
    scale = MLA_QK ** -0.5
    for h in range(MLA_HEADS):
        sl = slice(h * LANES, (h + 1) * LANES)
        qh = _rms(_dot(qn, wq_ref[:, sl]), nq_ref[...], MLA_QK)
        q_out[0, h] = (_rope(qh, cos, s1, s2) * scale).astype(BF16)
        kh = _rms(_dot(kvn, wk_ref[:, sl]) + kr, nk_ref[...], MLA_QK)
        k_out[0, h] = _rope(kh, cos, s1, s2).astype(BF16)
        v_out[0, h] = _dot(kvn, wv_ref[:, sl]).astype(BF16)


def _rope_tables(s):
    half = MLA_ROPE // 2
    freqs = ROPE_THETA ** (-jnp.arange(half, dtype=F32) / half)
    ang = jnp.arange(s, dtype=F32)[:, None] * freqs[None, :]
    cos, sin = jnp.cos(ang), jnp.sin(ang)
    z = lambda n: jnp.zeros((s, n), F32)
    cos_t = jnp.concatenate([jnp.ones((s, MLA_NOPE), F32), cos, cos, z(LANES - MLA_QK)], axis=1)
    s_lo = jnp.concatenate([z(MLA_NOPE), -sin, z(LANES - MLA_NOPE - half)], axis=1)
    s_hi = jnp.concatenate([z(MLA_NOPE + half), sin, z(LANES - MLA_QK)], axis=1)
    return cos_t, s_lo, s_hi


def _pad_cols(w, n):
    return jnp.pad(w, ((0, 0),) * (w.ndim - 1) + ((0, n - w.shape[-1]),))


def _mla_prep(q_lat, kv_lat, k_rope, bsz, s, q_norm, kv_norm, w_uq, w_ukv, qn_q, qn_k, tm=512):
    tm = min(tm, s)
    nt = s // tm
    hl = MLA_HEADS * LANES
    wq = _pad_cols(w_uq.reshape(MLA_Q_RANK, MLA_HEADS, MLA_QK), LANES).reshape(MLA_Q_RANK, hl)
    wkv = w_ukv.reshape(MLA_KV_RANK, MLA_HEADS, MLA_NOPE + MLA_V)
    wk = _pad_cols(wkv[..., :MLA_NOPE], LANES).reshape(MLA_KV_RANK, hl)
    wv = wkv[..., MLA_NOPE:]
    zero = jnp.zeros_like(wv)
    even = (jnp.arange(MLA_HEADS) % 2 == 0)[None, :, None]
    wv = jnp.concatenate([jnp.where(even, wv, zero), jnp.where(even, zero, wv)], axis=-1)
    wv = wv.reshape(MLA_KV_RANK, hl)
    cos_t, s_lo, s_hi = _rope_tables(s)
    row = lambda n: pl.BlockSpec((tm, n), lambda b, i: (b * nt + i, 0))
    full = lambda a: pl.BlockSpec(a.shape, lambda b, i: (0, 0))
    tab = pl.BlockSpec((tm, LANES), lambda b, i: (i, 0))
    head_out = pl.BlockSpec((1, MLA_HEADS, tm, LANES), lambda b, i: (b, 0, i, 0))
    args = (q_lat, kv_lat, k_rope, q_norm.reshape(1, -1), kv_norm.reshape(1, -1),
            wq.astype(BF16), wk.astype(BF16), wv.astype(BF16),
            _pad_cols(qn_q.reshape(1, -1), LANES), _pad_cols(qn_k.reshape(1, -1), LANES),
            cos_t, s_lo, s_hi)
    in_specs = [row(MLA_Q_RANK), row(MLA_KV_RANK), row(LANES)] + [full(a) for a in args[3:10]] + [tab] * 3
    shp = jax.ShapeDtypeStruct((bsz, MLA_HEADS, s, LANES), BF16)
    return pl.pallas_call(
        _mla_prep_kernel,
        grid=(bsz, nt),
        in_specs=in_specs,
        out_specs=[head_out] * 3,
        out_shape=[shp] * 3,
        compiler_params=_cparams(("parallel", "parallel")),
        name="mla_prep",
    )(*args)


def _mla_attn_kernel(q_ref, k_ref, v_ref, o_ref, m_ref, l_ref, acc_ref, *, tq, tk):
    qi = pl.program_id(2)
    kj = pl.program_id(3)
    last = (qi * tq + tq - 1) // tk

    @pl.when(kj == 0)
    def _():
        m_ref[...] = jnp.full_like(m_ref, -jnp.inf)
        l_ref[...] = jnp.zeros_like(l_ref)
        acc_ref[...] = jnp.zeros_like(acc_ref)

    def step(masked):
        lane_lo = lax.broadcasted_iota(jnp.int32, (tq, LANES), 1) < HALF
        alphas, pvs = [], []
        for j in range(2):
            s = _dot_t(q_ref[0, j], k_ref[0, j])
            if masked:
                qpos = qi * tq + lax.broadcasted_iota(jnp.int32, (tq, tk), 0)
                kpos = kj * tk + lax.broadcasted_iota(jnp.int32, (tq, tk), 1)
                s = jnp.where(kpos <= qpos, s, MASK_VALUE)
            m_old = m_ref[j]
            m_new = jnp.maximum(m_old, jnp.max(s, axis=-1, keepdims=True))
            alpha = jnp.exp(m_old - m_new)
            p = jnp.exp(s - m_new)
            l_ref[j] = alpha * l_ref[j] + jnp.sum(p, axis=-1, keepdims=True)
            m_ref[j] = m_new
            alphas.append(alpha)
            pvs.append(_dot(p.astype(BF16), v_ref[0, j]))
        acc_ref[...] = acc_ref[...] * jnp.where(lane_lo, alphas[0], alphas[1]) + pvs[0] + pvs[1]

    diag = (kj * tk + tk - 1) > (qi * tq)

    @pl.when((kj <= last) & diag)
    def _():
        step(True)

    @pl.when((kj <= last) & jnp.logical_not(diag))
    def _():
        step(False)

    @pl.when(kj == pl.num_programs(3) - 1)
    def _():
        lane_lo = lax.broadcasted_iota(jnp.int32, (tq, LANES), 1) < HALF
        o_ref[0] = (acc_ref[...] / jnp.where(lane_lo, l_ref[0], l_ref[1])).astype(o_ref.dtype)


def _mla_attention(q, k, v, tq=512, tk=512):
    bsz, nh, s, _ = q.shape
    tq, tk = min(tq, s), min(tk, s)
    kv_idx = lambda b, hp, i, j: (b, hp, jnp.minimum(j, (i * tq + tq - 1) // tk), 0)
    return pl.pallas_call(
        functools.partial(_mla_attn_kernel, tq=tq, tk=tk),
        grid=(bsz, nh // 2, s // tq, s // tk),
        in_specs=[pl.BlockSpec((1, 2, tq, LANES), lambda b, hp, i, j: (b, hp, i, 0)),
                  pl.BlockSpec((1, 2, tk, LANES), kv_idx),
                  pl.BlockSpec((1, 2, tk, LANES), kv_idx)],
        out_specs=pl.BlockSpec((1, tq, LANES), lambda b, hp, i, j: (b, i, hp)),
        out_shape=jax.ShapeDtypeStruct((bsz, s, nh * MLA_V), BF16),
        scratch_shapes=[pltpu.VMEM((2, tq, 1), F32), pltpu.VMEM((2, tq, 1), F32),
                        pltpu.VMEM((tq, LANES), F32)],
        compiler_params=_cparams(("parallel", "parallel", "parallel", "arbitrary")),
        name="mla_attn",
    )(q, k, v)


def _t5_bucket_np(dist):
    n = np.maximum(dist, 0)
    max_exact = NUM_BUCKETS // 2
    nf = np.maximum(n, 1).astype(np.float32)
    large = max_exact + (np.log(nf / np.float32(max_exact)) / np.float32(math.log(MAX_DISTANCE / max_exact))
                         * np.float32(NUM_BUCKETS - max_exact)).astype(np.int32)
    large = np.minimum(large, NUM_BUCKETS - 1)
    return np.where(n < max_exact, n, large).astype(np.int32)


def _bias_tiles_kernel(tab_ref, idx_ref, o_ref):
    slot = pl.program_id(0) * NSA_HPG + pl.program_id(2)
    idx = idx_ref[0]
    acc = jnp.full(idx.shape, MASK_VALUE, F32)
    for b in range(NUM_BUCKETS):
        acc = jnp.where(idx == b, tab_ref[slot, b], acc)
    o_ref[0, 0, 0] = acc


def _bias_tiles(tab_slots, idx):
    n, r, c = idx.shape
    return pl.pallas_call(
        _bias_tiles_kernel,
        grid=(NSA_GROUPS, n, NSA_HPG),
        in_specs=[pl.BlockSpec(memory_space=pltpu.SMEM),
                  pl.BlockSpec((1, r, c), lambda g, i, p: (i, 0, 0))],
        out_specs=pl.BlockSpec((1, 1, 1, r, c), lambda g, i, p: (g, i, p, 0, 0)),
        out_shape=jax.ShapeDtypeStruct((NSA_GROUPS, n, NSA_HPG, r, c), F32),
        compiler_params=_cparams(("parallel", "parallel", "parallel")),
        name="bias_tiles",
    )(tab_slots, jnp.asarray(idx))


def _toeplitz_bucket_ids(window):
    i = np.arange(NSA_TQ)[:, None]
    j = np.arange(NSA_TK)[None, :]
    tiles = []
    d = 0
    while True:
        dist = d * NSA_TQ + i - j
        ids = _t5_bucket_np(dist)
        ok = dist >= 0
        if window is not None:
            ok &= dist < window
        ids = np.where(ok, ids, NUM_BUCKETS).astype(np.int32)
        tiles.append(ids)
        if window is None and (ids == NUM_BUCKETS - 1).all():
            break
        if window is not None and not ok.any():
            tiles.pop()
            break
        d += 1
    return np.stack(tiles)


def _split_halves(x):
    lo = lax.broadcasted_iota(jnp.int32, x.shape, 1) < HALF
    return jnp.where(lo, x, 0.0), pltpu.roll(jnp.where(lo, 0.0, x), HALF, 1)


def _value_layouts(x):
    lo = lax.broadcasted_iota(jnp.int32, x.shape, 1) < HALF
    g0e = jnp.where(lo, x, 0.0)
    g1o = jnp.where(lo, 0.0, x)
    return (g0e, pltpu.roll(g0e, HALF, 1)), (pltpu.roll(g1o, HALF, 1), g1o)


def _nsa_prep_kernel(zq_ref, ks_ref, vs_ref, kw_ref, vw_ref, gq_ref, gks_ref, gkw_ref,
                     q_out, ks_out, kw_out, vs_out, vw_out, *, tm):
    i = pl.program_id(1)
    scale = NSA_DK ** -0.5
    for j in range(NSA_HEADS // 2):
        e, o = _split_halves(zq_ref[:, j * LANES:(j + 1) * LANES])
        g = (2 * j) // NSA_HPG
        pe = (2 * j - NSA_HPG * g) // 2
        q_out[0, NSA_HPG * g + pe] = (_rms(e, gq_ref[...], NSA_DK) * scale).astype(BF16)
        q_out[0, NSA_HPG * g + NSA_HPG // 2 + pe] = (_rms(o, gq_ref[...], NSA_DK) * scale).astype(BF16)

    lane = lax.broadcasted_iota(jnp.int32, (tm, LANES), 1)
    pos = i * tm + lax.broadcasted_iota(jnp.int32, (tm, LANES), 0)
    block_tag = jnp.where(lane - HALF == pos // SLC_BLOCK, MASK_VALUE, 0.0)
    for g, x in enumerate(_split_halves(ks_ref[...])):
        ks_out[0, g] = (_rms(x, gks_ref[...], NSA_DK) + block_tag).astype(BF16)
    for g, x in enumerate(_split_halves(kw_ref[...])):
        kw_out[0, g] = _rms(x, gkw_ref[...], NSA_DK).astype(BF16)
    for ref, out in ((vs_ref, vs_out), (vw_ref, vw_out)):
        for g, (ve, vo) in enumerate(_value_layouts(ref[...])):
            out[0, g, 0] = ve.astype(BF16)
            out[0, g, 1] = vo.astype(BF16)


def _nsa_prep(z_q, z_kv, bsz, s, q_norm, k_norm, tm=512):
    tm = min(tm, s)
    nt = s // tm
    row = lambda n, c: pl.BlockSpec((tm, n), lambda b, i: (b * nt + i, c))
    gain = pl.BlockSpec((1, LANES), lambda b, i: (0, 0))
    kspec = pl.BlockSpec((1, NSA_GROUPS, tm, LANES), lambda b, i: (b, 0, i, 0))
    vspec = pl.BlockSpec((1, NSA_GROUPS, 2, tm, LANES), lambda b, i: (b, 0, 0, i, 0))
    kshape = jax.ShapeDtypeStruct((bsz, NSA_GROUPS, s, LANES), BF16)
    vshape = jax.ShapeDtypeStruct((bsz, NSA_GROUPS, 2, s, LANES), BF16)
    pad = lambda v: _pad_cols(v.reshape(1, -1), LANES)
    return pl.pallas_call(
        functools.partial(_nsa_prep_kernel, tm=tm),
        grid=(bsz, nt),
        in_specs=[row(NSA_HEADS * NSA_DK, 0), row(LANES, 0), row(LANES, 1), row(LANES, 2), row(LANES, 3),
                  gain, gain, gain],
        out_specs=[pl.BlockSpec((1, NSA_HEADS, tm, LANES), lambda b, i: (b, 0, i, 0)),
                   kspec, kspec, vspec, vspec],
        out_shape=[jax.ShapeDtypeStruct((bsz, NSA_HEADS, s, LANES), BF16), kshape, kshape, vshape, vshape],
        compiler_params=_cparams(("parallel", "parallel")),
        name="nsa_prep",
    )(z_q, z_kv, z_kv, z_kv, z_kv, pad(q_norm), pad(k_norm[1]), pad(k_norm[2]))


def _compress_kernel(xk_ref, xv_ref, pk_ref, pv_ref, w1k_ref, w1v_ref, w2k_ref, w2v_ref, gk_ref,
                     ck_out, cv_out):
    def mlp(x, p_ref, w1_ref, w2_ref):
        nh = x.shape[0]
        top = _dot((x + p_ref[0:1]).astype(BF16), w1_ref[0])
        bot = _dot((x + p_ref[1:2]).astype(BF16), w1_ref[1])
        hid = _gelu_tanh(top + pltpu.roll(bot, nh - 1, 0))
        return _dot(hid.astype(BF16), w2_ref[...])

    yk = mlp(xk_ref[0], pk_ref, w1k_ref, w2k_ref)
    for g, x in enumerate(_split_halves(yk)):
        ck_out[0, g] = _rms(x, gk_ref[...], NSA_DK).astype(BF16)
    yv = mlp(xv_ref[0], pv_ref, w1v_ref, w2v_ref)
    for g, (ve, vo) in enumerate(_value_layouts(yv)):
        cv_out[0, g, 0] = ve.astype(BF16)
        cv_out[0, g, 1] = vo.astype(BF16)


def _compress_weights(pos, w1, w2):
    half = CMP_BLOCK // 2
    g = NSA_GROUPS
    eye = jnp.eye(g, dtype=F32)
    w1 = w1.reshape(2, half, NSA_DK, CMP_HIDDEN)
    w1 = w1[:, :, None, :, None, :] * eye[None, None, :, None, :, None]
    w1 = w1.reshape(2, half * g * NSA_DK, g * CMP_HIDDEN).astype(BF16)
    p = jnp.broadcast_to(pos.reshape(2, half, 1, NSA_DK), (2, half, g, NSA_DK)).reshape(2, half * g * NSA_DK)
    w2 = (w2[None, :, None, :] * eye[:, None, :, None]).reshape(g * CMP_HIDDEN, g * NSA_DK).astype(BF16)
    return p, w1, w2


def _compress(z_kc, z_vc, bsz, s, pos_k, w1_k, w2_k, pos_v, w1_v, w2_v, k_norm0):
    nh = s // CMP_STRIDE
    feat = CMP_STRIDE * LANES
    pk, w1k, w2k = _compress_weights(pos_k, w1_k, w2_k)
    pv, w1v, w2v = _compress_weights(pos_v, w1_v, w2_v)
    full = lambda a: pl.BlockSpec(a.shape, lambda b: (0,) * a.ndim)
    xspec = pl.BlockSpec((1, nh, feat), lambda b: (b, 0, 0))
    gk = _pad_cols(k_norm0.reshape(1, -1), LANES)
    return pl.pallas_call(
        _compress_kernel,
        grid=(bsz,),
        in_specs=[xspec, xspec, full(pk), full(pv), full(w1k), full(w1v), full(w2k), full(w2v), full(gk)],
        out_specs=[pl.BlockSpec((1, NSA_GROUPS, nh, LANES), lambda b: (b, 0, 0, 0)),
                   pl.BlockSpec((1, NSA_GROUPS, 2, nh, LANES), lambda b: (b, 0, 0, 0, 0))],
        out_shape=[jax.ShapeDtypeStruct((bsz, NSA_GROUPS, nh, LANES), BF16),
                   jax.ShapeDtypeStruct((bsz, NSA_GROUPS, 2, nh, LANES), BF16)],
        compiler_params=_cparams(("parallel",)),
        name="nsa_compress",
    )(z_kc.reshape(bsz, nh, feat), z_vc.reshape(bsz, nh, feat), pk, pv, w1k, w1v, w2k, w2v, gk)


def _pair_heads(o, tq):
    half = NSA_HPG // 2
    return [o[j * tq:(j + 1) * tq] + o[(half + j) * tq:(half + j + 1) * tq] for j in range(half)]


def _cmp_attn_kernel(q_ref, ck_ref, cv_ref, bias_ref, ov_ref, gate_ref, o_ref, q2_ref, *, tq, nsel):
    t = pl.program_id(1)
    rows = NSA_HPG * tq
    nh = ck_ref.shape[2]
    q = q_ref[0].reshape(rows, LANES)
    s = _dot_t(q, ck_ref[0, 0]) + bias_ref[0, 0].reshape(rows, nh)
    m = jnp.max(s, axis=-1, keepdims=True)
    e = jnp.exp(s - m)
    p = e / jnp.sum(e, axis=-1, keepdims=True)
    qrow = t * tq + jnp.bitwise_and(lax.broadcasted_iota(jnp.int32, (rows, 1), 0), tq - 1)
    p = jnp.where(qrow >= CMP_BLOCK - 1, p, 0.0)
    pb = p.astype(BF16)
    hr = rows // 2
    o = jnp.concatenate([_dot(pb[:hr], cv_ref[0, 0, 0]), _dot(pb[hr:], cv_ref[0, 0, 1])], axis=0)
    o = o * _sigmoid(gate_ref[0, 0, 0, 0])
    for j, oj in enumerate(_pair_heads(o, tq)):
        o_ref[0, :, j * LANES:(j + 1) * LANES] = oj.astype(o_ref.dtype)

    psum = p[0:tq]
    for h in range(1, NSA_HPG):
        psum = psum + p[h * tq:(h + 1) * tq]
    hi = psum.astype(BF16)
    lo = (psum - hi.astype(F32)).astype(BF16)
    imp = _dot(hi, ov_ref[...]) + _dot(lo, ov_ref[...])

    lane = lax.broadcasted_iota(jnp.int32, (tq, LANES), 1)
    blk = lane - HALF
    qpos = t * tq + lax.broadcasted_iota(jnp.int32, (tq, LANES), 0)
    cur = qpos // SLC_BLOCK
    valid = blk * SLC_BLOCK <= qpos
    forced = (blk == 0) | (blk == cur) | (blk == cur - 1)
    score = jnp.where(valid & forced, FORCE_SCORE, jnp.where(valid, imp, -1.0))
    score = jnp.where((blk >= 0) & (blk < nsel), score, REMOVED_SCORE)
    sel = jnp.zeros((tq, LANES), jnp.bool_)
    for _ in range(min(SLC_TOPN, nsel)):
        best = jnp.max(score, axis=-1, keepdims=True)
        first = jnp.min(jnp.where(score == best, lane, LANES), axis=-1, keepdims=True)
        pick = lane == first
        sel = sel | pick
        score = jnp.where(pick, REMOVED_SCORE, score)
    unselected = jnp.where((blk >= 0) & jnp.logical_not(sel), 1.0, 0.0).astype(BF16)
    for h in range(NSA_HPG):
        q2_ref[0, h] = q_ref[0, h] + unselected


def _cmp_attention(q, ck, cv, bias, ov, gates, s):
    bsz = q.shape[0]
    tq = NSA_TQ
    nqt = s // tq
    nh = ck.shape[2]
    rows = NSA_HPG * tq
    nsel = s // SLC_BLOCK
    return pl.pallas_call(
        functools.partial(_cmp_attn_kernel, tq=tq, nsel=nsel),
        grid=(NSA_GROUPS, nqt, bsz),
        in_specs=[pl.BlockSpec((1, NSA_HPG, tq, LANES), lambda g, t, b: (b, g, t, 0)),
                  pl.BlockSpec((1, 1, nh, LANES), lambda g, t, b: (b, g, 0, 0)),
                  pl.BlockSpec((1, 1, 2, nh, LANES), lambda g, t, b: (b, g, 0, 0, 0)),
                  pl.BlockSpec((1, 1, NSA_HPG, tq, nh), lambda g, t, b: (g, t, 0, 0, 0)),
                  pl.BlockSpec((nh, LANES), lambda g, t, b: (0, 0)),
                  pl.BlockSpec((1, 1, 1, 1, rows, 1), lambda g, t, b: (b, g, 0, t, 0, 0))],
        out_specs=[pl.BlockSpec((1, tq, NSA_HPG * NSA_DK), lambda g, t, b: (b, t, g)),
                   pl.BlockSpec((1, NSA_HPG, tq, LANES), lambda g, t, b: (b, g, t, 0))],
        out_shape=[jax.ShapeDtypeStruct((bsz, s, NSA_HEADS * NSA_DK), BF16),
                   jax.ShapeDtypeStruct(q.shape, BF16)],
        compiler_params=_cparams(("parallel", "parallel", "parallel")),
        name="nsa_cmp_attn",
    )(q, ck, cv, bias, ov, gates)


def _nsa_flash_kernel(q_ref, k_ref, v_ref, bias_ref, gate_ref, o_ref, m_ref, l_ref, acc_ref,
                      *, tq, window_steps):
    t = pl.program_id(2)
    j = pl.program_id(3)
    rows = NSA_HPG * tq
    ratio = NSA_TK // tq
    n_off = bias_ref.shape[1]
    if window_steps is None:
        kj = j
        active = j <= t // ratio
    else:
        kj = t // ratio - (window_steps - 1) + j
        active = kj >= 0

    @pl.when(j == 0)
    def _():
        m_ref[...] = jnp.full_like(m_ref, -jnp.inf)
        l_ref[...] = jnp.zeros_like(l_ref)
        acc_ref[...] = jnp.zeros_like(acc_ref)

    @pl.when(active)
    def _():
        off = jnp.minimum(t - ratio * kj, n_off - 1)
        q = q_ref[0].reshape(rows, LANES)
        s = _dot_t(q, k_ref[0, 0]) + bias_ref[0, off].reshape(rows, NSA_TK)
        m_old = m_ref[...]
        m_new = jnp.maximum(m_old, jnp.max(s, axis=-1, keepdims=True))
        alpha = jnp.exp(m_old - m_new)
        p = jnp.exp(s - m_new)
        l_ref[...] = alpha * l_ref[...] + jnp.sum(p, axis=-1, keepdims=True)
        m_ref[...] = m_new
        pb = p.astype(BF16)
        hr = rows // 2
        pv = jnp.concatenate([_dot(pb[:hr], v_ref[0, 0, 0]), _dot(pb[hr:], v_ref[0, 0, 1])], axis=0)
        acc_ref[...] = acc_ref[...] * alpha + pv

    @pl.when(j == pl.num_programs(3) - 1)
    def _():
        o = acc_ref[...] * (_sigmoid(gate_ref[0, 0, 0, 0]) / l_ref[...])
        for jj, oj in enumerate(_pair_heads(o, tq)):
            o_ref[0, :, jj * LANES:(jj + 1) * LANES] = oj.astype(o_ref.dtype)


def _nsa_flash(q, k, v, bias, gates, branch, s, window_steps=None):
    bsz = q.shape[0]
    tq = NSA_TQ
    nqt = s // tq
    rows = NSA_HPG * tq
    ratio = NSA_TK // tq
    n_off = bias.shape[1]
    if window_steps is None:
        nsteps = s // NSA_TK
        kv_of = lambda t, j: jnp.minimum(j, t // ratio)
    else:
        nsteps = window_steps
        kv_of = lambda t, j: jnp.maximum(t // ratio - (window_steps - 1) + j, 0)
    return pl.pallas_call(
        functools.partial(_nsa_flash_kernel, tq=tq, window_steps=window_steps),
        grid=(NSA_GROUPS, bsz, nqt, nsteps),
        in_specs=[pl.BlockSpec((1, NSA_HPG, tq, LANES), lambda g, b, t, j: (b, g, t, 0)),
                  pl.BlockSpec((1, 1, NSA_TK, LANES), lambda g, b, t, j: (b, g, kv_of(t, j), 0)),
                  pl.BlockSpec((1, 1, 2, NSA_TK, LANES), lambda g, b, t, j: (b, g, 0, kv_of(t, j), 0)),
                  pl.BlockSpec((1, n_off, NSA_HPG, tq, NSA_TK), lambda g, b, t, j: (g, 0, 0, 0, 0)),
                  pl.BlockSpec((1, 1, 1, 1, rows, 1), lambda g, b, t, j: (b, g, branch, t, 0, 0))],
        out_specs=pl.BlockSpec((1, tq, NSA_HPG * NSA_DK), lambda g, b, t, j: (b, t, g)),
        out_shape=jax.ShapeDtypeStruct((bsz, s, NSA_HEADS * NSA_DK), BF16),
        scratch_shapes=[pltpu.VMEM((rows, 1), F32), pltpu.VMEM((rows, 1), F32),
                        pltpu.VMEM((rows, LANES), F32)],
        compiler_params=_cparams(("parallel", "parallel", "parallel", "arbitrary")),
        name="nsa_flash_sel" if window_steps is None else "nsa_flash_win",
    )(q, k, v, bias, gates)


def _hybrid_layer(h, bsz, s, g_mix, g_mlp, w1, w2, w_in, conv_w, conv_b, wa, ba, wx, bx, lam,
                  q_norm, kv_norm, w_uq, w_ukv, qn_q, qn_k, w_out):
    d = h.shape[1]
    c = LRU_WIDTH
    o_kr = 2 * c + MLA_Q_RANK + MLA_KV_RANK
    w_kr = jnp.pad(w_in[:, o_kr:], ((0, 0), (MLA_NOPE, LANES - MLA_QK)))
    w_all = jnp.concatenate([w_in[:, :o_kr], w_kr], axis=1).astype(BF16)
    z_lru, q_lat, kv_lat, k_rope = _norm_proj(h, g_mix, w_all, (2 * c, MLA_Q_RANK, MLA_KV_RANK, LANES))
    lru = _rg_lru(z_lru.reshape(bsz, s, 2 * c), conv_w, conv_b, wa, ba, wx, bx, lam)
    q, k, v = _mla_prep(q_lat, kv_lat, k_rope, bsz, s, q_norm, kv_norm, w_uq, w_ukv, qn_q, qn_k)
    mla = _mla_attention(q, k, v)
    wo = w_out.astype(BF16)
    return _mix_mlp(h, [[lru.reshape(bsz * s, c)], [mla.reshape(bsz * s, -1)]], [wo[:c], wo[c:]],
                    g_mlp, w1.astype(BF16), w2.astype(BF16))


def _nsa_layer(h, bsz, s, g_mix, g_mlp, w1, w2, w_in, pos_k, w1_k, w2_k, pos_v, w1_v, w2_v,
               q_norm, k_norm, rel_bias, w_out):
    nq = NSA_HEADS * NSA_DK
    kvw = NSA_GROUPS * NSA_DK
    n_gate = 3 * NSA_HEADS
    w_gate = _pad_cols(w_in[:, nq + 6 * kvw:], LANES)
    w_all = jnp.concatenate([w_in[:, :nq + 6 * kvw], w_gate], axis=1).astype(BF16)
    z_q, z_kc, z_vc, z_kv, z_gate = _norm_proj(h, g_mix, w_all, (nq, kvw, kvw, 4 * kvw, LANES))

    q, ks, kw, vs, vw = _nsa_prep(z_q, z_kv, bsz, s, q_norm, k_norm)
    ck, cv = _compress(z_kc, z_vc, bsz, s, pos_k, w1_k, w2_k, pos_v, w1_v, w2_v, k_norm[0])

    tq = NSA_TQ
    nqt = s // tq
    order = np.array(HEAD_ORDER)
    gl = z_gate[:, :n_gate].reshape(bsz, nqt, tq, NSA_GROUPS, NSA_HPG, 3)[:, :, :, :, order, :]
    gl = gl.transpose(0, 3, 5, 1, 4, 2).reshape(bsz, NSA_GROUPS, 3, nqt, NSA_HPG * tq, 1)

    slot_heads = (np.arange(NSA_GROUPS)[:, None] * NSA_HPG + order[None, :]).reshape(-1)
    tab_slots = rel_bias.T[slot_heads]
    nh = s // CMP_STRIDE
    nc = (s - CMP_BLOCK) // CMP_STRIDE + 1
    cend = np.arange(nh) * CMP_STRIDE + CMP_BLOCK - 1
    qpos = np.arange(s)
    cdist = qpos[:, None] - cend[None, :]
    cmp_ids = np.where((cdist >= 0) & (np.arange(nh)[None, :] < nc), _t5_bucket_np(cdist), NUM_BUCKETS)
    cmp_bias = _bias_tiles(tab_slots, cmp_ids.reshape(nqt, tq, nh).astype(np.int32))
    sel_bias = _bias_tiles(tab_slots, _toeplitz_bucket_ids(None))
    win_ids = _toeplitz_bucket_ids(WINDOW)
    win_bias = _bias_tiles(tab_slots, win_ids)

    nsel = s // SLC_BLOCK
    cstart = np.arange(nh) * CMP_STRIDE
    sstart = np.arange(nsel) * SLC_BLOCK
    ov = np.clip(np.minimum(cstart[:, None] + CMP_BLOCK, sstart[None, :] + SLC_BLOCK)
                 - np.maximum(cstart[:, None], sstart[None, :]), 0, None).astype(np.float32) / CMP_BLOCK
    ov[nc:] = 0.0
    ov_p = np.zeros((nh, LANES), np.float32)
    ov_p[:, HALF:HALF + nsel] = ov

    o_c, q_sel = _cmp_attention(q, ck, cv, cmp_bias, jnp.asarray(ov_p, BF16), gl, s)
    o_s = _nsa_flash(q_sel, ks, vs, sel_bias, gl, 1, s)
    window_steps = (win_ids.shape[0] + 1) // (NSA_TK // tq)
    o_w = _nsa_flash(q, kw, vw, win_bias, gl, 2, s, window_steps=window_steps)

    flat = lambda a: a.reshape(bsz * s, -1)
    return _mix_mlp(h, [[flat(o_c), flat(o_s), flat(o_w)]], [w_out.astype(BF16)],
                    g_mlp, w1.astype(BF16), w2.astype(BF16))


def kernel(x, rel_bias, norm_mix, norm_mlp, mlp_w1, mlp_w2, hy_w_in, lru_conv_w, lru_conv_b, lru_wa, lru_ba, lru_wx, lru_bx, lru_lambda, mla_q_norm, mla_kv_norm, mla_w_uq, mla_w_ukv, mla_qn_q, mla_qn_k, hy_w_out, nsa_w_in, nsa_cmp_pos_k, nsa_cmp_w1_k, nsa_cmp_w2_k, nsa_cmp_pos_v, nsa_cmp_w1_v, nsa_cmp_w2_v, nsa_q_norm, nsa_k_norm, nsa_w_out):
    bsz, s, d = x.shape
    depth = norm_mix.shape[0]
    h = x.reshape(bsz * s, d)
    for layer in range(depth):
        if layer % 2 == 0:
            e = layer // 2
            h = _hybrid_layer(h, bsz, s, norm_mix[layer], norm_mlp[layer], mlp_w1[layer], mlp_w2[layer],
                              hy_w_in[e], lru_conv_w[e], lru_conv_b[e], lru_wa[e], lru_ba[e], lru_wx[e],
                              lru_bx[e], lru_lambda[e], mla_q_norm[e], mla_kv_norm[e], mla_w_uq[e],
                              mla_w_ukv[e], mla_qn_q[e], mla_qn_k[e], hy_w_out[e])
        else:
            o = layer // 2
            h = _nsa_layer(h, bsz, s, norm_mix[layer], norm_mlp[layer], mlp_w1[layer], mlp_w2[layer],
                           nsa_w_in[o], nsa_cmp_pos_k[o], nsa_cmp_w1_k[o], nsa_cmp_w2_k[o],
                           nsa_cmp_pos_v[o], nsa_cmp_w1_v[o], nsa_cmp_w2_v[o], nsa_q_norm[o],
                           nsa_k_norm[o], rel_bias, nsa_w_out[o])
    return h.reshape(bsz, s, d)
```

```python
import functools
import math

import numpy as np
import jax
import jax.numpy as jnp
from jax import lax
from jax.experimental import pallas as pl
from jax.experimental.pallas import tpu as pltpu

F32 = jnp.float32
BF16 = jnp.bfloat16

NORM_EPS = 1e-6
MASK_VALUE = -1e30
FORCE_SCORE = 1e4
REMOVED_SCORE = -3e38

NUM_BUCKETS = 32
MAX_DISTANCE = 1024

LRU_WIDTH = 512
LRU_BLOCKS = 8
CONV_WIDTH = 4
LRU_C = 8.0

MLA_HEADS = 8
MLA_NOPE = 64
MLA_ROPE = 32
MLA_QK = MLA_NOPE + MLA_ROPE
MLA_V = 64
MLA_Q_RANK = 384
MLA_KV_RANK = 256
ROPE_THETA = 10000.0

NSA_HEADS = 16
NSA_GROUPS = 2
NSA_HPG = NSA_HEADS // NSA_GROUPS
NSA_DK = 64
CMP_BLOCK = 32
CMP_STRIDE = 16
CMP_HIDDEN = 256
SLC_BLOCK = 64
SLC_TOPN = 8
WINDOW = 512

LANES = 128
HALF = LANES // 2
NSA_TQ = 128
NSA_TK = 256
MLA_TR = 128
MLA_TK = 256
HEAD_ORDER = (0, 2, 4, 6, 1, 3, 5, 7)
VMEM_LIMIT = 56 * 1024 * 1024


def _cparams(sem):
    return pltpu.CompilerParams(dimension_semantics=sem, vmem_limit_bytes=VMEM_LIMIT)


def _rms(x, g, n=None):
    n = x.shape[-1] if n is None else n
    ms = jnp.sum(x * x, axis=-1, keepdims=True) * (1.0 / n)
    return x * lax.rsqrt(ms + NORM_EPS) * g


def _gelu_tanh(x):
    return 0.5 * x * (1.0 + jnp.tanh(math.sqrt(2.0 / math.pi) * (x + 0.044715 * (x * x * x))))


def _sigmoid(x):
    return 1.0 / (1.0 + jnp.exp(-x))


def _dot(a, b):
    return jnp.dot(a, b, preferred_element_type=F32)


def _dot_t(a, b):
    return lax.dot_general(a, b, (((1,), (1,)), ((), ())), preferred_element_type=F32)


def _online_softmax(s, m_old, l_old):
    reps = s.shape[1] // LANES
    m_new = jnp.maximum(m_old, jnp.max(s, axis=-1, keepdims=True))
    alpha = jnp.exp(m_old - m_new)
    p = jnp.exp(s - jnp.concatenate([m_new] * reps, axis=1))
    l_new = alpha * l_old + jnp.sum(p, axis=-1, keepdims=True)
    return m_new, l_new, alpha, p


def _norm_proj_kernel(x_ref, g_ref, w_ref, *out_refs, splits):
    xn = _rms(x_ref[...], g_ref[...]).astype(BF16)
    off = 0
    for o_ref, n in zip(out_refs, splits):
        o_ref[...] = _dot(xn, w_ref[:, off:off + n]).astype(o_ref.dtype)
        off += n


def _norm_proj(x, g, w, splits, tm=512):
    t, d = x.shape
    n = w.shape[1]
    assert sum(splits) == n and t % tm == 0
    return pl.pallas_call(
        functools.partial(_norm_proj_kernel, splits=splits),
        grid=(t // tm,),
        in_specs=[pl.BlockSpec((tm, d), lambda i: (i, 0)),
                  pl.BlockSpec((1, d), lambda i: (0, 0)),
                  pl.BlockSpec((d, n), lambda i: (0, 0))],
        out_specs=[pl.BlockSpec((tm, s), lambda i: (i, 0)) for s in splits],
        out_shape=[jax.ShapeDtypeStruct((t, s), F32) for s in splits],
        compiler_params=_cparams(("parallel",)),
        name="norm_proj",
    )(x, g.reshape(1, d), w)


def _mix_mlp_kernel(*refs, group_sizes, gated):
    n_a = sum(group_sizes)
    n_g = len(group_sizes)
    h_ref = refs[0]
    a_refs = refs[1:1 + n_a]
    wo_refs = refs[1 + n_a:1 + n_a + n_g]
    rest = refs[1 + n_a + n_g:]
    if gated:
        zg_ref, e_ref = rest[:2]
        rest = rest[2:]
    g_ref, w1_ref, w2_ref, out_ref, hres, xn, acc = rest
    f = pl.program_id(1)

    @pl.when(f == 0)
    def _():
        h1 = h_ref[...]
        if gated:
            gate = _sigmoid(zg_ref[...])
            g_hi = gate.astype(BF16)
            g_lo = (gate - g_hi.astype(F32)).astype(BF16)
        k = 0
        for gi, gs in enumerate(group_sizes):
            a = None
            for j in range(gs):
                aj = a_refs[k + j][...].astype(F32)
                if gated:
                    aj = aj * (_dot(g_hi, e_ref[j]) + _dot(g_lo, e_ref[j]))
                a = aj if a is None else a + aj
            k += gs
            h1 = h1 + _dot(a.astype(BF16), wo_refs[gi][...])
        hres[...] = h1
        xn[...] = _rms(h1, g_ref[...]).astype(BF16)
        acc[...] = jnp.zeros_like(acc)

    u = jnp.maximum(_dot(xn[...], w1_ref[...]), 0.0)
    acc[...] += _dot((u * u).astype(BF16), w2_ref[...])

    @pl.when(f == pl.num_programs(1) - 1)
    def _():
        out_ref[...] = hres[...] + acc[...]


def _mix_mlp(h, groups, wos, g, w1, w2, gate=None, tm=512, tf=1024):
    t, d = h.shape
    ff = w1.shape[1]
    group_sizes = tuple(len(gr) for gr in groups)
    a_list = [a for gr in groups for a in gr]
    extra = [] if gate is None else list(gate)
    in_specs = [pl.BlockSpec((tm, d), lambda i, f: (i, 0))]
    in_specs += [pl.BlockSpec((tm, a.shape[1]), lambda i, f: (i, 0)) for a in a_list]
    in_specs += [pl.BlockSpec(w.shape, lambda i, f: (0, 0)) for w in wos]
    if gate is not None:
        in_specs += [pl.BlockSpec((tm, LANES), lambda i, f: (i, 0)),
                     pl.BlockSpec(gate[1].shape, lambda i, f: (0, 0, 0))]
    in_specs += [pl.BlockSpec((1, d), lambda i, f: (0, 0)),
                 pl.BlockSpec((d, tf), lambda i, f: (0, f)),
                 pl.BlockSpec((tf, d), lambda i, f: (f, 0))]
    return pl.pallas_call(
        functools.partial(_mix_mlp_kernel, group_sizes=group_sizes, gated=gate is not None),
        grid=(t // tm, ff // tf),
        in_specs=in_specs,
        out_specs=pl.BlockSpec((tm, d), lambda i, f: (i, 0)),
        out_shape=jax.ShapeDtypeStruct((t, d), F32),
        scratch_shapes=[pltpu.VMEM((tm, d), F32), pltpu.VMEM((tm, d), BF16),
                        pltpu.VMEM((tm, d), F32)],
        compiler_params=_cparams(("parallel", "arbitrary")),
        name="mix_mlp",
    )(h, *a_list, *wos, *extra, g.reshape(1, d), w1, w2)


def _lru_kernel(xr_ref, xg_ref, cw_ref, cb_ref, wa_ref, ba_ref, wx_ref, bx_ref, lam_ref,
                o_ref, tail_ref, hc_ref, *, tc):
    c = pl.program_id(1)

    @pl.when(c == 0)
    def _():
        tail_ref[...] = jnp.zeros_like(tail_ref)
        hc_ref[...] = jnp.zeros_like(hc_ref)

    x = xr_ref[0]
    width = x.shape[1]
    row = lax.broadcasted_iota(jnp.int32, (tc, width), 0)
    row8 = lax.broadcasted_iota(jnp.int32, (8, width), 0)
    tail = tail_ref[...]
    cw = cw_ref[...]
    y = x * cw[CONV_WIDTH - 1:CONV_WIDTH]
    for s in range(1, CONV_WIDTH):
        sh = pltpu.roll(x, s, 0)
        head = jnp.where(row8 < s, pltpu.roll(tail, s, 0), sh[0:8])
        sh = jnp.concatenate([head, sh[8:]], axis=0)
        y = y + sh * cw[CONV_WIDTH - 1 - s:CONV_WIDTH - s]
    y = y + cb_ref[...]
    tail_ref[...] = x[tc - 8:tc]

    yb = y.astype(BF16)
    r = _sigmoid(_dot(yb, wa_ref[...]) + ba_ref[...])
    i = _sigmoid(_dot(yb, wx_ref[...]) + bx_ref[...])
    nl = -lam_ref[...]
    softplus = jnp.maximum(nl, 0.0) + jnp.log(1.0 + jnp.exp(-jnp.abs(nl)))
    log_a = (-LRU_C) * r * softplus
    a = jnp.exp(log_a)
    mult = jnp.sqrt(1.0 - jnp.exp(2.0 * log_a))
    mult = jnp.where((row == 0) & (c == 0), 1.0, mult)
    b = mult * (i * y)

    d = 1
    while d < tc:
        keep = row >= d
        a_sh = jnp.where(keep, pltpu.roll(a, d, 0), 1.0)
        b_sh = jnp.where(keep, pltpu.roll(b, d, 0), 0.0)
        b = a * b_sh + b
        a = a * a_sh
        d *= 2
    h = b + a * hc_ref[...]
    hc_ref[...] = h[tc - 1:tc]
    o_ref[0] = (h * _gelu_tanh(xg_ref[0])).astype(o_ref.dtype)


def _block_diag(w):
    n, d, e = w.shape
    eye = jnp.eye(n, dtype=w.dtype)
    return (w[:, :, None, :] * eye[:, None, :, None]).reshape(n * d, n * e)


def _rg_lru(z_lru, conv_w, conv_b, wa, ba, wx, bx, lam, tc=256):
    bsz, s, _ = z_lru.shape
    c = LRU_WIDTH
    tc = min(tc, s)
    vec = lambda v: pl.BlockSpec(v, lambda b, i: (0, 0))
    return pl.pallas_call(
        functools.partial(_lru_kernel, tc=tc),
        grid=(bsz, s // tc),
        in_specs=[pl.BlockSpec((1, tc, c), lambda b, i: (b, i, 0)),
                  pl.BlockSpec((1, tc, c), lambda b, i: (b, i, 1)),
                  vec((CONV_WIDTH, c)), vec((1, c)), vec((c, c)), vec((1, c)),
                  vec((c, c)), vec((1, c)), vec((1, c))],
        out_specs=pl.BlockSpec((1, tc, c), lambda b, i: (b, i, 0)),
        out_shape=jax.ShapeDtypeStruct((bsz, s, c), BF16),
        scratch_shapes=[pltpu.VMEM((8, c), F32), pltpu.VMEM((1, c), F32)],
        compiler_params=_cparams(("parallel", "arbitrary")),
        name="rg_lru",
    )(z_lru, z_lru, conv_w, conv_b.reshape(1, c), _block_diag(wa).astype(BF16), ba.reshape(1, c),
      _block_diag(wx).astype(BF16), bx.reshape(1, c), lam.reshape(1, c))


def _rope(x, cos, sin_lo, sin_hi):
    return (x * cos + pltpu.roll(x, LANES - MLA_ROPE // 2, 1) * sin_lo
            + pltpu.roll(x, MLA_ROPE // 2, 1) * sin_hi)


def _mla_prep_kernel(ql_ref, kvl_ref, kr_ref, gq_ref, gkv_ref, wq_ref, wk_ref, wv_ref,
                     nq_ref, nk_ref, cos_ref, s1_ref, s2_ref, q_out, k_out, v_out):
    qn = _rms(ql_ref[...], gq_ref[...]).astype(BF16)
    kvn = _rms(kvl_ref[...], gkv_ref[...]).astype(BF16)
    kr = kr_ref[...]
    cos, s1, s2 = cos_ref[...], s1_ref[...], s2_ref[...]
    scale = MLA_QK ** -0.5
    for h in range(MLA_HEADS):
        sl = slice(h * LANES, (h + 1) * LANES)
        qh = _rms(_dot(qn, wq_ref[:, sl]), nq_ref[...], MLA_QK)
        q_out[0, h] = (_rope(qh, cos, s1, s2) * scale).astype(BF16)
        kh = _rms(_dot(kvn, wk_ref[:, sl]) + kr, nk_ref[...], MLA_QK)
        k_out[0, h] = _rope(kh, cos, s1, s2).astype(BF16)
        v_out[0, h] = _dot(kvn, wv_ref[:, sl]).astype(BF16)


def _rope_tables(s):
    half = MLA_ROPE // 2
    freqs = ROPE_THETA ** (-jnp.arange(half, dtype=F32) / half)
    ang = jnp.arange(s, dtype=F32)[:, None] * freqs[None, :]
    cos, sin = jnp.cos(ang), jnp.sin(ang)
    z = lambda n: jnp.zeros((s, n), F32)
    cos_t = jnp.concatenate([jnp.ones((s, MLA_NOPE), F32), cos, cos, z(LANES - MLA_QK)], axis=1)
    s_lo = jnp.concatenate([z(MLA_NOPE), -sin, z(LANES - MLA_NOPE - half)], axis=1)
    s_hi = jnp.concatenate([z(MLA_NOPE + half), sin, z(LANES - MLA_QK)], axis=1)
    return cos_t, s_lo, s_hi


def _pad_cols(w, n):
    return jnp.pad(w, ((0, 0),) * (w.ndim - 1) + ((0, n - w.shape[-1]),))


def _mla_prep(q_lat, kv_lat, k_rope, bsz, s, q_norm, kv_norm, w_uq, w_ukv, qn_q, qn_k, tm=512):
    tm = min(tm, s)
    nt = s // tm
    hl = MLA_HEADS * LANES
    wq = _pad_cols(w_uq.reshape(MLA_Q_RANK, MLA_HEADS, MLA_QK), LANES).reshape(MLA_Q_RANK, hl)
    wkv = w_ukv.reshape(MLA_KV_RANK, MLA_HEADS, MLA_NOPE + MLA_V)
    wk = _pad_cols(wkv[..., :MLA_NOPE], LANES).reshape(MLA_KV_RANK, hl)
    wv = wkv[..., MLA_NOPE:]
    zero = jnp.zeros_like(wv)
    even = (jnp.arange(MLA_HEADS) % 2 == 0)[None, :, None]
    wv = jnp.concatenate([jnp.where(even, wv, zero), jnp.where(even, zero, wv)], axis=-1)
    wv = wv.reshape(MLA_KV_RANK, hl)
    cos_t, s_lo, s_hi = _rope_tables(s)
    row = lambda n: pl.BlockSpec((tm, n), lambda b, i: (b * nt + i, 0))
    full = lambda a: pl.BlockSpec(a.shape, lambda b, i: (0, 0))
    tab = pl.BlockSpec((tm, LANES), lambda b, i: (i, 0))
    head_out = pl.BlockSpec((1, MLA_HEADS, tm, LANES), lambda b, i: (b, 0, i, 0))
    args = (q_lat, kv_lat, k_rope, q_norm.reshape(1, -1), kv_norm.reshape(1, -1),
            wq.astype(BF16), wk.astype(BF16), wv.astype(BF16),
            _pad_cols(qn_q.reshape(1, -1), LANES), _pad_cols(qn_k.reshape(1, -1), LANES),
            cos_t, s_lo, s_hi)
    in_specs = [row(MLA_Q_RANK), row(MLA_KV_RANK), row(LANES)] + [full(a) for a in args[3:10]] + [tab] * 3
    shp = jax.ShapeDtypeStruct((bsz, MLA_HEADS, s, LANES), BF16)
    return pl.pallas_call(
        _mla_prep_kernel,
        grid=(bsz, nt),
        in_specs=in_specs,
        out_specs=[head_out] * 3,
        out_shape=[shp] * 3,
        compiler_params=_cparams(("parallel", "parallel")),
        name="mla_prep",
    )(*args)


def _mla_attn_kernel(q_ref, k_ref, v_ref, o_ref, m_ref, l_ref, acc_ref, *, tq):
    qi = pl.program_id(2)
    n_rc = tq // MLA_TR
    m_ref[...] = jnp.full_like(m_ref, -jnp.inf)
    l_ref[...] = jnp.zeros_like(l_ref)
    acc_ref[...] = jnp.zeros_like(acc_ref)
    lane_lo = lax.broadcasted_iota(jnp.int32, (MLA_TR, LANES), 1) < HALF

    def update(kc, rc, row_offset):
        ks = pl.multiple_of(kc * MLA_TK, MLA_TK)
        rows = pl.ds(rc * MLA_TR, MLA_TR)
        alphas, pvs = [], []
        for j in range(2):
            s = _dot_t(q_ref[0, j, rows, :], k_ref[0, j, pl.ds(ks, MLA_TK), :])
            if row_offset is not None:
                qrow = row_offset + lax.broadcasted_iota(jnp.int32, s.shape, 0)
                s = jnp.where(lax.broadcasted_iota(jnp.int32, s.shape, 1) <= qrow, s, MASK_VALUE)
            m_new, l_new, alpha, p = _online_softmax(s, m_ref[j, rows, :], l_ref[j, rows, :])
            m_ref[j, rows, :] = m_new
            l_ref[j, rows, :] = l_new
            alphas.append(alpha)
            pvs.append(_dot(p.astype(BF16), v_ref[0, j, pl.ds(ks, MLA_TK), :]))
        acc_ref[rows, :] = acc_ref[rows, :] * jnp.where(lane_lo, alphas[0], alphas[1]) + pvs[0] + pvs[1]

    def full_chunk(kc, carry):
        for rc in range(n_rc):
            update(kc, rc, None)
        return carry

    n_full = (qi * tq) // MLA_TK
    lax.fori_loop(0, n_full, full_chunk, 0)
    for d in range(max(tq // MLA_TK, 1)):
        for rc in range(n_rc):
            off = rc * MLA_TR - d * MLA_TK
            if off + MLA_TR - 1 >= 0:
                update(n_full + d, rc, off)
    lane_all = lax.broadcasted_iota(jnp.int32, (tq, LANES), 1) < HALF
    o_ref[0] = (acc_ref[...] / jnp.where(lane_all, l_ref[0], l_ref[1])).astype(o_ref.dtype)


def _mla_attention(q, k, v, tq=256):
    bsz, nh, s, _ = q.shape
    tq = min(tq, s)
    assert tq % MLA_TK == 0 and tq % MLA_TR == 0
    kv_spec = pl.BlockSpec((1, 2, s, LANES), lambda b, hp, i: (b, hp, 0, 0))
    return pl.pallas_call(
        functools.partial(_mla_attn_kernel, tq=tq),
        grid=(bsz, nh // 2, s // tq),
        in_specs=[pl.BlockSpec((1, 2, tq, LANES), lambda b, hp, i: (b, hp, i, 0)), kv_spec, kv_spec],
        out_specs=pl.BlockSpec((1, tq, LANES), lambda b, hp, i: (b, i, hp)),
        out_shape=jax.ShapeDtypeStruct((bsz, s, nh * MLA_V), BF16),
        scratch_shapes=[pltpu.VMEM((2, tq, LANES), F32), pltpu.VMEM((2, tq, LANES), F32),
                        pltpu.VMEM((tq, LANES), F32)],
        compiler_params=_cparams(("parallel", "parallel", "parallel")),
        name="mla_attn",
    )(q, k, v)


def _t5_bucket_np(dist):
    n = np.maximum(dist, 0)
    max_exact = NUM_BUCKETS // 2
    nf = np.maximum(n, 1).astype(np.float32)
    large = max_exact + (np.log(nf / np.float32(max_exact)) / np.float32(math.log(MAX_DISTANCE / max_exact))
                         * np.float32(NUM_BUCKETS - max_exact)).astype(np.int32)
    large = np.minimum(large, NUM_BUCKETS - 1)
    return np.where(n < max_exact, n, large).astype(np.int32)


def _bias_tiles_kernel(tab_ref, idx_ref, o_ref):
    slot = pl.program_id(0) * NSA_HPG + pl.program_id(2)
    idx = idx_ref[0]
    acc = jnp.full(idx.shape, MASK_VALUE, F32)
    for b in range(NUM_BUCKETS):
        acc = jnp.where(idx == b, tab_ref[slot, b], acc)
    o_ref[0, 0, 0] = acc


def _bias_tiles(tab_slots, idx):
    n, r, c = idx.shape
    return pl.pallas_call(
        _bias_tiles_kernel,
        grid=(NSA_GROUPS, n, NSA_HPG),
        in_specs=[pl.BlockSpec(memory_space=pltpu.SMEM),
                  pl.BlockSpec((1, r, c), lambda g, i, p: (i, 0, 0))],
        out_specs=pl.BlockSpec((1, 1, 1, r, c), lambda g, i, p: (g, i, p, 0, 0)),
        out_shape=jax.ShapeDtypeStruct((NSA_GROUPS, n, NSA_HPG, r, c), F32),
        compiler_params=_cparams(("parallel", "parallel", "parallel")),
        name="bias_tiles",
    )(tab_slots, jnp.asarray(idx))


def _toeplitz_bucket_ids(window):
    i = np.arange(NSA_TQ)[:, None]
    j = np.arange(NSA_TK)[None, :]
    tiles = []
    d = 0
    while True:
        dist = d * NSA_TQ + i - j
        ids = _t5_bucket_np(dist)
        ok = dist >= 0
        if window is not None:
            ok &= dist < window
        ids = np.where(ok, ids, NUM_BUCKETS).astype(np.int32)
        tiles.append(ids)
        if window is None and (ids == NUM_BUCKETS - 1).all():
            break
        if window is not None and not ok.any():
            tiles.pop()
            break
        d += 1
    return np.stack(tiles)


def _split_halves(x):
    lo = lax.broadcasted_iota(jnp.int32, x.shape, 1) < HALF
    return jnp.where(lo, x, 0.0), pltpu.roll(jnp.where(lo, 0.0, x), HALF, 1)


def _value_layouts(x):
    lo = lax.broadcasted_iota(jnp.int32, x.shape, 1) < HALF
    g0e = jnp.where(lo, x, 0.0)
    g1o = jnp.where(lo, 0.0, x)
    return (g0e, pltpu.roll(g0e, HALF, 1)), (pltpu.roll(g1o, HALF, 1), g1o)


def _nsa_prep_kernel(zq_ref, ks_ref, vs_ref, kw_ref, vw_ref, gq_ref, gks_ref, gkw_ref,
                     q_out, ks_out, kw_out, vs_out, vw_out, *, tm):
    i = pl.program_id(1)
    scale = NSA_DK ** -0.5
    for j in range(NSA_HEADS // 2):
        e, o = _split_halves(zq_ref[:, j * LANES:(j + 1) * LANES])
        g = (2 * j) // NSA_HPG
        pe = (2 * j - NSA_HPG * g) // 2
        q_out[0, NSA_HPG * g + pe] = (_rms(e, gq_ref[...], NSA_DK) * scale).astype(BF16)
        q_out[0, NSA_HPG * g + NSA_HPG // 2 + pe] = (_rms(o, gq_ref[...], NSA_DK) * scale).astype(BF16)

    lane = lax.broadcasted_iota(jnp.int32, (tm, LANES), 1)
    pos = i * tm + lax.broadcasted_iota(jnp.int32, (tm, LANES), 0)
    block_tag = jnp.where(lane - HALF == pos // SLC_BLOCK, MASK_VALUE, 0.0)
    for g, x in enumerate(_split_halves(ks_ref[...])):
        ks_out[0, g] = (_rms(x, gks_ref[...], NSA_DK) + block_tag).astype(BF16)
    for g, x in enumerate(_split_halves(kw_ref[...])):
        kw_out[0, g] = _rms(x, gkw_ref[...], NSA_DK).astype(BF16)
    for ref, out in ((vs_ref, vs_out), (vw_ref, vw_out)):
        for g, (ve, vo) in enumerate(_value_layouts(ref[...])):
            out[0, g, 0] = ve.astype(BF16)
            out[0, g, 1] = vo.astype(BF16)


def _nsa_prep(z_q, z_kv, bsz, s, q_norm, k_norm, tm=512):
    tm = min(tm, s)
    nt = s // tm
    row = lambda n, c: pl.BlockSpec((tm, n), lambda b, i: (b * nt + i, c))
    gain = pl.BlockSpec((1, LANES), lambda b, i: (0, 0))
    kspec = pl.BlockSpec((1, NSA_GROUPS, tm, LANES), lambda b, i: (b, 0, i, 0))
    vspec = pl.BlockSpec((1, NSA_GROUPS, 2, tm, LANES), lambda b, i: (b, 0, 0, i, 0))
    kshape = jax.ShapeDtypeStruct((bsz, NSA_GROUPS, s, LANES), BF16)
    vshape = jax.ShapeDtypeStruct((bsz, NSA_GROUPS, 2, s, LANES), BF16)
    pad = lambda v: _pad_cols(v.reshape(1, -1), LANES)
    return pl.pallas_call(
        functools.partial(_nsa_prep_kernel, tm=tm),
        grid=(bsz, nt),
        in_specs=[row(NSA_HEADS * NSA_DK, 0), row(LANES, 0), row(LANES, 1), row(LANES, 2), row(LANES, 3),
                  gain, gain, gain],
        out_specs=[pl.BlockSpec((1, NSA_HEADS, tm, LANES), lambda b, i: (b, 0, i, 0)),
                   kspec, kspec, vspec, vspec],
        out_shape=[jax.ShapeDtypeStruct((bsz, NSA_HEADS, s, LANES), BF16), kshape, kshape, vshape, vshape],
        compiler_params=_cparams(("parallel", "parallel")),
        name="nsa_prep",
    )(z_q, z_kv, z_kv, z_kv, z_kv, pad(q_norm), pad(k_norm[1]), pad(k_norm[2]))


def _compress_kernel(xk_ref, xv_ref, pk_ref, pv_ref, w1k_ref, w1v_ref, w2k_ref, w2v_ref, gk_ref,
                     ck_out, cv_out):
    def mlp(x, p_ref, w1_ref, w2_ref):
        nh = x.shape[0]
        top = _dot((x + p_ref[0:1]).astype(BF16), w1_ref[0])
        bot = _dot((x + p_ref[1:2]).astype(BF16), w1_ref[1])
        hid = _gelu_tanh(top + pltpu.roll(bot, nh - 1, 0))
        return _dot(hid.astype(BF16), w2_ref[...])

    yk = mlp(xk_ref[0], pk_ref, w1k_ref, w2k_ref)
    for g, x in enumerate(_split_halves(yk)):
        ck_out[0, g] = _rms(x, gk_ref[...], NSA_DK).astype(BF16)
    yv = mlp(xv_ref[0], pv_ref, w1v_ref, w2v_ref)
    for g, (ve, vo) in enumerate(_value_layouts(yv)):
        cv_out[0, g, 0] = ve.astype(BF16)
        cv_out[0, g, 1] = vo.astype(BF16)


def _compress_weights(pos, w1, w2):
    half = CMP_BLOCK // 2
    g = NSA_GROUPS
    eye = jnp.eye(g, dtype=F32)
    w1 = w1.reshape(2, half, NSA_DK, CMP_HIDDEN)
    w1 = w1[:, :, None, :, None, :] * eye[None, None, :, None, :, None]
    w1 = w1.reshape(2, half * g * NSA_DK, g * CMP_HIDDEN).astype(BF16)
    p = jnp.broadcast_to(pos.reshape(2, half, 1, NSA_DK), (2, half, g, NSA_DK)).reshape(2, half * g * NSA_DK)
    w2 = (w2[None, :, None, :] * eye[:, None, :, None]).reshape(g * CMP_HIDDEN, g * NSA_DK).astype(BF16)
    return p, w1, w2


def _compress(z_kc, z_vc, bsz, s, pos_k, w1_k, w2_k, pos_v, w1_v, w2_v, k_norm0):
    nh = s // CMP_STRIDE
    feat = CMP_STRIDE * LANES
    pk, w1k, w2k = _compress_weights(pos_k, w1_k, w2_k)
    pv, w1v, w2v = _compress_weights(pos_v, w1_v, w2_v)
    full = lambda a: pl.BlockSpec(a.shape, lambda b: (0,) * a.ndim)
    xspec = pl.BlockSpec((1, nh, feat), lambda b: (b, 0, 0))
    gk = _pad_cols(k_norm0.reshape(1, -1), LANES)
    return pl.pallas_call(
        _compress_kernel,
        grid=(bsz,),
        in_specs=[xspec, xspec, full(pk), full(pv), full(w1k), full(w1v), full(w2k), full(w2v), full(gk)],
        out_specs=[pl.BlockSpec((1, NSA_GROUPS, nh, LANES), lambda b: (b, 0, 0, 0)),
                   pl.BlockSpec((1, NSA_GROUPS, 2, nh, LANES), lambda b: (b, 0, 0, 0, 0))],
        out_shape=[jax.ShapeDtypeStruct((bsz, NSA_GROUPS, nh, LANES), BF16),
                   jax.ShapeDtypeStruct((bsz, NSA_GROUPS, 2, nh, LANES), BF16)],
        compiler_params=_cparams(("parallel",)),
        name="nsa_compress",
    )(z_kc.reshape(bsz, nh, feat), z_vc.reshape(bsz, nh, feat), pk, pv, w1k, w1v, w2k, w2v, gk)


def _cmp_attn_kernel(q_ref, ck_ref, cv_ref, bias_ref, ov_ref, o_ref, q2_ref, *, tq, nsel):
    t = pl.program_id(1)
    half = NSA_HPG // 2
    nh = ck_ref.shape[2]
    qrow = t * tq + lax.broadcasted_iota(jnp.int32, (tq, nh), 0)
    row_ok = qrow >= CMP_BLOCK - 1
    ck = ck_ref[0, 0]
    outs = []
    psum = None
    for h in range(NSA_HPG):
        s = _dot_t(q_ref[0, h], ck) + bias_ref[0, 0, h]
        e = jnp.exp(s - jnp.max(s, axis=-1, keepdims=True))
        p = jnp.where(row_ok, e / jnp.sum(e, axis=-1, keepdims=True), 0.0)
        outs.append(_dot(p.astype(BF16), cv_ref[0, 0, h // half]))
        psum = p if psum is None else psum + p
    for j in range(half):
        o_ref[0, :, j * LANES:(j + 1) * LANES] = (outs[j] + outs[half + j]).astype(o_ref.dtype)

    hi = psum.astype(BF16)
    lo = (psum - hi.astype(F32)).astype(BF16)
    imp = _dot(hi, ov_ref[...]) + _dot(lo, ov_ref[...])

    lane = lax.broadcasted_iota(jnp.int32, (tq, LANES), 1)
    blk = lane - HALF
    qpos = t * tq + lax.broadcasted_iota(jnp.int32, (tq, LANES), 0)
    cur = qpos // SLC_BLOCK
    valid = blk * SLC_BLOCK <= qpos
    forced = (blk == 0) | (blk == cur) | (blk == cur - 1)
    score = jnp.where(valid & forced, FORCE_SCORE, jnp.where(valid, imp, -1.0))
    score = jnp.where((blk >= 0) & (blk < nsel), score, REMOVED_SCORE)
    sel = jnp.zeros((tq, LANES), jnp.bool_)
    for _ in range(min(SLC_TOPN, nsel)):
        best = jnp.max(score, axis=-1, keepdims=True)
        first = jnp.min(jnp.where(score == best, lane, LANES), axis=-1, keepdims=True)
        pick = lane == first
        sel = sel | pick
        score = jnp.where(pick, REMOVED_SCORE, score)
    unselected = jnp.where((blk >= 0) & jnp.logical_not(sel), 1.0, 0.0).astype(BF16)
    for h in range(NSA_HPG):
        q2_ref[0, h] = q_ref[0, h] + unselected


def _cmp_attention(q, ck, cv, bias, ov, s):
    bsz = q.shape[0]
    tq = NSA_TQ
    nqt = s // tq
    nh = ck.shape[2]
    nsel = s // SLC_BLOCK
    return pl.pallas_call(
        functools.partial(_cmp_attn_kernel, tq=tq, nsel=nsel),
        grid=(NSA_GROUPS, nqt, bsz),
        in_specs=[pl.BlockSpec((1, NSA_HPG, tq, LANES), lambda g, t, b: (b, g, t, 0)),
                  pl.BlockSpec((1, 1, nh, LANES), lambda g, t, b: (b, g, 0, 0)),
                  pl.BlockSpec((1, 1, 2, nh, LANES), lambda g, t, b: (b, g, 0, 0, 0)),
                  pl.BlockSpec((1, 1, NSA_HPG, tq, nh), lambda g, t, b: (g, t, 0, 0, 0)),
                  pl.BlockSpec((nh, LANES), lambda g, t, b: (0, 0))],
        out_specs=[pl.BlockSpec((1, tq, NSA_HPG * NSA_DK), lambda g, t, b: (b, t, g)),
                   pl.BlockSpec((1, NSA_HPG, tq, LANES), lambda g, t, b: (b, g, t, 0))],
        out_shape=[jax.ShapeDtypeStruct((bsz, s, NSA_HEADS * NSA_DK), BF16),
                   jax.ShapeDtypeStruct(q.shape, BF16)],
        compiler_params=_cparams(("parallel", "parallel", "parallel")),
        name="nsa_cmp_attn",
    )(q, ck, cv, bias, ov)


def _nsa_flash_kernel(q_ref, k_ref, v_ref, bias_ref, o_ref, m_ref, l_ref, acc_ref, *, tq, window_steps):
    t = pl.program_id(2)
    ratio = NSA_TK // tq
    n_off = bias_ref.shape[1]
    half = NSA_HPG // 2
    m_ref[...] = jnp.full_like(m_ref, -jnp.inf)
    l_ref[...] = jnp.zeros_like(l_ref)
    acc_ref[...] = jnp.zeros_like(acc_ref)
    last = t // ratio
    first = 0 if window_steps is None else jnp.maximum(last - (window_steps - 1), 0)

    def kv_body(kc, carry):
        off = jnp.minimum(t - ratio * kc, n_off - 1)
        ks = pl.multiple_of(kc * NSA_TK, NSA_TK)
        k = k_ref[0, 0, pl.ds(ks, NSA_TK), :]
        for h in range(NSA_HPG):
            s = _dot_t(q_ref[0, h], k) + bias_ref[0, off, h]
            m_new, l_new, alpha, p = _online_softmax(s, m_ref[h], l_ref[h])
            m_ref[h] = m_new
            l_ref[h] = l_new
            pv = _dot(p.astype(BF16), v_ref[0, 0, h // half, pl.ds(ks, NSA_TK), :])
            acc_ref[h] = acc_ref[h] * alpha + pv
        return carry

    lax.fori_loop(first, last + 1, kv_body, 0)
    for j in range(half):
        o = acc_ref[j] / l_ref[j] + acc_ref[half + j] / l_ref[half + j]
        o_ref[0, :, j * LANES:(j + 1) * LANES] = o.astype(o_ref.dtype)


def _nsa_flash(q, k, v, bias, s, window_steps=None):
    bsz = q.shape[0]
    tq = NSA_TQ
    nqt = s // tq
    n_off = bias.shape[1]
    return pl.pallas_call(
        functools.partial(_nsa_flash_kernel, tq=tq, window_steps=window_steps),
        grid=(NSA_GROUPS, bsz, nqt),
        in_specs=[pl.BlockSpec((1, NSA_HPG, tq, LANES), lambda g, b, t: (b, g, t, 0)),
                  pl.BlockSpec((1, 1, s, LANES), lambda g, b, t: (b, g, 0, 0)),
                  pl.BlockSpec((1, 1, 2, s, LANES), lambda g, b, t: (b, g, 0, 0, 0)),
                  pl.BlockSpec((1, n_off, NSA_HPG, tq, NSA_TK), lambda g, b, t: (g, 0, 0, 0, 0))],
        out_specs=pl.BlockSpec((1, tq, NSA_HPG * NSA_DK), lambda g, b, t: (b, t, g)),
        out_shape=jax.ShapeDtypeStruct((bsz, s, NSA_HEADS * NSA_DK), BF16),
        scratch_shapes=[pltpu.VMEM((NSA_HPG, tq, LANES), F32), pltpu.VMEM((NSA_HPG, tq, LANES), F32),
                        pltpu.VMEM((NSA_HPG, tq, LANES), F32)],
        compiler_params=_cparams(("parallel", "parallel", "parallel")),
        name="nsa_flash_sel" if window_steps is None else "nsa_flash_win",
    )(q, k, v, bias)


def _hybrid_layer(h, bsz, s, g_mix, g_mlp, w1, w2, w_in, conv_w, conv_b, wa, ba, wx, bx, lam,
                  q_norm, kv_norm, w_uq, w_ukv, qn_q, qn_k, w_out):
    d = h.shape[1]
    c = LRU_WIDTH
    o_kr = 2 * c + MLA_Q_RANK + MLA_KV_RANK
    w_kr = jnp.pad(w_in[:, o_kr:], ((0, 0), (MLA_NOPE, LANES - MLA_QK)))
    w_all = jnp.concatenate([w_in[:, :o_kr], w_kr], axis=1).astype(BF16)
    z_lru, q_lat, kv_lat, k_rope = _norm_proj(h, g_mix, w_all, (2 * c, MLA_Q_RANK, MLA_KV_RANK, LANES))
    lru = _rg_lru(z_lru.reshape(bsz, s, 2 * c), conv_w, conv_b, wa, ba, wx, bx, lam)
    q, k, v = _mla_prep(q_lat, kv_lat, k_rope, bsz, s, q_norm, kv_norm, w_uq, w_ukv, qn_q, qn_k)
    mla = _mla_attention(q, k, v)
    wo = w_out.astype(BF16)
    return _mix_mlp(h, [[lru.reshape(bsz * s, c)], [mla.reshape(bsz * s, -1)]], [wo[:c], wo[c:]],
                    g_mlp, w1.astype(BF16), w2.astype(BF16))


def _nsa_layer(h, bsz, s, g_mix, g_mlp, w1, w2, w_in, pos_k, w1_k, w2_k, pos_v, w1_v, w2_v,
               q_norm, k_norm, rel_bias, w_out):
    nq = NSA_HEADS * NSA_DK
    kvw = NSA_GROUPS * NSA_DK
    n_gate = 3 * NSA_HEADS
    w_gate = _pad_cols(w_in[:, nq + 6 * kvw:], LANES)
    w_all = jnp.concatenate([w_in[:, :nq + 6 * kvw], w_gate], axis=1).astype(BF16)
    z_q, z_kc, z_vc, z_kv, z_gate = _norm_proj(h, g_mix, w_all, (nq, kvw, kvw, 4 * kvw, LANES))

    q, ks, kw, vs, vw = _nsa_prep(z_q, z_kv, bsz, s, q_norm, k_norm)
    ck, cv = _compress(z_kc, z_vc, bsz, s, pos_k, w1_k, w2_k, pos_v, w1_v, w2_v, k_norm[0])

    tq = NSA_TQ
    nqt = s // tq
    order = np.array(HEAD_ORDER)
    expand = np.zeros((3, LANES, NSA_HEADS * NSA_DK), np.float32)
    for br in range(3):
        for hd in range(NSA_HEADS):
            expand[br, 3 * hd + br, hd * NSA_DK:(hd + 1) * NSA_DK] = 1.0

    slot_heads = (np.arange(NSA_GROUPS)[:, None] * NSA_HPG + order[None, :]).reshape(-1)
    tab_slots = rel_bias.T[slot_heads]
    nh = s // CMP_STRIDE
    nc = (s - CMP_BLOCK) // CMP_STRIDE + 1
    cend = np.arange(nh) * CMP_STRIDE + CMP_BLOCK - 1
    qpos = np.arange(s)
    cdist = qpos[:, None] - cend[None, :]
    cmp_ids = np.where((cdist >= 0) & (np.arange(nh)[None, :] < nc), _t5_bucket_np(cdist), NUM_BUCKETS)
    cmp_bias = _bias_tiles(tab_slots, cmp_ids.reshape(nqt, tq, nh).astype(np.int32))
    sel_bias = _bias_tiles(tab_slots, _toeplitz_bucket_ids(None))
    win_ids = _toeplitz_bucket_ids(WINDOW)
    win_bias = _bias_tiles(tab_slots, win_ids)

    nsel = s // SLC_BLOCK
    cstart = np.arange(nh) * CMP_STRIDE
    sstart = np.arange(nsel) * SLC_BLOCK
    ov = np.clip(np.minimum(cstart[:, None] + CMP_BLOCK, sstart[None, :] + SLC_BLOCK)
                 - np.maximum(cstart[:, None], sstart[None, :]), 0, None).astype(np.float32) / CMP_BLOCK
    ov[nc:] = 0.0
    ov_p = np.zeros((nh, LANES), np.float32)
    ov_p[:, HALF:HALF + nsel] = ov

    o_c, q_sel = _cmp_attention(q, ck, cv, cmp_bias, jnp.asarray(ov_p, BF16), s)
    o_s = _nsa_flash(q_sel, ks, vs, sel_bias, s)
    window_steps = (win_ids.shape[0] + 1) // (NSA_TK // tq)
    o_w = _nsa_flash(q, kw, vw, win_bias, s, window_steps=window_steps)

    flat = lambda a: a.reshape(bsz * s, -1)
    return _mix_mlp(h, [[flat(o_c), flat(o_s), flat(o_w)]], [w_out.astype(BF16)],
                    g_mlp, w1.astype(BF16), w2.astype(BF16), gate=(z_gate, jnp.asarray(expand, BF16)))


def kernel(x, rel_bias, norm_mix, norm_mlp, mlp_w1, mlp_w2, hy_w_in, lru_conv_w, lru_conv_b, lru_wa, lru_ba, lru_wx, lru_bx, lru_lambda, mla_q_norm, mla_kv_norm, mla_w_uq, mla_w_ukv, mla_qn_q, mla_qn_k, hy_w_out, nsa_w_in, nsa_cmp_pos_k, nsa_cmp_w1_k, nsa_cmp_w2_k, nsa_cmp_pos_v, nsa_cmp_w1_v, nsa_cmp_w2_v, nsa_q_norm, nsa_k_norm, nsa_w_out):
    bsz, s, d = x.shape
    depth = norm_mix.shape[0]
    h = x.reshape(bsz * s, d)
    for layer in range(depth):
        if layer % 2 == 0:
            e = layer // 2
            h = _hybrid_layer(h, bsz, s, norm_mix[layer], norm_mlp[layer], mlp_w1[layer], mlp_w2[layer],
                              hy_w_in[e], lru_conv_w[e], lru_conv_b[e], lru_wa[e], lru_ba[e], lru_wx[e],
                              lru_bx[e], lru_lambda[e], mla_q_norm[e], mla_kv_norm[e], mla_w_uq[e],
                              mla_w_ukv[e], mla_qn_q[e], mla_qn_k[e], hy_w_out[e])
        else:
            o = layer // 2
            h = _nsa_layer(h, bsz, s, norm_mix[layer], norm_mlp[layer], mlp_w1[layer], mlp_w2[layer],
                           nsa_w_in[o], nsa_cmp_pos_k[o], nsa_cmp_w1_k[o], nsa_cmp_w2_k[o],
                           nsa_cmp_pos_v[o], nsa_cmp_w1_v[o], nsa_cmp_w2_v[o], nsa_q_norm[o],
                           nsa_k_norm[o], rel_bias, nsa_w_out[o])
    return h.reshape(bsz, s, d)
```

```python
import functools
import math

import numpy as np
import jax
import jax.numpy as jnp
from jax import lax
from jax.experimental import pallas as pl
from jax.experimental.pallas import tpu as pltpu

F32 = jnp.float32
BF16 = jnp.bfloat16

NORM_EPS = 1e-6
MASK_VALUE = -1e30
FORCE_SCORE = 1e4
REMOVED_SCORE = -3e38

NUM_BUCKETS = 32
MAX_DISTANCE = 1024
MASKED_ID = NUM_BUCKETS
EXCLUDED_ID = NUM_BUCKETS + 1
LOG2E = math.log2(math.e)

LRU_WIDTH = 512
LRU_BLOCKS = 8
CONV_WIDTH = 4
LRU_C = 8.0

MLA_HEADS = 8
MLA_NOPE = 64
MLA_ROPE = 32
MLA_QK = MLA_NOPE + MLA_ROPE
MLA_V = 64
MLA_Q_RANK = 384
MLA_KV_RANK = 256
ROPE_THETA = 10000.0

NSA_HEADS = 16
NSA_GROUPS = 2
NSA_HPG = NSA_HEADS // NSA_GROUPS
NSA_DK = 64
CMP_BLOCK = 32
CMP_STRIDE = 16
CMP_HIDDEN = 256
SLC_BLOCK = 64
SLC_TOPN = 8
WINDOW = 512

LANES = 128
HALF = LANES // 2
NSA_TQ = 128
NSA_TK = 256
MLA_TR = 128
MLA_TK = 256
HEAD_ORDER = (0, 2, 4, 6, 1, 3, 5, 7)
VMEM_LIMIT = 56 * 1024 * 1024


def _cparams(sem):
    return pltpu.CompilerParams(dimension_semantics=sem, vmem_limit_bytes=VMEM_LIMIT)


def _rms(x, g, n=None):
    n = x.shape[-1] if n is None else n
    ms = jnp.sum(x * x, axis=-1, keepdims=True) * (1.0 / n)
    return x * lax.rsqrt(ms + NORM_EPS) * g


def _gelu_tanh(x):
    return 0.5 * x * (1.0 + jnp.tanh(math.sqrt(2.0 / math.pi) * (x + 0.044715 * (x * x * x))))


def _sigmoid(x):
    return 1.0 / (1.0 + jnp.exp(-x))


def _dot(a, b):
    return jnp.dot(a, b, preferred_element_type=F32)


def _dot_t(a, b):
    return lax.dot_general(a, b, (((1,), (1,)), ((), ())), preferred_element_type=F32)


def _flash_chunks(first, last, n_chains, scores, values, s_bufs, m_ref, acc_ref):
    stage_scores = scores

    def stage_update(kc, buf):
        for c in range(n_chains):
            s = buf[c]
            m_old = m_ref[c]
            m_new = jnp.maximum(m_old, jnp.max(s, axis=-1, keepdims=True))
            alpha = jnp.exp2(m_old - m_new)
            p = jnp.exp2(s - jnp.concatenate([m_new] * (s.shape[1] // LANES), axis=1))
            m_ref[c] = m_new
            acc_ref[c] = acc_ref[c] * alpha + _dot(p.astype(BF16), values(c, kc))

    stage_scores(first, s_bufs[0])

    def two_chunks(j, carry):
        kc = first + 2 * j
        stage_scores(kc + 1, s_bufs[1])
        stage_update(kc, s_bufs[0])
        stage_scores(kc + 2, s_bufs[0])
        stage_update(kc + 1, s_bufs[1])
        return carry

    lax.fori_loop(0, (last - first + 2) // 2, two_chunks, 0)


def _normalized_pair(acc_even, acc_odd):
    lo = lax.broadcasted_iota(jnp.int32, acc_even.shape, 1) < HALF
    return jnp.where(lo, acc_even / acc_even[:, LANES - 1:LANES], acc_odd / acc_odd[:, 0:1])


def _with_ones_column(v, odd):
    lane = lax.broadcasted_iota(jnp.int32, v.shape, 1)
    return jnp.where(lane == (0 if odd else LANES - 1), 1.0, v)


def _norm_proj_kernel(x_ref, g_ref, w_ref, *out_refs, splits):
    xn = _rms(x_ref[...], g_ref[...]).astype(BF16)
    off = 0
    for o_ref, n in zip(out_refs, splits):
        o_ref[...] = _dot(xn, w_ref[:, off:off + n]).astype(o_ref.dtype)
        off += n


def _norm_proj(x, g, w, splits, tm=512):
    t, d = x.shape
    n = w.shape[1]
    assert sum(splits) == n and t % tm == 0
    return pl.pallas_call(
        functools.partial(_norm_proj_kernel, splits=splits),
        grid=(t // tm,),
        in_specs=[pl.BlockSpec((tm, d), lambda i: (i, 0)),
                  pl.BlockSpec((1, d), lambda i: (0, 0)),
                  pl.BlockSpec((d, n), lambda i: (0, 0))],
        out_specs=[pl.BlockSpec((tm, s), lambda i: (i, 0)) for s in splits],
        out_shape=[jax.ShapeDtypeStruct((t, s), F32) for s in splits],
        compiler_params=_cparams(("parallel",)),
        name="norm_proj",
    )(x, g.reshape(1, d), w)


def _mix_mlp_kernel(*refs, group_sizes, gated):
    n_a = sum(group_sizes)
    n_g = len(group_sizes)
    h_ref = refs[0]
    a_refs = refs[1:1 + n_a]
    wo_refs = refs[1 + n_a:1 + n_a + n_g]
    rest = refs[1 + n_a + n_g:]
    if gated:
        zg_ref, e_ref = rest[:2]
        rest = rest[2:]
    g_ref, w1_ref, w2_ref, out_ref, hres, xn, acc = rest
    f = pl.program_id(1)

    @pl.when(f == 0)
    def _():
        h1 = h_ref[...]
        if gated:
            gate = _sigmoid(zg_ref[...])
            g_hi = gate.astype(BF16)
            g_lo = (gate - g_hi.astype(F32)).astype(BF16)
        k = 0
        for gi, gs in enumerate(group_sizes):
            a = None
            for j in range(gs):
                aj = a_refs[k + j][...].astype(F32)
                if gated:
                    aj = aj * (_dot(g_hi, e_ref[j]) + _dot(g_lo, e_ref[j]))
                a = aj if a is None else a + aj
            k += gs
            h1 = h1 + _dot(a.astype(BF16), wo_refs[gi][...])
        hres[...] = h1
        xn[...] = _rms(h1, g_ref[...]).astype(BF16)
        acc[...] = jnp.zeros_like(acc)

    u = jnp.maximum(_dot(xn[...], w1_ref[...]), 0.0)
    acc[...] += _dot((u * u).astype(BF16), w2_ref[...])

    @pl.when(f == pl.num_programs(1) - 1)
    def _():
        out_ref[...] = hres[...] + acc[...]


def _mix_mlp(h, groups, wos, g, w1, w2, gate=None, tm=512, tf=1024):
    t, d = h.shape
    ff = w1.shape[1]
    group_sizes = tuple(len(gr) for gr in groups)
    a_list = [a for gr in groups for a in gr]
    extra = [] if gate is None else list(gate)
    in_specs = [pl.BlockSpec((tm, d), lambda i, f: (i, 0))]
    in_specs += [pl.BlockSpec((tm, a.shape[1]), lambda i, f: (i, 0)) for a in a_list]
    in_specs += [pl.BlockSpec(w.shape, lambda i, f: (0, 0)) for w in wos]
    if gate is not None:
        in_specs += [pl.BlockSpec((tm, LANES), lambda i, f: (i, 0)),
                     pl.BlockSpec(gate[1].shape, lambda i, f: (0, 0, 0))]
    in_specs += [pl.BlockSpec((1, d), lambda i, f: (0, 0)),
                 pl.BlockSpec((d, tf), lambda i, f: (0, f)),
                 pl.BlockSpec((tf, d), lambda i, f: (f, 0))]
    return pl.pallas_call(
        functools.partial(_mix_mlp_kernel, group_sizes=group_sizes, gated=gate is not None),
        grid=(t // tm, ff // tf),
        in_specs=in_specs,
        out_specs=pl.BlockSpec((tm, d), lambda i, f: (i, 0)),
        out_shape=jax.ShapeDtypeStruct((t, d), F32),
        scratch_shapes=[pltpu.VMEM((tm, d), F32), pltpu.VMEM((tm, d), BF16),
                        pltpu.VMEM((tm, d), F32)],
        compiler_params=_cparams(("parallel", "arbitrary")),
        name="mix_mlp",
    )(h, *a_list, *wos, *extra, g.reshape(1, d), w1, w2)


def _lru_kernel(xr_ref, xg_ref, cw_ref, cb_ref, wa_ref, ba_ref, wx_ref, bx_ref, lam_ref,
                o_ref, tail_ref, hc_ref, *, tc):
    c = pl.program_id(1)

    @pl.when(c == 0)
    def _():
        tail_ref[...] = jnp.zeros_like(tail_ref)
        hc_ref[...] = jnp.zeros_like(hc_ref)

    x = xr_ref[0]
    width = x.shape[1]
    row = lax.broadcasted_iota(jnp.int32, (tc, width), 0)
    row8 = lax.broadcasted_iota(jnp.int32, (8, width), 0)
    tail = tail_ref[...]
    cw = cw_ref[...]
    y = x * cw[CONV_WIDTH - 1:CONV_WIDTH]
    for s in range(1, CONV_WIDTH):
        sh = pltpu.roll(x, s, 0)
        head = jnp.where(row8 < s, pltpu.roll(tail, s, 0), sh[0:8])
        sh = jnp.concatenate([head, sh[8:]], axis=0)
        y = y + sh * cw[CONV_WIDTH - 1 - s:CONV_WIDTH - s]
    y = y + cb_ref[...]
    tail_ref[...] = x[tc - 8:tc]

    yb = y.astype(BF16)
    r = _sigmoid(_dot(yb, wa_ref[...]) + ba_ref[...])
    i = _sigmoid(_dot(yb, wx_ref[...]) + bx_ref[...])
    nl = -lam_ref[...]
    softplus = jnp.maximum(nl, 0.0) + jnp.log(1.0 + jnp.exp(-jnp.abs(nl)))
    log_a = (-LRU_C) * r * softplus
    a = jnp.exp(log_a)
    mult = jnp.sqrt(1.0 - jnp.exp(2.0 * log_a))
    mult = jnp.where((row == 0) & (c == 0), 1.0, mult)
    b = mult * (i * y)

    d = 1
    while d < tc:
        keep = row >= d
        a_sh = jnp.where(keep, pltpu.roll(a, d, 0), 1.0)
        b_sh = jnp.where(keep, pltpu.roll(b, d, 0), 0.0)
        b = a * b_sh + b
        a = a * a_sh
        d *= 2
    h = b + a * hc_ref[...]
    hc_ref[...] = h[tc - 1:tc]
    o_ref[0] = (h * _gelu_tanh(xg_ref[0])).astype(o_ref.dtype)


def _block_diag(w):
    n, d, e = w.shape
    eye = jnp.eye(n, dtype=w.dtype)
    return (w[:, :, None, :] * eye[:, None, :, None]).reshape(n * d, n * e)


def _rg_lru(z_lru, conv_w, conv_b, wa, ba, wx, bx, lam, tc=256):
    bsz, s, _ = z_lru.shape
    c = LRU_WIDTH
    tc = min(tc, s)
    vec = lambda v: pl.BlockSpec(v, lambda b, i: (0, 0))
    return pl.pallas_call(
        functools.partial(_lru_kernel, tc=tc),
        grid=(bsz, s // tc),
        in_specs=[pl.BlockSpec((1, tc, c), lambda b, i: (b, i, 0)),
                  pl.BlockSpec((1, tc, c), lambda b, i: (b, i, 1)),
                  vec((CONV_WIDTH, c)), vec((1, c)), vec((c, c)), vec((1, c)),
                  vec((c, c)), vec((1, c)), vec((1, c))],
        out_specs=pl.BlockSpec((1, tc, c), lambda b, i: (b, i, 0)),
        out_shape=jax.ShapeDtypeStruct((bsz, s, c), BF16),
        scratch_shapes=[pltpu.VMEM((8, c), F32), pltpu.VMEM((1, c), F32)],
        compiler_params=_cparams(("parallel", "arbitrary")),
        name="rg_lru",
    )(z_lru, z_lru, conv_w, conv_b.reshape(1, c), _block_diag(wa).astype(BF16), ba.reshape(1, c),
      _block_diag(wx).astype(BF16), bx.reshape(1, c), lam.reshape(1, c))


def _rope(x, cos, sin_lo, sin_hi):
    return (x * cos + pltpu.roll(x, LANES - MLA_ROPE // 2, 1) * sin_lo
            + pltpu.roll(x, MLA_ROPE // 2, 1) * sin_hi)


def _mla_prep_kernel(ql_ref, kvl_ref, kr_ref, gq_ref, gkv_ref, wq_ref, wk_ref, wv_ref,
                     nq_ref, nk_ref, cos_ref, s1_ref, s2_ref, q_out, k_out, v_out):
    qn = _rms(ql_ref[...], gq_ref[...]).astype(BF16)
    kvn = _rms(kvl_ref[...], gkv_ref[...]).astype(BF16)
    kr = kr_ref[...]
    cos, s1, s2 = cos_ref[...], s1_ref[...], s2_ref[...]
    scale = MLA_QK ** -0.5 * LOG2E
    for h in range(MLA_HEADS):
        sl = slice(h * LANES, (h + 1) * LANES)
        qh = _rms(_dot(qn, wq_ref[:, sl]), nq_ref[...], MLA_QK)
        q_out[0, h] = (_rope(qh, cos, s1, s2) * scale).astype(BF16)
        kh = _rms(_dot(kvn, wk_ref[:, sl]) + kr, nk_ref[...], MLA_QK)
        k_out[0, h] = _rope(kh, cos, s1, s2).astype(BF16)
        v_out[0, h] = _with_ones_column(_dot(kvn, wv_ref[:, sl]), h % 2 == 1).astype(BF16)


def _rope_tables(s):
    half = MLA_ROPE // 2
    freqs = ROPE_THETA ** (-jnp.arange(half, dtype=F32) / half)
    ang = jnp.arange(s, dtype=F32)[:, None] * freqs[None, :]
    cos, sin = jnp.cos(ang), jnp.sin(ang)
    z = lambda n: jnp.zeros((s, n), F32)
    cos_t = jnp.concatenate([jnp.ones((s, MLA_NOPE), F32), cos, cos, z(LANES - MLA_QK)], axis=1)
    s_lo = jnp.concatenate([z(MLA_NOPE), -sin, z(LANES - MLA_NOPE - half)], axis=1)
    s_hi = jnp.concatenate([z(MLA_NOPE + half), sin, z(LANES - MLA_QK)], axis=1)
    return cos_t, s_lo, s_hi


def _pad_cols(w, n):
    return jnp.pad(w, ((0, 0),) * (w.ndim - 1) + ((0, n - w.shape[-1]),))


def _mla_prep(q_lat, kv_lat, k_rope, bsz, s, q_norm, kv_norm, w_uq, w_ukv, qn_q, qn_k, tm=512):
    tm = min(tm, s)
    nt = s // tm
    hl = MLA_HEADS * LANES
    wq = _pad_cols(w_uq.reshape(MLA_Q_RANK, MLA_HEADS, MLA_QK), LANES).reshape(MLA_Q_RANK, hl)
    wkv = w_ukv.reshape(MLA_KV_RANK, MLA_HEADS, MLA_NOPE + MLA_V)
    wk = _pad_cols(wkv[..., :MLA_NOPE], LANES).reshape(MLA_KV_RANK, hl)
    wv = wkv[..., MLA_NOPE:]
    zero = jnp.zeros_like(wv)
    even = (jnp.arange(MLA_HEADS) % 2 == 0)[None, :, None]
    wv = jnp.concatenate([jnp.where(even, wv, zero), jnp.where(even, zero, wv)], axis=-1)
    wv = wv.reshape(MLA_KV_RANK, hl)
    cos_t, s_lo, s_hi = _rope_tables(s)
    row = lambda n: pl.BlockSpec((tm, n), lambda b, i: (b * nt + i, 0))
    full = lambda a: pl.BlockSpec(a.shape, lambda b, i: (0, 0))
    tab = pl.BlockSpec((tm, LANES), lambda b, i: (i, 0))
    head_out = pl.BlockSpec((1, MLA_HEADS, tm, LANES), lambda b, i: (b, 0, i, 0))
    args = (q_lat, kv_lat, k_rope, q_norm.reshape(1, -1), kv_norm.reshape(1, -1),
            wq.astype(BF16), wk.astype(BF16), wv.astype(BF16),
            _pad_cols(qn_q.reshape(1, -1), LANES), _pad_cols(qn_k.reshape(1, -1), LANES),
            cos_t, s_lo, s_hi)
    in_specs = [row(MLA_Q_RANK), row(MLA_KV_RANK), row(LANES)] + [full(a) for a in args[3:10]] + [tab] * 3
    shp = jax.ShapeDtypeStruct((bsz, MLA_HEADS, s, LANES), BF16)
    return pl.pallas_call(
        _mla_prep_kernel,
        grid=(bsz, nt),
        in_specs=in_specs,
        out_specs=[head_out] * 3,
        out_shape=[shp] * 3,
        compiler_params=_cparams(("parallel", "parallel")),
        name="mla_prep",
    )(*args)


def _mla_attn_kernel(q_ref, k_ref, v_ref, mask_ref, o_ref, s0_ref, s1_ref, m_ref, acc_ref, *, tq):
    qi = pl.program_id(2)
    n_rc = tq // MLA_TR
    ratio = MLA_TK // MLA_TR
    n_tiles = mask_ref.shape[0]
    m_ref[...] = jnp.full_like(m_ref, MASK_VALUE)
    acc_ref[...] = jnp.zeros_like(acc_ref)
    last = (qi * tq + tq - 1) // MLA_TK

    def key_rows(kc):
        return pl.ds(pl.multiple_of(jnp.minimum(kc, last) * MLA_TK, MLA_TK), MLA_TK)

    def scores(kc, buf):
        for j in range(2):
            s = _dot_t(q_ref[0, j], k_ref[0, j, key_rows(kc), :])
            for rc in range(n_rc):
                offset = qi * n_rc + rc - ratio * kc
                tile = jnp.where(kc > last, 0, jnp.clip(offset + 1, 0, n_tiles - 1))
                buf[j * n_rc + rc] = s[rc * MLA_TR:(rc + 1) * MLA_TR] + mask_ref[tile]

    def values(c, kc):
        return v_ref[0, c // n_rc, key_rows(kc), :]

    _flash_chunks(0, last, 2 * n_rc, scores, values, (s0_ref, s1_ref), m_ref, acc_ref)
    for rc in range(n_rc):
        o = _normalized_pair(acc_ref[rc], acc_ref[n_rc + rc])
        o_ref[0, rc * MLA_TR:(rc + 1) * MLA_TR, :] = o.astype(o_ref.dtype)


def _mla_causal_tiles():
    i = np.arange(MLA_TR)[:, None]
    j = np.arange(MLA_TK)[None, :]
    tiles = [np.full((MLA_TR, MLA_TK), -np.inf, np.float32)]
    for d in range(MLA_TK // MLA_TR):
        tiles.append(np.where(j <= i + d * MLA_TR, 0.0, -np.inf).astype(np.float32))
    tiles.append(np.zeros((MLA_TR, MLA_TK), np.float32))
    return np.stack(tiles)


def _mla_attention(q, k, v, tq=512):
    bsz, nh, s, _ = q.shape
    tq = min(tq, s)
    assert tq % MLA_TK == 0 and tq % MLA_TR == 0
    n_chains = 2 * (tq // MLA_TR)
    mask = _mla_causal_tiles()
    kv_spec = pl.BlockSpec((1, 2, s, LANES), lambda b, hp, i: (b, hp, 0, 0))
    return pl.pallas_call(
        functools.partial(_mla_attn_kernel, tq=tq),
        grid=(bsz, nh // 2, s // tq),
        in_specs=[pl.BlockSpec((1, 2, tq, LANES), lambda b, hp, i: (b, hp, i, 0)), kv_spec, kv_spec,
                  pl.BlockSpec(mask.shape, lambda b, hp, i: (0, 0, 0))],
        out_specs=pl.BlockSpec((1, tq, LANES), lambda b, hp, i: (b, i, hp)),
        out_shape=jax.ShapeDtypeStruct((bsz, s, nh * MLA_V), BF16),
        scratch_shapes=[pltpu.VMEM((n_chains, MLA_TR, MLA_TK), F32), pltpu.VMEM((n_chains, MLA_TR, MLA_TK), F32),
                        pltpu.VMEM((n_chains, MLA_TR, LANES), F32), pltpu.VMEM((n_chains, MLA_TR, LANES), F32)],
        compiler_params=_cparams(("parallel", "parallel", "parallel")),
        name="mla_attn",
    )(q, k, v, jnp.asarray(mask))


def _t5_bucket_np(dist):
    n = np.maximum(dist, 0)
    max_exact = NUM_BUCKETS // 2
    nf = np.maximum(n, 1).astype(np.float32)
    large = max_exact + (np.log(nf / np.float32(max_exact)) / np.float32(math.log(MAX_DISTANCE / max_exact))
                         * np.float32(NUM_BUCKETS - max_exact)).astype(np.int32)
    large = np.minimum(large, NUM_BUCKETS - 1)
    return np.where(n < max_exact, n, large).astype(np.int32)


def _bias_tiles_kernel(tab_ref, idx_ref, o_ref):
    slot = pl.program_id(0) * NSA_HPG + pl.program_id(2)
    idx = idx_ref[0]
    acc = jnp.where(idx == EXCLUDED_ID, -jnp.inf, MASK_VALUE).astype(F32)
    for b in range(NUM_BUCKETS):
        acc = jnp.where(idx == b, tab_ref[slot, b] * LOG2E, acc)
    o_ref[0, 0, 0] = acc


def _bias_tiles(tab_slots, idx):
    n, r, c = idx.shape
    return pl.pallas_call(
        _bias_tiles_kernel,
        grid=(NSA_GROUPS, n, NSA_HPG),
        in_specs=[pl.BlockSpec(memory_space=pltpu.SMEM),
                  pl.BlockSpec((1, r, c), lambda g, i, p: (i, 0, 0))],
        out_specs=pl.BlockSpec((1, 1, 1, r, c), lambda g, i, p: (g, i, p, 0, 0)),
        out_shape=jax.ShapeDtypeStruct((NSA_GROUPS, n, NSA_HPG, r, c), F32),
        compiler_params=_cparams(("parallel", "parallel", "parallel")),
        name="bias_tiles",
    )(tab_slots, jnp.asarray(idx))


def _toeplitz_bucket_ids(window):
    i = np.arange(NSA_TQ)[:, None]
    j = np.arange(NSA_TK)[None, :]
    tiles = []
    d = 0
    while True:
        dist = d * NSA_TQ + i - j
        ids = _t5_bucket_np(dist)
        ok = dist >= 0
        if window is not None:
            ok &= dist < window
        ids = np.where(ok, ids, MASKED_ID).astype(np.int32)
        tiles.append(ids)
        if window is None and (ids == NUM_BUCKETS - 1).all():
            break
        if window is not None and not ok.any():
            tiles.pop()
            break
        d += 1
    n_real = len(tiles)
    tiles.append(np.full((NSA_TQ, NSA_TK), EXCLUDED_ID, np.int32))
    return np.stack(tiles), n_real


def _split_halves(x):
    lo = lax.broadcasted_iota(jnp.int32, x.shape, 1) < HALF
    return jnp.where(lo, x, 0.0), pltpu.roll(jnp.where(lo, 0.0, x), HALF, 1)


def _value_layouts(x):
    lo = lax.broadcasted_iota(jnp.int32, x.shape, 1) < HALF
    g0e = jnp.where(lo, x, 0.0)
    g1o = jnp.where(lo, 0.0, x)
    pair = lambda e, o: (_with_ones_column(e, False), _with_ones_column(o, True))
    return pair(g0e, pltpu.roll(g0e, HALF, 1)), pair(pltpu.roll(g1o, HALF, 1), g1o)


def _nsa_prep_kernel(zq_ref, ks_ref, vs_ref, kw_ref, vw_ref, gq_ref, gks_ref, gkw_ref,
                     q_out, ks_out, kw_out, vs_out, vw_out, *, tm):
    i = pl.program_id(1)
    scale = NSA_DK ** -0.5 * LOG2E
    for j in range(NSA_HEADS // 2):
        e, o = _split_halves(zq_ref[:, j * LANES:(j + 1) * LANES])
        g = (2 * j) // NSA_HPG
        pe = (2 * j - NSA_HPG * g) // 2
        q_out[0, NSA_HPG * g + pe] = (_rms(e, gq_ref[...], NSA_DK) * scale).astype(BF16)
        q_out[0, NSA_HPG * g + NSA_HPG // 2 + pe] = (_rms(o, gq_ref[...], NSA_DK) * scale).astype(BF16)

    lane = lax.broadcasted_iota(jnp.int32, (tm, LANES), 1)
    pos = i * tm + lax.broadcasted_iota(jnp.int32, (tm, LANES), 0)
    block_tag = jnp.where(lane - HALF == pos // SLC_BLOCK, MASK_VALUE, 0.0)
    for g, x in enumerate(_split_halves(ks_ref[...])):
        ks_out[0, g] = (_rms(x, gks_ref[...], NSA_DK) + block_tag).astype(BF16)
    for g, x in enumerate(_split_halves(kw_ref[...])):
        kw_out[0, g] = _rms(x, gkw_ref[...], NSA_DK).astype(BF16)
    for ref, out in ((vs_ref, vs_out), (vw_ref, vw_out)):
        for g, (ve, vo) in enumerate(_value_layouts(ref[...])):
            out[0, g, 0] = ve.astype(BF16)
            out[0, g, 1] = vo.astype(BF16)


def _nsa_prep(z_q, z_kv, bsz, s, q_norm, k_norm, tm=512):
    tm = min(tm, s)
    nt = s // tm
    row = lambda n, c: pl.BlockSpec((tm, n), lambda b, i: (b * nt + i, c))
    gain = pl.BlockSpec((1, LANES), lambda b, i: (0, 0))
    kspec = pl.BlockSpec((1, NSA_GROUPS, tm, LANES), lambda b, i: (b, 0, i, 0))
    vspec = pl.BlockSpec((1, NSA_GROUPS, 2, tm, LANES), lambda b, i: (b, 0, 0, i, 0))
    kshape = jax.ShapeDtypeStruct((bsz, NSA_GROUPS, s, LANES), BF16)
    vshape = jax.ShapeDtypeStruct((bsz, NSA_GROUPS, 2, s, LANES), BF16)
    pad = lambda v: _pad_cols(v.reshape(1, -1), LANES)
    return pl.pallas_call(
        functools.partial(_nsa_prep_kernel, tm=tm),
        grid=(bsz, nt),
        in_specs=[row(NSA_HEADS * NSA_DK, 0), row(LANES, 0), row(LANES, 1), row(LANES, 2), row(LANES, 3),
                  gain, gain, gain],
        out_specs=[pl.BlockSpec((1, NSA_HEADS, tm, LANES), lambda b, i: (b, 0, i, 0)),
                   kspec, kspec, vspec, vspec],
        out_shape=[jax.ShapeDtypeStruct((bsz, NSA_HEADS, s, LANES), BF16), kshape, kshape, vshape, vshape],
        compiler_params=_cparams(("parallel", "parallel")),
        name="nsa_prep",
    )(z_q, z_kv, z_kv, z_kv, z_kv, pad(q_norm), pad(k_norm[1]), pad(k_norm[2]))


def _compress_kernel(xk_ref, xv_ref, pk_ref, pv_ref, w1k_ref, w1v_ref, w2k_ref, w2v_ref, gk_ref,
                     ck_out, cv_out):
    def mlp(x, p_ref, w1_ref, w2_ref):
        nh = x.shape[0]
        top = _dot((x + p_ref[0:1]).astype(BF16), w1_ref[0])
        bot = _dot((x + p_ref[1:2]).astype(BF16), w1_ref[1])
        hid = _gelu_tanh(top + pltpu.roll(bot, nh - 1, 0))
        return _dot(hid.astype(BF16), w2_ref[...])

    yk = mlp(xk_ref[0], pk_ref, w1k_ref, w2k_ref)
    for g, x in enumerate(_split_halves(yk)):
        ck_out[0, g] = _rms(x, gk_ref[...], NSA_DK).astype(BF16)
    yv = mlp(xv_ref[0], pv_ref, w1v_ref, w2v_ref)
    for g, (ve, vo) in enumerate(_value_layouts(yv)):
        cv_out[0, g, 0] = ve.astype(BF16)
        cv_out[0, g, 1] = vo.astype(BF16)


def _compress_weights(pos, w1, w2):
    half = CMP_BLOCK // 2
    g = NSA_GROUPS
    eye = jnp.eye(g, dtype=F32)
    w1 = w1.reshape(2, half, NSA_DK, CMP_HIDDEN)
    w1 = w1[:, :, None, :, None, :] * eye[None, None, :, None, :, None]
    w1 = w1.reshape(2, half * g * NSA_DK, g * CMP_HIDDEN).astype(BF16)
    p = jnp.broadcast_to(pos.reshape(2, half, 1, NSA_DK), (2, half, g, NSA_DK)).reshape(2, half * g * NSA_DK)
    w2 = (w2[None, :, None, :] * eye[:, None, :, None]).reshape(g * CMP_HIDDEN, g * NSA_DK).astype(BF16)
    return p, w1, w2


def _compress(z_kc, z_vc, bsz, s, pos_k, w1_k, w2_k, pos_v, w1_v, w2_v, k_norm0):
    nh = s // CMP_STRIDE
    feat = CMP_STRIDE * LANES
    pk, w1k, w2k = _compress_weights(pos_k, w1_k, w2_k)
    pv, w1v, w2v = _compress_weights(pos_v, w1_v, w2_v)
    full = lambda a: pl.BlockSpec(a.shape, lambda b: (0,) * a.ndim)
    xspec = pl.BlockSpec((1, nh, feat), lambda b: (b, 0, 0))
    gk = _pad_cols(k_norm0.reshape(1, -1), LANES)
    return pl.pallas_call(
        _compress_kernel,
        grid=(bsz,),
        in_specs=[xspec, xspec, full(pk), full(pv), full(w1k), full(w1v), full(w2k), full(w2v), full(gk)],
        out_specs=[pl.BlockSpec((1, NSA_GROUPS, nh, LANES), lambda b: (b, 0, 0, 0)),
                   pl.BlockSpec((1, NSA_GROUPS, 2, nh, LANES), lambda b: (b, 0, 0, 0, 0))],
        out_shape=[jax.ShapeDtypeStruct((bsz, NSA_GROUPS, nh, LANES), BF16),
                   jax.ShapeDtypeStruct((bsz, NSA_GROUPS, 2, nh, LANES), BF16)],
        compiler_params=_cparams(("parallel",)),
        name="nsa_compress",
    )(z_kc.reshape(bsz, nh, feat), z_vc.reshape(bsz, nh, feat), pk, pv, w1k, w1v, w2k, w2v, gk)


def _cmp_attn_kernel(q_ref, ck_ref, cv_ref, bias_ref, ov_ref, o_ref, q2_ref, *, tq, nsel):
    t = pl.program_id(1)
    half = NSA_HPG // 2
    nh = ck_ref.shape[2]
    qrow = t * tq + lax.broadcasted_iota(jnp.int32, (tq, nh), 0)
    row_ok = qrow >= CMP_BLOCK - 1
    ck = ck_ref[0, 0]
    outs = []
    psum = None
    for h in range(NSA_HPG):
        s = _dot_t(q_ref[0, h], ck) + bias_ref[0, 0, h]
        e = jnp.exp2(s - jnp.max(s, axis=-1, keepdims=True))
        p = jnp.where(row_ok, e / jnp.sum(e, axis=-1, keepdims=True), 0.0)
        outs.append(_dot(p.astype(BF16), cv_ref[0, 0, h // half]))
        psum = p if psum is None else psum + p
    lane_lo = lax.broadcasted_iota(jnp.int32, (tq, LANES), 1) < HALF
    for j in range(half):
        o = jnp.where(lane_lo, outs[j], outs[half + j])
        o_ref[0, :, j * LANES:(j + 1) * LANES] = o.astype(o_ref.dtype)

    hi = psum.astype(BF16)
    lo = (psum - hi.astype(F32)).astype(BF16)
    imp = _dot(hi, ov_ref[...]) + _dot(lo, ov_ref[...])

    lane = lax.broadcasted_iota(jnp.int32, (tq, LANES), 1)
    blk = lane - HALF
    qpos = t * tq + lax.broadcasted_iota(jnp.int32, (tq, LANES), 0)
    cur = qpos // SLC_BLOCK
    valid = blk * SLC_BLOCK <= qpos
    forced = (blk == 0) | (blk == cur) | (blk == cur - 1)
    score = jnp.where(valid & forced, FORCE_SCORE, jnp.where(valid, imp, -1.0))
    score = jnp.where((blk >= 0) & (blk < nsel), score, REMOVED_SCORE)
    sel = jnp.zeros((tq, LANES), jnp.bool_)
    for _ in range(min(SLC_TOPN, nsel)):
        best = jnp.max(score, axis=-1, keepdims=True)
        first = jnp.min(jnp.where(score == best, lane, LANES), axis=-1, keepdims=True)
        pick = lane == first
        sel = sel | pick
        score = jnp.where(pick, REMOVED_SCORE, score)
    unselected = jnp.where((blk >= 0) & jnp.logical_not(sel), 1.0, 0.0).astype(BF16)
    for h in range(NSA_HPG):
        q2_ref[0, h] = q_ref[0, h] + unselected


def _cmp_attention(q, ck, cv, bias, ov, s):
    bsz = q.shape[0]
    tq = NSA_TQ
    nqt = s // tq
    nh = ck.shape[2]
    nsel = s // SLC_BLOCK
    return pl.pallas_call(
        functools.partial(_cmp_attn_kernel, tq=tq, nsel=nsel),
        grid=(NSA_GROUPS, nqt, bsz),
        in_specs=[pl.BlockSpec((1, NSA_HPG, tq, LANES), lambda g, t, b: (b, g, t, 0)),
                  pl.BlockSpec((1, 1, nh, LANES), lambda g, t, b: (b, g, 0, 0)),
                  pl.BlockSpec((1, 1, 2, nh, LANES), lambda g, t, b: (b, g, 0, 0, 0)),
                  pl.BlockSpec((1, 1, NSA_HPG, tq, nh), lambda g, t, b: (g, t, 0, 0, 0)),
                  pl.BlockSpec((nh, LANES), lambda g, t, b: (0, 0))],
        out_specs=[pl.BlockSpec((1, tq, NSA_HPG * NSA_DK), lambda g, t, b: (b, t, g)),
                   pl.BlockSpec((1, NSA_HPG, tq, LANES), lambda g, t, b: (b, g, t, 0))],
        out_shape=[jax.ShapeDtypeStruct((bsz, s, NSA_HEADS * NSA_DK), BF16),
                   jax.ShapeDtypeStruct(q.shape, BF16)],
        compiler_params=_cparams(("parallel", "parallel", "parallel")),
        name="nsa_cmp_attn",
    )(q, ck, cv, bias, ov)


def _nsa_flash_kernel(q_ref, k_ref, v_ref, bias_ref, o_ref, s0_ref, s1_ref, m_ref, acc_ref,
                      *, tq, window_steps):
    t = pl.program_id(2)
    ratio = NSA_TK // tq
    n_tiles = bias_ref.shape[1]
    half = NSA_HPG // 2
    m_ref[...] = jnp.full_like(m_ref, MASK_VALUE)
    acc_ref[...] = jnp.zeros_like(acc_ref)
    last = t // ratio
    first = 0 if window_steps is None else jnp.maximum(last - (window_steps - 1), 0)

    def key_rows(kc):
        return pl.ds(pl.multiple_of(jnp.minimum(kc, last) * NSA_TK, NSA_TK), NSA_TK)

    def scores(kc, buf):
        tile = jnp.where(kc > last, n_tiles - 1, jnp.minimum(t - ratio * kc, n_tiles - 2))
        s = _dot_t(q_ref[0].reshape(NSA_HPG * tq, LANES), k_ref[0, 0, key_rows(kc), :])
        buf[...] = s.reshape(NSA_HPG, tq, NSA_TK) + bias_ref[0, tile]

    def values(h, kc):
        return v_ref[0, 0, h // half, key_rows(kc), :]

    _flash_chunks(first, last, NSA_HPG, scores, values, (s0_ref, s1_ref), m_ref, acc_ref)
    for j in range(half):
        o = _normalized_pair(acc_ref[j], acc_ref[half + j])
        o_ref[0, :, j * LANES:(j + 1) * LANES] = o.astype(o_ref.dtype)


def _nsa_flash(q, k, v, bias, s, window_steps=None):
    bsz = q.shape[0]
    tq = NSA_TQ
    nqt = s // tq
    n_off = bias.shape[1]
    return pl.pallas_call(
        functools.partial(_nsa_flash_kernel, tq=tq, window_steps=window_steps),
        grid=(NSA_GROUPS, bsz, nqt),
        in_specs=[pl.BlockSpec((1, NSA_HPG, tq, LANES), lambda g, b, t: (b, g, t, 0)),
                  pl.BlockSpec((1, 1, s, LANES), lambda g, b, t: (b, g, 0, 0)),
                  pl.BlockSpec((1, 1, 2, s, LANES), lambda g, b, t: (b, g, 0, 0, 0)),
                  pl.BlockSpec((1, n_off, NSA_HPG, tq, NSA_TK), lambda g, b, t: (g, 0, 0, 0, 0))],
        out_specs=pl.BlockSpec((1, tq, NSA_HPG * NSA_DK), lambda g, b, t: (b, t, g)),
        out_shape=jax.ShapeDtypeStruct((bsz, s, NSA_HEADS * NSA_DK), BF16),
        scratch_shapes=[pltpu.VMEM((NSA_HPG, tq, NSA_TK), F32), pltpu.VMEM((NSA_HPG, tq, NSA_TK), F32),
                        pltpu.VMEM((NSA_HPG, tq, LANES), F32), pltpu.VMEM((NSA_HPG, tq, LANES), F32)],
        compiler_params=_cparams(("parallel", "parallel", "parallel")),
        name="nsa_flash_sel" if window_steps is None else "nsa_flash_win",
    )(q, k, v, bias)


def _hybrid_layer(h, bsz, s, g_mix, g_mlp, w1, w2, w_in, conv_w, conv_b, wa, ba, wx, bx, lam,
                  q_norm, kv_norm, w_uq, w_ukv, qn_q, qn_k, w_out):
    d = h.shape[1]
    c = LRU_WIDTH
    o_kr = 2 * c + MLA_Q_RANK + MLA_KV_RANK
    w_kr = jnp.pad(w_in[:, o_kr:], ((0, 0), (MLA_NOPE, LANES - MLA_QK)))
    w_all = jnp.concatenate([w_in[:, :o_kr], w_kr], axis=1).astype(BF16)
    z_lru, q_lat, kv_lat, k_rope = _norm_proj(h, g_mix, w_all, (2 * c, MLA_Q_RANK, MLA_KV_RANK, LANES))
    lru = _rg_lru(z_lru.reshape(bsz, s, 2 * c), conv_w, conv_b, wa, ba, wx, bx, lam)
    q, k, v = _mla_prep(q_lat, kv_lat, k_rope, bsz, s, q_norm, kv_norm, w_uq, w_ukv, qn_q, qn_k)
    mla = _mla_attention(q, k, v)
    wo = w_out.astype(BF16)
    return _mix_mlp(h, [[lru.reshape(bsz * s, c)], [mla.reshape(bsz * s, -1)]], [wo[:c], wo[c:]],
                    g_mlp, w1.astype(BF16), w2.astype(BF16))


def _nsa_layer(h, bsz, s, g_mix, g_mlp, w1, w2, w_in, pos_k, w1_k, w2_k, pos_v, w1_v, w2_v,
               q_norm, k_norm, rel_bias, w_out):
    nq = NSA_HEADS * NSA_DK
    kvw = NSA_GROUPS * NSA_DK
    n_gate = 3 * NSA_HEADS
    w_gate = _pad_cols(w_in[:, nq + 6 * kvw:], LANES)
    w_all = jnp.concatenate([w_in[:, :nq + 6 * kvw], w_gate], axis=1).astype(BF16)
    z_q, z_kc, z_vc, z_kv, z_gate = _norm_proj(h, g_mix, w_all, (nq, kvw, kvw, 4 * kvw, LANES))

    q, ks, kw, vs, vw = _nsa_prep(z_q, z_kv, bsz, s, q_norm, k_norm)
    ck, cv = _compress(z_kc, z_vc, bsz, s, pos_k, w1_k, w2_k, pos_v, w1_v, w2_v, k_norm[0])

    tq = NSA_TQ
    nqt = s // tq
    order = np.array(HEAD_ORDER)
    expand = np.zeros((3, LANES, NSA_HEADS * NSA_DK), np.float32)
    for br in range(3):
        for hd in range(NSA_HEADS):
            expand[br, 3 * hd + br, hd * NSA_DK:(hd + 1) * NSA_DK] = 1.0

    slot_heads = (np.arange(NSA_GROUPS)[:, None] * NSA_HPG + order[None, :]).reshape(-1)
    tab_slots = rel_bias.T[slot_heads]
    nh = s // CMP_STRIDE
    nc = (s - CMP_BLOCK) // CMP_STRIDE + 1
    cend = np.arange(nh) * CMP_STRIDE + CMP_BLOCK - 1
    qpos = np.arange(s)
    cdist = qpos[:, None] - cend[None, :]
    cmp_ids = np.where((cdist >= 0) & (np.arange(nh)[None, :] < nc), _t5_bucket_np(cdist), MASKED_ID)
    cmp_bias = _bias_tiles(tab_slots, cmp_ids.reshape(nqt, tq, nh).astype(np.int32))
    sel_bias = _bias_tiles(tab_slots, _toeplitz_bucket_ids(None)[0])
    win_ids, n_win_tiles = _toeplitz_bucket_ids(WINDOW)
    win_bias = _bias_tiles(tab_slots, win_ids)

    nsel = s // SLC_BLOCK
    cstart = np.arange(nh) * CMP_STRIDE
    sstart = np.arange(nsel) * SLC_BLOCK
    ov = np.clip(np.minimum(cstart[:, None] + CMP_BLOCK, sstart[None, :] + SLC_BLOCK)
                 - np.maximum(cstart[:, None], sstart[None, :]), 0, None).astype(np.float32) / CMP_BLOCK
    ov[nc:] = 0.0
    ov_p = np.zeros((nh, LANES), np.float32)
    ov_p[:, HALF:HALF + nsel] = ov

    o_c, q_sel = _cmp_attention(q, ck, cv, cmp_bias, jnp.asarray(ov_p, BF16), s)
    o_s = _nsa_flash(q_sel, ks, vs, sel_bias, s)
    window_steps = (n_win_tiles + 1) // (NSA_TK // tq)
    o_w = _nsa_flash(q, kw, vw, win_bias, s, window_steps=window_steps)

    flat = lambda a: a.reshape(bsz * s, -1)
    return _mix_mlp(h, [[flat(o_c), flat(o_s), flat(o_w)]], [w_out.astype(BF16)],
                    g_mlp, w1.astype(BF16), w2.astype(BF16), gate=(z_gate, jnp.asarray(expand, BF16)))


def kernel(x, rel_bias, norm_mix, norm_mlp, mlp_w1, mlp_w2, hy_w_in, lru_conv_w, lru_conv_b, lru_wa, lru_ba, lru_wx, lru_bx, lru_lambda, mla_q_norm, mla_kv_norm, mla_w_uq, mla_w_ukv, mla_qn_q, mla_qn_k, hy_w_out, nsa_w_in, nsa_cmp_pos_k, nsa_cmp_w1_k, nsa_cmp_w2_k, nsa_cmp_pos_v, nsa_cmp_w1_v, nsa_cmp_w2_v, nsa_q_norm, nsa_k_norm, nsa_w_out):
    bsz, s, d = x.shape
    depth = norm_mix.shape[0]
    h = x.reshape(bsz * s, d)
    for layer in range(depth):
        if layer % 2 == 0:
            e = layer // 2
            h = _hybrid_layer(h, bsz, s, norm_mix[layer], norm_mlp[layer], mlp_w1[layer], mlp_w2[layer],
                              hy_w_in[e], lru_conv_w[e], lru_conv_b[e], lru_wa[e], lru_ba[e], lru_wx[e],
                              lru_bx[e], lru_lambda[e], mla_q_norm[e], mla_kv_norm[e], mla_w_uq[e],
                              mla_w_ukv[e], mla_qn_q[e], mla_qn_k[e], hy_w_out[e])
        else:
            o = layer // 2
            h = _nsa_layer(h, bsz, s, norm_mix[layer], norm_mlp[layer], mlp_w1[layer], mlp_w2[layer],
                           nsa_w_in[o], nsa_cmp_pos_k[o], nsa_cmp_w1_k[o], nsa_cmp_w2_k[o],
                           nsa_cmp_pos_v[o], nsa_cmp_w1_v[o], nsa_cmp_w2_v[o], nsa_q_norm[o],
                           nsa_k_norm[o], rel_bias, nsa_w_out[o])
    return h.reshape(bsz, s, d)
```

```python
import functools
import math

import numpy as np
import jax
import jax.numpy as jnp
from jax import lax
from jax.experimental import pallas as pl
from jax.experimental.pallas import tpu as pltpu

F32 = jnp.float32
BF16 = jnp.bfloat16

NORM_EPS = 1e-6
MASK_VALUE = -1e30
FORCE_SCORE = 1e4
REMOVED_SCORE = -3e38

NUM_BUCKETS = 32
MAX_DISTANCE = 1024
MASKED_ID = NUM_BUCKETS
EXCLUDED_ID = NUM_BUCKETS + 1
LOG2E = math.log2(math.e)

LRU_WIDTH = 512
LRU_BLOCKS = 8
CONV_WIDTH = 4
LRU_C = 8.0

MLA_HEADS = 8
MLA_NOPE = 64
MLA_ROPE = 32
MLA_QK = MLA_NOPE + MLA_ROPE
MLA_V = 64
MLA_Q_RANK = 384
MLA_KV_RANK = 256
ROPE_THETA = 10000.0

NSA_HEADS = 16
NSA_GROUPS = 2
NSA_HPG = NSA_HEADS // NSA_GROUPS
NSA_DK = 64
CMP_BLOCK = 32
CMP_STRIDE = 16
CMP_HIDDEN = 256
SLC_BLOCK = 64
SLC_TOPN = 8
WINDOW = 512

LANES = 128
HALF = LANES // 2
NSA_TQ = 128
NSA_TK = 256
MLA_TR = 128
MLA_TK = 256
HEAD_ORDER = (0, 2, 4, 6, 1, 3, 5, 7)
VMEM_LIMIT = 56 * 1024 * 1024


def _cparams(sem):
    return pltpu.CompilerParams(dimension_semantics=sem, vmem_limit_bytes=VMEM_LIMIT)


def _rms(x, g, n=None):
    n = x.shape[-1] if n is None else n
    ms = jnp.sum(x * x, axis=-1, keepdims=True) * (1.0 / n)
    return x * lax.rsqrt(ms + NORM_EPS) * g


def _gelu_tanh(x):
    return 0.5 * x * (1.0 + jnp.tanh(math.sqrt(2.0 / math.pi) * (x + 0.044715 * (x * x * x))))


def _sigmoid(x):
    return 1.0 / (1.0 + jnp.exp(-x))


def _dot(a, b):
    return jnp.dot(a, b, preferred_element_type=F32)


def _dot_t(a, b):
    return lax.dot_general(a, b, (((1,), (1,)), ((), ())), preferred_element_type=F32)


def _flash_chunks(first, last, n_chains, scores, values, s_bufs, m_ref, acc_ref):
    stage_scores = scores

    def stage_update(kc, buf):
        for c in range(n_chains):
            s = buf[c]
            m_old = m_ref[c]
            m_new = jnp.maximum(m_old, jnp.max(s, axis=-1, keepdims=True))
            alpha = jnp.exp2(m_old - m_new)
            p = jnp.exp2(s - jnp.concatenate([m_new] * (s.shape[1] // LANES), axis=1))
            m_ref[c] = m_new
            acc_ref[c] = acc_ref[c] * alpha + _dot(p.astype(BF16), values(c, kc))

    stage_scores(first, s_bufs[0])

    def two_chunks(j, carry):
        kc = first + 2 * j
        stage_scores(kc + 1, s_bufs[1])
        stage_update(kc, s_bufs[0])
        stage_scores(kc + 2, s_bufs[0])
        stage_update(kc + 1, s_bufs[1])
        return carry

    lax.fori_loop(0, (last - first + 2) // 2, two_chunks, 0)


def _normalized_pair(acc_even, acc_odd):
    lo = lax.broadcasted_iota(jnp.int32, acc_even.shape, 1) < HALF
    return jnp.where(lo, acc_even / acc_even[:, LANES - 1:LANES], acc_odd / acc_odd[:, 0:1])


def _with_ones_column(v, odd):
    lane = lax.broadcasted_iota(jnp.int32, v.shape, 1)
    return jnp.where(lane == (0 if odd else LANES - 1), 1.0, v)


def _norm_proj_kernel(x_ref, g_ref, w_ref, *out_refs, splits):
    xn = _rms(x_ref[...], g_ref[...]).astype(BF16)
    off = 0
    for o_ref, n in zip(out_refs, splits):
        o_ref[...] = _dot(xn, w_ref[:, off:off + n]).astype(o_ref.dtype)
        off += n


def _norm_proj(x, g, w, splits, tm=512):
    t, d = x.shape
    n = w.shape[1]
    assert sum(splits) == n and t % tm == 0
    return pl.pallas_call(
        functools.partial(_norm_proj_kernel, splits=splits),
        grid=(t // tm,),
        in_specs=[pl.BlockSpec((tm, d), lambda i: (i, 0)),
                  pl.BlockSpec((1, d), lambda i: (0, 0)),
                  pl.BlockSpec((d, n), lambda i: (0, 0))],
        out_specs=[pl.BlockSpec((tm, s), lambda i: (i, 0)) for s in splits],
        out_shape=[jax.ShapeDtypeStruct((t, s), F32) for s in splits],
        compiler_params=_cparams(("parallel",)),
        name="norm_proj",
    )(x, g.reshape(1, d), w)


def _mix_mlp_kernel(*refs, group_sizes, gated):
    n_a = sum(group_sizes)
    n_g = len(group_sizes)
    h_ref = refs[0]
    a_refs = refs[1:1 + n_a]
    wo_refs = refs[1 + n_a:1 + n_a + n_g]
    rest = refs[1 + n_a + n_g:]
    if gated:
        zg_ref, e_ref = rest[:2]
        rest = rest[2:]
    g_ref, w1_ref, w2_ref, out_ref, hres, xn, acc = rest
    f = pl.program_id(1)

    @pl.when(f == 0)
    def _():
        h1 = h_ref[...]
        if gated:
            gate = _sigmoid(zg_ref[...])
            g_hi = gate.astype(BF16)
            g_lo = (gate - g_hi.astype(F32)).astype(BF16)
        k = 0
        for gi, gs in enumerate(group_sizes):
            a = None
            for j in range(gs):
                aj = a_refs[k + j][...].astype(F32)
                if gated:
                    aj = aj * (_dot(g_hi, e_ref[j]) + _dot(g_lo, e_ref[j]))
                a = aj if a is None else a + aj
            k += gs
            h1 = h1 + _dot(a.astype(BF16), wo_refs[gi][...])
        hres[...] = h1
        xn[...] = _rms(h1, g_ref[...]).astype(BF16)
        acc[...] = jnp.zeros_like(acc)

    u = jnp.maximum(_dot(xn[...], w1_ref[...]), 0.0)
    acc[...] += _dot((u * u).astype(BF16), w2_ref[...])

    @pl.when(f == pl.num_programs(1) - 1)
    def _():
        out_ref[...] = hres[...] + acc[...]


def _mix_mlp(h, groups, wos, g, w1, w2, gate=None, tm=512, tf=1024):
    t, d = h.shape
    ff = w1.shape[1]
    group_sizes = tuple(len(gr) for gr in groups)
    a_list = [a for gr in groups for a in gr]
    extra = [] if gate is None else list(gate)
    in_specs = [pl.BlockSpec((tm, d), lambda i, f: (i, 0))]
    in_specs += [pl.BlockSpec((tm, a.shape[1]), lambda i, f: (i, 0)) for a in a_list]
    in_specs += [pl.BlockSpec(w.shape, lambda i, f: (0, 0)) for w in wos]
    if gate is not None:
        in_specs += [pl.BlockSpec((tm, LANES), lambda i, f: (i, 0)),
                     pl.BlockSpec(gate[1].shape, lambda i, f: (0, 0, 0))]
    in_specs += [pl.BlockSpec((1, d), lambda i, f: (0, 0)),
                 pl.BlockSpec((d, tf), lambda i, f: (0, f)),
                 pl.BlockSpec((tf, d), lambda i, f: (f, 0))]
    return pl.pallas_call(
        functools.partial(_mix_mlp_kernel, group_sizes=group_sizes, gated=gate is not None),
        grid=(t // tm, ff // tf),
        in_specs=in_specs,
        out_specs=pl.BlockSpec((tm, d), lambda i, f: (i, 0)),
        out_shape=jax.ShapeDtypeStruct((t, d), F32),
        scratch_shapes=[pltpu.VMEM((tm, d), F32), pltpu.VMEM((tm, d), BF16),
                        pltpu.VMEM((tm, d), F32)],
        compiler_params=_cparams(("parallel", "arbitrary")),
        name="mix_mlp",
    )(h, *a_list, *wos, *extra, g.reshape(1, d), w1, w2)


def _lru_kernel(xr_ref, xg_ref, cw_ref, cb_ref, wa_ref, ba_ref, wx_ref, bx_ref, lam_ref,
                o_ref, tail_ref, hc_ref, *, tc):
    c = pl.program_id(1)

    @pl.when(c == 0)
    def _():
        tail_ref[...] = jnp.zeros_like(tail_ref)
        hc_ref[...] = jnp.zeros_like(hc_ref)

    x = xr_ref[0]
    width = x.shape[1]
    row = lax.broadcasted_iota(jnp.int32, (tc, width), 0)
    row8 = lax.broadcasted_iota(jnp.int32, (8, width), 0)
    tail = tail_ref[...]
    cw = cw_ref[...]
    y = x * cw[CONV_WIDTH - 1:CONV_WIDTH]
    for s in range(1, CONV_WIDTH):
        sh = pltpu.roll(x, s, 0)
        head = jnp.where(row8 < s, pltpu.roll(tail, s, 0), sh[0:8])
        sh = jnp.concatenate([head, sh[8:]], axis=0)
        y = y + sh * cw[CONV_WIDTH - 1 - s:CONV_WIDTH - s]
    y = y + cb_ref[...]
    tail_ref[...] = x[tc - 8:tc]

    yb = y.astype(BF16)
    r = _sigmoid(_dot(yb, wa_ref[...]) + ba_ref[...])
    i = _sigmoid(_dot(yb, wx_ref[...]) + bx_ref[...])
    nl = -lam_ref[...]
    softplus = jnp.maximum(nl, 0.0) + jnp.log(1.0 + jnp.exp(-jnp.abs(nl)))
    log_a = (-LRU_C) * r * softplus
    a = jnp.exp(log_a)
    mult = jnp.sqrt(1.0 - jnp.exp(2.0 * log_a))
    mult = jnp.where((row == 0) & (c == 0), 1.0, mult)
    b = mult * (i * y)

    d = 1
    while d < tc:
        keep = row >= d
        a_sh = jnp.where(keep, pltpu.roll(a, d, 0), 1.0)
        b_sh = jnp.where(keep, pltpu.roll(b, d, 0), 0.0)
        b = a * b_sh + b
        a = a * a_sh
        d *= 2
    h = b + a * hc_ref[...]
    hc_ref[...] = h[tc - 1:tc]
    o_ref[0] = (h * _gelu_tanh(xg_ref[0])).astype(o_ref.dtype)


def _block_diag(w):
    n, d, e = w.shape
    eye = jnp.eye(n, dtype=w.dtype)
    return (w[:, :, None, :] * eye[:, None, :, None]).reshape(n * d, n * e)


def _rg_lru(z_lru, conv_w, conv_b, wa, ba, wx, bx, lam, tc=256):
    bsz, s, _ = z_lru.shape
    c = LRU_WIDTH
    tc = min(tc, s)
    vec = lambda v: pl.BlockSpec(v, lambda b, i: (0, 0))
    return pl.pallas_call(
        functools.partial(_lru_kernel, tc=tc),
        grid=(bsz, s // tc),
        in_specs=[pl.BlockSpec((1, tc, c), lambda b, i: (b, i, 0)),
                  pl.BlockSpec((1, tc, c), lambda b, i: (b, i, 1)),
                  vec((CONV_WIDTH, c)), vec((1, c)), vec((c, c)), vec((1, c)),
                  vec((c, c)), vec((1, c)), vec((1, c))],
        out_specs=pl.BlockSpec((1, tc, c), lambda b, i: (b, i, 0)),
        out_shape=jax.ShapeDtypeStruct((bsz, s, c), BF16),
        scratch_shapes=[pltpu.VMEM((8, c), F32), pltpu.VMEM((1, c), F32)],
        compiler_params=_cparams(("parallel", "arbitrary")),
        name="rg_lru",
    )(z_lru, z_lru, conv_w, conv_b.reshape(1, c), _block_diag(wa).astype(BF16), ba.reshape(1, c),
      _block_diag(wx).astype(BF16), bx.reshape(1, c), lam.reshape(1, c))


def _rope(x, cos, sin_lo, sin_hi):
    return (x * cos + pltpu.roll(x, LANES - MLA_ROPE // 2, 1) * sin_lo
            + pltpu.roll(x, MLA_ROPE // 2, 1) * sin_hi)


def _mla_prep_kernel(ql_ref, kvl_ref, kr_ref, gq_ref, gkv_ref, wq_ref, wk_ref, wv_ref,
                     nq_ref, nk_ref, cos_ref, s1_ref, s2_ref, q_out, k_out, v_out):
    qn = _rms(ql_ref[...], gq_ref[...]).astype(BF16)
    kvn = _rms(kvl_ref[...], gkv_ref[...]).astype(BF16)
    kr = kr_ref[...]
    cos, s1, s2 = cos_ref[...], s1_ref[...], s2_ref[...]
    scale = MLA_QK ** -0.5 * LOG2E
    for h in range(MLA_HEADS):
        sl = slice(h * LANES, (h + 1) * LANES)
        qh = _rms(_dot(qn, wq_ref[:, sl]), nq_ref[...], MLA_QK)
        q_out[0, h] = (_rope(qh, cos, s1, s2) * scale).astype(BF16)
        kh = _rms(_dot(kvn, wk_ref[:, sl]) + kr, nk_ref[...], MLA_QK)
        k_out[0, h] = _rope(kh, cos, s1, s2).astype(BF16)
        v_out[0, h] = _with_ones_column(_dot(kvn, wv_ref[:, sl]), h % 2 == 1).astype(BF16)


def _rope_tables(s):
    half = MLA_ROPE // 2
    freqs = ROPE_THETA ** (-jnp.arange(half, dtype=F32) / half)
    ang = jnp.arange(s, dtype=F32)[:, None] * freqs[None, :]
    cos, sin = jnp.cos(ang), jnp.sin(ang)
    z = lambda n: jnp.zeros((s, n), F32)
    cos_t = jnp.concatenate([jnp.ones((s, MLA_NOPE), F32), cos, cos, z(LANES - MLA_QK)], axis=1)
    s_lo = jnp.concatenate([z(MLA_NOPE), -sin, z(LANES - MLA_NOPE - half)], axis=1)
    s_hi = jnp.concatenate([z(MLA_NOPE + half), sin, z(LANES - MLA_QK)], axis=1)
    return cos_t, s_lo, s_hi


def _pad_cols(w, n):
    return jnp.pad(w, ((0, 0),) * (w.ndim - 1) + ((0, n - w.shape[-1]),))


def _mla_prep(q_lat, kv_lat, k_rope, bsz, s, q_norm, kv_norm, w_uq, w_ukv, qn_q, qn_k, tm=512):
    tm = min(tm, s)
    nt = s // tm
    hl = MLA_HEADS * LANES
    wq = _pad_cols(w_uq.reshape(MLA_Q_RANK, MLA_HEADS, MLA_QK), LANES).reshape(MLA_Q_RANK, hl)
    wkv = w_ukv.reshape(MLA_KV_RANK, MLA_HEADS, MLA_NOPE + MLA_V)
    wk = _pad_cols(wkv[..., :MLA_NOPE], LANES).reshape(MLA_KV_RANK, hl)
    wv = wkv[..., MLA_NOPE:]
    zero = jnp.zeros_like(wv)
    even = (jnp.arange(MLA_HEADS) % 2 == 0)[None, :, None]
    wv = jnp.concatenate([jnp.where(even, wv, zero), jnp.where(even, zero, wv)], axis=-1)
    wv = wv.reshape(MLA_KV_RANK, hl)
    cos_t, s_lo, s_hi = _rope_tables(s)
    row = lambda n: pl.BlockSpec((tm, n), lambda b, i: (b * nt + i, 0))
    full = lambda a: pl.BlockSpec(a.shape, lambda b, i: (0, 0))
    tab = pl.BlockSpec((tm, LANES), lambda b, i: (i, 0))
    head_out = pl.BlockSpec((1, MLA_HEADS, tm, LANES), lambda b, i: (b, 0, i, 0))
    args = (q_lat, kv_lat, k_rope, q_norm.reshape(1, -1), kv_norm.reshape(1, -1),
            wq.astype(BF16), wk.astype(BF16), wv.astype(BF16),
            _pad_cols(qn_q.reshape(1, -1), LANES), _pad_cols(qn_k.reshape(1, -1), LANES),
            cos_t, s_lo, s_hi)
    in_specs = [row(MLA_Q_RANK), row(MLA_KV_RANK), row(LANES)] + [full(a) for a in args[3:10]] + [tab] * 3
    shp = jax.ShapeDtypeStruct((bsz, MLA_HEADS, s, LANES), BF16)
    return pl.pallas_call(
        _mla_prep_kernel,
        grid=(bsz, nt),
        in_specs=in_specs,
        out_specs=[head_out] * 3,
        out_shape=[shp] * 3,
        compiler_params=_cparams(("parallel", "parallel")),
        name="mla_prep",
    )(*args)


def _mla_attn_kernel(q_ref, k_ref, v_ref, mask_ref, o_ref, s0_ref, s1_ref, m_ref, acc_ref, *, tq):
    qi = pl.program_id(2)
    n_rc = tq // MLA_TR
    ratio = MLA_TK // MLA_TR
    n_tiles = mask_ref.shape[0]
    m_ref[...] = jnp.full_like(m_ref, MASK_VALUE)
    acc_ref[...] = jnp.zeros_like(acc_ref)
    last = (qi * tq + tq - 1) // MLA_TK

    def key_rows(kc):
        return pl.ds(pl.multiple_of(jnp.minimum(kc, last) * MLA_TK, MLA_TK), MLA_TK)

    def scores(kc, buf):
        for j in range(2):
            s = _dot_t(q_ref[0, j], k_ref[0, j, key_rows(kc), :])
            for rc in range(n_rc):
                offset = qi * n_rc + rc - ratio * kc
                tile = jnp.where(kc > last, 0, jnp.clip(offset + 1, 0, n_tiles - 1))
                buf[j * n_rc + rc] = s[rc * MLA_TR:(rc + 1) * MLA_TR] + mask_ref[tile]

    def values(c, kc):
        return v_ref[0, c // n_rc, key_rows(kc), :]

    _flash_chunks(0, last, 2 * n_rc, scores, values, (s0_ref, s1_ref), m_ref, acc_ref)
    for rc in range(n_rc):
        o = _normalized_pair(acc_ref[rc], acc_ref[n_rc + rc])
        o_ref[0, rc * MLA_TR:(rc + 1) * MLA_TR, :] = o.astype(o_ref.dtype)


def _mla_causal_tiles():
    i = np.arange(MLA_TR)[:, None]
    j = np.arange(MLA_TK)[None, :]
    tiles = [np.full((MLA_TR, MLA_TK), -np.inf, np.float32)]
    for d in range(MLA_TK // MLA_TR):
        tiles.append(np.where(j <= i + d * MLA_TR, 0.0, -np.inf).astype(np.float32))
    tiles.append(np.zeros((MLA_TR, MLA_TK), np.float32))
    return np.stack(tiles)


def _mla_attention(q, k, v, tq=512):
    bsz, nh, s, _ = q.shape
    tq = min(tq, s)
    assert tq % MLA_TK == 0 and tq % MLA_TR == 0
    n_chains = 2 * (tq // MLA_TR)
    mask = _mla_causal_tiles()
    kv_spec = pl.BlockSpec((1, 2, s, LANES), lambda b, hp, i: (b, hp, 0, 0))
    return pl.pallas_call(
        functools.partial(_mla_attn_kernel, tq=tq),
        grid=(bsz, nh // 2, s // tq),
        in_specs=[pl.BlockSpec((1, 2, tq, LANES), lambda b, hp, i: (b, hp, i, 0)), kv_spec, kv_spec,
                  pl.BlockSpec(mask.shape, lambda b, hp, i: (0, 0, 0))],
        out_specs=pl.BlockSpec((1, tq, LANES), lambda b, hp, i: (b, i, hp)),
        out_shape=jax.ShapeDtypeStruct((bsz, s, nh * MLA_V), BF16),
        scratch_shapes=[pltpu.VMEM((n_chains, MLA_TR, MLA_TK), F32), pltpu.VMEM((n_chains, MLA_TR, MLA_TK), F32),
                        pltpu.VMEM((n_chains, MLA_TR, LANES), F32), pltpu.VMEM((n_chains, MLA_TR, LANES), F32)],
        compiler_params=_cparams(("parallel", "parallel", "parallel")),
        name="mla_attn",
    )(q, k, v, jnp.asarray(mask))


def _t5_bucket_np(dist):
    n = np.maximum(dist, 0)
    max_exact = NUM_BUCKETS // 2
    nf = np.maximum(n, 1).astype(np.float32)
    large = max_exact + (np.log(nf / np.float32(max_exact)) / np.float32(math.log(MAX_DISTANCE / max_exact))
                         * np.float32(NUM_BUCKETS - max_exact)).astype(np.int32)
    large = np.minimum(large, NUM_BUCKETS - 1)
    return np.where(n < max_exact, n, large).astype(np.int32)


def _bias_tiles_kernel(tab_ref, idx_ref, o_ref):
    slot = pl.program_id(0) * NSA_HPG + pl.program_id(2)
    idx = idx_ref[0]
    acc = jnp.where(idx == EXCLUDED_ID, -jnp.inf, MASK_VALUE).astype(F32)
    for b in range(NUM_BUCKETS):
        acc = jnp.where(idx == b, tab_ref[slot, b] * LOG2E, acc)
    o_ref[0, 0, 0] = acc


def _bias_tiles(tab_slots, idx):
    n, r, c = idx.shape
    return pl.pallas_call(
        _bias_tiles_kernel,
        grid=(NSA_GROUPS, n, NSA_HPG),
        in_specs=[pl.BlockSpec(memory_space=pltpu.SMEM),
                  pl.BlockSpec((1, r, c), lambda g, i, p: (i, 0, 0))],
        out_specs=pl.BlockSpec((1, 1, 1, r, c), lambda g, i, p: (g, i, p, 0, 0)),
        out_shape=jax.ShapeDtypeStruct((NSA_GROUPS, n, NSA_HPG, r, c), F32),
        compiler_params=_cparams(("parallel", "parallel", "parallel")),
        name="bias_tiles",
    )(tab_slots, jnp.asarray(idx))


def _cmp_bias_ids(s):
    nqt = s // NSA_TQ
    nh = s // CMP_STRIDE
    per_tile = NSA_TQ // CMP_STRIDE
    i = np.arange(NSA_TQ)[:, None]
    c = np.arange(nh)[None, :]
    dist = i - (CMP_BLOCK - 1) - CMP_STRIDE * (c - per_tile * (nqt - 1))
    ids = np.where(dist >= 0, _t5_bucket_np(dist), MASKED_ID).astype(np.int32)
    return ids[None]


def _toeplitz_bucket_ids(window):
    i = np.arange(NSA_TQ)[:, None]
    j = np.arange(NSA_TK)[None, :]
    tiles = []
    d = 0
    while True:
        dist = d * NSA_TQ + i - j
        ids = _t5_bucket_np(dist)
        ok = dist >= 0
        if window is not None:
            ok &= dist < window
        ids = np.where(ok, ids, MASKED_ID).astype(np.int32)
        tiles.append(ids)
        if window is None and (ids == NUM_BUCKETS - 1).all():
            break
        if window is not None and not ok.any():
            tiles.pop()
            break
        d += 1
    n_real = len(tiles)
    tiles.append(np.full((NSA_TQ, NSA_TK), EXCLUDED_ID, np.int32))
    return np.stack(tiles), n_real


def _split_halves(x):
    lo = lax.broadcasted_iota(jnp.int32, x.shape, 1) < HALF
    return jnp.where(lo, x, 0.0), pltpu.roll(jnp.where(lo, 0.0, x), HALF, 1)


def _value_layouts(x):
    lo = lax.broadcasted_iota(jnp.int32, x.shape, 1) < HALF
    g0e = jnp.where(lo, x, 0.0)
    g1o = jnp.where(lo, 0.0, x)
    pair = lambda e, o: (_with_ones_column(e, False), _with_ones_column(o, True))
    return pair(g0e, pltpu.roll(g0e, HALF, 1)), pair(pltpu.roll(g1o, HALF, 1), g1o)


def _nsa_prep_kernel(zq_ref, ks_ref, vs_ref, kw_ref, vw_ref, gq_ref, gks_ref, gkw_ref,
                     q_out, ks_out, kw_out, vs_out, vw_out, *, tm):
    i = pl.program_id(1)
    scale = NSA_DK ** -0.5 * LOG2E
    for j in range(NSA_HEADS // 2):
        e, o = _split_halves(zq_ref[:, j * LANES:(j + 1) * LANES])
        g = (2 * j) // NSA_HPG
        pe = (2 * j - NSA_HPG * g) // 2
        q_out[0, NSA_HPG * g + pe] = (_rms(e, gq_ref[...], NSA_DK) * scale).astype(BF16)
        q_out[0, NSA_HPG * g + NSA_HPG // 2 + pe] = (_rms(o, gq_ref[...], NSA_DK) * scale).astype(BF16)

    lane = lax.broadcasted_iota(jnp.int32, (tm, LANES), 1)
    pos = i * tm + lax.broadcasted_iota(jnp.int32, (tm, LANES), 0)
    block_tag = jnp.where(lane - HALF == pos // SLC_BLOCK, MASK_VALUE, 0.0)
    for g, x in enumerate(_split_halves(ks_ref[...])):
        ks_out[0, g] = (_rms(x, gks_ref[...], NSA_DK) + block_tag).astype(BF16)
    for g, x in enumerate(_split_halves(kw_ref[...])):
        kw_out[0, g] = _rms(x, gkw_ref[...], NSA_DK).astype(BF16)
    for ref, out in ((vs_ref, vs_out), (vw_ref, vw_out)):
        for g, (ve, vo) in enumerate(_value_layouts(ref[...])):
            out[0, g, 0] = ve.astype(BF16)
            out[0, g, 1] = vo.astype(BF16)


def _nsa_prep(z_q, z_kv, bsz, s, q_norm, k_norm, tm=512):
    tm = min(tm, s)
    nt = s // tm
    row = lambda n, c: pl.BlockSpec((tm, n), lambda b, i: (b * nt + i, c))
    gain = pl.BlockSpec((1, LANES), lambda b, i: (0, 0))
    kspec = pl.BlockSpec((1, NSA_GROUPS, tm, LANES), lambda b, i: (b, 0, i, 0))
    vspec = pl.BlockSpec((1, NSA_GROUPS, 2, tm, LANES), lambda b, i: (b, 0, 0, i, 0))
    kshape = jax.ShapeDtypeStruct((bsz, NSA_GROUPS, s, LANES), BF16)
    vshape = jax.ShapeDtypeStruct((bsz, NSA_GROUPS, 2, s, LANES), BF16)
    pad = lambda v: _pad_cols(v.reshape(1, -1), LANES)
    return pl.pallas_call(
        functools.partial(_nsa_prep_kernel, tm=tm),
        grid=(bsz, nt),
        in_specs=[row(NSA_HEADS * NSA_DK, 0), row(LANES, 0), row(LANES, 1), row(LANES, 2), row(LANES, 3),
                  gain, gain, gain],
        out_specs=[pl.BlockSpec((1, NSA_HEADS, tm, LANES), lambda b, i: (b, 0, i, 0)),
                   kspec, kspec, vspec, vspec],
        out_shape=[jax.ShapeDtypeStruct((bsz, NSA_HEADS, s, LANES), BF16), kshape, kshape, vshape, vshape],
        compiler_params=_cparams(("parallel", "parallel")),
        name="nsa_prep",
    )(z_q, z_kv, z_kv, z_kv, z_kv, pad(q_norm), pad(k_norm[1]), pad(k_norm[2]))


def _compress_kernel(xk_ref, xv_ref, pk_ref, pv_ref, w1k_ref, w1v_ref, w2k_ref, w2v_ref, gk_ref,
                     ck_out, cv_out):
    def mlp(x, p_ref, w1_ref, w2_ref):
        nh = x.shape[0]
        top = _dot((x + p_ref[0:1]).astype(BF16), w1_ref[0])
        bot = _dot((x + p_ref[1:2]).astype(BF16), w1_ref[1])
        hid = _gelu_tanh(top + pltpu.roll(bot, nh - 1, 0))
        return _dot(hid.astype(BF16), w2_ref[...])

    nh = xk_ref.shape[1]
    yk = mlp(xk_ref[0], pk_ref, w1k_ref, w2k_ref)
    for g, x in enumerate(_split_halves(yk)):
        ck = _rms(x, gk_ref[...], NSA_DK)
        ck_out[0, g, 0:nh] = ck
        ck_out[0, g, nh:2 * nh] = ck
    yv = mlp(xv_ref[0], pv_ref, w1v_ref, w2v_ref)
    for g, layouts in enumerate(_value_layouts(yv)):
        for parity, v in enumerate(layouts):
            cv_out[0, g, parity, 0:nh] = v
            cv_out[0, g, parity, nh:2 * nh] = v


def _compress_weights(pos, w1, w2):
    half = CMP_BLOCK // 2
    g = NSA_GROUPS
    eye = jnp.eye(g, dtype=F32)
    w1 = w1.reshape(2, half, NSA_DK, CMP_HIDDEN)
    w1 = w1[:, :, None, :, None, :] * eye[None, None, :, None, :, None]
    w1 = w1.reshape(2, half * g * NSA_DK, g * CMP_HIDDEN).astype(BF16)
    p = jnp.broadcast_to(pos.reshape(2, half, 1, NSA_DK), (2, half, g, NSA_DK)).reshape(2, half * g * NSA_DK)
    w2 = (w2[None, :, None, :] * eye[:, None, :, None]).reshape(g * CMP_HIDDEN, g * NSA_DK).astype(BF16)
    return p, w1, w2


def _compress(z_kc, z_vc, bsz, s, pos_k, w1_k, w2_k, pos_v, w1_v, w2_v, k_norm0):
    nh = s // CMP_STRIDE
    feat = CMP_STRIDE * LANES
    pk, w1k, w2k = _compress_weights(pos_k, w1_k, w2_k)
    pv, w1v, w2v = _compress_weights(pos_v, w1_v, w2_v)
    full = lambda a: pl.BlockSpec(a.shape, lambda b: (0,) * a.ndim)
    xspec = pl.BlockSpec((1, nh, feat), lambda b: (b, 0, 0))
    gk = _pad_cols(k_norm0.reshape(1, -1), LANES)
    return pl.pallas_call(
        _compress_kernel,
        grid=(bsz,),
        in_specs=[xspec, xspec, full(pk), full(pv), full(w1k), full(w1v), full(w2k), full(w2v), full(gk)],
        out_specs=[pl.BlockSpec((1, NSA_GROUPS, 2 * nh, LANES), lambda b: (b, 0, 0, 0)),
                   pl.BlockSpec((1, NSA_GROUPS, 2, 2 * nh, LANES), lambda b: (b, 0, 0, 0, 0))],
        out_shape=[jax.ShapeDtypeStruct((bsz, NSA_GROUPS, 2 * nh, LANES), F32),
                   jax.ShapeDtypeStruct((bsz, NSA_GROUPS, 2, 2 * nh, LANES), F32)],
        compiler_params=_cparams(("parallel",)),
        name="nsa_compress",
    )(z_kc.reshape(bsz, nh, feat), z_vc.reshape(bsz, nh, feat), pk, pv, w1k, w1v, w2k, w2v, gk)


def _cmp_attn_kernel(q_ref, ck_ref, cv_ref, bias_ref, ov_ref, o_ref, q2_ref, score_ref, *, tq, nsel):
    t = pl.program_id(1)
    half = NSA_HPG // 2
    nh = bias_ref.shape[4]
    qrow = t * tq + lax.broadcasted_iota(jnp.int32, (tq, nh), 0)
    row_ok = qrow >= CMP_BLOCK - 1
    per_tile = tq // CMP_STRIDE
    c_last = per_tile * (pl.num_programs(1) - 1)
    start = lax.rem(per_tile * t + (nh - lax.rem(c_last, nh)), nh)
    cols = pl.ds(pl.multiple_of(start, per_tile), nh)
    col = lax.broadcasted_iota(jnp.int32, (nh, LANES), 0)
    lane_k = lax.broadcasted_iota(jnp.int32, (nh, LANES), 1)
    wrapped = (lane_k == HALF) & (col < c_last - per_tile * t)
    ck = jnp.where(wrapped, MASK_VALUE, ck_ref[0, 0, cols, :]).astype(BF16)
    cv = [cv_ref[0, 0, parity, cols, :].astype(BF16) for parity in range(2)]
    ov = ov_ref[cols, :].astype(BF16)
    lane_q = lax.broadcasted_iota(jnp.int32, (tq, LANES), 1)
    mask_lane = jnp.where(lane_q == HALF, 1.0, 0.0).astype(BF16)
    outs = []
    psum = None
    for h in range(NSA_HPG):
        s = _dot_t(q_ref[0, h] + mask_lane, ck) + bias_ref[0, 0, h]
        e = jnp.exp2(s - jnp.max(s, axis=-1, keepdims=True))
        p = jnp.where(row_ok, e / jnp.sum(e, axis=-1, keepdims=True), 0.0)
        outs.append(_dot(p.astype(BF16), cv[h // half]))
        psum = p if psum is None else psum + p
    for j in range(half):
        o = jnp.where(lane_q < HALF, outs[j], outs[half + j])
        o_ref[0, :, j * LANES:(j + 1) * LANES] = o.astype(o_ref.dtype)

    hi = psum.astype(BF16)
    lo = (psum - hi.astype(F32)).astype(BF16)
    imp = _dot(hi, ov) + _dot(lo, ov)

    n_blk = LANES - HALF
    blk = lax.broadcasted_iota(jnp.int32, (n_blk, tq), 0)
    qpos = t * tq + lax.broadcasted_iota(jnp.int32, (n_blk, tq), 1)
    cur = qpos // SLC_BLOCK
    valid = blk * SLC_BLOCK <= qpos
    forced = (blk == 0) | (blk == cur) | (blk == cur - 1)
    score = jnp.where(valid & forced, FORCE_SCORE, jnp.where(valid, imp.T[HALF:], -1.0))
    score = jnp.where(blk < nsel, score, REMOVED_SCORE)
    score_ref[...] = score
    rank = jnp.zeros((n_blk, tq), F32)
    for j in range(n_blk):
        sj = jnp.broadcast_to(score_ref[j:j + 1, :], (n_blk, tq))
        beats = (sj > score) | ((sj == score) & (blk > j))
        rank = rank + jnp.where(beats, 1.0, 0.0)
    unselected = jnp.where(rank < min(SLC_TOPN, nsel), 0.0, 1.0)
    unselected = jnp.concatenate([jnp.zeros((HALF, tq), F32), unselected], axis=0).T.astype(BF16)
    for h in range(NSA_HPG):
        q2_ref[0, h] = q_ref[0, h] + unselected


def _cmp_attention(q, ck, cv, bias, ov, s):
    bsz = q.shape[0]
    tq = NSA_TQ
    nqt = s // tq
    nh = s // CMP_STRIDE
    nsel = s // SLC_BLOCK
    return pl.pallas_call(
        functools.partial(_cmp_attn_kernel, tq=tq, nsel=nsel),
        grid=(NSA_GROUPS, nqt, bsz),
        in_specs=[pl.BlockSpec((1, NSA_HPG, tq, LANES), lambda g, t, b: (b, g, t, 0)),
                  pl.BlockSpec((1, 1, 2 * nh, LANES), lambda g, t, b: (b, g, 0, 0)),
                  pl.BlockSpec((1, 1, 2, 2 * nh, LANES), lambda g, t, b: (b, g, 0, 0, 0)),
                  pl.BlockSpec((1, 1, NSA_HPG, tq, nh), lambda g, t, b: (g, 0, 0, 0, 0)),
                  pl.BlockSpec((2 * nh, LANES), lambda g, t, b: (0, 0))],
        out_specs=[pl.BlockSpec((1, tq, NSA_HPG * NSA_DK), lambda g, t, b: (b, t, g)),
                   pl.BlockSpec((1, NSA_HPG, tq, LANES), lambda g, t, b: (b, g, t, 0))],
        out_shape=[jax.ShapeDtypeStruct((bsz, s, NSA_HEADS * NSA_DK), BF16),
                   jax.ShapeDtypeStruct(q.shape, BF16)],
        scratch_shapes=[pltpu.VMEM((LANES - HALF, tq), F32)],
        compiler_params=_cparams(("parallel", "parallel", "parallel")),
        name="nsa_cmp_attn",
    )(q, ck, cv, bias, ov)


def _nsa_window_kernel(q_ref, k_ref, v_ref, bias_ref, o_ref, s_ref, *, tq, window_steps):
    t = pl.program_id(2)
    ratio = NSA_TK // tq
    n_tiles = bias_ref.shape[1]
    half = NSA_HPG // 2
    last = t // ratio
    q = q_ref[0].reshape(NSA_HPG * tq, LANES)
    rows = []
    for i in range(window_steps):
        kc = last - (window_steps - 1) + i
        tile = jnp.where(kc < 0, n_tiles - 1, t - ratio * kc)
        rows.append(pl.ds(pl.multiple_of(jnp.maximum(kc, 0) * NSA_TK, NSA_TK), NSA_TK))
        s = _dot_t(q, k_ref[0, 0, rows[i], :]).reshape(NSA_HPG, tq, NSA_TK)
        s_ref[:, :, i * NSA_TK:(i + 1) * NSA_TK] = s + bias_ref[0, tile]
    accs = []
    for h in range(NSA_HPG):
        s = s_ref[h]
        p = jnp.exp2(s - jnp.max(s, axis=-1, keepdims=True)).astype(BF16)
        acc = None
        for i in range(window_steps):
            pv = _dot(p[:, i * NSA_TK:(i + 1) * NSA_TK], v_ref[0, 0, h // half, rows[i], :])
            acc = pv if acc is None else acc + pv
        accs.append(acc)
    for j in range(half):
        o = _normalized_pair(accs[j], accs[half + j])
        o_ref[0, :, j * LANES:(j + 1) * LANES] = o.astype(o_ref.dtype)


def _nsa_flash_kernel(q_ref, k_ref, v_ref, bias_ref, o_ref, s0_ref, s1_ref, m_ref, acc_ref, *, tq):
    t = pl.program_id(2)
    ratio = NSA_TK // tq
    n_tiles = bias_ref.shape[1]
    half = NSA_HPG // 2
    m_ref[...] = jnp.full_like(m_ref, MASK_VALUE)
    acc_ref[...] = jnp.zeros_like(acc_ref)
    last = t // ratio
    first = 0

    def key_rows(kc):
        return pl.ds(pl.multiple_of(jnp.minimum(kc, last) * NSA_TK, NSA_TK), NSA_TK)

    def scores(kc, buf):
        tile = jnp.where(kc > last, n_tiles - 1, jnp.minimum(t - ratio * kc, n_tiles - 2))
        s = _dot_t(q_ref[0].reshape(NSA_HPG * tq, LANES), k_ref[0, 0, key_rows(kc), :])
        buf[...] = s.reshape(NSA_HPG, tq, NSA_TK) + bias_ref[0, tile]

    def values(h, kc):
        return v_ref[0, 0, h // half, key_rows(kc), :]

    _flash_chunks(first, last, NSA_HPG, scores, values, (s0_ref, s1_ref), m_ref, acc_ref)
    for j in range(half):
        o = _normalized_pair(acc_ref[j], acc_ref[half + j])
        o_ref[0, :, j * LANES:(j + 1) * LANES] = o.astype(o_ref.dtype)


def _nsa_flash(q, k, v, bias, s, window_steps=None):
    bsz = q.shape[0]
    tq = NSA_TQ
    nqt = s // tq
    n_off = bias.shape[1]
    if window_steps is None:
        body = functools.partial(_nsa_flash_kernel, tq=tq)
        scratch = [pltpu.VMEM((NSA_HPG, tq, NSA_TK), F32), pltpu.VMEM((NSA_HPG, tq, NSA_TK), F32),
                   pltpu.VMEM((NSA_HPG, tq, LANES), F32), pltpu.VMEM((NSA_HPG, tq, LANES), F32)]
    else:
        body = functools.partial(_nsa_window_kernel, tq=tq, window_steps=window_steps)
        scratch = [pltpu.VMEM((NSA_HPG, tq, window_steps * NSA_TK), F32)]
    return pl.pallas_call(
        body,
        grid=(NSA_GROUPS, bsz, nqt),
        in_specs=[pl.BlockSpec((1, NSA_HPG, tq, LANES), lambda g, b, t: (b, g, t, 0)),
                  pl.BlockSpec((1, 1, s, LANES), lambda g, b, t: (b, g, 0, 0)),
                  pl.BlockSpec((1, 1, 2, s, LANES), lambda g, b, t: (b, g, 0, 0, 0)),
                  pl.BlockSpec((1, n_off, NSA_HPG, tq, NSA_TK), lambda g, b, t: (g, 0, 0, 0, 0))],
        out_specs=pl.BlockSpec((1, tq, NSA_HPG * NSA_DK), lambda g, b, t: (b, t, g)),
        out_shape=jax.ShapeDtypeStruct((bsz, s, NSA_HEADS * NSA_DK), BF16),
        scratch_shapes=scratch,
        compiler_params=_cparams(("parallel", "parallel", "parallel")),
        name="nsa_flash_sel" if window_steps is None else "nsa_flash_win",
    )(q, k, v, bias)


def _hybrid_layer(h, bsz, s, g_mix, g_mlp, w1, w2, w_in, conv_w, conv_b, wa, ba, wx, bx, lam,
                  q_norm, kv_norm, w_uq, w_ukv, qn_q, qn_k, w_out):
    d = h.shape[1]
    c = LRU_WIDTH
    o_kr = 2 * c + MLA_Q_RANK + MLA_KV_RANK
    w_kr = jnp.pad(w_in[:, o_kr:], ((0, 0), (MLA_NOPE, LANES - MLA_QK)))
    w_all = jnp.concatenate([w_in[:, :o_kr], w_kr], axis=1).astype(BF16)
    z_lru, q_lat, kv_lat, k_rope = _norm_proj(h, g_mix, w_all, (2 * c, MLA_Q_RANK, MLA_KV_RANK, LANES))
    lru = _rg_lru(z_lru.reshape(bsz, s, 2 * c), conv_w, conv_b, wa, ba, wx, bx, lam)
    q, k, v = _mla_prep(q_lat, kv_lat, k_rope, bsz, s, q_norm, kv_norm, w_uq, w_ukv, qn_q, qn_k)
    mla = _mla_attention(q, k, v)
    wo = w_out.astype(BF16)
    return _mix_mlp(h, [[lru.reshape(bsz * s, c)], [mla.reshape(bsz * s, -1)]], [wo[:c], wo[c:]],
                    g_mlp, w1.astype(BF16), w2.astype(BF16))


def _nsa_layer(h, bsz, s, g_mix, g_mlp, w1, w2, w_in, pos_k, w1_k, w2_k, pos_v, w1_v, w2_v,
               q_norm, k_norm, rel_bias, w_out):
    nq = NSA_HEADS * NSA_DK
    kvw = NSA_GROUPS * NSA_DK
    n_gate = 3 * NSA_HEADS
    w_gate = _pad_cols(w_in[:, nq + 6 * kvw:], LANES)
    w_all = jnp.concatenate([w_in[:, :nq + 6 * kvw], w_gate], axis=1).astype(BF16)
    z_q, z_kc, z_vc, z_kv, z_gate = _norm_proj(h, g_mix, w_all, (nq, kvw, kvw, 4 * kvw, LANES))

    q, ks, kw, vs, vw = _nsa_prep(z_q, z_kv, bsz, s, q_norm, k_norm)
    ck, cv = _compress(z_kc, z_vc, bsz, s, pos_k, w1_k, w2_k, pos_v, w1_v, w2_v, k_norm[0])

    tq = NSA_TQ
    nqt = s // tq
    order = np.array(HEAD_ORDER)
    expand = np.zeros((3, LANES, NSA_HEADS * NSA_DK), np.float32)
    for br in range(3):
        for hd in range(NSA_HEADS):
            expand[br, 3 * hd + br, hd * NSA_DK:(hd + 1) * NSA_DK] = 1.0

    slot_heads = (np.arange(NSA_GROUPS)[:, None] * NSA_HPG + order[None, :]).reshape(-1)
    tab_slots = rel_bias.T[slot_heads]
    nh = s // CMP_STRIDE
    nc = (s - CMP_BLOCK) // CMP_STRIDE + 1
    cmp_bias = _bias_tiles(tab_slots, _cmp_bias_ids(s))
    sel_bias = _bias_tiles(tab_slots, _toeplitz_bucket_ids(None)[0])
    win_ids, n_win_tiles = _toeplitz_bucket_ids(WINDOW)
    win_bias = _bias_tiles(tab_slots, win_ids)

    nsel = s // SLC_BLOCK
    cstart = np.arange(nh) * CMP_STRIDE
    sstart = np.arange(nsel) * SLC_BLOCK
    ov = np.clip(np.minimum(cstart[:, None] + CMP_BLOCK, sstart[None, :] + SLC_BLOCK)
                 - np.maximum(cstart[:, None], sstart[None, :]), 0, None).astype(np.float32) / CMP_BLOCK
    ov[nc:] = 0.0
    ov_p = np.zeros((nh, LANES), np.float32)
    ov_p[:, HALF:HALF + nsel] = ov

    o_c, q_sel = _cmp_attention(q, ck, cv, cmp_bias, jnp.asarray(np.concatenate([ov_p, ov_p])), s)
    o_s = _nsa_flash(q_sel, ks, vs, sel_bias, s)
    window_steps = (n_win_tiles + 1) // (NSA_TK // tq)
    o_w = _nsa_flash(q, kw, vw, win_bias, s, window_steps=window_steps)

    flat = lambda a: a.reshape(bsz * s, -1)
    return _mix_mlp(h, [[flat(o_c), flat(o_s), flat(o_w)]], [w_out.astype(BF16)],
                    g_mlp, w1.astype(BF16), w2.astype(BF16), gate=(z_gate, jnp.asarray(expand, BF16)))


def kernel(x, rel_bias, norm_mix, norm_mlp, mlp_w1, mlp_w2, hy_w_in, lru_conv_w, lru_conv_b, lru_wa, lru_ba, lru_wx, lru_bx, lru_lambda, mla_q_norm, mla_kv_norm, mla_w_uq, mla_w_ukv, mla_qn_q, mla_qn_k, hy_w_out, nsa_w_in, nsa_cmp_pos_k, nsa_cmp_w1_k, nsa_cmp_w2_k, nsa_cmp_pos_v, nsa_cmp_w1_v, nsa_cmp_w2_v, nsa_q_norm, nsa_k_norm, nsa_w_out):
    bsz, s, d = x.shape
    depth = norm_mix.shape[0]
    h = x.reshape(bsz * s, d)
    for layer in range(depth):
        if layer % 2 == 0:
            e = layer // 2
            h = _hybrid_layer(h, bsz, s, norm_mix[layer], norm_mlp[layer], mlp_w1[layer], mlp_w2[layer],
                              hy_w_in[e], lru_conv_w[e], lru_conv_b[e], lru_wa[e], lru_ba[e], lru_wx[e],
                              lru_bx[e], lru_lambda[e], mla_q_norm[e], mla_kv_norm[e], mla_w_uq[e],
                              mla_w_ukv[e], mla_qn_q[e], mla_qn_k[e], hy_w_out[e])
        else:
            o = layer // 2
            h = _nsa_layer(h, bsz, s, norm_mix[layer], norm_mlp[layer], mlp_w1[layer], mlp_w2[layer],
                           nsa_w_in[o], nsa_cmp_pos_k[o], nsa_cmp_w1_k[o], nsa_cmp_w2_k[o],
                           nsa_cmp_pos_v[o], nsa_cmp_w1_v[o], nsa_cmp_w2_v[o], nsa_q_norm[o],
                           nsa_k_norm[o], rel_bias, nsa_w_out[o])
    return h.reshape(bsz, s, d)
```

```python
import functools
import math

import numpy as np
import jax
import jax.numpy as jnp
from jax import lax
from jax.experimental import pallas as pl
from jax.experimental.pallas import tpu as pltpu

F32 = jnp.float32
BF16 = jnp.bfloat16

NORM_EPS = 1e-6
MASK_VALUE = -1e30
FORCE_SCORE = 1e4
REMOVED_SCORE = -3e38

NUM_BUCKETS = 32
MAX_DISTANCE = 1024
MASKED_ID = NUM_BUCKETS
EXCLUDED_ID = NUM_BUCKETS + 1
LOG2E = math.log2(math.e)

LRU_WIDTH = 512
LRU_BLOCKS = 8
CONV_WIDTH = 4
LRU_C = 8.0

MLA_HEADS = 8
MLA_NOPE = 64
MLA_ROPE = 32
MLA_QK = MLA_NOPE + MLA_ROPE
MLA_V = 64
MLA_Q_RANK = 384
MLA_KV_RANK = 256
ROPE_THETA = 10000.0

NSA_HEADS = 16
NSA_GROUPS = 2
NSA_HPG = NSA_HEADS // NSA_GROUPS
NSA_DK = 64
CMP_BLOCK = 32
CMP_STRIDE = 16
CMP_HIDDEN = 256
SLC_BLOCK = 64
SLC_TOPN = 8
WINDOW = 512

LANES = 128
HALF = LANES // 2
NSA_TQ = 128
NSA_TK = 256
SEL_TILES = 2
MLA_TR = 128
MLA_TK = 512
HEAD_ORDER = (0, 2, 4, 6, 1, 3, 5, 7)
VMEM_LIMIT = 56 * 1024 * 1024


def _cparams(sem):
    return pltpu.CompilerParams(dimension_semantics=sem, vmem_limit_bytes=VMEM_LIMIT)


def _rms(x, g, n=None):
    n = x.shape[-1] if n is None else n
    ms = jnp.sum(x * x, axis=-1, keepdims=True) * (1.0 / n)
    return x * lax.rsqrt(ms + NORM_EPS) * g


def _gelu_tanh(x):
    return 0.5 * x * (1.0 + jnp.tanh(math.sqrt(2.0 / math.pi) * (x + 0.044715 * (x * x * x))))


def _sigmoid(x):
    return 1.0 / (1.0 + jnp.exp(-x))


def _dot(a, b):
    return jnp.dot(a, b, preferred_element_type=F32)


def _dot_t(a, b):
    return lax.dot_general(a, b, (((1,), (1,)), ((), ())), preferred_element_type=F32)


def _flash_chunks(first, last, n_chains, scores, values, s_bufs, m_ref, acc_ref):
    stage_scores = scores

    def stage_update(kc, buf):
        for c in range(n_chains):
            m_old = m_ref[c]
            m_new = jnp.maximum(m_old, jnp.max(buf[c], axis=-1, keepdims=True))
            alpha = jnp.exp2(m_old - m_new)
            s = buf[c]
            p = jnp.exp2(s - jnp.concatenate([m_new] * (s.shape[1] // LANES), axis=1))
            m_ref[c] = m_new
            acc_ref[c] = acc_ref[c] * alpha + _dot(p.astype(BF16), values(c, kc))

    stage_scores(first, s_bufs[0])

    def two_chunks(j, carry):
        kc = first + 2 * j
        stage_scores(kc + 1, s_bufs[1])
        stage_update(kc, s_bufs[0])
        stage_scores(kc + 2, s_bufs[0])
        stage_update(kc + 1, s_bufs[1])
        return carry

    lax.fori_loop(0, (last - first + 2) // 2, two_chunks, 0)


def _normalized_pair(acc_even, acc_odd):
    lo = lax.broadcasted_iota(jnp.int32, acc_even.shape, 1) < HALF
    return jnp.where(lo, acc_even / acc_even[:, LANES - 1:LANES], acc_odd / acc_odd[:, 0:1])


def _with_ones_column(v, odd):
    lane = lax.broadcasted_iota(jnp.int32, v.shape, 1)
    return jnp.where(lane == (0 if odd else LANES - 1), 1.0, v)


def _norm_proj_kernel(x_ref, g_ref, w_ref, *out_refs, splits):
    xn = _rms(x_ref[...], g_ref[...]).astype(BF16)
    off = 0
    for o_ref, n in zip(out_refs, splits):
        o_ref[...] = _dot(xn, w_ref[:, off:off + n]).astype(o_ref.dtype)
        off += n


def _norm_proj(x, g, w, splits, tm=512):
    t, d = x.shape
    n = w.shape[1]
    assert sum(splits) == n and t % tm == 0
    return pl.pallas_call(
        functools.partial(_norm_proj_kernel, splits=splits),
        grid=(t // tm,),
        in_specs=[pl.BlockSpec((tm, d), lambda i: (i, 0)),
                  pl.BlockSpec((1, d), lambda i: (0, 0)),
                  pl.BlockSpec((d, n), lambda i: (0, 0))],
        out_specs=[pl.BlockSpec((tm, s), lambda i: (i, 0)) for s in splits],
        out_shape=[jax.ShapeDtypeStruct((t, s), F32) for s in splits],
        compiler_params=_cparams(("parallel",)),
        name="norm_proj",
    )(x, g.reshape(1, d), w)


def _mix_mlp_kernel(*refs, group_sizes, gated):
    n_a = sum(group_sizes)
    n_g = len(group_sizes)
    h_ref = refs[0]
    a_refs = refs[1:1 + n_a]
    wo_refs = refs[1 + n_a:1 + n_a + n_g]
    rest = refs[1 + n_a + n_g:]
    if gated:
        zg_ref, e_ref = rest[:2]
        rest = rest[2:]
    g_ref, w1_ref, w2_ref, out_ref, hres, xn, acc = rest
    f = pl.program_id(1)

    @pl.when(f == 0)
    def _():
        h1 = h_ref[...]
        if gated:
            gate = _sigmoid(zg_ref[...])
            g_hi = gate.astype(BF16)
            g_lo = (gate - g_hi.astype(F32)).astype(BF16)
        k = 0
        for gi, gs in enumerate(group_sizes):
            a = None
            for j in range(gs):
                aj = a_refs[k + j][...].astype(F32)
                if gated:
                    aj = aj * (_dot(g_hi, e_ref[j]) + _dot(g_lo, e_ref[j]))
                a = aj if a is None else a + aj
            k += gs
            h1 = h1 + _dot(a.astype(BF16), wo_refs[gi][...])
        hres[...] = h1
        xn[...] = _rms(h1, g_ref[...]).astype(BF16)
        acc[...] = jnp.zeros_like(acc)

    u = jnp.maximum(_dot(xn[...], w1_ref[...]), 0.0)
    acc[...] += _dot((u * u).astype(BF16), w2_ref[...])

    @pl.when(f == pl.num_programs(1) - 1)
    def _():
        out_ref[...] = hres[...] + acc[...]


def _mix_mlp(h, groups, wos, g, w1, w2, gate=None, tm=512, tf=1024):
    t, d = h.shape
    ff = w1.shape[1]
    group_sizes = tuple(len(gr) for gr in groups)
    a_list = [a for gr in groups for a in gr]
    extra = [] if gate is None else list(gate)
    in_specs = [pl.BlockSpec((tm, d), lambda i, f: (i, 0))]
    in_specs += [pl.BlockSpec((tm, a.shape[1]), lambda i, f: (i, 0)) for a in a_list]
    in_specs += [pl.BlockSpec(w.shape, lambda i, f: (0, 0)) for w in wos]
    if gate is not None:
        in_specs += [pl.BlockSpec((tm, LANES), lambda i, f: (i, 0)),
                     pl.BlockSpec(gate[1].shape, lambda i, f: (0, 0, 0))]
    in_specs += [pl.BlockSpec((1, d), lambda i, f: (0, 0)),
                 pl.BlockSpec((d, tf), lambda i, f: (0, f)),
                 pl.BlockSpec((tf, d), lambda i, f: (f, 0))]
    return pl.pallas_call(
        functools.partial(_mix_mlp_kernel, group_sizes=group_sizes, gated=gate is not None),
        grid=(t // tm, ff // tf),
        in_specs=in_specs,
        out_specs=pl.BlockSpec((tm, d), lambda i, f: (i, 0)),
        out_shape=jax.ShapeDtypeStruct((t, d), F32),
        scratch_shapes=[pltpu.VMEM((tm, d), F32), pltpu.VMEM((tm, d), BF16),
                        pltpu.VMEM((tm, d), F32)],
        compiler_params=_cparams(("parallel", "arbitrary")),
        name="mix_mlp",
    )(h, *a_list, *wos, *extra, g.reshape(1, d), w1, w2)


def _lru_kernel(xr_ref, xg_ref, cw_ref, cb_ref, wa_ref, ba_ref, wx_ref, bx_ref, lam_ref,
                o_ref, tail_ref, hc_ref, *, tc):
    c = pl.program_id(1)

    @pl.when(c == 0)
    def _():
        tail_ref[...] = jnp.zeros_like(tail_ref)
        hc_ref[...] = jnp.zeros_like(hc_ref)

    x = xr_ref[0]
    width = x.shape[1]
    row = lax.broadcasted_iota(jnp.int32, (tc, width), 0)
    row8 = lax.broadcasted_iota(jnp.int32, (8, width), 0)
    tail = tail_ref[...]
    cw = cw_ref[...]
    y = x * cw[CONV_WIDTH - 1:CONV_WIDTH]
    for s in range(1, CONV_WIDTH):
        sh = pltpu.roll(x, s, 0)
        head = jnp.where(row8 < s, pltpu.roll(tail, s, 0), sh[0:8])
        sh = jnp.concatenate([head, sh[8:]], axis=0)
        y = y + sh * cw[CONV_WIDTH - 1 - s:CONV_WIDTH - s]
    y = y + cb_ref[...]
    tail_ref[...] = x[tc - 8:tc]

    yb = y.astype(BF16)
    r = _sigmoid(_dot(yb, wa_ref[...]) + ba_ref[...])
    i = _sigmoid(_dot(yb, wx_ref[...]) + bx_ref[...])
    nl = -lam_ref[...]
    softplus = jnp.maximum(nl, 0.0) + jnp.log(1.0 + jnp.exp(-jnp.abs(nl)))
    log_a = (-LRU_C) * r * softplus
    a = jnp.exp(log_a)
    mult = jnp.sqrt(1.0 - jnp.exp(2.0 * log_a))
    mult = jnp.where((row == 0) & (c == 0), 1.0, mult)
    b = mult * (i * y)

    d = 1
    while d < tc:
        keep = row >= d
        a_sh = jnp.where(keep, pltpu.roll(a, d, 0), 1.0)
        b_sh = jnp.where(keep, pltpu.roll(b, d, 0), 0.0)
        b = a * b_sh + b
        a = a * a_sh
        d *= 2
    h = b + a * hc_ref[...]
    hc_ref[...] = h[tc - 1:tc]
    o_ref[0] = (h * _gelu_tanh(xg_ref[0])).astype(o_ref.dtype)


def _block_diag(w):
    n, d, e = w.shape
    eye = jnp.eye(n, dtype=w.dtype)
    return (w[:, :, None, :] * eye[:, None, :, None]).reshape(n * d, n * e)


def _rg_lru(z_lru, conv_w, conv_b, wa, ba, wx, bx, lam, tc=256):
    bsz, s, _ = z_lru.shape
    c = LRU_WIDTH
    tc = min(tc, s)
    vec = lambda v: pl.BlockSpec(v, lambda b, i: (0, 0))
    return pl.pallas_call(
        functools.partial(_lru_kernel, tc=tc),
        grid=(bsz, s // tc),
        in_specs=[pl.BlockSpec((1, tc, c), lambda b, i: (b, i, 0)),
                  pl.BlockSpec((1, tc, c), lambda b, i: (b, i, 1)),
                  vec((CONV_WIDTH, c)), vec((1, c)), vec((c, c)), vec((1, c)),
                  vec((c, c)), vec((1, c)), vec((1, c))],
        out_specs=pl.BlockSpec((1, tc, c), lambda b, i: (b, i, 0)),
        out_shape=jax.ShapeDtypeStruct((bsz, s, c), BF16),
        scratch_shapes=[pltpu.VMEM((8, c), F32), pltpu.VMEM((1, c), F32)],
        compiler_params=_cparams(("parallel", "arbitrary")),
        name="rg_lru",
    )(z_lru, z_lru, conv_w, conv_b.reshape(1, c), _block_diag(wa).astype(BF16), ba.reshape(1, c),
      _block_diag(wx).astype(BF16), bx.reshape(1, c), lam.reshape(1, c))


def _rope(x, cos, sin_lo, sin_hi):
    return (x * cos + pltpu.roll(x, LANES - MLA_ROPE // 2, 1) * sin_lo
            + pltpu.roll(x, MLA_ROPE // 2, 1) * sin_hi)


def _mla_prep_kernel(ql_ref, kvl_ref, kr_ref, gq_ref, gkv_ref, wq_ref, wk_ref, wv_ref,
                     nq_ref, nk_ref, cos_ref, s1_ref, s2_ref, q_out, k_out, v_out):
    qn = _rms(ql_ref[...], gq_ref[...]).astype(BF16)
    kvn = _rms(kvl_ref[...], gkv_ref[...]).astype(BF16)
    kr = kr_ref[...]
    cos, s1, s2 = cos_ref[...], s1_ref[...], s2_ref[...]
    scale = MLA_QK ** -0.5 * LOG2E
    for h in range(MLA_HEADS):
        sl = slice(h * LANES, (h + 1) * LANES)
        qh = _rms(_dot(qn, wq_ref[:, sl]), nq_ref[...], MLA_QK)
        q_out[0, h] = (_rope(qh, cos, s1, s2) * scale).astype(BF16)
        kh = _rms(_dot(kvn, wk_ref[:, sl]) + kr, nk_ref[...], MLA_QK)
        k_out[0, h] = _rope(kh, cos, s1, s2).astype(BF16)
        v_out[0, h] = _with_ones_column(_dot(kvn, wv_ref[:, sl]), h % 2 == 1).astype(BF16)


def _rope_tables(s):
    half = MLA_ROPE // 2
    freqs = ROPE_THETA ** (-jnp.arange(half, dtype=F32) / half)
    ang = jnp.arange(s, dtype=F32)[:, None] * freqs[None, :]
    cos, sin = jnp.cos(ang), jnp.sin(ang)
    z = lambda n: jnp.zeros((s, n), F32)
    cos_t = jnp.concatenate([jnp.ones((s, MLA_NOPE), F32), cos, cos, z(LANES - MLA_QK)], axis=1)
    s_lo = jnp.concatenate([z(MLA_NOPE), -sin, z(LANES - MLA_NOPE - half)], axis=1)
    s_hi = jnp.concatenate([z(MLA_NOPE + half), sin, z(LANES - MLA_QK)], axis=1)
    return cos_t, s_lo, s_hi


def _pad_cols(w, n):
    return jnp.pad(w, ((0, 0),) * (w.ndim - 1) + ((0, n - w.shape[-1]),))


def _mla_prep(q_lat, kv_lat, k_rope, bsz, s, q_norm, kv_norm, w_uq, w_ukv, qn_q, qn_k, tm=512):
    tm = min(tm, s)
    nt = s // tm
    hl = MLA_HEADS * LANES
    wq = _pad_cols(w_uq.reshape(MLA_Q_RANK, MLA_HEADS, MLA_QK), LANES).reshape(MLA_Q_RANK, hl)
    wkv = w_ukv.reshape(MLA_KV_RANK, MLA_HEADS, MLA_NOPE + MLA_V)
    wk = _pad_cols(wkv[..., :MLA_NOPE], LANES).reshape(MLA_KV_RANK, hl)
    wv = wkv[..., MLA_NOPE:]
    zero = jnp.zeros_like(wv)
    even = (jnp.arange(MLA_HEADS) % 2 == 0)[None, :, None]
    wv = jnp.concatenate([jnp.where(even, wv, zero), jnp.where(even, zero, wv)], axis=-1)
    wv = wv.reshape(MLA_KV_RANK, hl)
    cos_t, s_lo, s_hi = _rope_tables(s)
    row = lambda n: pl.BlockSpec((tm, n), lambda b, i: (b * nt + i, 0))
    full = lambda a: pl.BlockSpec(a.shape, lambda b, i: (0, 0))
    tab = pl.BlockSpec((tm, LANES), lambda b, i: (i, 0))
    head_out = pl.BlockSpec((1, MLA_HEADS, tm, LANES), lambda b, i: (b, 0, i, 0))
    args = (q_lat, kv_lat, k_rope, q_norm.reshape(1, -1), kv_norm.reshape(1, -1),
            wq.astype(BF16), wk.astype(BF16), wv.astype(BF16),
            _pad_cols(qn_q.reshape(1, -1), LANES), _pad_cols(qn_k.reshape(1, -1), LANES),
            cos_t, s_lo, s_hi)
    in_specs = [row(MLA_Q_RANK), row(MLA_KV_RANK), row(LANES)] + [full(a) for a in args[3:10]] + [tab] * 3
    shp = jax.ShapeDtypeStruct((bsz, MLA_HEADS, s, LANES), BF16)
    return pl.pallas_call(
        _mla_prep_kernel,
        grid=(bsz, nt),
        in_specs=in_specs,
        out_specs=[head_out] * 3,
        out_shape=[shp] * 3,
        compiler_params=_cparams(("parallel", "parallel")),
        name="mla_prep",
    )(*args)


def _mla_attn_kernel(q_ref, k_ref, v_ref, mask_ref, o_ref, s0_ref, s1_ref, m_ref, acc_ref, *, tq):
    qi = pl.program_id(2)
    n_rc = tq // MLA_TR
    ratio = MLA_TK // MLA_TR
    n_tiles = mask_ref.shape[0]
    m_ref[...] = jnp.full_like(m_ref, MASK_VALUE)
    acc_ref[...] = jnp.zeros_like(acc_ref)
    last = (qi * tq + tq - 1) // MLA_TK

    def key_rows(kc):
        return pl.ds(pl.multiple_of(jnp.minimum(kc, last) * MLA_TK, MLA_TK), MLA_TK)

    def scores(kc, buf):
        for j in range(2):
            s = _dot_t(q_ref[0, j], k_ref[0, j, key_rows(kc), :])
            for rc in range(n_rc):
                offset = qi * n_rc + rc - ratio * kc
                tile = jnp.where(kc > last, 0, jnp.clip(offset + 1, 0, n_tiles - 1))
                buf[j * n_rc + rc] = s[rc * MLA_TR:(rc + 1) * MLA_TR] + mask_ref[tile]

    def values(c, kc):
        return v_ref[0, c // n_rc, key_rows(kc), :]

    _flash_chunks(0, last, 2 * n_rc, scores, values, (s0_ref, s1_ref), m_ref, acc_ref)
    for rc in range(n_rc):
        o = _normalized_pair(acc_ref[rc], acc_ref[n_rc + rc])
        o_ref[0, rc * MLA_TR:(rc + 1) * MLA_TR, :] = o.astype(o_ref.dtype)


def _mla_causal_tiles():
    i = np.arange(MLA_TR)[:, None]
    j = np.arange(MLA_TK)[None, :]
    tiles = [np.full((MLA_TR, MLA_TK), -np.inf, np.float32)]
    for d in range(MLA_TK // MLA_TR):
        tiles.append(np.where(j <= i + d * MLA_TR, 0.0, -np.inf).astype(np.float32))
    tiles.append(np.zeros((MLA_TR, MLA_TK), np.float32))
    return np.stack(tiles)


def _mla_attention(q, k, v, tq=512):
    bsz, nh, s, _ = q.shape
    tq = min(tq, s)
    assert tq % MLA_TK == 0 and tq % MLA_TR == 0
    n_chains = 2 * (tq // MLA_TR)
    mask = _mla_causal_tiles()
    kv_spec = pl.BlockSpec((1, 2, s, LANES), lambda b, hp, i: (b, hp, 0, 0))
    return pl.pallas_call(
        functools.partial(_mla_attn_kernel, tq=tq),
        grid=(bsz, nh // 2, s // tq),
        in_specs=[pl.BlockSpec((1, 2, tq, LANES), lambda b, hp, i: (b, hp, i, 0)), kv_spec, kv_spec,
                  pl.BlockSpec(mask.shape, lambda b, hp, i: (0, 0, 0))],
        out_specs=pl.BlockSpec((1, tq, LANES), lambda b, hp, i: (b, i, hp)),
        out_shape=jax.ShapeDtypeStruct((bsz, s, nh * MLA_V), BF16),
        scratch_shapes=[pltpu.VMEM((n_chains, MLA_TR, MLA_TK), F32), pltpu.VMEM((n_chains, MLA_TR, MLA_TK), F32),
                        pltpu.VMEM((n_chains, MLA_TR, LANES), F32), pltpu.VMEM((n_chains, MLA_TR, LANES), F32)],
        compiler_params=_cparams(("parallel", "parallel", "parallel")),
        name="mla_attn",
    )(q, k, v, jnp.asarray(mask))


def _t5_bucket_np(dist):
    n = np.maximum(dist, 0)
    max_exact = NUM_BUCKETS // 2
    nf = np.maximum(n, 1).astype(np.float32)
    large = max_exact + (np.log(nf / np.float32(max_exact)) / np.float32(math.log(MAX_DISTANCE / max_exact))
                         * np.float32(NUM_BUCKETS - max_exact)).astype(np.int32)
    large = np.minimum(large, NUM_BUCKETS - 1)
    return np.where(n < max_exact, n, large).astype(np.int32)


def _bias_tiles_kernel(tab_ref, idx_ref, o_ref, *, buckets):
    slot = pl.program_id(0) * NSA_HPG + pl.program_id(1)
    for i, present in enumerate(buckets):
        idx = idx_ref[i]
        acc = jnp.where(idx == EXCLUDED_ID, -jnp.inf, MASK_VALUE).astype(F32)
        for b in present:
            acc = jnp.where(idx == b, tab_ref[slot, b] * LOG2E, acc)
        o_ref[0, i, 0] = acc


def _bias_tiles(tab_slots, idx):
    n, r, c = idx.shape
    buckets = tuple(tuple(int(b) for b in np.unique(tile) if b < NUM_BUCKETS) for tile in idx)
    return pl.pallas_call(
        functools.partial(_bias_tiles_kernel, buckets=buckets),
        grid=(NSA_GROUPS, NSA_HPG),
        in_specs=[pl.BlockSpec(memory_space=pltpu.SMEM),
                  pl.BlockSpec((n, r, c), lambda g, p: (0, 0, 0))],
        out_specs=pl.BlockSpec((1, n, 1, r, c), lambda g, p: (g, 0, p, 0, 0)),
        out_shape=jax.ShapeDtypeStruct((NSA_GROUPS, n, NSA_HPG, r, c), F32),
        compiler_params=_cparams(("parallel", "parallel")),
        name="bias_tiles",
    )(tab_slots, jnp.asarray(idx))


def _cmp_bias_ids(s):
    nqt = s // NSA_TQ
    nh = s // CMP_STRIDE
    per_tile = NSA_TQ // CMP_STRIDE
    i = np.arange(NSA_TQ)[:, None]
    c = np.arange(nh)[None, :]
    dist = i - (CMP_BLOCK - 1) - CMP_STRIDE * (c - per_tile * (nqt - 1))
    ids = np.where(dist >= 0, _t5_bucket_np(dist), MASKED_ID).astype(np.int32)
    return ids[None]


def _toeplitz_bucket_ids(window):
    i = np.arange(NSA_TQ)[:, None]
    j = np.arange(NSA_TK)[None, :]
    tiles = []
    d = 0
    while True:
        dist = d * NSA_TQ + i - j
        ids = _t5_bucket_np(dist)
        ok = dist >= 0
        if window is not None:
            ok &= dist < window
        ids = np.where(ok, ids, MASKED_ID).astype(np.int32)
        tiles.append(ids)
        if window is None and (ids == NUM_BUCKETS - 1).all():
            break
        if window is not None and not ok.any():
            tiles.pop()
            break
        d += 1
    n_real = len(tiles)
    tiles.append(np.full((NSA_TQ, NSA_TK), EXCLUDED_ID, np.int32))
    return np.stack(tiles), n_real


def _split_halves(x):
    lo = lax.broadcasted_iota(jnp.int32, x.shape, 1) < HALF
    return jnp.where(lo, x, 0.0), pltpu.roll(jnp.where(lo, 0.0, x), HALF, 1)


def _value_layouts(x):
    lo = lax.broadcasted_iota(jnp.int32, x.shape, 1) < HALF
    g0e = jnp.where(lo, x, 0.0)
    g1o = jnp.where(lo, 0.0, x)
    pair = lambda e, o: (_with_ones_column(e, False), _with_ones_column(o, True))
    return pair(g0e, pltpu.roll(g0e, HALF, 1)), pair(pltpu.roll(g1o, HALF, 1), g1o)


def _nsa_prep_kernel(zq_ref, ks_ref, vs_ref, kw_ref, vw_ref, gq_ref, gks_ref, gkw_ref,
                     q_out, ks_out, kw_out, vs_out, vw_out, *, tm):
    i = pl.program_id(1)
    scale = NSA_DK ** -0.5 * LOG2E
    for j in range(NSA_HEADS // 2):
        e, o = _split_halves(zq_ref[:, j * LANES:(j + 1) * LANES])
        g = (2 * j) // NSA_HPG
        pe = (2 * j - NSA_HPG * g) // 2
        q_out[0, NSA_HPG * g + pe] = (_rms(e, gq_ref[...], NSA_DK) * scale).astype(BF16)
        q_out[0, NSA_HPG * g + NSA_HPG // 2 + pe] = (_rms(o, gq_ref[...], NSA_DK) * scale).astype(BF16)

    lane = lax.broadcasted_iota(jnp.int32, (tm, LANES), 1)
    pos = i * tm + lax.broadcasted_iota(jnp.int32, (tm, LANES), 0)
    block_tag = jnp.where(lane - HALF == pos // SLC_BLOCK, MASK_VALUE, 0.0)
    for g, x in enumerate(_split_halves(ks_ref[...])):
        ks_out[0, g] = (_rms(x, gks_ref[...], NSA_DK) + block_tag).astype(BF16)
    for g, x in enumerate(_split_halves(kw_ref[...])):
        kw_out[0, g] = _rms(x, gkw_ref[...], NSA_DK).astype(BF16)
    for ref, out in ((vs_ref, vs_out), (vw_ref, vw_out)):
        for g, (ve, vo) in enumerate(_value_layouts(ref[...])):
            out[0, g, 0] = ve.astype(BF16)
            out[0, g, 1] = vo.astype(BF16)


def _nsa_prep(z_q, z_kv, bsz, s, q_norm, k_norm, tm=512):
    tm = min(tm, s)
    nt = s // tm
    row = lambda n, c: pl.BlockSpec((tm, n), lambda b, i: (b * nt + i, c))
    gain = pl.BlockSpec((1, LANES), lambda b, i: (0, 0))
    kspec = pl.BlockSpec((1, NSA_GROUPS, tm, LANES), lambda b, i: (b, 0, i, 0))
    vspec = pl.BlockSpec((1, NSA_GROUPS, 2, tm, LANES), lambda b, i: (b, 0, 0, i, 0))
    kshape = jax.ShapeDtypeStruct((bsz, NSA_GROUPS, s, LANES), BF16)
    vshape = jax.ShapeDtypeStruct((bsz, NSA_GROUPS, 2, s, LANES), BF16)
    pad = lambda v: _pad_cols(v.reshape(1, -1), LANES)
    return pl.pallas_call(
        functools.partial(_nsa_prep_kernel, tm=tm),
        grid=(bsz, nt),
        in_specs=[row(NSA_HEADS * NSA_DK, 0), row(LANES, 0), row(LANES, 1), row(LANES, 2), row(LANES, 3),
                  gain, gain, gain],
        out_specs=[pl.BlockSpec((1, NSA_HEADS, tm, LANES), lambda b, i: (b, 0, i, 0)),
                   kspec, kspec, vspec, vspec],
        out_shape=[jax.ShapeDtypeStruct((bsz, NSA_HEADS, s, LANES), BF16), kshape, kshape, vshape, vshape],
        compiler_params=_cparams(("parallel", "parallel")),
        name="nsa_prep",
    )(z_q, z_kv, z_kv, z_kv, z_kv, pad(q_norm), pad(k_norm[1]), pad(k_norm[2]))


def _compress_kernel(xk_ref, xv_ref, pk_ref, pv_ref, w1k_ref, w1v_ref, w2k_ref, w2v_ref, gk_ref,
                     ck_out, cv_out):
    def mlp(x, p_ref, w1_ref, w2_ref):
        nh = x.shape[0]
        top = _dot((x + p_ref[0:1]).astype(BF16), w1_ref[0])
        bot = _dot((x + p_ref[1:2]).astype(BF16), w1_ref[1])
        hid = _gelu_tanh(top + pltpu.roll(bot, nh - 1, 0))
        return _dot(hid.astype(BF16), w2_ref[...])

    nh = xk_ref.shape[1]
    yk = mlp(xk_ref[0], pk_ref, w1k_ref, w2k_ref)
    for g, x in enumerate(_split_halves(yk)):
        ck = _rms(x, gk_ref[...], NSA_DK)
        ck_out[0, g, 0:nh] = ck
        ck_out[0, g, nh:2 * nh] = ck
    yv = mlp(xv_ref[0], pv_ref, w1v_ref, w2v_ref)
    for g, layouts in enumerate(_value_layouts(yv)):
        for parity, v in enumerate(layouts):
            cv_out[0, g, parity, 0:nh] = v
            cv_out[0, g, parity, nh:2 * nh] = v


def _compress_weights(pos, w1, w2):
    half = CMP_BLOCK // 2
    g = NSA_GROUPS
    eye = jnp.eye(g, dtype=F32)
    w1 = w1.reshape(2, half, NSA_DK, CMP_HIDDEN)
    w1 = w1[:, :, None, :, None, :] * eye[None, None, :, None, :, None]
    w1 = w1.reshape(2, half * g * NSA_DK, g * CMP_HIDDEN).astype(BF16)
    p = jnp.broadcast_to(pos.reshape(2, half, 1, NSA_DK), (2, half, g, NSA_DK)).reshape(2, half * g * NSA_DK)
    w2 = (w2[None, :, None, :] * eye[:, None, :, None]).reshape(g * CMP_HIDDEN, g * NSA_DK).astype(BF16)
    return p, w1, w2


def _compress(z_kc, z_vc, bsz, s, pos_k, w1_k, w2_k, pos_v, w1_v, w2_v, k_norm0):
    nh = s // CMP_STRIDE
    feat = CMP_STRIDE * LANES
    pk, w1k, w2k = _compress_weights(pos_k, w1_k, w2_k)
    pv, w1v, w2v = _compress_weights(pos_v, w1_v, w2_v)
    full = lambda a: pl.BlockSpec(a.shape, lambda b: (0,) * a.ndim)
    xspec = pl.BlockSpec((1, nh, feat), lambda b: (b, 0, 0))
    gk = _pad_cols(k_norm0.reshape(1, -1), LANES)
    return pl.pallas_call(
        _compress_kernel,
        grid=(bsz,),
        in_specs=[xspec, xspec, full(pk), full(pv), full(w1k), full(w1v), full(w2k), full(w2v), full(gk)],
        out_specs=[pl.BlockSpec((1, NSA_GROUPS, 2 * nh, LANES), lambda b: (b, 0, 0, 0)),
                   pl.BlockSpec((1, NSA_GROUPS, 2, 2 * nh, LANES), lambda b: (b, 0, 0, 0, 0))],
        out_shape=[jax.ShapeDtypeStruct((bsz, NSA_GROUPS, 2 * nh, LANES), F32),
                   jax.ShapeDtypeStruct((bsz, NSA_GROUPS, 2, 2 * nh, LANES), F32)],
        compiler_params=_cparams(("parallel",)),
        name="nsa_compress",
    )(z_kc.reshape(bsz, nh, feat), z_vc.reshape(bsz, nh, feat), pk, pv, w1k, w1v, w2k, w2v, gk)


def _cmp_attn_kernel(q_ref, ck_ref, cv_ref, bias_ref, ov_ref, o_ref, q2_ref, score_ref, *, tq, nsel):
    t = pl.program_id(1)
    half = NSA_HPG // 2
    nh = bias_ref.shape[4]
    qrow = t * tq + lax.broadcasted_iota(jnp.int32, (tq, nh), 0)
    row_ok = qrow >= CMP_BLOCK - 1
    per_tile = tq // CMP_STRIDE
    c_last = per_tile * (pl.num_programs(1) - 1)
    start = lax.rem(per_tile * t + (nh - lax.rem(c_last, nh)), nh)
    cols = pl.ds(pl.multiple_of(start, per_tile), nh)
    col = lax.broadcasted_iota(jnp.int32, (nh, LANES), 0)
    lane_k = lax.broadcasted_iota(jnp.int32, (nh, LANES), 1)
    wrapped = (lane_k == HALF) & (col < c_last - per_tile * t)
    ck = jnp.where(wrapped, MASK_VALUE, ck_ref[0, 0, cols, :]).astype(BF16)
    cv = [cv_ref[0, 0, parity, cols, :].astype(BF16) for parity in range(2)]
    ov = ov_ref[cols, :].astype(BF16)
    lane_q = lax.broadcasted_iota(jnp.int32, (tq, LANES), 1)
    mask_lane = jnp.where(lane_q == HALF, 1.0, 0.0).astype(BF16)
    outs = []
    psum = None
    for h in range(NSA_HPG):
        s = _dot_t(q_ref[0, h] + mask_lane, ck) + bias_ref[0, 0, h]
        e = jnp.exp2(s - jnp.max(s, axis=-1, keepdims=True))
        p = jnp.where(row_ok, e / jnp.sum(e, axis=-1, keepdims=True), 0.0)
        outs.append(_dot(p.astype(BF16), cv[h // half]))
        psum = p if psum is None else psum + p
    for j in range(half):
        o = jnp.where(lane_q < HALF, outs[j], outs[half + j])
        o_ref[0, :, j * LANES:(j + 1) * LANES] = o.astype(o_ref.dtype)

    hi = psum.astype(BF16)
    lo = (psum - hi.astype(F32)).astype(BF16)
    imp = _dot(hi, ov) + _dot(lo, ov)

    n_blk = LANES - HALF
    blk = lax.broadcasted_iota(jnp.int32, (n_blk, tq), 0)
    qpos = t * tq + lax.broadcasted_iota(jnp.int32, (n_blk, tq), 1)
    cur = qpos // SLC_BLOCK
    valid = blk * SLC_BLOCK <= qpos
    forced = (blk == 0) | (blk == cur) | (blk == cur - 1)
    score = jnp.where(valid & forced, FORCE_SCORE, jnp.where(valid, imp.T[HALF:], -1.0))
    score = jnp.where(blk < nsel, score, REMOVED_SCORE)
    score_ref[...] = score
    rank = jnp.zeros((n_blk, tq), F32)
    for j in range(n_blk):
        sj = jnp.broadcast_to(score_ref[j:j + 1, :], (n_blk, tq))
        beats = (sj > score) | ((sj == score) & (blk > j))
        rank = rank + jnp.where(beats, 1.0, 0.0)
    unselected = jnp.where(rank < min(SLC_TOPN, nsel), 0.0, 1.0)
    unselected = jnp.concatenate([jnp.zeros((HALF, tq), F32), unselected], axis=0).T.astype(BF16)
    for h in range(NSA_HPG):
        q2_ref[0, h] = q_ref[0, h] + unselected


def _cmp_attention(q, ck, cv, bias, ov, s):
    bsz = q.shape[0]
    tq = NSA_TQ
    nqt = s // tq
    nh = s // CMP_STRIDE
    nsel = s // SLC_BLOCK
    return pl.pallas_call(
        functools.partial(_cmp_attn_kernel, tq=tq, nsel=nsel),
        grid=(NSA_GROUPS, nqt, bsz),
        in_specs=[pl.BlockSpec((1, NSA_HPG, tq, LANES), lambda g, t, b: (b, g, t, 0)),
                  pl.BlockSpec((1, 1, 2 * nh, LANES), lambda g, t, b: (b, g, 0, 0)),
                  pl.BlockSpec((1, 1, 2, 2 * nh, LANES), lambda g, t, b: (b, g, 0, 0, 0)),
                  pl.BlockSpec((1, 1, NSA_HPG, tq, nh), lambda g, t, b: (g, 0, 0, 0, 0)),
                  pl.BlockSpec((2 * nh, LANES), lambda g, t, b: (0, 0))],
        out_specs=[pl.BlockSpec((1, tq, NSA_HPG * NSA_DK), lambda g, t, b: (b, t, g)),
                   pl.BlockSpec((1, NSA_HPG, tq, LANES), lambda g, t, b: (b, g, t, 0))],
        out_shape=[jax.ShapeDtypeStruct((bsz, s, NSA_HEADS * NSA_DK), BF16),
                   jax.ShapeDtypeStruct(q.shape, BF16)],
        scratch_shapes=[pltpu.VMEM((LANES - HALF, tq), F32)],
        compiler_params=_cparams(("parallel", "parallel", "parallel")),
        name="nsa_cmp_attn",
    )(q, ck, cv, bias, ov)


def _nsa_window_kernel(q_ref, k_ref, v_ref, bias_ref, o_ref, s_ref, *, tq, window_steps):
    t = pl.program_id(2)
    ratio = NSA_TK // tq
    n_tiles = bias_ref.shape[1]
    half = NSA_HPG // 2
    last = t // ratio
    q = q_ref[0].reshape(NSA_HPG * tq, LANES)
    rows = []
    for i in range(window_steps):
        kc = last - (window_steps - 1) + i
        tile = jnp.where(kc < 0, n_tiles - 1, t - ratio * kc)
        rows.append(pl.ds(pl.multiple_of(jnp.maximum(kc, 0) * NSA_TK, NSA_TK), NSA_TK))
        s = _dot_t(q, k_ref[0, 0, rows[i], :]).reshape(NSA_HPG, tq, NSA_TK)
        s_ref[:, :, i * NSA_TK:(i + 1) * NSA_TK] = s + bias_ref[0, tile]
    accs = []
    for h in range(NSA_HPG):
        s = s_ref[h]
        p = jnp.exp2(s - jnp.max(s, axis=-1, keepdims=True)).astype(BF16)
        acc = None
        for i in range(window_steps):
            pv = _dot(p[:, i * NSA_TK:(i + 1) * NSA_TK], v_ref[0, 0, h // half, rows[i], :])
            acc = pv if acc is None else acc + pv
        accs.append(acc)
    for j in range(half):
        o = _normalized_pair(accs[j], accs[half + j])
        o_ref[0, :, j * LANES:(j + 1) * LANES] = o.astype(o_ref.dtype)


def _nsa_flash_kernel(q_ref, k_ref, v_ref, bias_ref, o_ref, s0_ref, s1_ref, m_ref, acc_ref, *, tq):
    t = pl.program_id(2)
    ratio = NSA_TK // tq
    n_tiles = bias_ref.shape[1]
    half = NSA_HPG // 2
    chunk = SEL_TILES * NSA_TK
    m_ref[...] = jnp.full_like(m_ref, MASK_VALUE)
    acc_ref[...] = jnp.zeros_like(acc_ref)
    last = (t * tq) // chunk

    def key_rows(kc):
        return pl.ds(pl.multiple_of(jnp.minimum(kc, last) * chunk, chunk), chunk)

    def scores(kc, buf):
        s = _dot_t(q_ref[0].reshape(NSA_HPG * tq, LANES), k_ref[0, 0, key_rows(kc), :])
        s = s.reshape(NSA_HPG, tq, chunk)
        for i in range(SEL_TILES):
            offset = t - ratio * (SEL_TILES * kc + i)
            tile = jnp.where((kc > last) | (offset < 0), n_tiles - 1, jnp.minimum(offset, n_tiles - 2))
            cols = slice(i * NSA_TK, (i + 1) * NSA_TK)
            buf[:, :, cols] = s[:, :, cols] + bias_ref[0, tile]

    def values(h, kc):
        return v_ref[0, 0, h // half, key_rows(kc), :]

    _flash_chunks(0, last, NSA_HPG, scores, values, (s0_ref, s1_ref), m_ref, acc_ref)
    for j in range(half):
        o = _normalized_pair(acc_ref[j], acc_ref[half + j])
        o_ref[0, :, j * LANES:(j + 1) * LANES] = o.astype(o_ref.dtype)


def _nsa_flash(q, k, v, bias, s, window_steps=None):
    bsz = q.shape[0]
    tq = NSA_TQ
    nqt = s // tq
    n_off = bias.shape[1]
    if window_steps is None:
        body = functools.partial(_nsa_flash_kernel, tq=tq)
        assert s % (SEL_TILES * NSA_TK) == 0
        s_buf = pltpu.VMEM((NSA_HPG, tq, SEL_TILES * NSA_TK), F32)
        scratch = [s_buf, s_buf, pltpu.VMEM((NSA_HPG, tq, LANES), F32), pltpu.VMEM((NSA_HPG, tq, LANES), F32)]
    else:
        body = functools.partial(_nsa_window_kernel, tq=tq, window_steps=window_steps)
        scratch = [pltpu.VMEM((NSA_HPG, tq, window_steps * NSA_TK), F32)]
    return pl.pallas_call(
        body,
        grid=(NSA_GROUPS, bsz, nqt),
        in_specs=[pl.BlockSpec((1, NSA_HPG, tq, LANES), lambda g, b, t: (b, g, t, 0)),
                  pl.BlockSpec((1, 1, s, LANES), lambda g, b, t: (b, g, 0, 0)),
                  pl.BlockSpec((1, 1, 2, s, LANES), lambda g, b, t: (b, g, 0, 0, 0)),
                  pl.BlockSpec((1, n_off, NSA_HPG, tq, NSA_TK), lambda g, b, t: (g, 0, 0, 0, 0))],
        out_specs=pl.BlockSpec((1, tq, NSA_HPG * NSA_DK), lambda g, b, t: (b, t, g)),
        out_shape=jax.ShapeDtypeStruct((bsz, s, NSA_HEADS * NSA_DK), BF16),
        scratch_shapes=scratch,
        compiler_params=_cparams(("parallel", "parallel", "parallel")),
        name="nsa_flash_sel" if window_steps is None else "nsa_flash_win",
    )(q, k, v, bias)


def _hybrid_layer(h, bsz, s, g_mix, g_mlp, w1, w2, w_in, conv_w, conv_b, wa, ba, wx, bx, lam,
                  q_norm, kv_norm, w_uq, w_ukv, qn_q, qn_k, w_out):
    d = h.shape[1]
    c = LRU_WIDTH
    o_kr = 2 * c + MLA_Q_RANK + MLA_KV_RANK
    w_kr = jnp.pad(w_in[:, o_kr:], ((0, 0), (MLA_NOPE, LANES - MLA_QK)))
    w_all = jnp.concatenate([w_in[:, :o_kr], w_kr], axis=1).astype(BF16)
    z_lru, q_lat, kv_lat, k_rope = _norm_proj(h, g_mix, w_all, (2 * c, MLA_Q_RANK, MLA_KV_RANK, LANES))
    lru = _rg_lru(z_lru.reshape(bsz, s, 2 * c), conv_w, conv_b, wa, ba, wx, bx, lam)
    q, k, v = _mla_prep(q_lat, kv_lat, k_rope, bsz, s, q_norm, kv_norm, w_uq, w_ukv, qn_q, qn_k)
    mla = _mla_attention(q, k, v)
    wo = w_out.astype(BF16)
    return _mix_mlp(h, [[lru.reshape(bsz * s, c)], [mla.reshape(bsz * s, -1)]], [wo[:c], wo[c:]],
                    g_mlp, w1.astype(BF16), w2.astype(BF16))


def _nsa_layer(h, bsz, s, g_mix, g_mlp, w1, w2, w_in, pos_k, w1_k, w2_k, pos_v, w1_v, w2_v,
               q_norm, k_norm, rel_bias, w_out):
    nq = NSA_HEADS * NSA_DK
    kvw = NSA_GROUPS * NSA_DK
    n_gate = 3 * NSA_HEADS
    w_gate = _pad_cols(w_in[:, nq + 6 * kvw:], LANES)
    w_all = jnp.concatenate([w_in[:, :nq + 6 * kvw], w_gate], axis=1).astype(BF16)
    z_q, z_kc, z_vc, z_kv, z_gate = _norm_proj(h, g_mix, w_all, (nq, kvw, kvw, 4 * kvw, LANES))

    q, ks, kw, vs, vw = _nsa_prep(z_q, z_kv, bsz, s, q_norm, k_norm)
    ck, cv = _compress(z_kc, z_vc, bsz, s, pos_k, w1_k, w2_k, pos_v, w1_v, w2_v, k_norm[0])

    tq = NSA_TQ
    nqt = s // tq
    order = np.array(HEAD_ORDER)
    expand = np.zeros((3, LANES, NSA_HEADS * NSA_DK), np.float32)
    for br in range(3):
        for hd in range(NSA_HEADS):
            expand[br, 3 * hd + br, hd * NSA_DK:(hd + 1) * NSA_DK] = 1.0

    slot_heads = (np.arange(NSA_GROUPS)[:, None] * NSA_HPG + order[None, :]).reshape(-1)
    tab_slots = rel_bias.T[slot_heads]
    nh = s // CMP_STRIDE
    nc = (s - CMP_BLOCK) // CMP_STRIDE + 1
    cmp_bias = _bias_tiles(tab_slots, _cmp_bias_ids(s))
    sel_bias = _bias_tiles(tab_slots, _toeplitz_bucket_ids(None)[0])
    win_ids, n_win_tiles = _toeplitz_bucket_ids(WINDOW)
    win_bias = _bias_tiles(tab_slots, win_ids)

    nsel = s // SLC_BLOCK
    cstart = np.arange(nh) * CMP_STRIDE
    sstart = np.arange(nsel) * SLC_BLOCK
    ov = np.clip(np.minimum(cstart[:, None] + CMP_BLOCK, sstart[None, :] + SLC_BLOCK)
                 - np.maximum(cstart[:, None], sstart[None, :]), 0, None).astype(np.float32) / CMP_BLOCK
    ov[nc:] = 0.0
    ov_p = np.zeros((nh, LANES), np.float32)
    ov_p[:, HALF:HALF + nsel] = ov

    o_c, q_sel = _cmp_attention(q, ck, cv, cmp_bias, jnp.asarray(np.concatenate([ov_p, ov_p])), s)
    o_s = _nsa_flash(q_sel, ks, vs, sel_bias, s)
    window_steps = (n_win_tiles + 1) // (NSA_TK // tq)
    o_w = _nsa_flash(q, kw, vw, win_bias, s, window_steps=window_steps)

    flat = lambda a: a.reshape(bsz * s, -1)
    return _mix_mlp(h, [[flat(o_c), flat(o_s), flat(o_w)]], [w_out.astype(BF16)],
                    g_mlp, w1.astype(BF16), w2.astype(BF16), gate=(z_gate, jnp.asarray(expand, BF16)))


def kernel(x, rel_bias, norm_mix, norm_mlp, mlp_w1, mlp_w2, hy_w_in, lru_conv_w, lru_conv_b, lru_wa, lru_ba, lru_wx, lru_bx, lru_lambda, mla_q_norm, mla_kv_norm, mla_w_uq, mla_w_ukv, mla_qn_q, mla_qn_k, hy_w_out, nsa_w_in, nsa_cmp_pos_k, nsa_cmp_w1_k, nsa_cmp_w2_k, nsa_cmp_pos_v, nsa_cmp_w1_v, nsa_cmp_w2_v, nsa_q_norm, nsa_k_norm, nsa_w_out):
    bsz, s, d = x.shape
    depth = norm_mix.shape[0]
    h = x.reshape(bsz * s, d)
    for layer in range(depth):
        if layer % 2 == 0:
            e = layer // 2
            h = _hybrid_layer(h, bsz, s, norm_mix[layer], norm_mlp[layer], mlp_w1[layer], mlp_w2[layer],
                              hy_w_in[e], lru_conv_w[e], lru_conv_b[e], lru_wa[e], lru_ba[e], lru_wx[e],
                              lru_bx[e], lru_lambda[e], mla_q_norm[e], mla_kv_norm[e], mla_w_uq[e],
                              mla_w_ukv[e], mla_qn_q[e], mla_qn_k[e], hy_w_out[e])
        else:
            o = layer // 2
            h = _nsa_layer(h, bsz, s, norm_mix[layer], norm_mlp[layer], mlp_w1[layer], mlp_w2[layer],
                           nsa_w_in[o], nsa_cmp_pos_k[o], nsa_cmp_w1_k[o], nsa_cmp_w2_k[o],
                           nsa_cmp_pos_v[o], nsa_cmp_w1_v[o], nsa_cmp_w2_v[o], nsa_q_norm[o],
                           nsa_k_norm[o], rel_bias, nsa_w_out[o])
    return h.reshape(bsz, s, d)
```

```python
import functools
import math

import numpy as np
import jax
import jax.numpy as jnp
from jax import lax
from jax.experimental import pallas as pl
from jax.experimental.pallas import tpu as pltpu

F32 = jnp.float32
BF16 = jnp.bfloat16

NORM_EPS = 1e-6
MASK_VALUE = -1e30
FORCE_SCORE = 1e4
REMOVED_SCORE = -3e38

NUM_BUCKETS = 32
MAX_DISTANCE = 1024
MASKED_ID = NUM_BUCKETS
EXCLUDED_ID = NUM_BUCKETS + 1
LOG2E = math.log2(math.e)

LRU_WIDTH = 512
LRU_BLOCKS = 8
CONV_WIDTH = 4
LRU_C = 8.0

MLA_HEADS = 8
MLA_NOPE = 64
MLA_ROPE = 32
MLA_QK = MLA_NOPE + MLA_ROPE
MLA_V = 64
MLA_Q_RANK = 384
MLA_KV_RANK = 256
ROPE_THETA = 10000.0

NSA_HEADS = 16
NSA_GROUPS = 2
NSA_HPG = NSA_HEADS // NSA_GROUPS
NSA_DK = 64
CMP_BLOCK = 32
CMP_STRIDE = 16
CMP_HIDDEN = 256
SLC_BLOCK = 64
SLC_TOPN = 8
WINDOW = 512

LANES = 128
HALF = LANES // 2
NSA_TQ = 128
NSA_TK = 256
SEL_TILES = 2
MLA_TR = 128
MLA_TK = 512
HEAD_ORDER = (0, 2, 4, 6, 1, 3, 5, 7)
VMEM_LIMIT = 56 * 1024 * 1024


def _cparams(sem):
    return pltpu.CompilerParams(dimension_semantics=sem, vmem_limit_bytes=VMEM_LIMIT)


def _rms(x, g, n=None):
    n = x.shape[-1] if n is None else n
    ms = jnp.sum(x * x, axis=-1, keepdims=True) * (1.0 / n)
    return x * lax.rsqrt(ms + NORM_EPS) * g


def _gelu_tanh(x):
    return 0.5 * x * (1.0 + jnp.tanh(math.sqrt(2.0 / math.pi) * (x + 0.044715 * (x * x * x))))


def _sigmoid(x):
    return 1.0 / (1.0 + jnp.exp(-x))


def _dot(a, b):
    return jnp.dot(a, b, preferred_element_type=F32)


def _dot_t(a, b):
    return lax.dot_general(a, b, (((1,), (1,)), ((), ())), preferred_element_type=F32)


def _causal_item_stream(n_tiles, last_chunk, n_chains, scores, values, write_out, s_bufs, m_ref, acc_ref):
    m_ref[...] = jnp.full_like(m_ref, MASK_VALUE)
    acc_ref[...] = jnp.zeros_like(acc_ref)
    n_items = sum(last_chunk(t) + 1 for t in range(n_tiles))

    def advance(t, kc):
        wrap = kc >= last_chunk(t)
        return jnp.where(wrap, t + 1, t), jnp.where(wrap, 0, kc + 1)

    def stage_scores(t, kc, buf):
        valid = t < n_tiles
        scores(jnp.minimum(t, n_tiles - 1), jnp.where(valid, kc, 0), valid, buf)

    def stage_update(t, kc, buf, maybe_padding):
        tile_done = kc >= last_chunk(t)
        kc = jnp.where(t < n_tiles, kc, 0) if maybe_padding else kc
        accs = []
        for c in range(n_chains):
            m_old = m_ref[c]
            m_new = jnp.maximum(m_old, jnp.max(buf[c], axis=-1, keepdims=True))
            alpha = jnp.exp2(m_old - m_new)
            s = buf[c]
            p = jnp.exp2(s - jnp.concatenate([m_new] * (s.shape[1] // LANES), axis=1))
            acc = acc_ref[c] * alpha + _dot(p.astype(BF16), values(c, kc))
            m_ref[c] = jnp.where(tile_done, MASK_VALUE, m_new)
            acc_ref[c] = jnp.where(tile_done, 0.0, acc)
            accs.append(acc)
        if maybe_padding:
            pl.when(t < n_tiles)(lambda: write_out(t, accs))
        else:
            write_out(t, accs)

    start = (jnp.int32(0), jnp.int32(0))
    stage_scores(*start, s_bufs[0])

    def two_items(_, item):
        item1 = advance(*item)
        stage_scores(*item1, s_bufs[1])
        stage_update(*item, s_bufs[0], False)
        item2 = advance(*item1)
        stage_scores(*item2, s_bufs[0])
        stage_update(*item1, s_bufs[1], n_items % 2 == 1)
        return item2

    lax.fori_loop(0, (n_items + 1) // 2, two_items, start)


def _normalized_pair(acc_even, acc_odd):
    lo = lax.broadcasted_iota(jnp.int32, acc_even.shape, 1) < HALF
    return jnp.where(lo, acc_even / acc_even[:, LANES - 1:LANES], acc_odd / acc_odd[:, 0:1])


def _with_ones_column(v, odd):
    lane = lax.broadcasted_iota(jnp.int32, v.shape, 1)
    return jnp.where(lane == (0 if odd else LANES - 1), 1.0, v)


def _norm_proj_kernel(x_ref, g_ref, w_ref, *out_refs, splits):
    xn = _rms(x_ref[...], g_ref[...]).astype(BF16)
    off = 0
    for o_ref, n in zip(out_refs, splits):
        o_ref[...] = _dot(xn, w_ref[:, off:off + n]).astype(o_ref.dtype)
        off += n


def _norm_proj(x, g, w, splits, tm=512):
    t, d = x.shape
    n = w.shape[1]
    assert sum(splits) == n and t % tm == 0
    return pl.pallas_call(
        functools.partial(_norm_proj_kernel, splits=splits),
        grid=(t // tm,),
        in_specs=[pl.BlockSpec((tm, d), lambda i: (i, 0)),
                  pl.BlockSpec((1, d), lambda i: (0, 0)),
                  pl.BlockSpec((d, n), lambda i: (0, 0))],
        out_specs=[pl.BlockSpec((tm, s), lambda i: (i, 0)) for s in splits],
        out_shape=[jax.ShapeDtypeStruct((t, s), F32) for s in splits],
        compiler_params=_cparams(("parallel",)),
        name="norm_proj",
    )(x, g.reshape(1, d), w)


def _mix_mlp_kernel(*refs, group_sizes, gated):
    n_a = sum(group_sizes)
    n_g = len(group_sizes)
    h_ref = refs[0]
    a_refs = refs[1:1 + n_a]
    wo_refs = refs[1 + n_a:1 + n_a + n_g]
    rest = refs[1 + n_a + n_g:]
    if gated:
        zg_ref, e_ref = rest[:2]
        rest = rest[2:]
    g_ref, w1_ref, w2_ref, out_ref, hres, xn, acc = rest
    f = pl.program_id(1)

    @pl.when(f == 0)
    def _():
        h1 = h_ref[...]
        if gated:
            gate = _sigmoid(zg_ref[...])
            g_hi = gate.astype(BF16)
            g_lo = (gate - g_hi.astype(F32)).astype(BF16)
        k = 0
        for gi, gs in enumerate(group_sizes):
            a = None
            for j in range(gs):
                aj = a_refs[k + j][...].astype(F32)
                if gated:
                    aj = aj * (_dot(g_hi, e_ref[j]) + _dot(g_lo, e_ref[j]))
                a = aj if a is None else a + aj
            k += gs
            h1 = h1 + _dot(a.astype(BF16), wo_refs[gi][...])
        hres[...] = h1
        xn[...] = _rms(h1, g_ref[...]).astype(BF16)
        acc[...] = jnp.zeros_like(acc)

    u = jnp.maximum(_dot(xn[...], w1_ref[...]), 0.0)
    acc[...] += _dot((u * u).astype(BF16), w2_ref[...])

    @pl.when(f == pl.num_programs(1) - 1)
    def _():
        out_ref[...] = hres[...] + acc[...]


def _mix_mlp(h, groups, wos, g, w1, w2, gate=None, tm=512, tf=1024):
    t, d = h.shape
    ff = w1.shape[1]
    group_sizes = tuple(len(gr) for gr in groups)
    a_list = [a for gr in groups for a in gr]
    extra = [] if gate is None else list(gate)
    in_specs = [pl.BlockSpec((tm, d), lambda i, f: (i, 0))]
    in_specs += [pl.BlockSpec((tm, a.shape[1]), lambda i, f: (i, 0)) for a in a_list]
    in_specs += [pl.BlockSpec(w.shape, lambda i, f: (0, 0)) for w in wos]
    if gate is not None:
        in_specs += [pl.BlockSpec((tm, LANES), lambda i, f: (i, 0)),
                     pl.BlockSpec(gate[1].shape, lambda i, f: (0, 0, 0))]
    in_specs += [pl.BlockSpec((1, d), lambda i, f: (0, 0)),
                 pl.BlockSpec((d, tf), lambda i, f: (0, f)),
                 pl.BlockSpec((tf, d), lambda i, f: (f, 0))]
    return pl.pallas_call(
        functools.partial(_mix_mlp_kernel, group_sizes=group_sizes, gated=gate is not None),
        grid=(t // tm, ff // tf),
        in_specs=in_specs,
        out_specs=pl.BlockSpec((tm, d), lambda i, f: (i, 0)),
        out_shape=jax.ShapeDtypeStruct((t, d), F32),
        scratch_shapes=[pltpu.VMEM((tm, d), F32), pltpu.VMEM((tm, d), BF16),
                        pltpu.VMEM((tm, d), F32)],
        compiler_params=_cparams(("parallel", "arbitrary")),
        name="mix_mlp",
    )(h, *a_list, *wos, *extra, g.reshape(1, d), w1, w2)


def _lru_kernel(xr_ref, xg_ref, cw_ref, cb_ref, wa_ref, ba_ref, wx_ref, bx_ref, lam_ref,
                o_ref, tail_ref, hc_ref, *, tc):
    c = pl.program_id(1)

    @pl.when(c == 0)
    def _():
        tail_ref[...] = jnp.zeros_like(tail_ref)
        hc_ref[...] = jnp.zeros_like(hc_ref)

    x = xr_ref[0]
    width = x.shape[1]
    row = lax.broadcasted_iota(jnp.int32, (tc, width), 0)
    row8 = lax.broadcasted_iota(jnp.int32, (8, width), 0)
    tail = tail_ref[...]
    cw = cw_ref[...]
    y = x * cw[CONV_WIDTH - 1:CONV_WIDTH]
    for s in range(1, CONV_WIDTH):
        sh = pltpu.roll(x, s, 0)
        head = jnp.where(row8 < s, pltpu.roll(tail, s, 0), sh[0:8])
        sh = jnp.concatenate([head, sh[8:]], axis=0)
        y = y + sh * cw[CONV_WIDTH - 1 - s:CONV_WIDTH - s]
    y = y + cb_ref[...]
    tail_ref[...] = x[tc - 8:tc]

    yb = y.astype(BF16)
    r = _sigmoid(_dot(yb, wa_ref[...]) + ba_ref[...])
    i = _sigmoid(_dot(yb, wx_ref[...]) + bx_ref[...])
    nl = -lam_ref[...]
    softplus = jnp.maximum(nl, 0.0) + jnp.log(1.0 + jnp.exp(-jnp.abs(nl)))
    log_a = (-LRU_C) * r * softplus
    a = jnp.exp(log_a)
    mult = jnp.sqrt(1.0 - jnp.exp(2.0 * log_a))
    mult = jnp.where((row == 0) & (c == 0), 1.0, mult)
    b = mult * (i * y)

    d = 1
    while d < tc:
        keep = row >= d
        a_sh = jnp.where(keep, pltpu.roll(a, d, 0), 1.0)
        b_sh = jnp.where(keep, pltpu.roll(b, d, 0), 0.0)
        b = a * b_sh + b
        a = a * a_sh
        d *= 2
    h = b + a * hc_ref[...]
    hc_ref[...] = h[tc - 1:tc]
    o_ref[0] = (h * _gelu_tanh(xg_ref[0])).astype(o_ref.dtype)


def _block_diag(w):
    n, d, e = w.shape
    eye = jnp.eye(n, dtype=w.dtype)
    return (w[:, :, None, :] * eye[:, None, :, None]).reshape(n * d, n * e)


def _rg_lru(z_lru, conv_w, conv_b, wa, ba, wx, bx, lam, tc=256):
    bsz, s, _ = z_lru.shape
    c = LRU_WIDTH
    tc = min(tc, s)
    vec = lambda v: pl.BlockSpec(v, lambda b, i: (0, 0))
    return pl.pallas_call(
        functools.partial(_lru_kernel, tc=tc),
        grid=(bsz, s // tc),
        in_specs=[pl.BlockSpec((1, tc, c), lambda b, i: (b, i, 0)),
                  pl.BlockSpec((1, tc, c), lambda b, i: (b, i, 1)),
                  vec((CONV_WIDTH, c)), vec((1, c)), vec((c, c)), vec((1, c)),
                  vec((c, c)), vec((1, c)), vec((1, c))],
        out_specs=pl.BlockSpec((1, tc, c), lambda b, i: (b, i, 0)),
        out_shape=jax.ShapeDtypeStruct((bsz, s, c), BF16),
        scratch_shapes=[pltpu.VMEM((8, c), F32), pltpu.VMEM((1, c), F32)],
        compiler_params=_cparams(("parallel", "arbitrary")),
        name="rg_lru",
    )(z_lru, z_lru, conv_w, conv_b.reshape(1, c), _block_diag(wa).astype(BF16), ba.reshape(1, c),
      _block_diag(wx).astype(BF16), bx.reshape(1, c), lam.reshape(1, c))


def _rope(x, cos, sin_lo, sin_hi):
    return (x * cos + pltpu.roll(x, LANES - MLA_ROPE // 2, 1) * sin_lo
            + pltpu.roll(x, MLA_ROPE // 2, 1) * sin_hi)


def _mla_prep_kernel(ql_ref, kvl_ref, kr_ref, gq_ref, gkv_ref, wq_ref, wk_ref, wv_ref,
                     nq_ref, nk_ref, cos_ref, s1_ref, s2_ref, q_out, k_out, v_out):
    qn = _rms(ql_ref[...], gq_ref[...]).astype(BF16)
    kvn = _rms(kvl_ref[...], gkv_ref[...]).astype(BF16)
    kr = kr_ref[...]
    cos, s1, s2 = cos_ref[...], s1_ref[...], s2_ref[...]
    scale = MLA_QK ** -0.5 * LOG2E
    for h in range(MLA_HEADS):
        sl = slice(h * LANES, (h + 1) * LANES)
        qh = _rms(_dot(qn, wq_ref[:, sl]), nq_ref[...], MLA_QK)
        q_out[0, h] = (_rope(qh, cos, s1, s2) * scale).astype(BF16)
        kh = _rms(_dot(kvn, wk_ref[:, sl]) + kr, nk_ref[...], MLA_QK)
        k_out[0, h] = _rope(kh, cos, s1, s2).astype(BF16)
        v_out[0, h] = _with_ones_column(_dot(kvn, wv_ref[:, sl]), h % 2 == 1).astype(BF16)


def _rope_tables(s):
    half = MLA_ROPE // 2
    freqs = ROPE_THETA ** (-jnp.arange(half, dtype=F32) / half)
    ang = jnp.arange(s, dtype=F32)[:, None] * freqs[None, :]
    cos, sin = jnp.cos(ang), jnp.sin(ang)
    z = lambda n: jnp.zeros((s, n), F32)
    cos_t = jnp.concatenate([jnp.ones((s, MLA_NOPE), F32), cos, cos, z(LANES - MLA_QK)], axis=1)
    s_lo = jnp.concatenate([z(MLA_NOPE), -sin, z(LANES - MLA_NOPE - half)], axis=1)
    s_hi = jnp.concatenate([z(MLA_NOPE + half), sin, z(LANES - MLA_QK)], axis=1)
    return cos_t, s_lo, s_hi


def _pad_cols(w, n):
    return jnp.pad(w, ((0, 0),) * (w.ndim - 1) + ((0, n - w.shape[-1]),))


def _mla_prep(q_lat, kv_lat, k_rope, bsz, s, q_norm, kv_norm, w_uq, w_ukv, qn_q, qn_k, tm=512):
    tm = min(tm, s)
    nt = s // tm
    hl = MLA_HEADS * LANES
    wq = _pad_cols(w_uq.reshape(MLA_Q_RANK, MLA_HEADS, MLA_QK), LANES).reshape(MLA_Q_RANK, hl)
    wkv = w_ukv.reshape(MLA_KV_RANK, MLA_HEADS, MLA_NOPE + MLA_V)
    wk = _pad_cols(wkv[..., :MLA_NOPE], LANES).reshape(MLA_KV_RANK, hl)
    wv = wkv[..., MLA_NOPE:]
    zero = jnp.zeros_like(wv)
    even = (jnp.arange(MLA_HEADS) % 2 == 0)[None, :, None]
    wv = jnp.concatenate([jnp.where(even, wv, zero), jnp.where(even, zero, wv)], axis=-1)
    wv = wv.reshape(MLA_KV_RANK, hl)
    cos_t, s_lo, s_hi = _rope_tables(s)
    row = lambda n: pl.BlockSpec((tm, n), lambda b, i: (b * nt + i, 0))
    full = lambda a: pl.BlockSpec(a.shape, lambda b, i: (0, 0))
    tab = pl.BlockSpec((tm, LANES), lambda b, i: (i, 0))
    head_out = pl.BlockSpec((1, MLA_HEADS, tm, LANES), lambda b, i: (b, 0, i, 0))
    args = (q_lat, kv_lat, k_rope, q_norm.reshape(1, -1), kv_norm.reshape(1, -1),
            wq.astype(BF16), wk.astype(BF16), wv.astype(BF16),
            _pad_cols(qn_q.reshape(1, -1), LANES), _pad_cols(qn_k.reshape(1, -1), LANES),
            cos_t, s_lo, s_hi)
    in_specs = [row(MLA_Q_RANK), row(MLA_KV_RANK), row(LANES)] + [full(a) for a in args[3:10]] + [tab] * 3
    shp = jax.ShapeDtypeStruct((bsz, MLA_HEADS, s, LANES), BF16)
    return pl.pallas_call(
        _mla_prep_kernel,
        grid=(bsz, nt),
        in_specs=in_specs,
        out_specs=[head_out] * 3,
        out_shape=[shp] * 3,
        compiler_params=_cparams(("parallel", "parallel")),
        name="mla_prep",
    )(*args)


def _mla_attn_kernel(q_ref, k_ref, v_ref, mask_ref, o_ref, s0_ref, s1_ref, m_ref, acc_ref, *, tq):
    n_rc = tq // MLA_TR
    ratio = MLA_TK // MLA_TR
    n_tiles = mask_ref.shape[0]

    def key_rows(kc):
        return pl.ds(pl.multiple_of(kc * MLA_TK, MLA_TK), MLA_TK)

    def scores(qi, kc, valid, buf):
        rows = pl.ds(pl.multiple_of(qi * tq, tq), tq)
        for j in range(2):
            s = _dot_t(q_ref[0, j, rows, :], k_ref[0, j, key_rows(kc), :])
            for rc in range(n_rc):
                offset = qi * n_rc + rc - ratio * kc
                tile = jnp.where(valid, jnp.clip(offset + 1, 0, n_tiles - 1), 0)
                buf[j * n_rc + rc] = s[rc * MLA_TR:(rc + 1) * MLA_TR] + mask_ref[tile]

    def values(c, kc):
        return v_ref[0, c // n_rc, key_rows(kc), :]

    def write_out(qi, accs):
        for rc in range(n_rc):
            rows = pl.ds(pl.multiple_of(qi * tq + rc * MLA_TR, MLA_TR), MLA_TR)
            o_ref[0, rows, :] = _normalized_pair(accs[rc], accs[n_rc + rc]).astype(o_ref.dtype)

    _causal_item_stream(q_ref.shape[2] // tq, lambda qi: (qi * tq + tq - 1) // MLA_TK, 2 * n_rc,
                        scores, values, write_out, (s0_ref, s1_ref), m_ref, acc_ref)


def _mla_causal_tiles():
    i = np.arange(MLA_TR)[:, None]
    j = np.arange(MLA_TK)[None, :]
    tiles = [np.full((MLA_TR, MLA_TK), -np.inf, np.float32)]
    for d in range(MLA_TK // MLA_TR):
        tiles.append(np.where(j <= i + d * MLA_TR, 0.0, -np.inf).astype(np.float32))
    tiles.append(np.zeros((MLA_TR, MLA_TK), np.float32))
    return np.stack(tiles)


def _mla_attention(q, k, v, tq=512):
    bsz, nh, s, _ = q.shape
    tq = min(tq, s)
    assert tq % MLA_TK == 0 and tq % MLA_TR == 0
    n_chains = 2 * (tq // MLA_TR)
    mask = _mla_causal_tiles()
    seq_spec = pl.BlockSpec((1, 2, s, LANES), lambda b, hp: (b, hp, 0, 0))
    return pl.pallas_call(
        functools.partial(_mla_attn_kernel, tq=tq),
        grid=(bsz, nh // 2),
        in_specs=[seq_spec, seq_spec, seq_spec, pl.BlockSpec(mask.shape, lambda b, hp: (0, 0, 0))],
        out_specs=pl.BlockSpec((1, s, LANES), lambda b, hp: (b, 0, hp)),
        out_shape=jax.ShapeDtypeStruct((bsz, s, nh * MLA_V), BF16),
        scratch_shapes=[pltpu.VMEM((n_chains, MLA_TR, MLA_TK), F32), pltpu.VMEM((n_chains, MLA_TR, MLA_TK), F32),
                        pltpu.VMEM((n_chains, MLA_TR, LANES), F32), pltpu.VMEM((n_chains, MLA_TR, LANES), F32)],
        compiler_params=_cparams(("parallel", "parallel")),
        name="mla_attn",
    )(q, k, v, jnp.asarray(mask))


def _t5_bucket_np(dist):
    n = np.maximum(dist, 0)
    max_exact = NUM_BUCKETS // 2
    nf = np.maximum(n, 1).astype(np.float32)
    large = max_exact + (np.log(nf / np.float32(max_exact)) / np.float32(math.log(MAX_DISTANCE / max_exact))
                         * np.float32(NUM_BUCKETS - max_exact)).astype(np.int32)
    large = np.minimum(large, NUM_BUCKETS - 1)
    return np.where(n < max_exact, n, large).astype(np.int32)


def _bias_tiles_kernel(tab_ref, idx_ref, o_ref, *, buckets):
    slot = pl.program_id(0) * NSA_HPG + pl.program_id(1)
    for i, present in enumerate(buckets):
        idx = idx_ref[i]
        acc = jnp.where(idx == EXCLUDED_ID, -jnp.inf, MASK_VALUE).astype(F32)
        for b in present:
            acc = jnp.where(idx == b, tab_ref[slot, b] * LOG2E, acc)
        o_ref[0, i, 0] = acc


def _bias_tiles(tab_slots, idx):
    n, r, c = idx.shape
    buckets = tuple(tuple(int(b) for b in np.unique(tile) if b < NUM_BUCKETS) for tile in idx)
    return pl.pallas_call(
        functools.partial(_bias_tiles_kernel, buckets=buckets),
        grid=(NSA_GROUPS, NSA_HPG),
        in_specs=[pl.BlockSpec(memory_space=pltpu.SMEM),
                  pl.BlockSpec((n, r, c), lambda g, p: (0, 0, 0))],
        out_specs=pl.BlockSpec((1, n, 1, r, c), lambda g, p: (g, 0, p, 0, 0)),
        out_shape=jax.ShapeDtypeStruct((NSA_GROUPS, n, NSA_HPG, r, c), F32),
        compiler_params=_cparams(("parallel", "parallel")),
        name="bias_tiles",
    )(tab_slots, jnp.asarray(idx))


def _cmp_bias_ids(s):
    nqt = s // NSA_TQ
    nh = s // CMP_STRIDE
    per_tile = NSA_TQ // CMP_STRIDE
    i = np.arange(NSA_TQ)[:, None]
    c = np.arange(nh)[None, :]
    dist = i - (CMP_BLOCK - 1) - CMP_STRIDE * (c - per_tile * (nqt - 1))
    ids = np.where(dist >= 0, _t5_bucket_np(dist), MASKED_ID).astype(np.int32)
    return ids[None]


def _toeplitz_bucket_ids(window):
    i = np.arange(NSA_TQ)[:, None]
    j = np.arange(NSA_TK)[None, :]
    tiles = []
    d = 0
    while True:
        dist = d * NSA_TQ + i - j
        ids = _t5_bucket_np(dist)
        ok = dist >= 0
        if window is not None:
            ok &= dist < window
        ids = np.where(ok, ids, MASKED_ID).astype(np.int32)
        tiles.append(ids)
        if window is None and (ids == NUM_BUCKETS - 1).all():
            break
        if window is not None and not ok.any():
            tiles.pop()
            break
        d += 1
    n_real = len(tiles)
    tiles.append(np.full((NSA_TQ, NSA_TK), EXCLUDED_ID, np.int32))
    return np.stack(tiles), n_real


def _split_halves(x):
    lo = lax.broadcasted_iota(jnp.int32, x.shape, 1) < HALF
    return jnp.where(lo, x, 0.0), pltpu.roll(jnp.where(lo, 0.0, x), HALF, 1)


def _value_layouts(x):
    lo = lax.broadcasted_iota(jnp.int32, x.shape, 1) < HALF
    g0e = jnp.where(lo, x, 0.0)
    g1o = jnp.where(lo, 0.0, x)
    pair = lambda e, o: (_with_ones_column(e, False), _with_ones_column(o, True))
    return pair(g0e, pltpu.roll(g0e, HALF, 1)), pair(pltpu.roll(g1o, HALF, 1), g1o)


def _nsa_prep_kernel(zq_ref, ks_ref, vs_ref, kw_ref, vw_ref, gq_ref, gks_ref, gkw_ref,
                     q_out, ks_out, kw_out, vs_out, vw_out, *, tm):
    i = pl.program_id(1)
    scale = NSA_DK ** -0.5 * LOG2E
    for j in range(NSA_HEADS // 2):
        e, o = _split_halves(zq_ref[:, j * LANES:(j + 1) * LANES])
        g = (2 * j) // NSA_HPG
        pe = (2 * j - NSA_HPG * g) // 2
        q_out[0, NSA_HPG * g + pe] = (_rms(e, gq_ref[...], NSA_DK) * scale).astype(BF16)
        q_out[0, NSA_HPG * g + NSA_HPG // 2 + pe] = (_rms(o, gq_ref[...], NSA_DK) * scale).astype(BF16)

    lane = lax.broadcasted_iota(jnp.int32, (tm, LANES), 1)
    pos = i * tm + lax.broadcasted_iota(jnp.int32, (tm, LANES), 0)
    block_tag = jnp.where(lane - HALF == pos // SLC_BLOCK, MASK_VALUE, 0.0)
    for g, x in enumerate(_split_halves(ks_ref[...])):
        ks_out[0, g] = (_rms(x, gks_ref[...], NSA_DK) + block_tag).astype(BF16)
    for g, x in enumerate(_split_halves(kw_ref[...])):
        kw_out[0, g] = _rms(x, gkw_ref[...], NSA_DK).astype(BF16)
    for ref, out in ((vs_ref, vs_out), (vw_ref, vw_out)):
        for g, (ve, vo) in enumerate(_value_layouts(ref[...])):
            out[0, g, 0] = ve.astype(BF16)
            out[0, g, 1] = vo.astype(BF16)


def _nsa_prep(z_q, z_kv, bsz, s, q_norm, k_norm, tm=512):
    tm = min(tm, s)
    nt = s // tm
    row = lambda n, c: pl.BlockSpec((tm, n), lambda b, i: (b * nt + i, c))
    gain = pl.BlockSpec((1, LANES), lambda b, i: (0, 0))
    kspec = pl.BlockSpec((1, NSA_GROUPS, tm, LANES), lambda b, i: (b, 0, i, 0))
    vspec = pl.BlockSpec((1, NSA_GROUPS, 2, tm, LANES), lambda b, i: (b, 0, 0, i, 0))
    kshape = jax.ShapeDtypeStruct((bsz, NSA_GROUPS, s, LANES), BF16)
    vshape = jax.ShapeDtypeStruct((bsz, NSA_GROUPS, 2, s, LANES), BF16)
    pad = lambda v: _pad_cols(v.reshape(1, -1), LANES)
    return pl.pallas_call(
        functools.partial(_nsa_prep_kernel, tm=tm),
        grid=(bsz, nt),
        in_specs=[row(NSA_HEADS * NSA_DK, 0), row(LANES, 0), row(LANES, 1), row(LANES, 2), row(LANES, 3),
                  gain, gain, gain],
        out_specs=[pl.BlockSpec((1, NSA_HEADS, tm, LANES), lambda b, i: (b, 0, i, 0)),
                   kspec, kspec, vspec, vspec],
        out_shape=[jax.ShapeDtypeStruct((bsz, NSA_HEADS, s, LANES), BF16), kshape, kshape, vshape, vshape],
        compiler_params=_cparams(("parallel", "parallel")),
        name="nsa_prep",
    )(z_q, z_kv, z_kv, z_kv, z_kv, pad(q_norm), pad(k_norm[1]), pad(k_norm[2]))


def _compress_kernel(xk_ref, xv_ref, pk_ref, pv_ref, w1k_ref, w1v_ref, w2k_ref, w2v_ref, gk_ref,
                     ck_out, cv_out):
    def mlp(x, p_ref, w1_ref, w2_ref):
        nh = x.shape[0]
        top = _dot((x + p_ref[0:1]).astype(BF16), w1_ref[0])
        bot = _dot((x + p_ref[1:2]).astype(BF16), w1_ref[1])
        hid = _gelu_tanh(top + pltpu.roll(bot, nh - 1, 0))
        return _dot(hid.astype(BF16), w2_ref[...])

    nh = xk_ref.shape[1]
    yk = mlp(xk_ref[0], pk_ref, w1k_ref, w2k_ref)
    for g, x in enumerate(_split_halves(yk)):
        ck = _rms(x, gk_ref[...], NSA_DK)
        ck_out[0, g, 0:nh] = ck
        ck_out[0, g, nh:2 * nh] = ck
    yv = mlp(xv_ref[0], pv_ref, w1v_ref, w2v_ref)
    for g, layouts in enumerate(_value_layouts(yv)):
        for parity, v in enumerate(layouts):
            cv_out[0, g, parity, 0:nh] = v
            cv_out[0, g, parity, nh:2 * nh] = v


def _compress_weights(pos, w1, w2):
    half = CMP_BLOCK // 2
    g = NSA_GROUPS
    eye = jnp.eye(g, dtype=F32)
    w1 = w1.reshape(2, half, NSA_DK, CMP_HIDDEN)
    w1 = w1[:, :, None, :, None, :] * eye[None, None, :, None, :, None]
    w1 = w1.reshape(2, half * g * NSA_DK, g * CMP_HIDDEN).astype(BF16)
    p = jnp.broadcast_to(pos.reshape(2, half, 1, NSA_DK), (2, half, g, NSA_DK)).reshape(2, half * g * NSA_DK)
    w2 = (w2[None, :, None, :] * eye[:, None, :, None]).reshape(g * CMP_HIDDEN, g * NSA_DK).astype(BF16)
    return p, w1, w2


def _compress(z_kc, z_vc, bsz, s, pos_k, w1_k, w2_k, pos_v, w1_v, w2_v, k_norm0):
    nh = s // CMP_STRIDE
    feat = CMP_STRIDE * LANES
    pk, w1k, w2k = _compress_weights(pos_k, w1_k, w2_k)
    pv, w1v, w2v = _compress_weights(pos_v, w1_v, w2_v)
    full = lambda a: pl.BlockSpec(a.shape, lambda b: (0,) * a.ndim)
    xspec = pl.BlockSpec((1, nh, feat), lambda b: (b, 0, 0))
    gk = _pad_cols(k_norm0.reshape(1, -1), LANES)
    return pl.pallas_call(
        _compress_kernel,
        grid=(bsz,),
        in_specs=[xspec, xspec, full(pk), full(pv), full(w1k), full(w1v), full(w2k), full(w2v), full(gk)],
        out_specs=[pl.BlockSpec((1, NSA_GROUPS, 2 * nh, LANES), lambda b: (b, 0, 0, 0)),
                   pl.BlockSpec((1, NSA_GROUPS, 2, 2 * nh, LANES), lambda b: (b, 0, 0, 0, 0))],
        out_shape=[jax.ShapeDtypeStruct((bsz, NSA_GROUPS, 2 * nh, LANES), F32),
                   jax.ShapeDtypeStruct((bsz, NSA_GROUPS, 2, 2 * nh, LANES), F32)],
        compiler_params=_cparams(("parallel",)),
        name="nsa_compress",
    )(z_kc.reshape(bsz, nh, feat), z_vc.reshape(bsz, nh, feat), pk, pv, w1k, w1v, w2k, w2v, gk)


def _cmp_attn_kernel(q_ref, ck_ref, cv_ref, bias_ref, ov_ref, o_ref, q2_ref, score_ref, *, tq, nsel):
    t = pl.program_id(1)
    half = NSA_HPG // 2
    nh = bias_ref.shape[4]
    qrow = t * tq + lax.broadcasted_iota(jnp.int32, (tq, nh), 0)
    row_ok = qrow >= CMP_BLOCK - 1
    per_tile = tq // CMP_STRIDE
    c_last = per_tile * (pl.num_programs(1) - 1)
    start = lax.rem(per_tile * t + (nh - lax.rem(c_last, nh)), nh)
    cols = pl.ds(pl.multiple_of(start, per_tile), nh)
    col = lax.broadcasted_iota(jnp.int32, (nh, LANES), 0)
    lane_k = lax.broadcasted_iota(jnp.int32, (nh, LANES), 1)
    wrapped = (lane_k == HALF) & (col < c_last - per_tile * t)
    ck = jnp.where(wrapped, MASK_VALUE, ck_ref[0, 0, cols, :]).astype(BF16)
    cv = [cv_ref[0, 0, parity, cols, :].astype(BF16) for parity in range(2)]
    ov = ov_ref[cols, :].astype(BF16)
    lane_q = lax.broadcasted_iota(jnp.int32, (tq, LANES), 1)
    mask_lane = jnp.where(lane_q == HALF, 1.0, 0.0).astype(BF16)
    outs = []
    psum = None
    for h in range(NSA_HPG):
        s = _dot_t(q_ref[0, h] + mask_lane, ck) + bias_ref[0, 0, h]
        e = jnp.exp2(s - jnp.max(s, axis=-1, keepdims=True))
        p = jnp.where(row_ok, e / jnp.sum(e, axis=-1, keepdims=True), 0.0)
        outs.append(_dot(p.astype(BF16), cv[h // half]))
        psum = p if psum is None else psum + p
    for j in range(half):
        o = jnp.where(lane_q < HALF, outs[j], outs[half + j])
        o_ref[0, :, j * LANES:(j + 1) * LANES] = o.astype(o_ref.dtype)

    hi = psum.astype(BF16)
    lo = (psum - hi.astype(F32)).astype(BF16)
    imp = _dot(hi, ov) + _dot(lo, ov)

    n_blk = LANES - HALF
    blk = lax.broadcasted_iota(jnp.int32, (n_blk, tq), 0)
    qpos = t * tq + lax.broadcasted_iota(jnp.int32, (n_blk, tq), 1)
    cur = qpos // SLC_BLOCK
    valid = blk * SLC_BLOCK <= qpos
    forced = (blk == 0) | (blk == cur) | (blk == cur - 1)
    score = jnp.where(valid & forced, FORCE_SCORE, jnp.where(valid, imp.T[HALF:], -1.0))
    score = jnp.where(blk < nsel, score, REMOVED_SCORE)
    score_ref[...] = score
    rank = jnp.zeros((n_blk, tq), F32)
    for j in range(n_blk):
        sj = jnp.broadcast_to(score_ref[j:j + 1, :], (n_blk, tq))
        beats = (sj > score) | ((sj == score) & (blk > j))
        rank = rank + jnp.where(beats, 1.0, 0.0)
    unselected = jnp.where(rank < min(SLC_TOPN, nsel), 0.0, 1.0)
    unselected = jnp.concatenate([jnp.zeros((HALF, tq), F32), unselected], axis=0).T.astype(BF16)
    for h in range(NSA_HPG):
        q2_ref[0, h] = q_ref[0, h] + unselected


def _cmp_attention(q, ck, cv, bias, ov, s):
    bsz = q.shape[0]
    tq = NSA_TQ
    nqt = s // tq
    nh = s // CMP_STRIDE
    nsel = s // SLC_BLOCK
    return pl.pallas_call(
        functools.partial(_cmp_attn_kernel, tq=tq, nsel=nsel),
        grid=(NSA_GROUPS, nqt, bsz),
        in_specs=[pl.BlockSpec((1, NSA_HPG, tq, LANES), lambda g, t, b: (b, g, t, 0)),
                  pl.BlockSpec((1, 1, 2 * nh, LANES), lambda g, t, b: (b, g, 0, 0)),
                  pl.BlockSpec((1, 1, 2, 2 * nh, LANES), lambda g, t, b: (b, g, 0, 0, 0)),
                  pl.BlockSpec((1, 1, NSA_HPG, tq, nh), lambda g, t, b: (g, 0, 0, 0, 0)),
                  pl.BlockSpec((2 * nh, LANES), lambda g, t, b: (0, 0))],
        out_specs=[pl.BlockSpec((1, tq, NSA_HPG * NSA_DK), lambda g, t, b: (b, t, g)),
                   pl.BlockSpec((1, NSA_HPG, tq, LANES), lambda g, t, b: (b, g, t, 0))],
        out_shape=[jax.ShapeDtypeStruct((bsz, s, NSA_HEADS * NSA_DK), BF16),
                   jax.ShapeDtypeStruct(q.shape, BF16)],
        scratch_shapes=[pltpu.VMEM((LANES - HALF, tq), F32)],
        compiler_params=_cparams(("parallel", "parallel", "parallel")),
        name="nsa_cmp_attn",
    )(q, ck, cv, bias, ov)


def _nsa_window_kernel(q_ref, k_ref, v_ref, bias_ref, o_ref, s_ref, *, tq, window_steps):
    t = pl.program_id(2)
    ratio = NSA_TK // tq
    n_tiles = bias_ref.shape[1]
    half = NSA_HPG // 2
    last = t // ratio
    q = q_ref[0].reshape(NSA_HPG * tq, LANES)
    rows = []
    for i in range(window_steps):
        kc = last - (window_steps - 1) + i
        tile = jnp.where(kc < 0, n_tiles - 1, t - ratio * kc)
        rows.append(pl.ds(pl.multiple_of(jnp.maximum(kc, 0) * NSA_TK, NSA_TK), NSA_TK))
        s = _dot_t(q, k_ref[0, 0, rows[i], :]).reshape(NSA_HPG, tq, NSA_TK)
        s_ref[:, :, i * NSA_TK:(i + 1) * NSA_TK] = s + bias_ref[0, tile]
    accs = []
    for h in range(NSA_HPG):
        s = s_ref[h]
        p = jnp.exp2(s - jnp.max(s, axis=-1, keepdims=True)).astype(BF16)
        acc = None
        for i in range(window_steps):
            pv = _dot(p[:, i * NSA_TK:(i + 1) * NSA_TK], v_ref[0, 0, h // half, rows[i], :])
            acc = pv if acc is None else acc + pv
        accs.append(acc)
    for j in range(half):
        o = _normalized_pair(accs[j], accs[half + j])
        o_ref[0, :, j * LANES:(j + 1) * LANES] = o.astype(o_ref.dtype)


def _nsa_selected_kernel(q_ref, k_ref, v_ref, bias_ref, o_ref, s0_ref, s1_ref, m_ref, acc_ref, *, tq):
    nqt = q_ref.shape[2] // tq
    ratio = NSA_TK // tq
    n_tiles = bias_ref.shape[1]
    half = NSA_HPG // 2
    chunk = SEL_TILES * NSA_TK
    per = chunk // tq

    def key_rows(kc):
        return pl.ds(pl.multiple_of(kc * chunk, chunk), chunk)

    def scores(t, kc, valid, buf):
        q = q_ref[0, :, pl.ds(pl.multiple_of(t * tq, tq), tq), :].reshape(NSA_HPG * tq, LANES)
        s = _dot_t(q, k_ref[0, 0, key_rows(kc), :]).reshape(NSA_HPG, tq, chunk)
        for i in range(SEL_TILES):
            offset = t - ratio * (SEL_TILES * kc + i)
            tile = jnp.where(valid & (offset >= 0), jnp.minimum(offset, n_tiles - 2), n_tiles - 1)
            cols = slice(i * NSA_TK, (i + 1) * NSA_TK)
            buf[:, :, cols] = s[:, :, cols] + bias_ref[0, tile]

    def values(h, kc):
        return v_ref[0, 0, h // half, key_rows(kc), :]

    def write_out(t, accs):
        rows = pl.ds(pl.multiple_of(t * tq, tq), tq)
        for j in range(half):
            o = _normalized_pair(accs[j], accs[half + j])
            o_ref[0, rows, j * LANES:(j + 1) * LANES] = o.astype(o_ref.dtype)

    _causal_item_stream(nqt, lambda t: t // per, NSA_HPG, scores, values, write_out,
                        (s0_ref, s1_ref), m_ref, acc_ref)


def _nsa_selected(q, k, v, bias, s):
    bsz = q.shape[0]
    tq = NSA_TQ
    n_off = bias.shape[1]
    assert s % (SEL_TILES * NSA_TK) == 0
    s_buf = pltpu.VMEM((NSA_HPG, tq, SEL_TILES * NSA_TK), F32)
    once = pl.Buffered(1)
    return pl.pallas_call(
        functools.partial(_nsa_selected_kernel, tq=tq),
        grid=(NSA_GROUPS, bsz),
        in_specs=[pl.BlockSpec((1, NSA_HPG, s, LANES), lambda g, b: (b, g, 0, 0), pipeline_mode=once),
                  pl.BlockSpec((1, 1, s, LANES), lambda g, b: (b, g, 0, 0)),
                  pl.BlockSpec((1, 1, 2, s, LANES), lambda g, b: (b, g, 0, 0, 0)),
                  pl.BlockSpec((1, n_off, NSA_HPG, tq, NSA_TK), lambda g, b: (g, 0, 0, 0, 0), pipeline_mode=once)],
        out_specs=pl.BlockSpec((1, s, NSA_HPG * NSA_DK), lambda g, b: (b, 0, g)),
        out_shape=jax.ShapeDtypeStruct((bsz, s, NSA_HEADS * NSA_DK), BF16),
        scratch_shapes=[s_buf, s_buf, pltpu.VMEM((NSA_HPG, tq, LANES), F32), pltpu.VMEM((NSA_HPG, tq, LANES), F32)],
        compiler_params=_cparams(("parallel", "parallel")),
        name="nsa_flash_sel",
    )(q, k, v, bias)


def _nsa_window(q, k, v, bias, s, window_steps):
    bsz = q.shape[0]
    tq = NSA_TQ
    nqt = s // tq
    n_off = bias.shape[1]
    return pl.pallas_call(
        functools.partial(_nsa_window_kernel, tq=tq, window_steps=window_steps),
        grid=(NSA_GROUPS, bsz, nqt),
        in_specs=[pl.BlockSpec((1, NSA_HPG, tq, LANES), lambda g, b, t: (b, g, t, 0)),
                  pl.BlockSpec((1, 1, s, LANES), lambda g, b, t: (b, g, 0, 0)),
                  pl.BlockSpec((1, 1, 2, s, LANES), lambda g, b, t: (b, g, 0, 0, 0)),
                  pl.BlockSpec((1, n_off, NSA_HPG, tq, NSA_TK), lambda g, b, t: (g, 0, 0, 0, 0))],
        out_specs=pl.BlockSpec((1, tq, NSA_HPG * NSA_DK), lambda g, b, t: (b, t, g)),
        out_shape=jax.ShapeDtypeStruct((bsz, s, NSA_HEADS * NSA_DK), BF16),
        scratch_shapes=[pltpu.VMEM((NSA_HPG, tq, window_steps * NSA_TK), F32)],
        compiler_params=_cparams(("parallel", "parallel", "parallel")),
        name="nsa_flash_win",
    )(q, k, v, bias)


def _hybrid_layer(h, bsz, s, g_mix, g_mlp, w1, w2, w_in, conv_w, conv_b, wa, ba, wx, bx, lam,
                  q_norm, kv_norm, w_uq, w_ukv, qn_q, qn_k, w_out):
    d = h.shape[1]
    c = LRU_WIDTH
    o_kr = 2 * c + MLA_Q_RANK + MLA_KV_RANK
    w_kr = jnp.pad(w_in[:, o_kr:], ((0, 0), (MLA_NOPE, LANES - MLA_QK)))
    w_all = jnp.concatenate([w_in[:, :o_kr], w_kr], axis=1).astype(BF16)
    z_lru, q_lat, kv_lat, k_rope = _norm_proj(h, g_mix, w_all, (2 * c, MLA_Q_RANK, MLA_KV_RANK, LANES))
    lru = _rg_lru(z_lru.reshape(bsz, s, 2 * c), conv_w, conv_b, wa, ba, wx, bx, lam)
    q, k, v = _mla_prep(q_lat, kv_lat, k_rope, bsz, s, q_norm, kv_norm, w_uq, w_ukv, qn_q, qn_k)
    mla = _mla_attention(q, k, v)
    wo = w_out.astype(BF16)
    return _mix_mlp(h, [[lru.reshape(bsz * s, c)], [mla.reshape(bsz * s, -1)]], [wo[:c], wo[c:]],
                    g_mlp, w1.astype(BF16), w2.astype(BF16))


def _nsa_layer(h, bsz, s, g_mix, g_mlp, w1, w2, w_in, pos_k, w1_k, w2_k, pos_v, w1_v, w2_v,
               q_norm, k_norm, rel_bias, w_out):
    nq = NSA_HEADS * NSA_DK
    kvw = NSA_GROUPS * NSA_DK
    n_gate = 3 * NSA_HEADS
    w_gate = _pad_cols(w_in[:, nq + 6 * kvw:], LANES)
    w_all = jnp.concatenate([w_in[:, :nq + 6 * kvw], w_gate], axis=1).astype(BF16)
    z_q, z_kc, z_vc, z_kv, z_gate = _norm_proj(h, g_mix, w_all, (nq, kvw, kvw, 4 * kvw, LANES))

    q, ks, kw, vs, vw = _nsa_prep(z_q, z_kv, bsz, s, q_norm, k_norm)
    ck, cv = _compress(z_kc, z_vc, bsz, s, pos_k, w1_k, w2_k, pos_v, w1_v, w2_v, k_norm[0])

    tq = NSA_TQ
    nqt = s // tq
    order = np.array(HEAD_ORDER)
    expand = np.zeros((3, LANES, NSA_HEADS * NSA_DK), np.float32)
    for br in range(3):
        for hd in range(NSA_HEADS):
            expand[br, 3 * hd + br, hd * NSA_DK:(hd + 1) * NSA_DK] = 1.0

    slot_heads = (np.arange(NSA_GROUPS)[:, None] * NSA_HPG + order[None, :]).reshape(-1)
    tab_slots = rel_bias.T[slot_heads]
    nh = s // CMP_STRIDE
    nc = (s - CMP_BLOCK) // CMP_STRIDE + 1
    cmp_bias = _bias_tiles(tab_slots, _cmp_bias_ids(s))
    sel_bias = _bias_tiles(tab_slots, _toeplitz_bucket_ids(None)[0])
    win_ids, n_win_tiles = _toeplitz_bucket_ids(WINDOW)
    win_bias = _bias_tiles(tab_slots, win_ids)

    nsel = s // SLC_BLOCK
    cstart = np.arange(nh) * CMP_STRIDE
    sstart = np.arange(nsel) * SLC_BLOCK
    ov = np.clip(np.minimum(cstart[:, None] + CMP_BLOCK, sstart[None, :] + SLC_BLOCK)
                 - np.maximum(cstart[:, None], sstart[None, :]), 0, None).astype(np.float32) / CMP_BLOCK
    ov[nc:] = 0.0
    ov_p = np.zeros((nh, LANES), np.float32)
    ov_p[:, HALF:HALF + nsel] = ov

    o_c, q_sel = _cmp_attention(q, ck, cv, cmp_bias, jnp.asarray(np.concatenate([ov_p, ov_p])), s)
    o_s = _nsa_selected(q_sel, ks, vs, sel_bias, s)
    window_steps = (n_win_tiles + 1) // (NSA_TK // tq)
    o_w = _nsa_window(q, kw, vw, win_bias, s, window_steps)

    flat = lambda a: a.reshape(bsz * s, -1)
    return _mix_mlp(h, [[flat(o_c), flat(o_s), flat(o_w)]], [w_out.astype(BF16)],
                    g_mlp, w1.astype(BF16), w2.astype(BF16), gate=(z_gate, jnp.asarray(expand, BF16)))


def kernel(x, rel_bias, norm_mix, norm_mlp, mlp_w1, mlp_w2, hy_w_in, lru_conv_w, lru_conv_b, lru_wa, lru_ba, lru_wx, lru_bx, lru_lambda, mla_q_norm, mla_kv_norm, mla_w_uq, mla_w_ukv, mla_qn_q, mla_qn_k, hy_w_out, nsa_w_in, nsa_cmp_pos_k, nsa_cmp_w1_k, nsa_cmp_w2_k, nsa_cmp_pos_v, nsa_cmp_w1_v, nsa_cmp_w2_v, nsa_q_norm, nsa_k_norm, nsa_w_out):
    bsz, s, d = x.shape
    depth = norm_mix.shape[0]
    h = x.reshape(bsz * s, d)
    for layer in range(depth):
        if layer % 2 == 0:
            e = layer // 2
            h = _hybrid_layer(h, bsz, s, norm_mix[layer], norm_mlp[layer], mlp_w1[layer], mlp_w2[layer],
                              hy_w_in[e], lru_conv_w[e], lru_conv_b[e], lru_wa[e], lru_ba[e], lru_wx[e],
                              lru_bx[e], lru_lambda[e], mla_q_norm[e], mla_kv_norm[e], mla_w_uq[e],
                              mla_w_ukv[e], mla_qn_q[e], mla_qn_k[e], hy_w_out[e])
        else:
            o = layer // 2
            h = _nsa_layer(h, bsz, s, norm_mix[layer], norm_mlp[layer], mlp_w1[layer], mlp_w2[layer],
                           nsa_w_in[o], nsa_cmp_pos_k[o], nsa_cmp_w1_k[o], nsa_cmp_w2_k[o],
                           nsa_cmp_pos_v[o], nsa_cmp_w1_v[o], nsa_cmp_w2_v[o], nsa_q_norm[o],
                           nsa_k_norm[o], rel_bias, nsa_w_out[o])
    return h.reshape(bsz, s, d)
```

```python
import functools
import math

import numpy as np
import jax
import jax.numpy as jnp
from jax import lax
from jax.experimental import pallas as pl
from jax.experimental.pallas import tpu as pltpu

F32 = jnp.float32
BF16 = jnp.bfloat16

NORM_EPS = 1e-6
MASK_VALUE = -1e30
FORCE_SCORE = 1e4
REMOVED_SCORE = -3e38

NUM_BUCKETS = 32
MAX_DISTANCE = 1024
MASKED_ID = NUM_BUCKETS
EXCLUDED_ID = NUM_BUCKETS + 1
LOG2E = math.log2(math.e)

LRU_WIDTH = 512
LRU_BLOCKS = 8
CONV_WIDTH = 4
LRU_C = 8.0

MLA_HEADS = 8
MLA_NOPE = 64
MLA_ROPE = 32
MLA_QK = MLA_NOPE + MLA_ROPE
MLA_V = 64
MLA_Q_RANK = 384
MLA_KV_RANK = 256
ROPE_THETA = 10000.0

NSA_HEADS = 16
NSA_GROUPS = 2
NSA_HPG = NSA_HEADS // NSA_GROUPS
NSA_DK = 64
CMP_BLOCK = 32
CMP_STRIDE = 16
CMP_HIDDEN = 256
SLC_BLOCK = 64
SLC_TOPN = 8
WINDOW = 512

LANES = 128
HALF = LANES // 2
NSA_TQ = 128
NSA_TK = 256
SEL_TILES = 2
MLA_TR = 128
MLA_TK = 512
HEAD_ORDER = (0, 2, 4, 6, 1, 3, 5, 7)
VMEM_LIMIT = 56 * 1024 * 1024


def _cparams(sem):
    return pltpu.CompilerParams(dimension_semantics=sem, vmem_limit_bytes=VMEM_LIMIT)


def _rms(x, g, n=None):
    n = x.shape[-1] if n is None else n
    ms = jnp.sum(x * x, axis=-1, keepdims=True) * (1.0 / n)
    return x * lax.rsqrt(ms + NORM_EPS) * g


def _gelu_tanh(x):
    return 0.5 * x * (1.0 + jnp.tanh(math.sqrt(2.0 / math.pi) * (x + 0.044715 * (x * x * x))))


def _sigmoid(x):
    return 1.0 / (1.0 + jnp.exp(-x))


def _dot(a, b):
    return jnp.dot(a, b, preferred_element_type=F32)


def _dot_t(a, b):
    return lax.dot_general(a, b, (((1,), (1,)), ((), ())), preferred_element_type=F32)


def _causal_item_stream(n_tiles, last_chunk, n_chains, scores, values, write_out, s_bufs, m_ref, acc_ref):
    m_ref[...] = jnp.full_like(m_ref, MASK_VALUE)
    acc_ref[...] = jnp.zeros_like(acc_ref)
    n_items = sum(last_chunk(t) + 1 for t in range(n_tiles))

    def advance(t, kc):
        wrap = kc >= last_chunk(t)
        return jnp.where(wrap, t + 1, t), jnp.where(wrap, 0, kc + 1)

    def stage_scores(t, kc, buf):
        valid = t < n_tiles
        scores(jnp.minimum(t, n_tiles - 1), jnp.where(valid, kc, 0), valid, buf)

    def stage_update(t, kc, buf, maybe_padding):
        tile_done = kc >= last_chunk(t)
        kc = jnp.where(t < n_tiles, kc, 0) if maybe_padding else kc
        accs = []
        for c in range(n_chains):
            m_old = m_ref[c]
            m_new = jnp.maximum(m_old, jnp.max(buf[c], axis=-1, keepdims=True))
            alpha = jnp.exp2(m_old - m_new)
            s = buf[c]
            p = jnp.exp2(s - jnp.concatenate([m_new] * (s.shape[1] // LANES), axis=1))
            acc = acc_ref[c] * alpha + _dot(p.astype(BF16), values(c, kc))
            m_ref[c] = jnp.where(tile_done, MASK_VALUE, m_new)
            acc_ref[c] = jnp.where(tile_done, 0.0, acc)
            accs.append(acc)
        if maybe_padding:
            pl.when(t < n_tiles)(lambda: write_out(t, accs))
        else:
            write_out(t, accs)

    start = (jnp.int32(0), jnp.int32(0))
    stage_scores(*start, s_bufs[0])

    def two_items(_, item):
        item1 = advance(*item)
        stage_scores(*item1, s_bufs[1])
        stage_update(*item, s_bufs[0], False)
        item2 = advance(*item1)
        stage_scores(*item2, s_bufs[0])
        stage_update(*item1, s_bufs[1], n_items % 2 == 1)
        return item2

    lax.fori_loop(0, (n_items + 1) // 2, two_items, start)


def _normalized_pair(acc_even, acc_odd):
    lo = lax.broadcasted_iota(jnp.int32, acc_even.shape, 1) < HALF
    return jnp.where(lo, acc_even / acc_even[:, LANES - 1:LANES], acc_odd / acc_odd[:, 0:1])


def _with_ones_column(v, odd):
    lane = lax.broadcasted_iota(jnp.int32, v.shape, 1)
    return jnp.where(lane == (0 if odd else LANES - 1), 1.0, v)


def _norm_proj_kernel(x_ref, g_ref, w_ref, *out_refs, splits):
    xn = _rms(x_ref[...], g_ref[...]).astype(BF16)
    off = 0
    for o_ref, n in zip(out_refs, splits):
        o_ref[...] = _dot(xn, w_ref[:, off:off + n]).astype(o_ref.dtype)
        off += n


def _norm_proj(x, g, w, splits, tm=512):
    t, d = x.shape
    n = w.shape[1]
    assert sum(splits) == n and t % tm == 0
    return pl.pallas_call(
        functools.partial(_norm_proj_kernel, splits=splits),
        grid=(t // tm,),
        in_specs=[pl.BlockSpec((tm, d), lambda i: (i, 0)),
                  pl.BlockSpec((1, d), lambda i: (0, 0)),
                  pl.BlockSpec((d, n), lambda i: (0, 0))],
        out_specs=[pl.BlockSpec((tm, s), lambda i: (i, 0)) for s in splits],
        out_shape=[jax.ShapeDtypeStruct((t, s), F32) for s in splits],
        compiler_params=_cparams(("parallel",)),
        name="norm_proj",
    )(x, g.reshape(1, d), w)


def _mix_mlp_kernel(*refs, group_sizes, gated):
    n_a = sum(group_sizes)
    n_g = len(group_sizes)
    h_ref = refs[0]
    a_refs = refs[1:1 + n_a]
    wo_refs = refs[1 + n_a:1 + n_a + n_g]
    rest = refs[1 + n_a + n_g:]
    if gated:
        zg_ref, e_ref = rest[:2]
        rest = rest[2:]
    g_ref, w1_ref, w2_ref, out_ref, hres, xn, acc = rest
    f = pl.program_id(1)

    @pl.when(f == 0)
    def _():
        h1 = h_ref[...]
        if gated:
            gate = _sigmoid(zg_ref[...])
            g_hi = gate.astype(BF16)
            g_lo = (gate - g_hi.astype(F32)).astype(BF16)
        k = 0
        for gi, gs in enumerate(group_sizes):
            a = None
            for j in range(gs):
                aj = a_refs[k + j][...].astype(F32)
                if gated:
                    aj = aj * (_dot(g_hi, e_ref[j]) + _dot(g_lo, e_ref[j]))
                a = aj if a is None else a + aj
            k += gs
            h1 = h1 + _dot(a.astype(BF16), wo_refs[gi][...])
        hres[...] = h1
        xn[...] = _rms(h1, g_ref[...]).astype(BF16)
        acc[...] = jnp.zeros_like(acc)

    u = jnp.maximum(_dot(xn[...], w1_ref[...]), 0.0)
    acc[...] += _dot((u * u).astype(BF16), w2_ref[...])

    @pl.when(f == pl.num_programs(1) - 1)
    def _():
        out_ref[...] = hres[...] + acc[...]


def _mix_mlp(h, groups, wos, g, w1, w2, layer, gate=None, tm=512, tf=1024):
    t, d = h.shape
    ff = w1.shape[2]
    group_sizes = tuple(len(gr) for gr in groups)
    a_list = [a for gr in groups for a in gr]
    extra = [] if gate is None else list(gate)
    in_specs = [pl.BlockSpec((tm, d), lambda i, f: (i, 0))]
    in_specs += [pl.BlockSpec((tm, a.shape[1]), lambda i, f: (i, 0)) for a in a_list]
    in_specs += [pl.BlockSpec(w.shape, lambda i, f: (0, 0)) for w in wos]
    if gate is not None:
        in_specs += [pl.BlockSpec((tm, LANES), lambda i, f: (i, 0)),
                     pl.BlockSpec(gate[1].shape, lambda i, f: (0, 0, 0))]
    in_specs += [pl.BlockSpec((1, d), lambda i, f: (0, 0)),
                 pl.BlockSpec((None, d, tf), lambda i, f: (layer, 0, f)),
                 pl.BlockSpec((None, tf, d), lambda i, f: (layer, f, 0))]
    return pl.pallas_call(
        functools.partial(_mix_mlp_kernel, group_sizes=group_sizes, gated=gate is not None),
        grid=(t // tm, ff // tf),
        in_specs=in_specs,
        out_specs=pl.BlockSpec((tm, d), lambda i, f: (i, 0)),
        out_shape=jax.ShapeDtypeStruct((t, d), F32),
        scratch_shapes=[pltpu.VMEM((tm, d), F32), pltpu.VMEM((tm, d), BF16),
                        pltpu.VMEM((tm, d), F32)],
        compiler_params=_cparams(("parallel", "arbitrary")),
        name="mix_mlp",
    )(h, *a_list, *wos, *extra, g.reshape(1, d), w1, w2)


def _lru_kernel(xr_ref, xg_ref, cw_ref, cb_ref, wa_ref, ba_ref, wx_ref, bx_ref, lam_ref,
                o_ref, tail_ref, hc_ref, *, tc):
    c = pl.program_id(1)

    @pl.when(c == 0)
    def _():
        tail_ref[...] = jnp.zeros_like(tail_ref)
        hc_ref[...] = jnp.zeros_like(hc_ref)

    x = xr_ref[0]
    width = x.shape[1]
    row = lax.broadcasted_iota(jnp.int32, (tc, width), 0)
    row8 = lax.broadcasted_iota(jnp.int32, (8, width), 0)
    tail = tail_ref[...]
    cw = cw_ref[...]
    y = x * cw[CONV_WIDTH - 1:CONV_WIDTH]
    for s in range(1, CONV_WIDTH):
        sh = pltpu.roll(x, s, 0)
        head = jnp.where(row8 < s, pltpu.roll(tail, s, 0), sh[0:8])
        sh = jnp.concatenate([head, sh[8:]], axis=0)
        y = y + sh * cw[CONV_WIDTH - 1 - s:CONV_WIDTH - s]
    y = y + cb_ref[...]
    tail_ref[...] = x[tc - 8:tc]

    yb = y.astype(BF16)
    r = _sigmoid(_dot(yb, wa_ref[...]) + ba_ref[...])
    i = _sigmoid(_dot(yb, wx_ref[...]) + bx_ref[...])
    nl = -lam_ref[...]
    softplus = jnp.maximum(nl, 0.0) + jnp.log(1.0 + jnp.exp(-jnp.abs(nl)))
    log_a = (-LRU_C) * r * softplus
    a = jnp.exp(log_a)
    mult = jnp.sqrt(1.0 - jnp.exp(2.0 * log_a))
    mult = jnp.where((row == 0) & (c == 0), 1.0, mult)
    b = mult * (i * y)

    d = 1
    while d < tc:
        keep = row >= d
        a_sh = jnp.where(keep, pltpu.roll(a, d, 0), 1.0)
        b_sh = jnp.where(keep, pltpu.roll(b, d, 0), 0.0)
        b = a * b_sh + b
        a = a * a_sh
        d *= 2
    h = b + a * hc_ref[...]
    hc_ref[...] = h[tc - 1:tc]
    o_ref[0] = (h * _gelu_tanh(xg_ref[0])).astype(o_ref.dtype)


def _block_diag(w):
    n, d, e = w.shape
    eye = jnp.eye(n, dtype=w.dtype)
    return (w[:, :, None, :] * eye[:, None, :, None]).reshape(n * d, n * e)


def _rg_lru(z_lru, conv_w, conv_b, wa, ba, wx, bx, lam, tc=256):
    bsz, s, _ = z_lru.shape
    c = LRU_WIDTH
    tc = min(tc, s)
    vec = lambda v: pl.BlockSpec(v, lambda b, i: (0, 0))
    return pl.pallas_call(
        functools.partial(_lru_kernel, tc=tc),
        grid=(bsz, s // tc),
        in_specs=[pl.BlockSpec((1, tc, c), lambda b, i: (b, i, 0)),
                  pl.BlockSpec((1, tc, c), lambda b, i: (b, i, 1)),
                  vec((CONV_WIDTH, c)), vec((1, c)), vec((c, c)), vec((1, c)),
                  vec((c, c)), vec((1, c)), vec((1, c))],
        out_specs=pl.BlockSpec((1, tc, c), lambda b, i: (b, i, 0)),
        out_shape=jax.ShapeDtypeStruct((bsz, s, c), BF16),
        scratch_shapes=[pltpu.VMEM((8, c), F32), pltpu.VMEM((1, c), F32)],
        compiler_params=_cparams(("parallel", "arbitrary")),
        name="rg_lru",
    )(z_lru, z_lru, conv_w, conv_b.reshape(1, c), _block_diag(wa).astype(BF16), ba.reshape(1, c),
      _block_diag(wx).astype(BF16), bx.reshape(1, c), lam.reshape(1, c))


def _rope(x, cos, sin_lo, sin_hi):
    return (x * cos + pltpu.roll(x, LANES - MLA_ROPE // 2, 1) * sin_lo
            + pltpu.roll(x, MLA_ROPE // 2, 1) * sin_hi)


def _mla_prep_kernel(ql_ref, kvl_ref, kr_ref, gq_ref, gkv_ref, wq_ref, wk_ref, wv_ref,
                     nq_ref, nk_ref, cos_ref, s1_ref, s2_ref, q_out, k_out, v_out):
    qn = _rms(ql_ref[...], gq_ref[...]).astype(BF16)
    kvn = _rms(kvl_ref[...], gkv_ref[...]).astype(BF16)
    kr = kr_ref[...]
    cos, s1, s2 = cos_ref[...], s1_ref[...], s2_ref[...]
    scale = MLA_QK ** -0.5 * LOG2E
    for h in range(MLA_HEADS):
        sl = slice(h * LANES, (h + 1) * LANES)
        qh = _rms(_dot(qn, wq_ref[:, sl]), nq_ref[...], MLA_QK)
        q_out[0, h] = (_rope(qh, cos, s1, s2) * scale).astype(BF16)
        kh = _rms(_dot(kvn, wk_ref[:, sl]) + kr, nk_ref[...], MLA_QK)
        k_out[0, h] = _rope(kh, cos, s1, s2).astype(BF16)
        v_out[0, h] = _with_ones_column(_dot(kvn, wv_ref[:, sl]), h % 2 == 1).astype(BF16)


def _rope_tables(s):
    half = MLA_ROPE // 2
    freqs = ROPE_THETA ** (-jnp.arange(half, dtype=F32) / half)
    ang = jnp.arange(s, dtype=F32)[:, None] * freqs[None, :]
    cos, sin = jnp.cos(ang), jnp.sin(ang)
    z = lambda n: jnp.zeros((s, n), F32)
    cos_t = jnp.concatenate([jnp.ones((s, MLA_NOPE), F32), cos, cos, z(LANES - MLA_QK)], axis=1)
    s_lo = jnp.concatenate([z(MLA_NOPE), -sin, z(LANES - MLA_NOPE - half)], axis=1)
    s_hi = jnp.concatenate([z(MLA_NOPE + half), sin, z(LANES - MLA_QK)], axis=1)
    return cos_t, s_lo, s_hi


def _pad_cols(w, n):
    return jnp.pad(w, ((0, 0),) * (w.ndim - 1) + ((0, n - w.shape[-1]),))


def _mla_prep(q_lat, kv_lat, k_rope, bsz, s, q_norm, kv_norm, w_uq, w_ukv, qn_q, qn_k, tm=512):
    tm = min(tm, s)
    nt = s // tm
    hl = MLA_HEADS * LANES
    wq = _pad_cols(w_uq.reshape(MLA_Q_RANK, MLA_HEADS, MLA_QK), LANES).reshape(MLA_Q_RANK, hl)
    wkv = w_ukv.reshape(MLA_KV_RANK, MLA_HEADS, MLA_NOPE + MLA_V)
    wk = _pad_cols(wkv[..., :MLA_NOPE], LANES).reshape(MLA_KV_RANK, hl)
    wv = wkv[..., MLA_NOPE:]
    zero = jnp.zeros_like(wv)
    even = (jnp.arange(MLA_HEADS) % 2 == 0)[None, :, None]
    wv = jnp.concatenate([jnp.where(even, wv, zero), jnp.where(even, zero, wv)], axis=-1)
    wv = wv.reshape(MLA_KV_RANK, hl)
    cos_t, s_lo, s_hi = _rope_tables(s)
    row = lambda n: pl.BlockSpec((tm, n), lambda b, i: (b * nt + i, 0))
    full = lambda a: pl.BlockSpec(a.shape, lambda b, i: (0, 0))
    tab = pl.BlockSpec((tm, LANES), lambda b, i: (i, 0))
    head_out = pl.BlockSpec((1, MLA_HEADS, tm, LANES), lambda b, i: (b, 0, i, 0))
    args = (q_lat, kv_lat, k_rope, q_norm.reshape(1, -1), kv_norm.reshape(1, -1),
            wq.astype(BF16), wk.astype(BF16), wv.astype(BF16),
            _pad_cols(qn_q.reshape(1, -1), LANES), _pad_cols(qn_k.reshape(1, -1), LANES),
            cos_t, s_lo, s_hi)
    in_specs = [row(MLA_Q_RANK), row(MLA_KV_RANK), row(LANES)] + [full(a) for a in args[3:10]] + [tab] * 3
    shp = jax.ShapeDtypeStruct((bsz, MLA_HEADS, s, LANES), BF16)
    return pl.pallas_call(
        _mla_prep_kernel,
        grid=(bsz, nt),
        in_specs=in_specs,
        out_specs=[head_out] * 3,
        out_shape=[shp] * 3,
        compiler_params=_cparams(("parallel", "parallel")),
        name="mla_prep",
    )(*args)


def _mla_attn_kernel(q_ref, k_ref, v_ref, mask_ref, o_ref, s0_ref, s1_ref, m_ref, acc_ref, *, tq):
    n_rc = tq // MLA_TR
    ratio = MLA_TK // MLA_TR
    n_tiles = mask_ref.shape[0]

    def key_rows(kc):
        return pl.ds(pl.multiple_of(kc * MLA_TK, MLA_TK), MLA_TK)

    def scores(qi, kc, valid, buf):
        rows = pl.ds(pl.multiple_of(qi * tq, tq), tq)
        for j in range(2):
            s = _dot_t(q_ref[0, j, rows, :], k_ref[0, j, key_rows(kc), :])
            for rc in range(n_rc):
                offset = qi * n_rc + rc - ratio * kc
                tile = jnp.where(valid, jnp.clip(offset + 1, 0, n_tiles - 1), 0)
                buf[j * n_rc + rc] = s[rc * MLA_TR:(rc + 1) * MLA_TR] + mask_ref[tile]

    def values(c, kc):
        return v_ref[0, c // n_rc, key_rows(kc), :]

    def write_out(qi, accs):
        for rc in range(n_rc):
            rows = pl.ds(pl.multiple_of(qi * tq + rc * MLA_TR, MLA_TR), MLA_TR)
            o_ref[0, rows, :] = _normalized_pair(accs[rc], accs[n_rc + rc]).astype(o_ref.dtype)

    _causal_item_stream(q_ref.shape[2] // tq, lambda qi: (qi * tq + tq - 1) // MLA_TK, 2 * n_rc,
                        scores, values, write_out, (s0_ref, s1_ref), m_ref, acc_ref)


def _mla_causal_tiles():
    i = np.arange(MLA_TR)[:, None]
    j = np.arange(MLA_TK)[None, :]
    tiles = [np.full((MLA_TR, MLA_TK), -np.inf, np.float32)]
    for d in range(MLA_TK // MLA_TR):
        tiles.append(np.where(j <= i + d * MLA_TR, 0.0, -np.inf).astype(np.float32))
    tiles.append(np.zeros((MLA_TR, MLA_TK), np.float32))
    return np.stack(tiles)


def _mla_attention(q, k, v, tq=512):
    bsz, nh, s, _ = q.shape
    tq = min(tq, s)
    assert tq % MLA_TK == 0 and tq % MLA_TR == 0
    n_chains = 2 * (tq // MLA_TR)
    mask = _mla_causal_tiles()
    seq_spec = pl.BlockSpec((1, 2, s, LANES), lambda b, hp: (b, hp, 0, 0))
    return pl.pallas_call(
        functools.partial(_mla_attn_kernel, tq=tq),
        grid=(bsz, nh // 2),
        in_specs=[seq_spec, seq_spec, seq_spec, pl.BlockSpec(mask.shape, lambda b, hp: (0, 0, 0))],
        out_specs=pl.BlockSpec((1, s, LANES), lambda b, hp: (b, 0, hp)),
        out_shape=jax.ShapeDtypeStruct((bsz, s, nh * MLA_V), BF16),
        scratch_shapes=[pltpu.VMEM((n_chains, MLA_TR, MLA_TK), F32), pltpu.VMEM((n_chains, MLA_TR, MLA_TK), F32),
                        pltpu.VMEM((n_chains, MLA_TR, LANES), F32), pltpu.VMEM((n_chains, MLA_TR, LANES), F32)],
        compiler_params=_cparams(("parallel", "parallel")),
        name="mla_attn",
    )(q, k, v, jnp.asarray(mask))


def _t5_bucket_np(dist):
    n = np.maximum(dist, 0)
    max_exact = NUM_BUCKETS // 2
    nf = np.maximum(n, 1).astype(np.float32)
    large = max_exact + (np.log(nf / np.float32(max_exact)) / np.float32(math.log(MAX_DISTANCE / max_exact))
                         * np.float32(NUM_BUCKETS - max_exact)).astype(np.int32)
    large = np.minimum(large, NUM_BUCKETS - 1)
    return np.where(n < max_exact, n, large).astype(np.int32)


def _bias_tiles_kernel(tab_ref, idx_ref, o_ref, *, buckets):
    slot = pl.program_id(0) * NSA_HPG + pl.program_id(1)
    for i, present in enumerate(buckets):
        idx = idx_ref[i]
        acc = jnp.where(idx == EXCLUDED_ID, -jnp.inf, MASK_VALUE).astype(F32)
        for b in present:
            acc = jnp.where(idx == b, tab_ref[slot, b] * LOG2E, acc)
        o_ref[0, i, 0] = acc


def _bias_tiles(tab_slots, idx):
    n, r, c = idx.shape
    buckets = tuple(tuple(int(b) for b in np.unique(tile) if b < NUM_BUCKETS) for tile in idx)
    return pl.pallas_call(
        functools.partial(_bias_tiles_kernel, buckets=buckets),
        grid=(NSA_GROUPS, NSA_HPG),
        in_specs=[pl.BlockSpec(memory_space=pltpu.SMEM),
                  pl.BlockSpec((n, r, c), lambda g, p: (0, 0, 0))],
        out_specs=pl.BlockSpec((1, n, 1, r, c), lambda g, p: (g, 0, p, 0, 0)),
        out_shape=jax.ShapeDtypeStruct((NSA_GROUPS, n, NSA_HPG, r, c), F32),
        compiler_params=_cparams(("parallel", "parallel")),
        name="bias_tiles",
    )(tab_slots, jnp.asarray(idx))


def _cmp_bias_ids(s):
    nqt = s // NSA_TQ
    nh = s // CMP_STRIDE
    per_tile = NSA_TQ // CMP_STRIDE
    i = np.arange(NSA_TQ)[:, None]
    c = np.arange(nh)[None, :]
    dist = i - (CMP_BLOCK - 1) - CMP_STRIDE * (c - per_tile * (nqt - 1))
    ids = np.where(dist >= 0, _t5_bucket_np(dist), MASKED_ID).astype(np.int32)
    return ids[None]


def _toeplitz_bucket_ids(window):
    i = np.arange(NSA_TQ)[:, None]
    j = np.arange(NSA_TK)[None, :]
    tiles = []
    d = 0
    while True:
        dist = d * NSA_TQ + i - j
        ids = _t5_bucket_np(dist)
        ok = dist >= 0
        if window is not None:
            ok &= dist < window
        ids = np.where(ok, ids, MASKED_ID).astype(np.int32)
        tiles.append(ids)
        if window is None and (ids == NUM_BUCKETS - 1).all():
            break
        if window is not None and not ok.any():
            tiles.pop()
            break
        d += 1
    n_real = len(tiles)
    tiles.append(np.full((NSA_TQ, NSA_TK), EXCLUDED_ID, np.int32))
    return np.stack(tiles), n_real


def _split_halves(x):
    lo = lax.broadcasted_iota(jnp.int32, x.shape, 1) < HALF
    return jnp.where(lo, x, 0.0), pltpu.roll(jnp.where(lo, 0.0, x), HALF, 1)


def _value_layouts(x):
    lo = lax.broadcasted_iota(jnp.int32, x.shape, 1) < HALF
    g0e = jnp.where(lo, x, 0.0)
    g1o = jnp.where(lo, 0.0, x)
    pair = lambda e, o: (_with_ones_column(e, False), _with_ones_column(o, True))
    return pair(g0e, pltpu.roll(g0e, HALF, 1)), pair(pltpu.roll(g1o, HALF, 1), g1o)


def _nsa_prep_kernel(zq_ref, ks_ref, vs_ref, kw_ref, vw_ref, gq_ref, gks_ref, gkw_ref,
                     q_out, ks_out, kw_out, vs_out, vw_out, *, tm):
    i = pl.program_id(1)
    scale = NSA_DK ** -0.5 * LOG2E
    for j in range(NSA_HEADS // 2):
        e, o = _split_halves(zq_ref[:, j * LANES:(j + 1) * LANES])
        g = (2 * j) // NSA_HPG
        pe = (2 * j - NSA_HPG * g) // 2
        q_out[0, NSA_HPG * g + pe] = (_rms(e, gq_ref[...], NSA_DK) * scale).astype(BF16)
        q_out[0, NSA_HPG * g + NSA_HPG // 2 + pe] = (_rms(o, gq_ref[...], NSA_DK) * scale).astype(BF16)

    lane = lax.broadcasted_iota(jnp.int32, (tm, LANES), 1)
    pos = i * tm + lax.broadcasted_iota(jnp.int32, (tm, LANES), 0)
    block_tag = jnp.where(lane - HALF == pos // SLC_BLOCK, MASK_VALUE, 0.0)
    for g, x in enumerate(_split_halves(ks_ref[...])):
        ks_out[0, g] = (_rms(x, gks_ref[...], NSA_DK) + block_tag).astype(BF16)
    for g, x in enumerate(_split_halves(kw_ref[...])):
        kw_out[0, g] = _rms(x, gkw_ref[...], NSA_DK).astype(BF16)
    for ref, out in ((vs_ref, vs_out), (vw_ref, vw_out)):
        for g, (ve, vo) in enumerate(_value_layouts(ref[...])):
            out[0, g, 0] = ve.astype(BF16)
            out[0, g, 1] = vo.astype(BF16)


def _nsa_prep(z_q, z_kv, bsz, s, q_norm, k_norm, tm=512):
    tm = min(tm, s)
    nt = s // tm
    row = lambda n, c: pl.BlockSpec((tm, n), lambda b, i: (b * nt + i, c))
    gain = pl.BlockSpec((1, LANES), lambda b, i: (0, 0))
    kspec = pl.BlockSpec((1, NSA_GROUPS, tm, LANES), lambda b, i: (b, 0, i, 0))
    vspec = pl.BlockSpec((1, NSA_GROUPS, 2, tm, LANES), lambda b, i: (b, 0, 0, i, 0))
    kshape = jax.ShapeDtypeStruct((bsz, NSA_GROUPS, s, LANES), BF16)
    vshape = jax.ShapeDtypeStruct((bsz, NSA_GROUPS, 2, s, LANES), BF16)
    pad = lambda v: _pad_cols(v.reshape(1, -1), LANES)
    return pl.pallas_call(
        functools.partial(_nsa_prep_kernel, tm=tm),
        grid=(bsz, nt),
        in_specs=[row(NSA_HEADS * NSA_DK, 0), row(LANES, 0), row(LANES, 1), row(LANES, 2), row(LANES, 3),
                  gain, gain, gain],
        out_specs=[pl.BlockSpec((1, NSA_HEADS, tm, LANES), lambda b, i: (b, 0, i, 0)),
                   kspec, kspec, vspec, vspec],
        out_shape=[jax.ShapeDtypeStruct((bsz, NSA_HEADS, s, LANES), BF16), kshape, kshape, vshape, vshape],
        compiler_params=_cparams(("parallel", "parallel")),
        name="nsa_prep",
    )(z_q, z_kv, z_kv, z_kv, z_kv, pad(q_norm), pad(k_norm[1]), pad(k_norm[2]))


def _compress_kernel(xk_ref, xv_ref, pk_ref, pv_ref, w1k_ref, w1v_ref, w2k_ref, w2v_ref, gk_ref,
                     ck_out, cv_out):
    def mlp(x, p_ref, w1_ref, w2_ref):
        nh = x.shape[0]
        top = _dot((x + p_ref[0:1]).astype(BF16), w1_ref[0])
        bot = _dot((x + p_ref[1:2]).astype(BF16), w1_ref[1])
        hid = _gelu_tanh(top + pltpu.roll(bot, nh - 1, 0))
        return _dot(hid.astype(BF16), w2_ref[...])

    nh = xk_ref.shape[1]
    yk = mlp(xk_ref[0], pk_ref, w1k_ref, w2k_ref)
    for g, x in enumerate(_split_halves(yk)):
        ck = _rms(x, gk_ref[...], NSA_DK)
        ck_out[0, g, 0:nh] = ck
        ck_out[0, g, nh:2 * nh] = ck
    yv = mlp(xv_ref[0], pv_ref, w1v_ref, w2v_ref)
    for g, layouts in enumerate(_value_layouts(yv)):
        for parity, v in enumerate(layouts):
            cv_out[0, g, parity, 0:nh] = v
            cv_out[0, g, parity, nh:2 * nh] = v


def _compress_weights(pos, w1, w2):
    half = CMP_BLOCK // 2
    g = NSA_GROUPS
    eye = jnp.eye(g, dtype=F32)
    w1 = w1.reshape(2, half, NSA_DK, CMP_HIDDEN)
    w1 = w1[:, :, None, :, None, :] * eye[None, None, :, None, :, None]
    w1 = w1.reshape(2, half * g * NSA_DK, g * CMP_HIDDEN).astype(BF16)
    p = jnp.broadcast_to(pos.reshape(2, half, 1, NSA_DK), (2, half, g, NSA_DK)).reshape(2, half * g * NSA_DK)
    w2 = (w2[None, :, None, :] * eye[:, None, :, None]).reshape(g * CMP_HIDDEN, g * NSA_DK).astype(BF16)
    return p, w1, w2


def _compress(z_kc, z_vc, bsz, s, pos_k, w1_k, w2_k, pos_v, w1_v, w2_v, k_norm0):
    nh = s // CMP_STRIDE
    feat = CMP_STRIDE * LANES
    pk, w1k, w2k = _compress_weights(pos_k, w1_k, w2_k)
    pv, w1v, w2v = _compress_weights(pos_v, w1_v, w2_v)
    full = lambda a: pl.BlockSpec(a.shape, lambda b: (0,) * a.ndim)
    xspec = pl.BlockSpec((1, nh, feat), lambda b: (b, 0, 0))
    gk = _pad_cols(k_norm0.reshape(1, -1), LANES)
    return pl.pallas_call(
        _compress_kernel,
        grid=(bsz,),
        in_specs=[xspec, xspec, full(pk), full(pv), full(w1k), full(w1v), full(w2k), full(w2v), full(gk)],
        out_specs=[pl.BlockSpec((1, NSA_GROUPS, 2 * nh, LANES), lambda b: (b, 0, 0, 0)),
                   pl.BlockSpec((1, NSA_GROUPS, 2, 2 * nh, LANES), lambda b: (b, 0, 0, 0, 0))],
        out_shape=[jax.ShapeDtypeStruct((bsz, NSA_GROUPS, 2 * nh, LANES), F32),
                   jax.ShapeDtypeStruct((bsz, NSA_GROUPS, 2, 2 * nh, LANES), F32)],
        compiler_params=_cparams(("parallel",)),
        name="nsa_compress",
    )(z_kc.reshape(bsz, nh, feat), z_vc.reshape(bsz, nh, feat), pk, pv, w1k, w1v, w2k, w2v, gk)


def _cmp_attn_kernel(q_ref, ck_ref, cv_ref, bias_ref, ov_ref, o_ref, q2_ref, s0_ref, s1_ref,
                     rank0_ref, rank1_ref, *, tq, nsel):
    nqt = q_ref.shape[2] // tq
    assert nqt % 2 == 0
    half = NSA_HPG // 2
    nh = bias_ref.shape[4]
    per_tile = tq // CMP_STRIDE
    c_last = per_tile * (nqt - 1)
    n_blk = LANES - HALF

    def block_window(t):
        start = lax.rem(per_tile * t + (nh - c_last % nh), nh)
        return pl.ds(pl.multiple_of(start, per_tile), nh)

    def query_rows(t):
        return pl.ds(pl.multiple_of(t * tq, tq), tq)

    def stage_scores(t, buf):
        t = jnp.minimum(t, nqt - 1)
        col = lax.broadcasted_iota(jnp.int32, (nh, LANES), 0)
        lane_k = lax.broadcasted_iota(jnp.int32, (nh, LANES), 1)
        wrapped = (lane_k == HALF) & (col < c_last - per_tile * t)
        ck = jnp.where(wrapped, MASK_VALUE, ck_ref[0, 0, block_window(t), :]).astype(BF16)
        lane_q = lax.broadcasted_iota(jnp.int32, (NSA_HPG, tq, LANES), 2)
        q = q_ref[0, :, query_rows(t), :] + jnp.where(lane_q == HALF, 1.0, 0.0).astype(BF16)
        s = _dot_t(q.reshape(NSA_HPG * tq, LANES), ck)
        buf[...] = s.reshape(NSA_HPG, tq, nh) + bias_ref[0, 0]

    def stage_select(t, buf, rank_ref):
        cols = block_window(t)
        cv = [cv_ref[0, 0, parity, cols, :].astype(BF16) for parity in range(2)]
        ov = ov_ref[cols, :].astype(BF16)
        row_ok = t * tq + lax.broadcasted_iota(jnp.int32, (tq, nh), 0) >= CMP_BLOCK - 1
        outs = []
        psum = None
        for h in range(NSA_HPG):
            s = buf[h]
            e = jnp.exp2(s - jnp.max(s, axis=-1, keepdims=True))
            p = jnp.where(row_ok, e / jnp.sum(e, axis=-1, keepdims=True), 0.0)
            outs.append(_dot(p.astype(BF16), cv[h // half]))
            psum = p if psum is None else psum + p
        lane_lo = lax.broadcasted_iota(jnp.int32, (tq, LANES), 1) < HALF
        for j in range(half):
            o = jnp.where(lane_lo, outs[j], outs[half + j])
            o_ref[0, query_rows(t), j * LANES:(j + 1) * LANES] = o.astype(o_ref.dtype)

        hi = psum.astype(BF16)
        lo = (psum - hi.astype(F32)).astype(BF16)
        imp = _dot(hi, ov) + _dot(lo, ov)

        blk = lax.broadcasted_iota(jnp.int32, (n_blk, tq), 0)
        qpos = t * tq + lax.broadcasted_iota(jnp.int32, (n_blk, tq), 1)
        cur = qpos // SLC_BLOCK
        valid = blk * SLC_BLOCK <= qpos
        forced = (blk == 0) | (blk == cur) | (blk == cur - 1)
        score = jnp.where(valid & forced, FORCE_SCORE, jnp.where(valid, imp.T[HALF:], -1.0))
        score = jnp.where(blk < nsel, score, REMOVED_SCORE)
        rank_ref[...] = score
        rank = jnp.zeros((n_blk, tq), F32)
        for j in range(n_blk):
            sj = jnp.broadcast_to(rank_ref[j:j + 1, :], (n_blk, tq))
            beats = (sj > score) | ((sj == score) & (blk > j))
            rank = rank + jnp.where(beats, 1.0, 0.0)
        unselected = jnp.where(rank < min(SLC_TOPN, nsel), 0.0, 1.0)
        unselected = jnp.concatenate([jnp.zeros((HALF, tq), F32), unselected], axis=0).T.astype(BF16)
        for h in range(NSA_HPG):
            q2_ref[0, h, query_rows(t), :] = q_ref[0, h, query_rows(t), :] + unselected

    stage_scores(0, s0_ref)

    def two_tiles(j, carry):
        t = 2 * j
        stage_scores(t + 1, s1_ref)
        stage_select(t, s0_ref, rank0_ref)
        stage_scores(t + 2, s0_ref)
        stage_select(t + 1, s1_ref, rank1_ref)
        return carry

    lax.fori_loop(0, nqt // 2, two_tiles, 0)


def _cmp_attention(q, ck, cv, bias, ov, s):
    bsz = q.shape[0]
    tq = NSA_TQ
    nh = s // CMP_STRIDE
    nsel = s // SLC_BLOCK
    s_buf = pltpu.VMEM((NSA_HPG, tq, nh), F32)
    rank_buf = pltpu.VMEM((LANES - HALF, tq), F32)
    return pl.pallas_call(
        functools.partial(_cmp_attn_kernel, tq=tq, nsel=nsel),
        grid=(NSA_GROUPS, bsz),
        in_specs=[pl.BlockSpec((1, NSA_HPG, s, LANES), lambda g, b: (b, g, 0, 0), pipeline_mode=pl.Buffered(1)),
                  pl.BlockSpec((1, 1, 2 * nh, LANES), lambda g, b: (b, g, 0, 0)),
                  pl.BlockSpec((1, 1, 2, 2 * nh, LANES), lambda g, b: (b, g, 0, 0, 0)),
                  pl.BlockSpec((1, 1, NSA_HPG, tq, nh), lambda g, b: (g, 0, 0, 0, 0)),
                  pl.BlockSpec((2 * nh, LANES), lambda g, b: (0, 0))],
        out_specs=[pl.BlockSpec((1, s, NSA_HPG * NSA_DK), lambda g, b: (b, 0, g)),
                   pl.BlockSpec((1, NSA_HPG, s, LANES), lambda g, b: (b, g, 0, 0))],
        out_shape=[jax.ShapeDtypeStruct((bsz, s, NSA_HEADS * NSA_DK), BF16),
                   jax.ShapeDtypeStruct(q.shape, BF16)],
        scratch_shapes=[s_buf, s_buf, rank_buf, rank_buf],
        compiler_params=_cparams(("parallel", "parallel")),
        name="nsa_cmp_attn",
    )(q, ck, cv, bias, ov)


def _nsa_window_kernel(q_ref, k_ref, v_ref, bias_ref, o_ref, s0_ref, s1_ref, *, tq, window_steps):
    nqt = q_ref.shape[2] // tq
    assert nqt % 2 == 0
    ratio = NSA_TK // tq
    n_tiles = bias_ref.shape[1]
    half = NSA_HPG // 2

    def key_chunk(t, i):
        return t // ratio - (window_steps - 1) + i

    def key_rows(t, i):
        return pl.ds(pl.multiple_of(jnp.maximum(key_chunk(t, i), 0) * NSA_TK, NSA_TK), NSA_TK)

    def stage_scores(t, buf):
        t = jnp.minimum(t, nqt - 1)
        q = q_ref[0, :, pl.ds(pl.multiple_of(t * tq, tq), tq), :].reshape(NSA_HPG * tq, LANES)
        for i in range(window_steps):
            kc = key_chunk(t, i)
            tile = jnp.where(kc < 0, n_tiles - 1, t - ratio * kc)
            s = _dot_t(q, k_ref[0, 0, key_rows(t, i), :]).reshape(NSA_HPG, tq, NSA_TK)
            buf[:, :, i * NSA_TK:(i + 1) * NSA_TK] = s + bias_ref[0, tile]

    def stage_out(t, buf):
        accs = []
        for h in range(NSA_HPG):
            s = buf[h]
            p = jnp.exp2(s - jnp.max(s, axis=-1, keepdims=True)).astype(BF16)
            acc = None
            for i in range(window_steps):
                pv = _dot(p[:, i * NSA_TK:(i + 1) * NSA_TK], v_ref[0, 0, h // half, key_rows(t, i), :])
                acc = pv if acc is None else acc + pv
            accs.append(acc)
        rows = pl.ds(pl.multiple_of(t * tq, tq), tq)
        for j in range(half):
            o = _normalized_pair(accs[j], accs[half + j])
            o_ref[0, rows, j * LANES:(j + 1) * LANES] = o.astype(o_ref.dtype)

    stage_scores(0, s0_ref)

    def two_tiles(j, carry):
        t = 2 * j
        stage_scores(t + 1, s1_ref)
        stage_out(t, s0_ref)
        stage_scores(t + 2, s0_ref)
        stage_out(t + 1, s1_ref)
        return carry

    lax.fori_loop(0, nqt // 2, two_tiles, 0)


def _nsa_selected_kernel(q_ref, k_ref, v_ref, bias_ref, o_ref, s0_ref, s1_ref, m_ref, acc_ref, *, tq):
    nqt = q_ref.shape[2] // tq
    ratio = NSA_TK // tq
    n_tiles = bias_ref.shape[1]
    half = NSA_HPG // 2
    chunk = SEL_TILES * NSA_TK
    per = chunk // tq

    def key_rows(kc):
        return pl.ds(pl.multiple_of(kc * chunk, chunk), chunk)

    def scores(t, kc, valid, buf):
        q = q_ref[0, :, pl.ds(pl.multiple_of(t * tq, tq), tq), :].reshape(NSA_HPG * tq, LANES)
        s = _dot_t(q, k_ref[0, 0, key_rows(kc), :]).reshape(NSA_HPG, tq, chunk)
        for i in range(SEL_TILES):
            offset = t - ratio * (SEL_TILES * kc + i)
            tile = jnp.where(valid & (offset >= 0), jnp.minimum(offset, n_tiles - 2), n_tiles - 1)
            cols = slice(i * NSA_TK, (i + 1) * NSA_TK)
            buf[:, :, cols] = s[:, :, cols] + bias_ref[0, tile]

    def values(h, kc):
        return v_ref[0, 0, h // half, key_rows(kc), :]

    def write_out(t, accs):
        rows = pl.ds(pl.multiple_of(t * tq, tq), tq)
        for j in range(half):
            o = _normalized_pair(accs[j], accs[half + j])
            o_ref[0, rows, j * LANES:(j + 1) * LANES] = o.astype(o_ref.dtype)

    _causal_item_stream(nqt, lambda t: t // per, NSA_HPG, scores, values, write_out,
                        (s0_ref, s1_ref), m_ref, acc_ref)


def _nsa_selected(q, k, v, bias, s):
    bsz = q.shape[0]
    tq = NSA_TQ
    n_off = bias.shape[1]
    assert s % (SEL_TILES * NSA_TK) == 0
    s_buf = pltpu.VMEM((NSA_HPG, tq, SEL_TILES * NSA_TK), F32)
    once = pl.Buffered(1)
    return pl.pallas_call(
        functools.partial(_nsa_selected_kernel, tq=tq),
        grid=(NSA_GROUPS, bsz),
        in_specs=[pl.BlockSpec((1, NSA_HPG, s, LANES), lambda g, b: (b, g, 0, 0), pipeline_mode=once),
                  pl.BlockSpec((1, 1, s, LANES), lambda g, b: (b, g, 0, 0)),
                  pl.BlockSpec((1, 1, 2, s, LANES), lambda g, b: (b, g, 0, 0, 0)),
                  pl.BlockSpec((1, n_off, NSA_HPG, tq, NSA_TK), lambda g, b: (g, 0, 0, 0, 0), pipeline_mode=once)],
        out_specs=pl.BlockSpec((1, s, NSA_HPG * NSA_DK), lambda g, b: (b, 0, g)),
        out_shape=jax.ShapeDtypeStruct((bsz, s, NSA_HEADS * NSA_DK), BF16),
        scratch_shapes=[s_buf, s_buf, pltpu.VMEM((NSA_HPG, tq, LANES), F32), pltpu.VMEM((NSA_HPG, tq, LANES), F32)],
        compiler_params=_cparams(("parallel", "parallel")),
        name="nsa_flash_sel",
    )(q, k, v, bias)


def _nsa_window(q, k, v, bias, s, window_steps):
    bsz = q.shape[0]
    tq = NSA_TQ
    n_off = bias.shape[1]
    s_buf = pltpu.VMEM((NSA_HPG, tq, window_steps * NSA_TK), F32)
    once = pl.Buffered(1)
    return pl.pallas_call(
        functools.partial(_nsa_window_kernel, tq=tq, window_steps=window_steps),
        grid=(NSA_GROUPS, bsz),
        in_specs=[pl.BlockSpec((1, NSA_HPG, s, LANES), lambda g, b: (b, g, 0, 0), pipeline_mode=once),
                  pl.BlockSpec((1, 1, s, LANES), lambda g, b: (b, g, 0, 0)),
                  pl.BlockSpec((1, 1, 2, s, LANES), lambda g, b: (b, g, 0, 0, 0)),
                  pl.BlockSpec((1, n_off, NSA_HPG, tq, NSA_TK), lambda g, b: (g, 0, 0, 0, 0), pipeline_mode=once)],
        out_specs=pl.BlockSpec((1, s, NSA_HPG * NSA_DK), lambda g, b: (b, 0, g)),
        out_shape=jax.ShapeDtypeStruct((bsz, s, NSA_HEADS * NSA_DK), BF16),
        scratch_shapes=[s_buf, s_buf],
        compiler_params=_cparams(("parallel", "parallel")),
        name="nsa_flash_win",
    )(q, k, v, bias)


def _hybrid_layer(h, bsz, s, layer, g_mix, g_mlp, w1, w2, w_in, conv_w, conv_b, wa, ba, wx, bx, lam,
                  q_norm, kv_norm, w_uq, w_ukv, qn_q, qn_k, w_out):
    d = h.shape[1]
    c = LRU_WIDTH
    o_kr = 2 * c + MLA_Q_RANK + MLA_KV_RANK
    w_kr = jnp.pad(w_in[:, o_kr:], ((0, 0), (MLA_NOPE, LANES - MLA_QK)))
    w_all = jnp.concatenate([w_in[:, :o_kr], w_kr], axis=1).astype(BF16)
    z_lru, q_lat, kv_lat, k_rope = _norm_proj(h, g_mix, w_all, (2 * c, MLA_Q_RANK, MLA_KV_RANK, LANES))
    lru = _rg_lru(z_lru.reshape(bsz, s, 2 * c), conv_w, conv_b, wa, ba, wx, bx, lam)
    q, k, v = _mla_prep(q_lat, kv_lat, k_rope, bsz, s, q_norm, kv_norm, w_uq, w_ukv, qn_q, qn_k)
    mla = _mla_attention(q, k, v)
    wo = w_out.astype(BF16)
    return _mix_mlp(h, [[lru.reshape(bsz * s, c)], [mla.reshape(bsz * s, -1)]], [wo[:c], wo[c:]],
                    g_mlp, w1, w2, layer)


def _nsa_layer(h, bsz, s, layer, g_mix, g_mlp, w1, w2, w_in, pos_k, w1_k, w2_k, pos_v, w1_v, w2_v,
               q_norm, k_norm, rel_bias, w_out):
    nq = NSA_HEADS * NSA_DK
    kvw = NSA_GROUPS * NSA_DK
    n_gate = 3 * NSA_HEADS
    w_gate = _pad_cols(w_in[:, nq + 6 * kvw:], LANES)
    w_all = jnp.concatenate([w_in[:, :nq + 6 * kvw], w_gate], axis=1).astype(BF16)
    z_q, z_kc, z_vc, z_kv, z_gate = _norm_proj(h, g_mix, w_all, (nq, kvw, kvw, 4 * kvw, LANES))

    q, ks, kw, vs, vw = _nsa_prep(z_q, z_kv, bsz, s, q_norm, k_norm)
    ck, cv = _compress(z_kc, z_vc, bsz, s, pos_k, w1_k, w2_k, pos_v, w1_v, w2_v, k_norm[0])

    tq = NSA_TQ
    nqt = s // tq
    order = np.array(HEAD_ORDER)
    expand = np.zeros((3, LANES, NSA_HEADS * NSA_DK), np.float32)
    for br in range(3):
        for hd in range(NSA_HEADS):
            expand[br, 3 * hd + br, hd * NSA_DK:(hd + 1) * NSA_DK] = 1.0

    slot_heads = (np.arange(NSA_GROUPS)[:, None] * NSA_HPG + order[None, :]).reshape(-1)
    tab_slots = rel_bias.T[slot_heads]
    nh = s // CMP_STRIDE
    nc = (s - CMP_BLOCK) // CMP_STRIDE + 1
    cmp_bias = _bias_tiles(tab_slots, _cmp_bias_ids(s))
    sel_bias = _bias_tiles(tab_slots, _toeplitz_bucket_ids(None)[0])
    win_ids, n_win_tiles = _toeplitz_bucket_ids(WINDOW)
    win_bias = _bias_tiles(tab_slots, win_ids)

    nsel = s // SLC_BLOCK
    cstart = np.arange(nh) * CMP_STRIDE
    sstart = np.arange(nsel) * SLC_BLOCK
    ov = np.clip(np.minimum(cstart[:, None] + CMP_BLOCK, sstart[None, :] + SLC_BLOCK)
                 - np.maximum(cstart[:, None], sstart[None, :]), 0, None).astype(np.float32) / CMP_BLOCK
    ov[nc:] = 0.0
    ov_p = np.zeros((nh, LANES), np.float32)
    ov_p[:, HALF:HALF + nsel] = ov

    o_c, q_sel = _cmp_attention(q, ck, cv, cmp_bias, jnp.asarray(np.concatenate([ov_p, ov_p])), s)
    o_s = _nsa_selected(q_sel, ks, vs, sel_bias, s)
    window_steps = (n_win_tiles + 1) // (NSA_TK // tq)
    o_w = _nsa_window(q, kw, vw, win_bias, s, window_steps)

    flat = lambda a: a.reshape(bsz * s, -1)
    return _mix_mlp(h, [[flat(o_c), flat(o_s), flat(o_w)]], [w_out.astype(BF16)],
                    g_mlp, w1, w2, layer, gate=(z_gate, jnp.asarray(expand, BF16)))


def kernel(x, rel_bias, norm_mix, norm_mlp, mlp_w1, mlp_w2, hy_w_in, lru_conv_w, lru_conv_b, lru_wa, lru_ba, lru_wx, lru_bx, lru_lambda, mla_q_norm, mla_kv_norm, mla_w_uq, mla_w_ukv, mla_qn_q, mla_qn_k, hy_w_out, nsa_w_in, nsa_cmp_pos_k, nsa_cmp_w1_k, nsa_cmp_w2_k, nsa_cmp_pos_v, nsa_cmp_w1_v, nsa_cmp_w2_v, nsa_q_norm, nsa_k_norm, nsa_w_out):
    bsz, s, d = x.shape
    depth = norm_mix.shape[0]
    h = x.reshape(bsz * s, d)
    w1_all, w2_all = mlp_w1.astype(BF16), mlp_w2.astype(BF16)
    for layer in range(depth):
        if layer % 2 == 0:
            e = layer // 2
            h = _hybrid_layer(h, bsz, s, layer, norm_mix[layer], norm_mlp[layer], w1_all, w2_all,
                              hy_w_in[e], lru_conv_w[e], lru_conv_b[e], lru_wa[e], lru_ba[e], lru_wx[e],
                              lru_bx[e], lru_lambda[e], mla_q_norm[e], mla_kv_norm[e], mla_w_uq[e],
                              mla_w_ukv[e], mla_qn_q[e], mla_qn_k[e], hy_w_out[e])
        else:
            o = layer // 2
            h = _nsa_layer(h, bsz, s, layer, norm_mix[layer], norm_mlp[layer], w1_all, w2_all,
                           nsa_w_in[o], nsa_cmp_pos_k[o], nsa_cmp_w1_k[o], nsa_cmp_w2_k[o],
                           nsa_cmp_pos_v[o], nsa_cmp_w1_v[o], nsa_cmp_w2_v[o], nsa_q_norm[o],
                           nsa_k_norm[o], rel_bias, nsa_w_out[o])
    return h.reshape(bsz, s, d)
```

```python
import functools
import math

import numpy as np
import jax
import jax.numpy as jnp
from jax import lax
from jax.experimental import pallas as pl
from jax.experimental.pallas import tpu as pltpu

F32 = jnp.float32
BF16 = jnp.bfloat16

NORM_EPS = 1e-6
MASK_VALUE = -1e30
FORCE_SCORE = 1e4
REMOVED_SCORE = -3e38

NUM_BUCKETS = 32
MAX_DISTANCE = 1024
MASKED_ID = NUM_BUCKETS
EXCLUDED_ID = NUM_BUCKETS + 1
LOG2E = math.log2(math.e)

LRU_WIDTH = 512
LRU_BLOCKS = 8
CONV_WIDTH = 4
LRU_C = 8.0

MLA_HEADS = 8
MLA_NOPE = 64
MLA_ROPE = 32
MLA_QK = MLA_NOPE + MLA_ROPE
MLA_V = 64
MLA_Q_RANK = 384
MLA_KV_RANK = 256
ROPE_THETA = 10000.0

NSA_HEADS = 16
NSA_GROUPS = 2
NSA_HPG = NSA_HEADS // NSA_GROUPS
NSA_DK = 64
CMP_BLOCK = 32
CMP_STRIDE = 16
CMP_HIDDEN = 256
SLC_BLOCK = 64
SLC_TOPN = 8
WINDOW = 512

LANES = 128
SUBLANES = 8
HALF = LANES // 2
NSA_TQ = 128
NSA_TK = 256
SEL_TILES = 2
MLA_TR = 128
MLA_TK = 512
HEAD_ORDER = (0, 2, 4, 6, 1, 3, 5, 7)
V7X_VMEM_BYTES = 64 * 1024 * 1024
VMEM_LIMIT = V7X_VMEM_BYTES * 7 // 8


def _cparams(sem):
    return pltpu.CompilerParams(dimension_semantics=sem, vmem_limit_bytes=VMEM_LIMIT)


def _rms(x, g, n=None):
    n = x.shape[-1] if n is None else n
    ms = jnp.sum(x * x, axis=-1, keepdims=True) * (1.0 / n)
    return x * lax.rsqrt(ms + NORM_EPS) * g


def _gelu_tanh(x):
    return 0.5 * x * (1.0 + jnp.tanh(math.sqrt(2.0 / math.pi) * (x + 0.044715 * (x * x * x))))


def _sigmoid(x):
    return 1.0 / (1.0 + jnp.exp(-x))


def _dot(a, b):
    return jnp.dot(a, b, preferred_element_type=F32)


def _dot_t(a, b):
    return lax.dot_general(a, b, (((1,), (1,)), ((), ())), preferred_element_type=F32)


def _causal_item_stream(n_tiles, last_chunk, n_chains, scores, values, write_out, s_bufs, m_ref, acc_ref):
    m_ref[...] = jnp.full_like(m_ref, MASK_VALUE)
    acc_ref[...] = jnp.zeros_like(acc_ref)
    n_items = sum(last_chunk(t) + 1 for t in range(n_tiles))

    def advance(t, kc):
        wrap = kc >= last_chunk(t)
        return jnp.where(wrap, t + 1, t), jnp.where(wrap, 0, kc + 1)

    def stage_scores(t, kc, buf):
        valid = t < n_tiles
        scores(jnp.minimum(t, n_tiles - 1), jnp.where(valid, kc, 0), valid, buf)

    def stage_update(t, kc, buf, maybe_padding):
        tile_done = kc >= last_chunk(t)
        kc = jnp.where(t < n_tiles, kc, 0) if maybe_padding else kc
        accs = []
        for c in range(n_chains):
            m_old = m_ref[c]
            m_new = jnp.maximum(m_old, jnp.max(buf[c], axis=-1, keepdims=True))
            alpha = jnp.exp2(m_old - m_new)
            s = buf[c]
            p = jnp.exp2(s - jnp.concatenate([m_new] * (s.shape[1] // LANES), axis=1))
            acc = acc_ref[c] * alpha + _dot(p.astype(BF16), values(c, kc))
            m_ref[c] = jnp.where(tile_done, MASK_VALUE, m_new)
            acc_ref[c] = jnp.where(tile_done, 0.0, acc)
            accs.append(acc)
        if maybe_padding:
            pl.when(t < n_tiles)(lambda: write_out(t, accs))
        else:
            write_out(t, accs)

    start = (jnp.int32(0), jnp.int32(0))
    stage_scores(*start, s_bufs[0])

    def two_items(_, item):
        item1 = advance(*item)
        stage_scores(*item1, s_bufs[1])
        stage_update(*item, s_bufs[0], False)
        item2 = advance(*item1)
        stage_scores(*item2, s_bufs[0])
        stage_update(*item1, s_bufs[1], n_items % 2 == 1)
        return item2

    lax.fori_loop(0, (n_items + 1) // 2, two_items, start)


def _normalized_pair(acc_even, acc_odd):
    lo = lax.broadcasted_iota(jnp.int32, acc_even.shape, 1) < HALF
    return jnp.where(lo, acc_even / acc_even[:, LANES - 1:LANES], acc_odd / acc_odd[:, 0:1])


def _with_ones_column(v, odd):
    lane = lax.broadcasted_iota(jnp.int32, v.shape, 1)
    return jnp.where(lane == (0 if odd else LANES - 1), 1.0, v)


def _norm_proj_kernel(x_ref, g_ref, w_ref, *out_refs, splits):
    xn = _rms(x_ref[...], g_ref[...]).astype(BF16)
    off = 0
    for o_ref, n in zip(out_refs, splits):
        o_ref[...] = _dot(xn, w_ref[:, off:off + n]).astype(o_ref.dtype)
        off += n


def _norm_proj(x, g, w, splits, tm=512):
    t, d = x.shape
    n = w.shape[1]
    assert sum(splits) == n and t % tm == 0
    return pl.pallas_call(
        functools.partial(_norm_proj_kernel, splits=splits),
        grid=(t // tm,),
        in_specs=[pl.BlockSpec((tm, d), lambda i: (i, 0)),
                  pl.BlockSpec((1, d), lambda i: (0, 0)),
                  pl.BlockSpec((d, n), lambda i: (0, 0))],
        out_specs=[pl.BlockSpec((tm, s), lambda i: (i, 0)) for s in splits],
        out_shape=[jax.ShapeDtypeStruct((t, s), F32) for s in splits],
        compiler_params=_cparams(("parallel",)),
        name="norm_proj",
    )(x, g.reshape(1, d), w)


def _mix_mlp_kernel(*refs, group_sizes, gated):
    n_a = sum(group_sizes)
    n_g = len(group_sizes)
    h_ref = refs[0]
    a_refs = refs[1:1 + n_a]
    wo_refs = refs[1 + n_a:1 + n_a + n_g]
    rest = refs[1 + n_a + n_g:]
    if gated:
        zg_ref, e_ref = rest[:2]
        rest = rest[2:]
    g_ref, w1_ref, w2_ref, out_ref, hres, xn, acc = rest
    f = pl.program_id(1)

    @pl.when(f == 0)
    def _():
        h1 = h_ref[...]
        if gated:
            gate = _sigmoid(zg_ref[...])
            g_hi = gate.astype(BF16)
            g_parts = jnp.concatenate([g_hi, (gate - g_hi.astype(F32)).astype(BF16)], axis=1)
        k = 0
        for gi, gs in enumerate(group_sizes):
            a = None
            for j in range(gs):
                aj = a_refs[k + j][...].astype(F32)
                if gated:
                    aj = aj * _dot(g_parts, e_ref[j])
                a = aj if a is None else a + aj
            k += gs
            h1 = h1 + _dot(a.astype(BF16), wo_refs[gi][...])
        hres[...] = h1
        xn[...] = _rms(h1, g_ref[...]).astype(BF16)
        acc[...] = jnp.zeros_like(acc)

    u = jnp.maximum(_dot(xn[...], w1_ref[...]), 0.0)
    acc[...] += _dot((u * u).astype(BF16), w2_ref[...])

    @pl.when(f == pl.num_programs(1) - 1)
    def _():
        out_ref[...] = hres[...] + acc[...]


def _mix_mlp(h, groups, wos, g, w1, w2, layer, gate=None, tm=512, tf=2048):
    t, d = h.shape
    ff = w1.shape[2]
    group_sizes = tuple(len(gr) for gr in groups)
    a_list = [a for gr in groups for a in gr]
    extra = [] if gate is None else list(gate)
    in_specs = [pl.BlockSpec((tm, d), lambda i, f: (i, 0))]
    in_specs += [pl.BlockSpec((tm, a.shape[1]), lambda i, f: (i, 0)) for a in a_list]
    in_specs += [pl.BlockSpec(w.shape, lambda i, f: (0, 0)) for w in wos]
    if gate is not None:
        in_specs += [pl.BlockSpec((tm, LANES), lambda i, f: (i, 0)),
                     pl.BlockSpec(gate[1].shape, lambda i, f: (0, 0, 0))]
    in_specs += [pl.BlockSpec((1, d), lambda i, f: (0, 0)),
                 pl.BlockSpec((None, d, tf), lambda i, f: (layer, 0, f)),
                 pl.BlockSpec((None, tf, d), lambda i, f: (layer, f, 0))]
    return pl.pallas_call(
        functools.partial(_mix_mlp_kernel, group_sizes=group_sizes, gated=gate is not None),
        grid=(t // tm, ff // tf),
        in_specs=in_specs,
        out_specs=pl.BlockSpec((tm, d), lambda i, f: (i, 0)),
        out_shape=jax.ShapeDtypeStruct((t, d), F32),
        scratch_shapes=[pltpu.VMEM((tm, d), F32), pltpu.VMEM((tm, d), BF16),
                        pltpu.VMEM((tm, d), F32)],
        compiler_params=_cparams(("parallel", "arbitrary")),
        name="mix_mlp",
    )(h, *a_list, *wos, *extra, g.reshape(1, d), w1, w2)


def _lru_kernel(xr_ref, xg_ref, cw_ref, cb_ref, wa_ref, ba_ref, wx_ref, bx_ref, lam_ref,
                o_ref, tail_ref, hc_ref, *, tc):
    c = pl.program_id(1)

    @pl.when(c == 0)
    def _():
        tail_ref[...] = jnp.zeros_like(tail_ref)
        hc_ref[...] = jnp.zeros_like(hc_ref)

    x = xr_ref[0]
    width = x.shape[1]
    row = lax.broadcasted_iota(jnp.int32, (tc, width), 0)
    tail_row = lax.broadcasted_iota(jnp.int32, (SUBLANES, width), 0)
    tail = tail_ref[...]
    cw = cw_ref[...]
    y = x * cw[CONV_WIDTH - 1:CONV_WIDTH]
    for s in range(1, CONV_WIDTH):
        sh = pltpu.roll(x, s, 0)
        head = jnp.where(tail_row < s, pltpu.roll(tail, s, 0), sh[0:SUBLANES])
        sh = jnp.concatenate([head, sh[SUBLANES:]], axis=0)
        y = y + sh * cw[CONV_WIDTH - 1 - s:CONV_WIDTH - s]
    y = y + cb_ref[...]
    tail_ref[...] = x[tc - SUBLANES:tc]

    yb = y.astype(BF16)
    r = _sigmoid(_dot(yb, wa_ref[...]) + ba_ref[...])
    i = _sigmoid(_dot(yb, wx_ref[...]) + bx_ref[...])
    nl = -lam_ref[...]
    softplus = jnp.maximum(nl, 0.0) + jnp.log(1.0 + jnp.exp(-jnp.abs(nl)))
    log_a = (-LRU_C) * r * softplus
    a = jnp.exp(log_a)
    mult = jnp.sqrt(1.0 - jnp.exp(2.0 * log_a))
    mult = jnp.where((row == 0) & (c == 0), 1.0, mult)
    b = mult * (i * y)

    d = 1
    while d < tc:
        keep = row >= d
        a_sh = jnp.where(keep, pltpu.roll(a, d, 0), 1.0)
        b_sh = jnp.where(keep, pltpu.roll(b, d, 0), 0.0)
        b = a * b_sh + b
        a = a * a_sh
        d *= 2
    h = b + a * hc_ref[...]
    hc_ref[...] = h[tc - 1:tc]
    o_ref[0] = (h * _gelu_tanh(xg_ref[0])).astype(o_ref.dtype)


def _block_diag(w):
    n, d, e = w.shape
    eye = jnp.eye(n, dtype=w.dtype)
    return (w[:, :, None, :] * eye[:, None, :, None]).reshape(n * d, n * e)


def _rg_lru(z_lru, conv_w, conv_b, wa, ba, wx, bx, lam, tc=256):
    bsz, s, _ = z_lru.shape
    c = LRU_WIDTH
    tc = min(tc, s)
    vec = lambda v: pl.BlockSpec(v, lambda b, i: (0, 0))
    return pl.pallas_call(
        functools.partial(_lru_kernel, tc=tc),
        grid=(bsz, s // tc),
        in_specs=[pl.BlockSpec((1, tc, c), lambda b, i: (b, i, 0)),
                  pl.BlockSpec((1, tc, c), lambda b, i: (b, i, 1)),
                  vec((CONV_WIDTH, c)), vec((1, c)), vec((c, c)), vec((1, c)),
                  vec((c, c)), vec((1, c)), vec((1, c))],
        out_specs=pl.BlockSpec((1, tc, c), lambda b, i: (b, i, 0)),
        out_shape=jax.ShapeDtypeStruct((bsz, s, c), BF16),
        scratch_shapes=[pltpu.VMEM((SUBLANES, c), F32), pltpu.VMEM((1, c), F32)],
        compiler_params=_cparams(("parallel", "arbitrary")),
        name="rg_lru",
    )(z_lru, z_lru, conv_w, conv_b.reshape(1, c), _block_diag(wa).astype(BF16), ba.reshape(1, c),
      _block_diag(wx).astype(BF16), bx.reshape(1, c), lam.reshape(1, c))


def _rope(x, cos, sin_lo, sin_hi):
    return (x * cos + pltpu.roll(x, LANES - MLA_ROPE // 2, 1) * sin_lo
            + pltpu.roll(x, MLA_ROPE // 2, 1) * sin_hi)


def _mla_prep_kernel(ql_ref, kvl_ref, kr_ref, gq_ref, gkv_ref, wq_ref, wk_ref, wv_ref,
                     nq_ref, nk_ref, cos_ref, s1_ref, s2_ref, q_out, k_out, v_out):
    qn = _rms(ql_ref[...], gq_ref[...]).astype(BF16)
    kvn = _rms(kvl_ref[...], gkv_ref[...]).astype(BF16)
    kr = kr_ref[...]
    cos, s1, s2 = cos_ref[...], s1_ref[...], s2_ref[...]
    scale = MLA_QK ** -0.5 * LOG2E
    for h in range(MLA_HEADS):
        sl = slice(h * LANES, (h + 1) * LANES)
        qh = _rms(_dot(qn, wq_ref[:, sl]), nq_ref[...], MLA_QK)
        q_out[0, h] = (_rope(qh, cos, s1, s2) * scale).astype(BF16)
        kh = _rms(_dot(kvn, wk_ref[:, sl]) + kr, nk_ref[...], MLA_QK)
        k_out[0, h] = _rope(kh, cos, s1, s2).astype(BF16)
        v_out[0, h] = _with_ones_column(_dot(kvn, wv_ref[:, sl]), h % 2 == 1).astype(BF16)


def _rope_tables(s):
    half = MLA_ROPE // 2
    freqs = ROPE_THETA ** (-jnp.arange(half, dtype=F32) / half)
    ang = jnp.arange(s, dtype=F32)[:, None] * freqs[None, :]
    cos, sin = jnp.cos(ang), jnp.sin(ang)
    z = lambda n: jnp.zeros((s, n), F32)
    cos_t = jnp.concatenate([jnp.ones((s, MLA_NOPE), F32), cos, cos, z(LANES - MLA_QK)], axis=1)
    s_lo = jnp.concatenate([z(MLA_NOPE), -sin, z(LANES - MLA_NOPE - half)], axis=1)
    s_hi = jnp.concatenate([z(MLA_NOPE + half), sin, z(LANES - MLA_QK)], axis=1)
    return cos_t, s_lo, s_hi


def _pad_cols(w, n):
    return jnp.pad(w, ((0, 0),) * (w.ndim - 1) + ((0, n - w.shape[-1]),))


def _mla_prep(q_lat, kv_lat, k_rope, bsz, s, q_norm, kv_norm, w_uq, w_ukv, qn_q, qn_k, tm=512):
    tm = min(tm, s)
    nt = s // tm
    hl = MLA_HEADS * LANES
    wq = _pad_cols(w_uq.reshape(MLA_Q_RANK, MLA_HEADS, MLA_QK), LANES).reshape(MLA_Q_RANK, hl)
    wkv = w_ukv.reshape(MLA_KV_RANK, MLA_HEADS, MLA_NOPE + MLA_V)
    wk = _pad_cols(wkv[..., :MLA_NOPE], LANES).reshape(MLA_KV_RANK, hl)
    wv = wkv[..., MLA_NOPE:]
    zero = jnp.zeros_like(wv)
    even = (jnp.arange(MLA_HEADS) % 2 == 0)[None, :, None]
    wv = jnp.concatenate([jnp.where(even, wv, zero), jnp.where(even, zero, wv)], axis=-1)
    wv = wv.reshape(MLA_KV_RANK, hl)
    cos_t, s_lo, s_hi = _rope_tables(s)
    row = lambda n: pl.BlockSpec((tm, n), lambda b, i: (b * nt + i, 0))
    full = lambda a: pl.BlockSpec(a.shape, lambda b, i: (0, 0))
    tab = pl.BlockSpec((tm, LANES), lambda b, i: (i, 0))
    head_out = pl.BlockSpec((1, MLA_HEADS, tm, LANES), lambda b, i: (b, 0, i, 0))
    args = (q_lat, kv_lat, k_rope, q_norm.reshape(1, -1), kv_norm.reshape(1, -1),
            wq.astype(BF16), wk.astype(BF16), wv.astype(BF16),
            _pad_cols(qn_q.reshape(1, -1), LANES), _pad_cols(qn_k.reshape(1, -1), LANES),
            cos_t, s_lo, s_hi)
    in_specs = [row(MLA_Q_RANK), row(MLA_KV_RANK), row(LANES)] + [full(a) for a in args[3:10]] + [tab] * 3
    shp = jax.ShapeDtypeStruct((bsz, MLA_HEADS, s, LANES), BF16)
    return pl.pallas_call(
        _mla_prep_kernel,
        grid=(bsz, nt),
        in_specs=in_specs,
        out_specs=[head_out] * 3,
        out_shape=[shp] * 3,
        compiler_params=_cparams(("parallel", "parallel")),
        name="mla_prep",
    )(*args)


def _mla_attn_kernel(q_ref, k_ref, v_ref, mask_ref, o_ref, s0_ref, s1_ref, m_ref, acc_ref, *, tq):
    n_rc = tq // MLA_TR
    ratio = MLA_TK // MLA_TR
    n_tiles = mask_ref.shape[0]

    def key_rows(kc):
        return pl.ds(pl.multiple_of(kc * MLA_TK, MLA_TK), MLA_TK)

    def scores(qi, kc, valid, buf):
        rows = pl.ds(pl.multiple_of(qi * tq, tq), tq)
        for j in range(2):
            s = _dot_t(q_ref[0, j, rows, :], k_ref[0, j, key_rows(kc), :])
            for rc in range(n_rc):
                offset = qi * n_rc + rc - ratio * kc
                tile = jnp.where(valid, jnp.clip(offset + 1, 0, n_tiles - 1), 0)
                buf[j * n_rc + rc] = s[rc * MLA_TR:(rc + 1) * MLA_TR] + mask_ref[tile]

    def values(c, kc):
        return v_ref[0, c // n_rc, key_rows(kc), :]

    def write_out(qi, accs):
        for rc in range(n_rc):
            rows = pl.ds(pl.multiple_of(qi * tq + rc * MLA_TR, MLA_TR), MLA_TR)
            o_ref[0, rows, :] = _normalized_pair(accs[rc], accs[n_rc + rc]).astype(o_ref.dtype)

    _causal_item_stream(q_ref.shape[2] // tq, lambda qi: (qi * tq + tq - 1) // MLA_TK, 2 * n_rc,
                        scores, values, write_out, (s0_ref, s1_ref), m_ref, acc_ref)


def _mla_causal_tiles():
    i = np.arange(MLA_TR)[:, None]
    j = np.arange(MLA_TK)[None, :]
    tiles = [np.full((MLA_TR, MLA_TK), -np.inf, np.float32)]
    for d in range(MLA_TK // MLA_TR):
        tiles.append(np.where(j <= i + d * MLA_TR, 0.0, -np.inf).astype(np.float32))
    tiles.append(np.zeros((MLA_TR, MLA_TK), np.float32))
    return np.stack(tiles)


def _mla_attention(q, k, v, tq=512):
    bsz, nh, s, _ = q.shape
    tq = min(tq, s)
    assert tq % MLA_TK == 0 and tq % MLA_TR == 0
    n_chains = 2 * (tq // MLA_TR)
    mask = _mla_causal_tiles()
    seq_spec = pl.BlockSpec((1, 2, s, LANES), lambda b, hp: (b, hp, 0, 0))
    return pl.pallas_call(
        functools.partial(_mla_attn_kernel, tq=tq),
        grid=(bsz, nh // 2),
        in_specs=[seq_spec, seq_spec, seq_spec, pl.BlockSpec(mask.shape, lambda b, hp: (0, 0, 0))],
        out_specs=pl.BlockSpec((1, s, LANES), lambda b, hp: (b, 0, hp)),
        out_shape=jax.ShapeDtypeStruct((bsz, s, nh * MLA_V), BF16),
        scratch_shapes=[pltpu.VMEM((n_chains, MLA_TR, MLA_TK), F32), pltpu.VMEM((n_chains, MLA_TR, MLA_TK), F32),
                        pltpu.VMEM((n_chains, MLA_TR, LANES), F32), pltpu.VMEM((n_chains, MLA_TR, LANES), F32)],
        compiler_params=_cparams(("parallel", "parallel")),
        name="mla_attn",
    )(q, k, v, jnp.asarray(mask))


def _t5_bucket_np(dist):
    n = np.maximum(dist, 0)
    max_exact = NUM_BUCKETS // 2
    nf = np.maximum(n, 1).astype(np.float32)
    large = max_exact + (np.log(nf / np.float32(max_exact)) / np.float32(math.log(MAX_DISTANCE / max_exact))
                         * np.float32(NUM_BUCKETS - max_exact)).astype(np.int32)
    large = np.minimum(large, NUM_BUCKETS - 1)
    return np.where(n < max_exact, n, large).astype(np.int32)


def _bias_tiles_kernel(tab_ref, idx_ref, o_ref, *, buckets):
    slot = pl.program_id(0) * NSA_HPG + pl.program_id(1)
    for i, present in enumerate(buckets):
        idx = idx_ref[i]
        acc = jnp.where(idx == EXCLUDED_ID, -jnp.inf, MASK_VALUE).astype(F32)
        for b in present:
            acc = jnp.where(idx == b, tab_ref[slot, b] * LOG2E, acc)
        o_ref[0, i, 0] = acc


def _bias_tiles(tab_slots, idx):
    n, r, c = idx.shape
    buckets = tuple(tuple(int(b) for b in np.unique(tile) if b < NUM_BUCKETS) for tile in idx)
    return pl.pallas_call(
        functools.partial(_bias_tiles_kernel, buckets=buckets),
        grid=(NSA_GROUPS, NSA_HPG),
        in_specs=[pl.BlockSpec(memory_space=pltpu.SMEM),
                  pl.BlockSpec((n, r, c), lambda g, p: (0, 0, 0))],
        out_specs=pl.BlockSpec((1, n, 1, r, c), lambda g, p: (g, 0, p, 0, 0)),
        out_shape=jax.ShapeDtypeStruct((NSA_GROUPS, n, NSA_HPG, r, c), F32),
        compiler_params=_cparams(("parallel", "parallel")),
        name="bias_tiles",
    )(tab_slots, jnp.asarray(idx))


def _cmp_bias_ids(s):
    nqt = s // NSA_TQ
    nh = s // CMP_STRIDE
    per_tile = NSA_TQ // CMP_STRIDE
    i = np.arange(NSA_TQ)[:, None]
    c = np.arange(nh)[None, :]
    dist = i - (CMP_BLOCK - 1) - CMP_STRIDE * (c - per_tile * (nqt - 1))
    ids = np.where(dist >= 0, _t5_bucket_np(dist), MASKED_ID).astype(np.int32)
    return ids[None]


def _toeplitz_bucket_ids(window):
    i = np.arange(NSA_TQ)[:, None]
    j = np.arange(NSA_TK)[None, :]
    tiles = []
    d = 0
    while True:
        dist = d * NSA_TQ + i - j
        ids = _t5_bucket_np(dist)
        ok = dist >= 0
        if window is not None:
            ok &= dist < window
        ids = np.where(ok, ids, MASKED_ID).astype(np.int32)
        tiles.append(ids)
        if window is None and (ids == NUM_BUCKETS - 1).all():
            break
        if window is not None and not ok.any():
            tiles.pop()
            break
        d += 1
    n_real = len(tiles)
    tiles.append(np.full((NSA_TQ, NSA_TK), EXCLUDED_ID, np.int32))
    return np.stack(tiles), n_real


def _split_halves(x):
    lo = lax.broadcasted_iota(jnp.int32, x.shape, 1) < HALF
    return jnp.where(lo, x, 0.0), pltpu.roll(jnp.where(lo, 0.0, x), HALF, 1)


def _value_layouts(x):
    lo = lax.broadcasted_iota(jnp.int32, x.shape, 1) < HALF
    g0e = jnp.where(lo, x, 0.0)
    g1o = jnp.where(lo, 0.0, x)
    pair = lambda e, o: (_with_ones_column(e, False), _with_ones_column(o, True))
    return pair(g0e, pltpu.roll(g0e, HALF, 1)), pair(pltpu.roll(g1o, HALF, 1), g1o)


def _nsa_prep_kernel(zq_ref, ks_ref, vs_ref, kw_ref, vw_ref, gq_ref, gks_ref, gkw_ref,
                     q_out, ks_out, kw_out, vs_out, vw_out, *, tm):
    i = pl.program_id(1)
    scale = NSA_DK ** -0.5 * LOG2E
    for j in range(NSA_HEADS // 2):
        e, o = _split_halves(zq_ref[:, j * LANES:(j + 1) * LANES])
        g = (2 * j) // NSA_HPG
        pe = (2 * j - NSA_HPG * g) // 2
        q_out[0, NSA_HPG * g + pe] = (_rms(e, gq_ref[...], NSA_DK) * scale).astype(BF16)
        q_out[0, NSA_HPG * g + NSA_HPG // 2 + pe] = (_rms(o, gq_ref[...], NSA_DK) * scale).astype(BF16)

    lane = lax.broadcasted_iota(jnp.int32, (tm, LANES), 1)
    pos = i * tm + lax.broadcasted_iota(jnp.int32, (tm, LANES), 0)
    block_tag = jnp.where(lane - HALF == pos // SLC_BLOCK, MASK_VALUE, 0.0)
    for g, x in enumerate(_split_halves(ks_ref[...])):
        ks_out[0, g] = (_rms(x, gks_ref[...], NSA_DK) + block_tag).astype(BF16)
    for g, x in enumerate(_split_halves(kw_ref[...])):
        kw_out[0, g] = _rms(x, gkw_ref[...], NSA_DK).astype(BF16)
    for ref, out in ((vs_ref, vs_out), (vw_ref, vw_out)):
        for g, (ve, vo) in enumerate(_value_layouts(ref[...])):
            out[0, g, 0] = ve.astype(BF16)
            out[0, g, 1] = vo.astype(BF16)


def _nsa_prep(z_q, z_kv, bsz, s, q_norm, k_norm, tm=512):
    tm = min(tm, s)
    nt = s // tm
    row = lambda n, c: pl.BlockSpec((tm, n), lambda b, i: (b * nt + i, c))
    gain = pl.BlockSpec((1, LANES), lambda b, i: (0, 0))
    kspec = pl.BlockSpec((1, NSA_GROUPS, tm, LANES), lambda b, i: (b, 0, i, 0))
    vspec = pl.BlockSpec((1, NSA_GROUPS, 2, tm, LANES), lambda b, i: (b, 0, 0, i, 0))
    kshape = jax.ShapeDtypeStruct((bsz, NSA_GROUPS, s, LANES), BF16)
    vshape = jax.ShapeDtypeStruct((bsz, NSA_GROUPS, 2, s, LANES), BF16)
    pad = lambda v: _pad_cols(v.reshape(1, -1), LANES)
    return pl.pallas_call(
        functools.partial(_nsa_prep_kernel, tm=tm),
        grid=(bsz, nt),
        in_specs=[row(NSA_HEADS * NSA_DK, 0), row(LANES, 0), row(LANES, 1), row(LANES, 2), row(LANES, 3),
                  gain, gain, gain],
        out_specs=[pl.BlockSpec((1, NSA_HEADS, tm, LANES), lambda b, i: (b, 0, i, 0)),
                   kspec, kspec, vspec, vspec],
        out_shape=[jax.ShapeDtypeStruct((bsz, NSA_HEADS, s, LANES), BF16), kshape, kshape, vshape, vshape],
        compiler_params=_cparams(("parallel", "parallel")),
        name="nsa_prep",
    )(z_q, z_kv, z_kv, z_kv, z_kv, pad(q_norm), pad(k_norm[1]), pad(k_norm[2]))


def _compress_kernel(xk_ref, xv_ref, pk_ref, pv_ref, w1k_ref, w1v_ref, w2k_ref, w2v_ref, gk_ref,
                     ck_out, cv_out):
    def mlp(x, p_ref, w1_ref, w2_ref):
        nh = x.shape[0]
        top = _dot((x + p_ref[0:1]).astype(BF16), w1_ref[0])
        bot = _dot((x + p_ref[1:2]).astype(BF16), w1_ref[1])
        hid = _gelu_tanh(top + pltpu.roll(bot, nh - 1, 0))
        return _dot(hid.astype(BF16), w2_ref[...])

    nh = xk_ref.shape[1]
    yk = mlp(xk_ref[0], pk_ref, w1k_ref, w2k_ref)
    for g, x in enumerate(_split_halves(yk)):
        ck = _rms(x, gk_ref[...], NSA_DK)
        ck_out[0, g, 0:nh] = ck
        ck_out[0, g, nh:2 * nh] = ck
    yv = mlp(xv_ref[0], pv_ref, w1v_ref, w2v_ref)
    for g, layouts in enumerate(_value_layouts(yv)):
        for parity, v in enumerate(layouts):
            cv_out[0, g, parity, 0:nh] = v
            cv_out[0, g, parity, nh:2 * nh] = v


def _compress_weights(pos, w1, w2):
    half = CMP_BLOCK // 2
    g = NSA_GROUPS
    eye = jnp.eye(g, dtype=F32)
    w1 = w1.reshape(2, half, NSA_DK, CMP_HIDDEN)
    w1 = w1[:, :, None, :, None, :] * eye[None, None, :, None, :, None]
    w1 = w1.reshape(2, half * g * NSA_DK, g * CMP_HIDDEN).astype(BF16)
    p = jnp.broadcast_to(pos.reshape(2, half, 1, NSA_DK), (2, half, g, NSA_DK)).reshape(2, half * g * NSA_DK)
    w2 = (w2[None, :, None, :] * eye[:, None, :, None]).reshape(g * CMP_HIDDEN, g * NSA_DK).astype(BF16)
    return p, w1, w2


def _compress(z_kc, z_vc, bsz, s, pos_k, w1_k, w2_k, pos_v, w1_v, w2_v, k_norm0):
    nh = s // CMP_STRIDE
    feat = CMP_STRIDE * LANES
    pk, w1k, w2k = _compress_weights(pos_k, w1_k, w2_k)
    pv, w1v, w2v = _compress_weights(pos_v, w1_v, w2_v)
    full = lambda a: pl.BlockSpec(a.shape, lambda b: (0,) * a.ndim)
    xspec = pl.BlockSpec((1, nh, feat), lambda b: (b, 0, 0))
    gk = _pad_cols(k_norm0.reshape(1, -1), LANES)
    return pl.pallas_call(
        _compress_kernel,
        grid=(bsz,),
        in_specs=[xspec, xspec, full(pk), full(pv), full(w1k), full(w1v), full(w2k), full(w2v), full(gk)],
        out_specs=[pl.BlockSpec((1, NSA_GROUPS, 2 * nh, LANES), lambda b: (b, 0, 0, 0)),
                   pl.BlockSpec((1, NSA_GROUPS, 2, 2 * nh, LANES), lambda b: (b, 0, 0, 0, 0))],
        out_shape=[jax.ShapeDtypeStruct((bsz, NSA_GROUPS, 2 * nh, LANES), F32),
                   jax.ShapeDtypeStruct((bsz, NSA_GROUPS, 2, 2 * nh, LANES), F32)],
        compiler_params=_cparams(("parallel",)),
        name="nsa_compress",
    )(z_kc.reshape(bsz, nh, feat), z_vc.reshape(bsz, nh, feat), pk, pv, w1k, w1v, w2k, w2v, gk)


def _cmp_attn_kernel(q_ref, ck_ref, cv_ref, bias_ref, ov_ref, o_ref, q2_ref, s0_ref, s1_ref,
                     rank0_ref, rank1_ref, *, tq, nsel):
    nqt = q_ref.shape[2] // tq
    assert nqt % 2 == 0
    half = NSA_HPG // 2
    nh = bias_ref.shape[4]
    per_tile = tq // CMP_STRIDE
    c_last = per_tile * (nqt - 1)
    n_blk = LANES - HALF

    def block_window(t):
        start = lax.rem(per_tile * t + (nh - c_last % nh), nh)
        return pl.ds(pl.multiple_of(start, per_tile), nh)

    def query_rows(t):
        return pl.ds(pl.multiple_of(t * tq, tq), tq)

    def stage_scores(t, buf):
        t = jnp.minimum(t, nqt - 1)
        col = lax.broadcasted_iota(jnp.int32, (nh, LANES), 0)
        lane_k = lax.broadcasted_iota(jnp.int32, (nh, LANES), 1)
        wrapped = (lane_k == HALF) & (col < c_last - per_tile * t)
        ck = jnp.where(wrapped, MASK_VALUE, ck_ref[0, 0, block_window(t), :]).astype(BF16)
        lane_q = lax.broadcasted_iota(jnp.int32, (NSA_HPG, tq, LANES), 2)
        q = q_ref[0, :, query_rows(t), :] + jnp.where(lane_q == HALF, 1.0, 0.0).astype(BF16)
        s = _dot_t(q.reshape(NSA_HPG * tq, LANES), ck)
        buf[...] = s.reshape(NSA_HPG, tq, nh) + bias_ref[0, 0]

    def stage_select(t, buf, rank_ref):
        cols = block_window(t)
        cv = [cv_ref[0, 0, parity, cols, :].astype(BF16) for parity in range(2)]
        ov = ov_ref[cols, :].astype(BF16)
        row_ok = t * tq + lax.broadcasted_iota(jnp.int32, (tq, nh), 0) >= CMP_BLOCK - 1
        outs = []
        psum = None
        for h in range(NSA_HPG):
            s = buf[h]
            e = jnp.exp2(s - jnp.max(s, axis=-1, keepdims=True))
            p = jnp.where(row_ok, e / jnp.sum(e, axis=-1, keepdims=True), 0.0)
            outs.append(_dot(p.astype(BF16), cv[h // half]))
            psum = p if psum is None else psum + p
        lane_lo = lax.broadcasted_iota(jnp.int32, (tq, LANES), 1) < HALF
        for j in range(half):
            o = jnp.where(lane_lo, outs[j], outs[half + j])
            o_ref[0, query_rows(t), j * LANES:(j + 1) * LANES] = o.astype(o_ref.dtype)

        hi = psum.astype(BF16)
        lo = (psum - hi.astype(F32)).astype(BF16)
        imp = _dot(hi, ov) + _dot(lo, ov)

        blk = lax.broadcasted_iota(jnp.int32, (n_blk, tq), 0)
        qpos = t * tq + lax.broadcasted_iota(jnp.int32, (n_blk, tq), 1)
        cur = qpos // SLC_BLOCK
        valid = blk * SLC_BLOCK <= qpos
        forced = (blk == 0) | (blk == cur) | (blk == cur - 1)
        score = jnp.where(valid & forced, FORCE_SCORE, jnp.where(valid, imp.T[HALF:], -1.0))
        score = jnp.where(blk < nsel, score, REMOVED_SCORE)
        rank_ref[...] = score
        rank = jnp.zeros((n_blk, tq), F32)
        for j in range(n_blk):
            sj = jnp.broadcast_to(rank_ref[j:j + 1, :], (n_blk, tq))
            beats = (sj > score) | ((sj == score) & (blk > j))
            rank = rank + jnp.where(beats, 1.0, 0.0)
        unselected = jnp.where(rank < min(SLC_TOPN, nsel), 0.0, 1.0)
        unselected = jnp.concatenate([jnp.zeros((HALF, tq), F32), unselected], axis=0).T.astype(BF16)
        for h in range(NSA_HPG):
            q2_ref[0, h, query_rows(t), :] = q_ref[0, h, query_rows(t), :] + unselected

    stage_scores(0, s0_ref)

    def two_tiles(j, carry):
        t = 2 * j
        stage_scores(t + 1, s1_ref)
        stage_select(t, s0_ref, rank0_ref)
        stage_scores(t + 2, s0_ref)
        stage_select(t + 1, s1_ref, rank1_ref)
        return carry

    lax.fori_loop(0, nqt // 2, two_tiles, 0)


def _cmp_attention(q, ck, cv, bias, ov, s):
    bsz = q.shape[0]
    tq = NSA_TQ
    nh = s // CMP_STRIDE
    nsel = s // SLC_BLOCK
    s_buf = pltpu.VMEM((NSA_HPG, tq, nh), F32)
    rank_buf = pltpu.VMEM((LANES - HALF, tq), F32)
    return pl.pallas_call(
        functools.partial(_cmp_attn_kernel, tq=tq, nsel=nsel),
        grid=(NSA_GROUPS, bsz),
        in_specs=[pl.BlockSpec((1, NSA_HPG, s, LANES), lambda g, b: (b, g, 0, 0), pipeline_mode=pl.Buffered(1)),
                  pl.BlockSpec((1, 1, 2 * nh, LANES), lambda g, b: (b, g, 0, 0)),
                  pl.BlockSpec((1, 1, 2, 2 * nh, LANES), lambda g, b: (b, g, 0, 0, 0)),
                  pl.BlockSpec((1, 1, NSA_HPG, tq, nh), lambda g, b: (g, 0, 0, 0, 0)),
                  pl.BlockSpec((2 * nh, LANES), lambda g, b: (0, 0))],
        out_specs=[pl.BlockSpec((1, s, NSA_HPG * NSA_DK), lambda g, b: (b, 0, g)),
                   pl.BlockSpec((1, NSA_HPG, s, LANES), lambda g, b: (b, g, 0, 0))],
        out_shape=[jax.ShapeDtypeStruct((bsz, s, NSA_HEADS * NSA_DK), BF16),
                   jax.ShapeDtypeStruct(q.shape, BF16)],
        scratch_shapes=[s_buf, s_buf, rank_buf, rank_buf],
        compiler_params=_cparams(("parallel", "parallel")),
        name="nsa_cmp_attn",
    )(q, ck, cv, bias, ov)


def _nsa_window_kernel(q_ref, k_ref, v_ref, bias_ref, o_ref, s0_ref, s1_ref, *, tq, window_steps):
    nqt = q_ref.shape[2] // tq
    assert nqt % 2 == 0
    ratio = NSA_TK // tq
    n_tiles = bias_ref.shape[1]
    half = NSA_HPG // 2

    def key_chunk(t, i):
        return t // ratio - (window_steps - 1) + i

    def key_rows(t, i):
        return pl.ds(pl.multiple_of(jnp.maximum(key_chunk(t, i), 0) * NSA_TK, NSA_TK), NSA_TK)

    def stage_scores(t, buf):
        t = jnp.minimum(t, nqt - 1)
        q = q_ref[0, :, pl.ds(pl.multiple_of(t * tq, tq), tq), :].reshape(NSA_HPG * tq, LANES)
        for i in range(window_steps):
            kc = key_chunk(t, i)
            tile = jnp.where(kc < 0, n_tiles - 1, t - ratio * kc)
            s = _dot_t(q, k_ref[0, 0, key_rows(t, i), :]).reshape(NSA_HPG, tq, NSA_TK)
            buf[:, :, i * NSA_TK:(i + 1) * NSA_TK] = s + bias_ref[0, tile]

    def stage_out(t, buf):
        accs = []
        for h in range(NSA_HPG):
            s = buf[h]
            p = jnp.exp2(s - jnp.max(s, axis=-1, keepdims=True)).astype(BF16)
            acc = None
            for i in range(window_steps):
                pv = _dot(p[:, i * NSA_TK:(i + 1) * NSA_TK], v_ref[0, 0, h // half, key_rows(t, i), :])
                acc = pv if acc is None else acc + pv
            accs.append(acc)
        rows = pl.ds(pl.multiple_of(t * tq, tq), tq)
        for j in range(half):
            o = _normalized_pair(accs[j], accs[half + j])
            o_ref[0, rows, j * LANES:(j + 1) * LANES] = o.astype(o_ref.dtype)

    stage_scores(0, s0_ref)

    def two_tiles(j, carry):
        t = 2 * j
        stage_scores(t + 1, s1_ref)
        stage_out(t, s0_ref)
        stage_scores(t + 2, s0_ref)
        stage_out(t + 1, s1_ref)
        return carry

    lax.fori_loop(0, nqt // 2, two_tiles, 0)


def _nsa_selected_kernel(q_ref, k_ref, v_ref, bias_ref, o_ref, s0_ref, s1_ref, m_ref, acc_ref, *, tq):
    nqt = q_ref.shape[2] // tq
    ratio = NSA_TK // tq
    n_tiles = bias_ref.shape[1]
    half = NSA_HPG // 2
    chunk = SEL_TILES * NSA_TK
    per = chunk // tq

    def key_rows(kc):
        return pl.ds(pl.multiple_of(kc * chunk, chunk), chunk)

    def scores(t, kc, valid, buf):
        q = q_ref[0, :, pl.ds(pl.multiple_of(t * tq, tq), tq), :].reshape(NSA_HPG * tq, LANES)
        s = _dot_t(q, k_ref[0, 0, key_rows(kc), :]).reshape(NSA_HPG, tq, chunk)
        for i in range(SEL_TILES):
            offset = t - ratio * (SEL_TILES * kc + i)
            tile = jnp.where(valid & (offset >= 0), jnp.minimum(offset, n_tiles - 2), n_tiles - 1)
            cols = slice(i * NSA_TK, (i + 1) * NSA_TK)
            buf[:, :, cols] = s[:, :, cols] + bias_ref[0, tile]

    def values(h, kc):
        return v_ref[0, 0, h // half, key_rows(kc), :]

    def write_out(t, accs):
        rows = pl.ds(pl.multiple_of(t * tq, tq), tq)
        for j in range(half):
            o = _normalized_pair(accs[j], accs[half + j])
            o_ref[0, rows, j * LANES:(j + 1) * LANES] = o.astype(o_ref.dtype)

    _causal_item_stream(nqt, lambda t: t // per, NSA_HPG, scores, values, write_out,
                        (s0_ref, s1_ref), m_ref, acc_ref)


def _nsa_selected(q, k, v, bias, s):
    bsz = q.shape[0]
    tq = NSA_TQ
    n_off = bias.shape[1]
    assert s % (SEL_TILES * NSA_TK) == 0
    s_buf = pltpu.VMEM((NSA_HPG, tq, SEL_TILES * NSA_TK), F32)
    once = pl.Buffered(1)
    return pl.pallas_call(
        functools.partial(_nsa_selected_kernel, tq=tq),
        grid=(NSA_GROUPS, bsz),
        in_specs=[pl.BlockSpec((1, NSA_HPG, s, LANES), lambda g, b: (b, g, 0, 0), pipeline_mode=once),
                  pl.BlockSpec((1, 1, s, LANES), lambda g, b: (b, g, 0, 0)),
                  pl.BlockSpec((1, 1, 2, s, LANES), lambda g, b: (b, g, 0, 0, 0)),
                  pl.BlockSpec((1, n_off, NSA_HPG, tq, NSA_TK), lambda g, b: (g, 0, 0, 0, 0), pipeline_mode=once)],
        out_specs=pl.BlockSpec((1, s, NSA_HPG * NSA_DK), lambda g, b: (b, 0, g)),
        out_shape=jax.ShapeDtypeStruct((bsz, s, NSA_HEADS * NSA_DK), BF16),
        scratch_shapes=[s_buf, s_buf, pltpu.VMEM((NSA_HPG, tq, LANES), F32), pltpu.VMEM((NSA_HPG, tq, LANES), F32)],
        compiler_params=_cparams(("parallel", "parallel")),
        name="nsa_flash_sel",
    )(q, k, v, bias)


def _nsa_window(q, k, v, bias, s, window_steps):
    bsz = q.shape[0]
    tq = NSA_TQ
    n_off = bias.shape[1]
    s_buf = pltpu.VMEM((NSA_HPG, tq, window_steps * NSA_TK), F32)
    once = pl.Buffered(1)
    return pl.pallas_call(
        functools.partial(_nsa_window_kernel, tq=tq, window_steps=window_steps),
        grid=(NSA_GROUPS, bsz),
        in_specs=[pl.BlockSpec((1, NSA_HPG, s, LANES), lambda g, b: (b, g, 0, 0), pipeline_mode=once),
                  pl.BlockSpec((1, 1, s, LANES), lambda g, b: (b, g, 0, 0)),
                  pl.BlockSpec((1, 1, 2, s, LANES), lambda g, b: (b, g, 0, 0, 0)),
                  pl.BlockSpec((1, n_off, NSA_HPG, tq, NSA_TK), lambda g, b: (g, 0, 0, 0, 0), pipeline_mode=once)],
        out_specs=pl.BlockSpec((1, s, NSA_HPG * NSA_DK), lambda g, b: (b, 0, g)),
        out_shape=jax.ShapeDtypeStruct((bsz, s, NSA_HEADS * NSA_DK), BF16),
        scratch_shapes=[s_buf, s_buf],
        compiler_params=_cparams(("parallel", "parallel")),
        name="nsa_flash_win",
    )(q, k, v, bias)


def _hybrid_layer(h, bsz, s, layer, g_mix, g_mlp, w1, w2, w_in, conv_w, conv_b, wa, ba, wx, bx, lam,
                  q_norm, kv_norm, w_uq, w_ukv, qn_q, qn_k, w_out):
    d = h.shape[1]
    c = LRU_WIDTH
    o_kr = 2 * c + MLA_Q_RANK + MLA_KV_RANK
    w_kr = jnp.pad(w_in[:, o_kr:], ((0, 0), (MLA_NOPE, LANES - MLA_QK)))
    w_all = jnp.concatenate([w_in[:, :o_kr], w_kr], axis=1).astype(BF16)
    z_lru, q_lat, kv_lat, k_rope = _norm_proj(h, g_mix, w_all, (2 * c, MLA_Q_RANK, MLA_KV_RANK, LANES))
    lru = _rg_lru(z_lru.reshape(bsz, s, 2 * c), conv_w, conv_b, wa, ba, wx, bx, lam)
    q, k, v = _mla_prep(q_lat, kv_lat, k_rope, bsz, s, q_norm, kv_norm, w_uq, w_ukv, qn_q, qn_k)
    mla = _mla_attention(q, k, v)
    wo = w_out.astype(BF16)
    return _mix_mlp(h, [[lru.reshape(bsz * s, c)], [mla.reshape(bsz * s, -1)]], [wo[:c], wo[c:]],
                    g_mlp, w1, w2, layer)


def _nsa_layer(h, bsz, s, layer, g_mix, g_mlp, w1, w2, w_in, pos_k, w1_k, w2_k, pos_v, w1_v, w2_v,
               q_norm, k_norm, rel_bias, w_out):
    nq = NSA_HEADS * NSA_DK
    kvw = NSA_GROUPS * NSA_DK
    w_gate = _pad_cols(w_in[:, nq + 6 * kvw:], LANES)
    w_all = jnp.concatenate([w_in[:, :nq + 6 * kvw], w_gate], axis=1).astype(BF16)
    z_q, z_kc, z_vc, z_kv, z_gate = _norm_proj(h, g_mix, w_all, (nq, kvw, kvw, 4 * kvw, LANES))

    q, ks, kw, vs, vw = _nsa_prep(z_q, z_kv, bsz, s, q_norm, k_norm)
    ck, cv = _compress(z_kc, z_vc, bsz, s, pos_k, w1_k, w2_k, pos_v, w1_v, w2_v, k_norm[0])

    order = np.array(HEAD_ORDER)
    expand = np.zeros((3, LANES, NSA_HEADS * NSA_DK), np.float32)
    for br in range(3):
        for hd in range(NSA_HEADS):
            expand[br, 3 * hd + br, hd * NSA_DK:(hd + 1) * NSA_DK] = 1.0

    slot_heads = (np.arange(NSA_GROUPS)[:, None] * NSA_HPG + order[None, :]).reshape(-1)
    tab_slots = rel_bias.T[slot_heads]
    nh = s // CMP_STRIDE
    nc = (s - CMP_BLOCK) // CMP_STRIDE + 1
    cmp_bias = _bias_tiles(tab_slots, _cmp_bias_ids(s))
    sel_bias = _bias_tiles(tab_slots, _toeplitz_bucket_ids(None)[0])
    win_ids, n_win_tiles = _toeplitz_bucket_ids(WINDOW)
    win_bias = _bias_tiles(tab_slots, win_ids)

    nsel = s // SLC_BLOCK
    cstart = np.arange(nh) * CMP_STRIDE
    sstart = np.arange(nsel) * SLC_BLOCK
    ov = np.clip(np.minimum(cstart[:, None] + CMP_BLOCK, sstart[None, :] + SLC_BLOCK)
                 - np.maximum(cstart[:, None], sstart[None, :]), 0, None).astype(np.float32) / CMP_BLOCK
    ov[nc:] = 0.0
    ov_p = np.zeros((nh, LANES), np.float32)
    ov_p[:, HALF:HALF + nsel] = ov

    o_c, q_sel = _cmp_attention(q, ck, cv, cmp_bias, jnp.asarray(np.concatenate([ov_p, ov_p])), s)
    o_s = _nsa_selected(q_sel, ks, vs, sel_bias, s)
    window_steps = (n_win_tiles + 1) // (NSA_TK // NSA_TQ)
    o_w = _nsa_window(q, kw, vw, win_bias, s, window_steps)

    flat = lambda a: a.reshape(bsz * s, -1)
    return _mix_mlp(h, [[flat(o_c), flat(o_s), flat(o_w)]], [w_out.astype(BF16)],
                    g_mlp, w1, w2, layer,
                    gate=(z_gate, jnp.asarray(np.concatenate([expand, expand], axis=1), BF16)))


def kernel(x, rel_bias, norm_mix, norm_mlp, mlp_w1, mlp_w2, hy_w_in, lru_conv_w, lru_conv_b, lru_wa, lru_ba, lru_wx, lru_bx, lru_lambda, mla_q_norm, mla_kv_norm, mla_w_uq, mla_w_ukv, mla_qn_q, mla_qn_k, hy_w_out, nsa_w_in, nsa_cmp_pos_k, nsa_cmp_w1_k, nsa_cmp_w2_k, nsa_cmp_pos_v, nsa_cmp_w1_v, nsa_cmp_w2_v, nsa_q_norm, nsa_k_norm, nsa_w_out):
    bsz, s, d = x.shape
    depth = norm_mix.shape[0]
    h = x.reshape(bsz * s, d)
    w1_all, w2_all = mlp_w1.astype(BF16), mlp_w2.astype(BF16)
    for layer in range(depth):
        if layer % 2 == 0:
            e = layer // 2
            h = _hybrid_layer(h, bsz, s, layer, norm_mix[layer], norm_mlp[layer], w1_all, w2_all,
                              hy_w_in[e], lru_conv_w[e], lru_conv_b[e], lru_wa[e], lru_ba[e], lru_wx[e],
                              lru_bx[e], lru_lambda[e], mla_q_norm[e], mla_kv_norm[e], mla_w_uq[e],
                              mla_w_ukv[e], mla_qn_q[e], mla_qn_k[e], hy_w_out[e])
        else:
            o = layer // 2
            h = _nsa_layer(h, bsz, s, layer, norm_mix[layer], norm_mlp[layer], w1_all, w2_all,
                           nsa_w_in[o], nsa_cmp_pos_k[o], nsa_cmp_w1_k[o], nsa_cmp_w2_k[o],
                           nsa_cmp_pos_v[o], nsa_cmp_w1_v[o], nsa_cmp_w2_v[o], nsa_q_norm[o],
                           nsa_k_norm[o], rel_bias, nsa_w_out[o])
    return h.reshape(bsz, s, d)
```

```python
import functools
import math

import numpy as np
import jax
import jax.numpy as jnp
from jax import lax
from jax.experimental import pallas as pl
from jax.experimental.pallas import tpu as pltpu

F32 = jnp.float32
BF16 = jnp.bfloat16

NORM_EPS = 1e-6
MASK_VALUE = -1e30
FORCE_SCORE = 1e4
REMOVED_SCORE = -3e38

NUM_BUCKETS = 32
MAX_DISTANCE = 1024
MASKED_ID = NUM_BUCKETS
EXCLUDED_ID = NUM_BUCKETS + 1
LOG2E = math.log2(math.e)

LRU_WIDTH = 512
LRU_BLOCKS = 8
CONV_WIDTH = 4
LRU_C = 8.0

MLA_HEADS = 8
MLA_NOPE = 64
MLA_ROPE = 32
MLA_QK = MLA_NOPE + MLA_ROPE
MLA_V = 64
MLA_Q_RANK = 384
MLA_KV_RANK = 256
ROPE_THETA = 10000.0

NSA_HEADS = 16
NSA_GROUPS = 2
NSA_HPG = NSA_HEADS // NSA_GROUPS
NSA_DK = 64
CMP_BLOCK = 32
CMP_STRIDE = 16
CMP_HIDDEN = 256
SLC_BLOCK = 64
SLC_TOPN = 8
WINDOW = 512

LANES = 128
SUBLANES = 8
HALF = LANES // 2
NSA_TQ = 128
NSA_TK = 256
SEL_TILES = 2
MLA_TR = 128
MLA_TK = 512
HEAD_ORDER = (0, 2, 4, 6, 1, 3, 5, 7)
V7X_VMEM_BYTES = 64 * 1024 * 1024
VMEM_LIMIT = V7X_VMEM_BYTES * 7 // 8


def _cparams(sem):
    return pltpu.CompilerParams(dimension_semantics=sem, vmem_limit_bytes=VMEM_LIMIT)


def _rms(x, g, n=None):
    n = x.shape[-1] if n is None else n
    ms = jnp.sum(x * x, axis=-1, keepdims=True) * (1.0 / n)
    return x * lax.rsqrt(ms + NORM_EPS) * g


def _rms_lanes(x, g, n):
    y = x * x
    hi = y.astype(BF16)
    parts = jnp.concatenate([hi, (y - hi.astype(F32)).astype(BF16)], axis=1)
    ss = _dot(parts, jnp.ones((2 * LANES, LANES), BF16))
    return x * lax.rsqrt(ss * (1.0 / n) + NORM_EPS) * g


def _rms_half_lanes(x, g):
    y = x * x
    hi = y.astype(BF16)
    parts = jnp.concatenate([hi, (y - hi.astype(F32)).astype(BF16)], axis=1)
    row_half = lax.broadcasted_iota(jnp.int32, (2 * LANES, LANES), 0) % LANES // HALF
    col_half = lax.broadcasted_iota(jnp.int32, (2 * LANES, LANES), 1) // HALF
    ss = _dot(parts, jnp.where(row_half == col_half, 1.0, 0.0).astype(BF16))
    return x * lax.rsqrt(ss * (1.0 / HALF) + NORM_EPS) * g


def _gelu_tanh(x):
    return 0.5 * x * (1.0 + jnp.tanh(math.sqrt(2.0 / math.pi) * (x + 0.044715 * (x * x * x))))


def _sigmoid(x):
    return 1.0 / (1.0 + jnp.exp(-x))


def _dot(a, b):
    return jnp.dot(a, b, preferred_element_type=F32)


def _dot_t(a, b):
    return lax.dot_general(a, b, (((1,), (1,)), ((), ())), preferred_element_type=F32)


def _causal_item_stream(n_tiles, last_chunk, n_chains, scores, values, write_out, s_bufs, m_ref, acc_ref):
    m_ref[...] = jnp.full_like(m_ref, MASK_VALUE)
    acc_ref[...] = jnp.zeros_like(acc_ref)
    n_items = sum(last_chunk(t) + 1 for t in range(n_tiles))

    def advance(t, kc):
        wrap = kc >= last_chunk(t)
        return jnp.where(wrap, t + 1, t), jnp.where(wrap, 0, kc + 1)

    def stage_scores(t, kc, buf):
        valid = t < n_tiles
        scores(jnp.minimum(t, n_tiles - 1), jnp.where(valid, kc, 0), valid, buf)

    def stage_update(t, kc, buf, maybe_padding):
        tile_done = kc >= last_chunk(t)
        kc = jnp.where(t < n_tiles, kc, 0) if maybe_padding else kc
        accs = []
        for c in range(n_chains):
            m_old = m_ref[c]
            m_new = jnp.maximum(m_old, jnp.max(buf[c], axis=-1, keepdims=True))
            alpha = jnp.exp2(m_old - m_new)
            s = buf[c]
            p = jnp.exp2(s - jnp.concatenate([m_new] * (s.shape[1] // LANES), axis=1))
            acc = acc_ref[c] * alpha + _dot(p.astype(BF16), values(c, kc))
            m_ref[c] = jnp.where(tile_done, MASK_VALUE, m_new)
            acc_ref[c] = jnp.where(tile_done, 0.0, acc)
            accs.append(acc)
        if maybe_padding:
            pl.when(t < n_tiles)(lambda: write_out(t, accs))
        else:
            write_out(t, accs)

    start = (jnp.int32(0), jnp.int32(0))
    stage_scores(*start, s_bufs[0])

    def two_items(_, item):
        item1 = advance(*item)
        stage_scores(*item1, s_bufs[1])
        stage_update(*item, s_bufs[0], False)
        item2 = advance(*item1)
        stage_scores(*item2, s_bufs[0])
        stage_update(*item1, s_bufs[1], n_items % 2 == 1)
        return item2

    lax.fori_loop(0, (n_items + 1) // 2, two_items, start)


def _normalized_pair(acc_even, acc_odd):
    lo = lax.broadcasted_iota(jnp.int32, acc_even.shape, 1) < HALF
    return jnp.where(lo, acc_even / acc_even[:, LANES - 1:LANES], acc_odd / acc_odd[:, 0:1])


def _with_ones_column(v, odd):
    lane = lax.broadcasted_iota(jnp.int32, v.shape, 1)
    return jnp.where(lane == (0 if odd else LANES - 1), 1.0, v)


def _norm_proj_kernel(x_ref, g_ref, w_ref, *out_refs, splits):
    xn = _rms(x_ref[...], g_ref[...]).astype(BF16)
    off = 0
    for o_ref, n in zip(out_refs, splits):
        o_ref[...] = _dot(xn, w_ref[:, off:off + n]).astype(o_ref.dtype)
        off += n


def _norm_proj(x, g, w, splits, tm=512):
    t, d = x.shape
    n = w.shape[1]
    assert sum(splits) == n and t % tm == 0
    return pl.pallas_call(
        functools.partial(_norm_proj_kernel, splits=splits),
        grid=(t // tm,),
        in_specs=[pl.BlockSpec((tm, d), lambda i: (i, 0)),
                  pl.BlockSpec((1, d), lambda i: (0, 0)),
                  pl.BlockSpec((d, n), lambda i: (0, 0))],
        out_specs=[pl.BlockSpec((tm, s), lambda i: (i, 0)) for s in splits],
        out_shape=[jax.ShapeDtypeStruct((t, s), F32) for s in splits],
        compiler_params=_cparams(("parallel",)),
        name="norm_proj",
    )(x, g.reshape(1, d), w)


def _mix_mlp_kernel(*refs, group_sizes, gated):
    n_a = sum(group_sizes)
    n_g = len(group_sizes)
    h_ref = refs[0]
    a_refs = refs[1:1 + n_a]
    wo_refs = refs[1 + n_a:1 + n_a + n_g]
    rest = refs[1 + n_a + n_g:]
    if gated:
        zg_ref, e_ref = rest[:2]
        rest = rest[2:]
    g_ref, w1_ref, w2_ref, out_ref, hres, xn, acc = rest
    f = pl.program_id(1)

    @pl.when(f == 0)
    def _():
        h1 = h_ref[...]
        if gated:
            gate = _sigmoid(zg_ref[...])
            g_hi = gate.astype(BF16)
            g_parts = jnp.concatenate([g_hi, (gate - g_hi.astype(F32)).astype(BF16)], axis=1)
        k = 0
        for gi, gs in enumerate(group_sizes):
            a = None
            for j in range(gs):
                aj = a_refs[k + j][...].astype(F32)
                if gated:
                    aj = aj * _dot(g_parts, e_ref[j])
                a = aj if a is None else a + aj
            k += gs
            h1 = h1 + _dot(a.astype(BF16), wo_refs[gi][...])
        hres[...] = h1
        xn[...] = _rms(h1, g_ref[...]).astype(BF16)
        acc[...] = jnp.zeros_like(acc)

    u = jnp.maximum(_dot(xn[...], w1_ref[...]), 0.0)
    acc[...] += _dot((u * u).astype(BF16), w2_ref[...])

    @pl.when(f == pl.num_programs(1) - 1)
    def _():
        out_ref[...] = hres[...] + acc[...]


def _mix_mlp(h, groups, wos, g, w1, w2, layer, gate=None, tm=512, tf=2048):
    t, d = h.shape
    ff = w1.shape[2]
    group_sizes = tuple(len(gr) for gr in groups)
    a_list = [a for gr in groups for a in gr]
    extra = [] if gate is None else list(gate)
    in_specs = [pl.BlockSpec((tm, d), lambda i, f: (i, 0))]
    in_specs += [pl.BlockSpec((tm, a.shape[1]), lambda i, f: (i, 0)) for a in a_list]
    in_specs += [pl.BlockSpec(w.shape, lambda i, f: (0, 0)) for w in wos]
    if gate is not None:
        in_specs += [pl.BlockSpec((tm, LANES), lambda i, f: (i, 0)),
                     pl.BlockSpec(gate[1].shape, lambda i, f: (0, 0, 0))]
    in_specs += [pl.BlockSpec((1, d), lambda i, f: (0, 0)),
                 pl.BlockSpec((None, d, tf), lambda i, f: (layer, 0, f)),
                 pl.BlockSpec((None, tf, d), lambda i, f: (layer, f, 0))]
    return pl.pallas_call(
        functools.partial(_mix_mlp_kernel, group_sizes=group_sizes, gated=gate is not None),
        grid=(t // tm, ff // tf),
        in_specs=in_specs,
        out_specs=pl.BlockSpec((tm, d), lambda i, f: (i, 0)),
        out_shape=jax.ShapeDtypeStruct((t, d), F32),
        scratch_shapes=[pltpu.VMEM((tm, d), F32), pltpu.VMEM((tm, d), BF16),
                        pltpu.VMEM((tm, d), F32)],
        compiler_params=_cparams(("parallel", "arbitrary")),
        name="mix_mlp",
    )(h, *a_list, *wos, *extra, g.reshape(1, d), w1, w2)


def _lru_kernel(xr_ref, xg_ref, cw_ref, cb_ref, wa_ref, ba_ref, wx_ref, bx_ref, lam_ref,
                o_ref, tail_ref, hc_ref, *, tc):
    c = pl.program_id(1)

    @pl.when(c == 0)
    def _():
        tail_ref[...] = jnp.zeros_like(tail_ref)
        hc_ref[...] = jnp.zeros_like(hc_ref)

    x = xr_ref[0]
    width = x.shape[1]
    row = lax.broadcasted_iota(jnp.int32, (tc, width), 0)
    tail_row = lax.broadcasted_iota(jnp.int32, (SUBLANES, width), 0)
    tail = tail_ref[...]
    cw = cw_ref[...]
    y = x * cw[CONV_WIDTH - 1:CONV_WIDTH]
    for s in range(1, CONV_WIDTH):
        sh = pltpu.roll(x, s, 0)
        head = jnp.where(tail_row < s, pltpu.roll(tail, s, 0), sh[0:SUBLANES])
        sh = jnp.concatenate([head, sh[SUBLANES:]], axis=0)
        y = y + sh * cw[CONV_WIDTH - 1 - s:CONV_WIDTH - s]
    y = y + cb_ref[...]
    tail_ref[...] = x[tc - SUBLANES:tc]

    yb = y.astype(BF16)
    r = _sigmoid(_dot(yb, wa_ref[...]) + ba_ref[...])
    i = _sigmoid(_dot(yb, wx_ref[...]) + bx_ref[...])
    nl = -lam_ref[...]
    softplus = jnp.maximum(nl, 0.0) + jnp.log(1.0 + jnp.exp(-jnp.abs(nl)))
    log_a = (-LRU_C) * r * softplus
    a = jnp.exp(log_a)
    mult = jnp.sqrt(1.0 - jnp.exp(2.0 * log_a))
    mult = jnp.where((row == 0) & (c == 0), 1.0, mult)
    b = mult * (i * y)

    d = 1
    while d < tc:
        keep = row >= d
        a_sh = jnp.where(keep, pltpu.roll(a, d, 0), 1.0)
        b_sh = jnp.where(keep, pltpu.roll(b, d, 0), 0.0)
        b = a * b_sh + b
        a = a * a_sh
        d *= 2
    h = b + a * hc_ref[...]
    hc_ref[...] = h[tc - 1:tc]
    o_ref[0] = (h * _gelu_tanh(xg_ref[0])).astype(o_ref.dtype)


def _block_diag(w):
    n, d, e = w.shape
    eye = jnp.eye(n, dtype=w.dtype)
    return (w[:, :, None, :] * eye[:, None, :, None]).reshape(n * d, n * e)


def _rg_lru(z_lru, conv_w, conv_b, wa, ba, wx, bx, lam, tc=256):
    bsz, s, _ = z_lru.shape
    c = LRU_WIDTH
    tc = min(tc, s)
    vec = lambda v: pl.BlockSpec(v, lambda b, i: (0, 0))
    return pl.pallas_call(
        functools.partial(_lru_kernel, tc=tc),
        grid=(bsz, s // tc),
        in_specs=[pl.BlockSpec((1, tc, c), lambda b, i: (b, i, 0)),
                  pl.BlockSpec((1, tc, c), lambda b, i: (b, i, 1)),
                  vec((CONV_WIDTH, c)), vec((1, c)), vec((c, c)), vec((1, c)),
                  vec((c, c)), vec((1, c)), vec((1, c))],
        out_specs=pl.BlockSpec((1, tc, c), lambda b, i: (b, i, 0)),
        out_shape=jax.ShapeDtypeStruct((bsz, s, c), BF16),
        scratch_shapes=[pltpu.VMEM((SUBLANES, c), F32), pltpu.VMEM((1, c), F32)],
        compiler_params=_cparams(("parallel", "arbitrary")),
        name="rg_lru",
    )(z_lru, z_lru, conv_w, conv_b.reshape(1, c), _block_diag(wa).astype(BF16), ba.reshape(1, c),
      _block_diag(wx).astype(BF16), bx.reshape(1, c), lam.reshape(1, c))


def _rope(x, cos, sin_lo, sin_hi):
    return (x * cos + pltpu.roll(x, LANES - MLA_ROPE // 2, 1) * sin_lo
            + pltpu.roll(x, MLA_ROPE // 2, 1) * sin_hi)


def _mla_prep_kernel(ql_ref, kvl_ref, kr_ref, gq_ref, gkv_ref, wq_ref, wk_ref, wv_ref,
                     nq_ref, nk_ref, cos_ref, s1_ref, s2_ref, q_out, k_out, v_out):
    qn = _rms(ql_ref[...], gq_ref[...]).astype(BF16)
    kvn = _rms(kvl_ref[...], gkv_ref[...]).astype(BF16)
    kr = kr_ref[...]
    cos, s1, s2 = cos_ref[...], s1_ref[...], s2_ref[...]
    scale = MLA_QK ** -0.5 * LOG2E
    for h in range(MLA_HEADS):
        sl = slice(h * LANES, (h + 1) * LANES)
        qh = _rms_lanes(_dot(qn, wq_ref[:, sl]), nq_ref[...], MLA_QK)
        q_out[0, h] = (_rope(qh, cos, s1, s2) * scale).astype(BF16)
        kh = _rms_lanes(_dot(kvn, wk_ref[:, sl]) + kr, nk_ref[...], MLA_QK)
        k_out[0, h] = _rope(kh, cos, s1, s2).astype(BF16)
        v_out[0, h] = _with_ones_column(_dot(kvn, wv_ref[:, sl]), h % 2 == 1).astype(BF16)


def _rope_tables(s):
    half = MLA_ROPE // 2
    freqs = ROPE_THETA ** (-jnp.arange(half, dtype=F32) / half)
    ang = jnp.arange(s, dtype=F32)[:, None] * freqs[None, :]
    cos, sin = jnp.cos(ang), jnp.sin(ang)
    z = lambda n: jnp.zeros((s, n), F32)
    cos_t = jnp.concatenate([jnp.ones((s, MLA_NOPE), F32), cos, cos, z(LANES - MLA_QK)], axis=1)
    s_lo = jnp.concatenate([z(MLA_NOPE), -sin, z(LANES - MLA_NOPE - half)], axis=1)
    s_hi = jnp.concatenate([z(MLA_NOPE + half), sin, z(LANES - MLA_QK)], axis=1)
    return cos_t, s_lo, s_hi


def _pad_cols(w, n):
    return jnp.pad(w, ((0, 0),) * (w.ndim - 1) + ((0, n - w.shape[-1]),))


def _mla_prep(q_lat, kv_lat, k_rope, bsz, s, q_norm, kv_norm, w_uq, w_ukv, qn_q, qn_k, tm=512):
    tm = min(tm, s)
    nt = s // tm
    hl = MLA_HEADS * LANES
    wq = _pad_cols(w_uq.reshape(MLA_Q_RANK, MLA_HEADS, MLA_QK), LANES).reshape(MLA_Q_RANK, hl)
    wkv = w_ukv.reshape(MLA_KV_RANK, MLA_HEADS, MLA_NOPE + MLA_V)
    wk = _pad_cols(wkv[..., :MLA_NOPE], LANES).reshape(MLA_KV_RANK, hl)
    wv = wkv[..., MLA_NOPE:]
    zero = jnp.zeros_like(wv)
    even = (jnp.arange(MLA_HEADS) % 2 == 0)[None, :, None]
    wv = jnp.concatenate([jnp.where(even, wv, zero), jnp.where(even, zero, wv)], axis=-1)
    wv = wv.reshape(MLA_KV_RANK, hl)
    cos_t, s_lo, s_hi = _rope_tables(s)
    row = lambda n: pl.BlockSpec((tm, n), lambda b, i: (b * nt + i, 0))
    full = lambda a: pl.BlockSpec(a.shape, lambda b, i: (0, 0))
    tab = pl.BlockSpec((tm, LANES), lambda b, i: (i, 0))
    head_out = pl.BlockSpec((1, MLA_HEADS, tm, LANES), lambda b, i: (b, 0, i, 0))
    args = (q_lat, kv_lat, k_rope, q_norm.reshape(1, -1), kv_norm.reshape(1, -1),
            wq.astype(BF16), wk.astype(BF16), wv.astype(BF16),
            _pad_cols(qn_q.reshape(1, -1), LANES), _pad_cols(qn_k.reshape(1, -1), LANES),
            cos_t, s_lo, s_hi)
    in_specs = [row(MLA_Q_RANK), row(MLA_KV_RANK), row(LANES)] + [full(a) for a in args[3:10]] + [tab] * 3
    shp = jax.ShapeDtypeStruct((bsz, MLA_HEADS, s, LANES), BF16)
    return pl.pallas_call(
        _mla_prep_kernel,
        grid=(bsz, nt),
        in_specs=in_specs,
        out_specs=[head_out] * 3,
        out_shape=[shp] * 3,
        compiler_params=_cparams(("parallel", "parallel")),
        name="mla_prep",
    )(*args)


def _mla_attn_kernel(q_ref, k_ref, v_ref, mask_ref, o_ref, s0_ref, s1_ref, m_ref, acc_ref, *, tq):
    n_rc = tq // MLA_TR
    ratio = MLA_TK // MLA_TR
    n_tiles = mask_ref.shape[0]

    def key_rows(kc):
        return pl.ds(pl.multiple_of(kc * MLA_TK, MLA_TK), MLA_TK)

    def scores(qi, kc, valid, buf):
        rows = pl.ds(pl.multiple_of(qi * tq, tq), tq)
        for j in range(2):
            s = _dot_t(q_ref[0, j, rows, :], k_ref[0, j, key_rows(kc), :])
            for rc in range(n_rc):
                offset = qi * n_rc + rc - ratio * kc
                tile = jnp.where(valid, jnp.clip(offset + 1, 0, n_tiles - 1), 0)
                buf[j * n_rc + rc] = s[rc * MLA_TR:(rc + 1) * MLA_TR] + mask_ref[tile]

    def values(c, kc):
        return v_ref[0, c // n_rc, key_rows(kc), :]

    def write_out(qi, accs):
        for rc in range(n_rc):
            rows = pl.ds(pl.multiple_of(qi * tq + rc * MLA_TR, MLA_TR), MLA_TR)
            o_ref[0, rows, :] = _normalized_pair(accs[rc], accs[n_rc + rc]).astype(o_ref.dtype)

    _causal_item_stream(q_ref.shape[2] // tq, lambda qi: (qi * tq + tq - 1) // MLA_TK, 2 * n_rc,
                        scores, values, write_out, (s0_ref, s1_ref), m_ref, acc_ref)


def _mla_causal_tiles():
    i = np.arange(MLA_TR)[:, None]
    j = np.arange(MLA_TK)[None, :]
    tiles = [np.full((MLA_TR, MLA_TK), -np.inf, np.float32)]
    for d in range(MLA_TK // MLA_TR):
        tiles.append(np.where(j <= i + d * MLA_TR, 0.0, -np.inf).astype(np.float32))
    tiles.append(np.zeros((MLA_TR, MLA_TK), np.float32))
    return np.stack(tiles)


def _mla_attention(q, k, v, tq=512):
    bsz, nh, s, _ = q.shape
    tq = min(tq, s)
    assert tq % MLA_TK == 0 and tq % MLA_TR == 0
    n_chains = 2 * (tq // MLA_TR)
    mask = _mla_causal_tiles()
    seq_spec = pl.BlockSpec((1, 2, s, LANES), lambda b, hp: (b, hp, 0, 0))
    return pl.pallas_call(
        functools.partial(_mla_attn_kernel, tq=tq),
        grid=(bsz, nh // 2),
        in_specs=[seq_spec, seq_spec, seq_spec, pl.BlockSpec(mask.shape, lambda b, hp: (0, 0, 0))],
        out_specs=pl.BlockSpec((1, s, LANES), lambda b, hp: (b, 0, hp)),
        out_shape=jax.ShapeDtypeStruct((bsz, s, nh * MLA_V), BF16),
        scratch_shapes=[pltpu.VMEM((n_chains, MLA_TR, MLA_TK), F32), pltpu.VMEM((n_chains, MLA_TR, MLA_TK), F32),
                        pltpu.VMEM((n_chains, MLA_TR, LANES), F32), pltpu.VMEM((n_chains, MLA_TR, LANES), F32)],
        compiler_params=_cparams(("parallel", "parallel")),
        name="mla_attn",
    )(q, k, v, jnp.asarray(mask))


def _t5_bucket_np(dist):
    n = np.maximum(dist, 0)
    max_exact = NUM_BUCKETS // 2
    nf = np.maximum(n, 1).astype(np.float32)
    large = max_exact + (np.log(nf / np.float32(max_exact)) / np.float32(math.log(MAX_DISTANCE / max_exact))
                         * np.float32(NUM_BUCKETS - max_exact)).astype(np.int32)
    large = np.minimum(large, NUM_BUCKETS - 1)
    return np.where(n < max_exact, n, large).astype(np.int32)


def _bias_tiles_kernel(tab_ref, idx_ref, o_ref, *, buckets):
    slot = pl.program_id(0) * NSA_HPG + pl.program_id(1)
    for i, present in enumerate(buckets):
        idx = idx_ref[i]
        acc = jnp.where(idx == EXCLUDED_ID, -jnp.inf, MASK_VALUE).astype(F32)
        for b in present:
            acc = jnp.where(idx == b, tab_ref[slot, b] * LOG2E, acc)
        o_ref[0, i, 0] = acc


def _bias_tiles(tab_slots, idx):
    n, r, c = idx.shape
    buckets = tuple(tuple(int(b) for b in np.unique(tile) if b < NUM_BUCKETS) for tile in idx)
    return pl.pallas_call(
        functools.partial(_bias_tiles_kernel, buckets=buckets),
        grid=(NSA_GROUPS, NSA_HPG),
        in_specs=[pl.BlockSpec(memory_space=pltpu.SMEM),
                  pl.BlockSpec((n, r, c), lambda g, p: (0, 0, 0))],
        out_specs=pl.BlockSpec((1, n, 1, r, c), lambda g, p: (g, 0, p, 0, 0)),
        out_shape=jax.ShapeDtypeStruct((NSA_GROUPS, n, NSA_HPG, r, c), F32),
        compiler_params=_cparams(("parallel", "parallel")),
        name="bias_tiles",
    )(tab_slots, jnp.asarray(idx))


def _cmp_bias_ids(s):
    nqt = s // NSA_TQ
    nh = s // CMP_STRIDE
    per_tile = NSA_TQ // CMP_STRIDE
    i = np.arange(NSA_TQ)[:, None]
    c = np.arange(nh)[None, :]
    dist = i - (CMP_BLOCK - 1) - CMP_STRIDE * (c - per_tile * (nqt - 1))
    ids = np.where(dist >= 0, _t5_bucket_np(dist), MASKED_ID).astype(np.int32)
    return ids[None]


def _toeplitz_bucket_ids(window):
    i = np.arange(NSA_TQ)[:, None]
    j = np.arange(NSA_TK)[None, :]
    tiles = []
    d = 0
    while True:
        dist = d * NSA_TQ + i - j
        ids = _t5_bucket_np(dist)
        ok = dist >= 0
        if window is not None:
            ok &= dist < window
        ids = np.where(ok, ids, MASKED_ID).astype(np.int32)
        tiles.append(ids)
        if window is None and (ids == NUM_BUCKETS - 1).all():
            break
        if window is not None and not ok.any():
            tiles.pop()
            break
        d += 1
    n_real = len(tiles)
    tiles.append(np.full((NSA_TQ, NSA_TK), EXCLUDED_ID, np.int32))
    return np.stack(tiles), n_real


def _split_halves(x):
    lo = lax.broadcasted_iota(jnp.int32, x.shape, 1) < HALF
    return jnp.where(lo, x, 0.0), pltpu.roll(jnp.where(lo, 0.0, x), HALF, 1)


def _value_layouts(x):
    lo = lax.broadcasted_iota(jnp.int32, x.shape, 1) < HALF
    g0e = jnp.where(lo, x, 0.0)
    g1o = jnp.where(lo, 0.0, x)
    pair = lambda e, o: (_with_ones_column(e, False), _with_ones_column(o, True))
    return pair(g0e, pltpu.roll(g0e, HALF, 1)), pair(pltpu.roll(g1o, HALF, 1), g1o)


def _nsa_prep_kernel(zq_ref, ks_ref, vs_ref, kw_ref, vw_ref, gq_ref, gks_ref, gkw_ref,
                     q_out, ks_out, kw_out, vs_out, vw_out, *, tm):
    i = pl.program_id(1)
    scale = NSA_DK ** -0.5 * LOG2E
    for j in range(NSA_HEADS // 2):
        e, o = _split_halves(_rms_half_lanes(zq_ref[:, j * LANES:(j + 1) * LANES], gq_ref[...]) * scale)
        g = (2 * j) // NSA_HPG
        pe = (2 * j - NSA_HPG * g) // 2
        q_out[0, NSA_HPG * g + pe] = e.astype(BF16)
        q_out[0, NSA_HPG * g + NSA_HPG // 2 + pe] = o.astype(BF16)

    lane = lax.broadcasted_iota(jnp.int32, (tm, LANES), 1)
    pos = i * tm + lax.broadcasted_iota(jnp.int32, (tm, LANES), 0)
    block_tag = jnp.where(lane - HALF == pos // SLC_BLOCK, MASK_VALUE, 0.0)
    for g, x in enumerate(_split_halves(_rms_half_lanes(ks_ref[...], gks_ref[...]))):
        ks_out[0, g] = (x + block_tag).astype(BF16)
    for g, x in enumerate(_split_halves(_rms_half_lanes(kw_ref[...], gkw_ref[...]))):
        kw_out[0, g] = x.astype(BF16)
    for ref, out in ((vs_ref, vs_out), (vw_ref, vw_out)):
        for g, (ve, vo) in enumerate(_value_layouts(ref[...])):
            out[0, g, 0] = ve.astype(BF16)
            out[0, g, 1] = vo.astype(BF16)


def _nsa_prep(z_q, z_kv, bsz, s, q_norm, k_norm, tm=512):
    tm = min(tm, s)
    nt = s // tm
    row = lambda n, c: pl.BlockSpec((tm, n), lambda b, i: (b * nt + i, c))
    gain = pl.BlockSpec((1, LANES), lambda b, i: (0, 0))
    kspec = pl.BlockSpec((1, NSA_GROUPS, tm, LANES), lambda b, i: (b, 0, i, 0))
    vspec = pl.BlockSpec((1, NSA_GROUPS, 2, tm, LANES), lambda b, i: (b, 0, 0, i, 0))
    kshape = jax.ShapeDtypeStruct((bsz, NSA_GROUPS, s, LANES), BF16)
    vshape = jax.ShapeDtypeStruct((bsz, NSA_GROUPS, 2, s, LANES), BF16)
    pad = lambda v: jnp.tile(v.reshape(1, -1), (1, LANES // NSA_DK))
    return pl.pallas_call(
        functools.partial(_nsa_prep_kernel, tm=tm),
        grid=(bsz, nt),
        in_specs=[row(NSA_HEADS * NSA_DK, 0), row(LANES, 0), row(LANES, 1), row(LANES, 2), row(LANES, 3),
                  gain, gain, gain],
        out_specs=[pl.BlockSpec((1, NSA_HEADS, tm, LANES), lambda b, i: (b, 0, i, 0)),
                   kspec, kspec, vspec, vspec],
        out_shape=[jax.ShapeDtypeStruct((bsz, NSA_HEADS, s, LANES), BF16), kshape, kshape, vshape, vshape],
        compiler_params=_cparams(("parallel", "parallel")),
        name="nsa_prep",
    )(z_q, z_kv, z_kv, z_kv, z_kv, pad(q_norm), pad(k_norm[1]), pad(k_norm[2]))


def _compress_kernel(xk_ref, xv_ref, pk_ref, pv_ref, w1k_ref, w1v_ref, w2k_ref, w2v_ref, gk_ref,
                     ck_out, cv_out):
    def mlp(x, p_ref, w1_ref, w2_ref):
        nh = x.shape[0]
        top = _dot((x + p_ref[0:1]).astype(BF16), w1_ref[0])
        bot = _dot((x + p_ref[1:2]).astype(BF16), w1_ref[1])
        hid = _gelu_tanh(top + pltpu.roll(bot, nh - 1, 0))
        return _dot(hid.astype(BF16), w2_ref[...])

    nh = xk_ref.shape[1]
    yk = mlp(xk_ref[0], pk_ref, w1k_ref, w2k_ref)
    for g, x in enumerate(_split_halves(yk)):
        ck = _rms(x, gk_ref[...], NSA_DK)
        ck_out[0, g, 0:nh] = ck
        ck_out[0, g, nh:2 * nh] = ck
    yv = mlp(xv_ref[0], pv_ref, w1v_ref, w2v_ref)
    for g, layouts in enumerate(_value_layouts(yv)):
        for parity, v in enumerate(layouts):
            cv_out[0, g, parity, 0:nh] = v
            cv_out[0, g, parity, nh:2 * nh] = v


def _compress_weights(pos, w1, w2):
    half = CMP_BLOCK // 2
    g = NSA_GROUPS
    eye = jnp.eye(g, dtype=F32)
    w1 = w1.reshape(2, half, NSA_DK, CMP_HIDDEN)
    w1 = w1[:, :, None, :, None, :] * eye[None, None, :, None, :, None]
    w1 = w1.reshape(2, half * g * NSA_DK, g * CMP_HIDDEN).astype(BF16)
    p = jnp.broadcast_to(pos.reshape(2, half, 1, NSA_DK), (2, half, g, NSA_DK)).reshape(2, half * g * NSA_DK)
    w2 = (w2[None, :, None, :] * eye[:, None, :, None]).reshape(g * CMP_HIDDEN, g * NSA_DK).astype(BF16)
    return p, w1, w2


def _compress(z_kc, z_vc, bsz, s, pos_k, w1_k, w2_k, pos_v, w1_v, w2_v, k_norm0):
    nh = s // CMP_STRIDE
    feat = CMP_STRIDE * LANES
    pk, w1k, w2k = _compress_weights(pos_k, w1_k, w2_k)
    pv, w1v, w2v = _compress_weights(pos_v, w1_v, w2_v)
    full = lambda a: pl.BlockSpec(a.shape, lambda b: (0,) * a.ndim)
    xspec = pl.BlockSpec((1, nh, feat), lambda b: (b, 0, 0))
    gk = _pad_cols(k_norm0.reshape(1, -1), LANES)
    return pl.pallas_call(
        _compress_kernel,
        grid=(bsz,),
        in_specs=[xspec, xspec, full(pk), full(pv), full(w1k), full(w1v), full(w2k), full(w2v), full(gk)],
        out_specs=[pl.BlockSpec((1, NSA_GROUPS, 2 * nh, LANES), lambda b: (b, 0, 0, 0)),
                   pl.BlockSpec((1, NSA_GROUPS, 2, 2 * nh, LANES), lambda b: (b, 0, 0, 0, 0))],
        out_shape=[jax.ShapeDtypeStruct((bsz, NSA_GROUPS, 2 * nh, LANES), F32),
                   jax.ShapeDtypeStruct((bsz, NSA_GROUPS, 2, 2 * nh, LANES), F32)],
        compiler_params=_cparams(("parallel",)),
        name="nsa_compress",
    )(z_kc.reshape(bsz, nh, feat), z_vc.reshape(bsz, nh, feat), pk, pv, w1k, w1v, w2k, w2v, gk)


def _cmp_attn_kernel(q_ref, ck_ref, cv_ref, bias_ref, ov_ref, o_ref, q2_ref, s0_ref, s1_ref,
                     rank0_ref, rank1_ref, *, tq, nsel):
    nqt = q_ref.shape[2] // tq
    assert nqt % 2 == 0
    half = NSA_HPG // 2
    nh = bias_ref.shape[4]
    per_tile = tq // CMP_STRIDE
    c_last = per_tile * (nqt - 1)
    n_blk = LANES - HALF

    def block_window(t):
        start = lax.rem(per_tile * t + (nh - c_last % nh), nh)
        return pl.ds(pl.multiple_of(start, per_tile), nh)

    def query_rows(t):
        return pl.ds(pl.multiple_of(t * tq, tq), tq)

    def stage_scores(t, buf):
        t = jnp.minimum(t, nqt - 1)
        col = lax.broadcasted_iota(jnp.int32, (nh, LANES), 0)
        lane_k = lax.broadcasted_iota(jnp.int32, (nh, LANES), 1)
        wrapped = (lane_k == HALF) & (col < c_last - per_tile * t)
        ck = jnp.where(wrapped, MASK_VALUE, ck_ref[0, 0, block_window(t), :]).astype(BF16)
        lane_q = lax.broadcasted_iota(jnp.int32, (NSA_HPG, tq, LANES), 2)
        q = q_ref[0, :, query_rows(t), :] + jnp.where(lane_q == HALF, 1.0, 0.0).astype(BF16)
        s = _dot_t(q.reshape(NSA_HPG * tq, LANES), ck)
        buf[...] = s.reshape(NSA_HPG, tq, nh) + bias_ref[0, 0]

    def stage_select(t, buf, rank_ref):
        cols = block_window(t)
        cv = [cv_ref[0, 0, parity, cols, :].astype(BF16) for parity in range(2)]
        ov = ov_ref[cols, :].astype(BF16)
        row_ok = t * tq + lax.broadcasted_iota(jnp.int32, (tq, nh), 0) >= CMP_BLOCK - 1
        outs = []
        psum = None
        for h in range(NSA_HPG):
            s = buf[h]
            e = jnp.exp2(s - jnp.max(s, axis=-1, keepdims=True))
            p = jnp.where(row_ok, e / jnp.sum(e, axis=-1, keepdims=True), 0.0)
            outs.append(_dot(p.astype(BF16), cv[h // half]))
            psum = p if psum is None else psum + p
        lane_lo = lax.broadcasted_iota(jnp.int32, (tq, LANES), 1) < HALF
        for j in range(half):
            o = jnp.where(lane_lo, outs[j], outs[half + j])
            o_ref[0, query_rows(t), j * LANES:(j + 1) * LANES] = o.astype(o_ref.dtype)

        hi = psum.astype(BF16)
        lo = (psum - hi.astype(F32)).astype(BF16)
        imp = _dot(hi, ov) + _dot(lo, ov)

        blk = lax.broadcasted_iota(jnp.int32, (n_blk, tq), 0)
        qpos = t * tq + lax.broadcasted_iota(jnp.int32, (n_blk, tq), 1)
        cur = qpos // SLC_BLOCK
        valid = blk * SLC_BLOCK <= qpos
        forced = (blk == 0) | (blk == cur) | (blk == cur - 1)
        score = jnp.where(valid & forced, FORCE_SCORE, jnp.where(valid, imp.T[HALF:], -1.0))
        score = jnp.where(blk < nsel, score, REMOVED_SCORE)
        rank_ref[...] = score
        rank = jnp.zeros((n_blk, tq), F32)
        for j in range(n_blk):
            sj = jnp.broadcast_to(rank_ref[j:j + 1, :], (n_blk, tq))
            beats = (sj > score) | ((sj == score) & (blk > j))
            rank = rank + jnp.where(beats, 1.0, 0.0)
        unselected = jnp.where(rank < min(SLC_TOPN, nsel), 0.0, 1.0)
        unselected = jnp.concatenate([jnp.zeros((HALF, tq), F32), unselected], axis=0).T.astype(BF16)
        for h in range(NSA_HPG):
            q2_ref[0, h, query_rows(t), :] = q_ref[0, h, query_rows(t), :] + unselected

    stage_scores(0, s0_ref)

    def two_tiles(j, carry):
        t = 2 * j
        stage_scores(t + 1, s1_ref)
        stage_select(t, s0_ref, rank0_ref)
        stage_scores(t + 2, s0_ref)
        stage_select(t + 1, s1_ref, rank1_ref)
        return carry

    lax.fori_loop(0, nqt // 2, two_tiles, 0)


def _cmp_attention(q, ck, cv, bias, ov, s):
    bsz = q.shape[0]
    tq = NSA_TQ
    nh = s // CMP_STRIDE
    nsel = s // SLC_BLOCK
    s_buf = pltpu.VMEM((NSA_HPG, tq, nh), F32)
    rank_buf = pltpu.VMEM((LANES - HALF, tq), F32)
    return pl.pallas_call(
        functools.partial(_cmp_attn_kernel, tq=tq, nsel=nsel),
        grid=(NSA_GROUPS, bsz),
        in_specs=[pl.BlockSpec((1, NSA_HPG, s, LANES), lambda g, b: (b, g, 0, 0), pipeline_mode=pl.Buffered(1)),
                  pl.BlockSpec((1, 1, 2 * nh, LANES), lambda g, b: (b, g, 0, 0)),
                  pl.BlockSpec((1, 1, 2, 2 * nh, LANES), lambda g, b: (b, g, 0, 0, 0)),
                  pl.BlockSpec((1, 1, NSA_HPG, tq, nh), lambda g, b: (g, 0, 0, 0, 0)),
                  pl.BlockSpec((2 * nh, LANES), lambda g, b: (0, 0))],
        out_specs=[pl.BlockSpec((1, s, NSA_HPG * NSA_DK), lambda g, b: (b, 0, g)),
                   pl.BlockSpec((1, NSA_HPG, s, LANES), lambda g, b: (b, g, 0, 0))],
        out_shape=[jax.ShapeDtypeStruct((bsz, s, NSA_HEADS * NSA_DK), BF16),
                   jax.ShapeDtypeStruct(q.shape, BF16)],
        scratch_shapes=[s_buf, s_buf, rank_buf, rank_buf],
        compiler_params=_cparams(("parallel", "parallel")),
        name="nsa_cmp_attn",
    )(q, ck, cv, bias, ov)


def _nsa_window_kernel(q_ref, k_ref, v_ref, bias_ref, o_ref, s0_ref, s1_ref, *, tq, window_steps):
    nqt = q_ref.shape[2] // tq
    assert nqt % 2 == 0
    ratio = NSA_TK // tq
    n_tiles = bias_ref.shape[1]
    half = NSA_HPG // 2

    def key_chunk(t, i):
        return t // ratio - (window_steps - 1) + i

    def key_rows(t, i):
        return pl.ds(pl.multiple_of(jnp.maximum(key_chunk(t, i), 0) * NSA_TK, NSA_TK), NSA_TK)

    def stage_scores(t, buf):
        t = jnp.minimum(t, nqt - 1)
        q = q_ref[0, :, pl.ds(pl.multiple_of(t * tq, tq), tq), :].reshape(NSA_HPG * tq, LANES)
        for i in range(window_steps):
            kc = key_chunk(t, i)
            tile = jnp.where(kc < 0, n_tiles - 1, t - ratio * kc)
            s = _dot_t(q, k_ref[0, 0, key_rows(t, i), :]).reshape(NSA_HPG, tq, NSA_TK)
            buf[:, :, i * NSA_TK:(i + 1) * NSA_TK] = s + bias_ref[0, tile]

    def stage_out(t, buf):
        accs = []
        for h in range(NSA_HPG):
            s = buf[h]
            p = jnp.exp2(s - jnp.max(s, axis=-1, keepdims=True)).astype(BF16)
            acc = None
            for i in range(window_steps):
                pv = _dot(p[:, i * NSA_TK:(i + 1) * NSA_TK], v_ref[0, 0, h // half, key_rows(t, i), :])
                acc = pv if acc is None else acc + pv
            accs.append(acc)
        rows = pl.ds(pl.multiple_of(t * tq, tq), tq)
        for j in range(half):
            o = _normalized_pair(accs[j], accs[half + j])
            o_ref[0, rows, j * LANES:(j + 1) * LANES] = o.astype(o_ref.dtype)

    stage_scores(0, s0_ref)

    def two_tiles(j, carry):
        t = 2 * j
        stage_scores(t + 1, s1_ref)
        stage_out(t, s0_ref)
        stage_scores(t + 2, s0_ref)
        stage_out(t + 1, s1_ref)
        return carry

    lax.fori_loop(0, nqt // 2, two_tiles, 0)


def _nsa_selected_kernel(q_ref, k_ref, v_ref, bias_ref, o_ref, s0_ref, s1_ref, m_ref, acc_ref, *, tq):
    nqt = q_ref.shape[2] // tq
    ratio = NSA_TK // tq
    n_tiles = bias_ref.shape[1]
    half = NSA_HPG // 2
    chunk = SEL_TILES * NSA_TK
    per = chunk // tq

    def key_rows(kc):
        return pl.ds(pl.multiple_of(kc * chunk, chunk), chunk)

    def scores(t, kc, valid, buf):
        q = q_ref[0, :, pl.ds(pl.multiple_of(t * tq, tq), tq), :].reshape(NSA_HPG * tq, LANES)
        s = _dot_t(q, k_ref[0, 0, key_rows(kc), :]).reshape(NSA_HPG, tq, chunk)
        for i in range(SEL_TILES):
            offset = t - ratio * (SEL_TILES * kc + i)
            tile = jnp.where(valid & (offset >= 0), jnp.minimum(offset, n_tiles - 2), n_tiles - 1)
            cols = slice(i * NSA_TK, (i + 1) * NSA_TK)
            buf[:, :, cols] = s[:, :, cols] + bias_ref[0, tile]

    def values(h, kc):
        return v_ref[0, 0, h // half, key_rows(kc), :]

    def write_out(t, accs):
        rows = pl.ds(pl.multiple_of(t * tq, tq), tq)
        for j in range(half):
            o = _normalized_pair(accs[j], accs[half + j])
            o_ref[0, rows, j * LANES:(j + 1) * LANES] = o.astype(o_ref.dtype)

    _causal_item_stream(nqt, lambda t: t // per, NSA_HPG, scores, values, write_out,
                        (s0_ref, s1_ref), m_ref, acc_ref)


def _nsa_selected(q, k, v, bias, s):
    bsz = q.shape[0]
    tq = NSA_TQ
    n_off = bias.shape[1]
    assert s % (SEL_TILES * NSA_TK) == 0
    s_buf = pltpu.VMEM((NSA_HPG, tq, SEL_TILES * NSA_TK), F32)
    once = pl.Buffered(1)
    return pl.pallas_call(
        functools.partial(_nsa_selected_kernel, tq=tq),
        grid=(NSA_GROUPS, bsz),
        in_specs=[pl.BlockSpec((1, NSA_HPG, s, LANES), lambda g, b: (b, g, 0, 0), pipeline_mode=once),
                  pl.BlockSpec((1, 1, s, LANES), lambda g, b: (b, g, 0, 0)),
                  pl.BlockSpec((1, 1, 2, s, LANES), lambda g, b: (b, g, 0, 0, 0)),
                  pl.BlockSpec((1, n_off, NSA_HPG, tq, NSA_TK), lambda g, b: (g, 0, 0, 0, 0), pipeline_mode=once)],
        out_specs=pl.BlockSpec((1, s, NSA_HPG * NSA_DK), lambda g, b: (b, 0, g)),
        out_shape=jax.ShapeDtypeStruct((bsz, s, NSA_HEADS * NSA_DK), BF16),
        scratch_shapes=[s_buf, s_buf, pltpu.VMEM((NSA_HPG, tq, LANES), F32), pltpu.VMEM((NSA_HPG, tq, LANES), F32)],
        compiler_params=_cparams(("parallel", "parallel")),
        name="nsa_flash_sel",
    )(q, k, v, bias)


def _nsa_window(q, k, v, bias, s, window_steps):
    bsz = q.shape[0]
    tq = NSA_TQ
    n_off = bias.shape[1]
    s_buf = pltpu.VMEM((NSA_HPG, tq, window_steps * NSA_TK), F32)
    once = pl.Buffered(1)
    return pl.pallas_call(
        functools.partial(_nsa_window_kernel, tq=tq, window_steps=window_steps),
        grid=(NSA_GROUPS, bsz),
        in_specs=[pl.BlockSpec((1, NSA_HPG, s, LANES), lambda g, b: (b, g, 0, 0), pipeline_mode=once),
                  pl.BlockSpec((1, 1, s, LANES), lambda g, b: (b, g, 0, 0)),
                  pl.BlockSpec((1, 1, 2, s, LANES), lambda g, b: (b, g, 0, 0, 0)),
                  pl.BlockSpec((1, n_off, NSA_HPG, tq, NSA_TK), lambda g, b: (g, 0, 0, 0, 0), pipeline_mode=once)],
        out_specs=pl.BlockSpec((1, s, NSA_HPG * NSA_DK), lambda g, b: (b, 0, g)),
        out_shape=jax.ShapeDtypeStruct((bsz, s, NSA_HEADS * NSA_DK), BF16),
        scratch_shapes=[s_buf, s_buf],
        compiler_params=_cparams(("parallel", "parallel")),
        name="nsa_flash_win",
    )(q, k, v, bias)


def _hybrid_layer(h, bsz, s, layer, g_mix, g_mlp, w1, w2, w_in, conv_w, conv_b, wa, ba, wx, bx, lam,
                  q_norm, kv_norm, w_uq, w_ukv, qn_q, qn_k, w_out):
    d = h.shape[1]
    c = LRU_WIDTH
    o_kr = 2 * c + MLA_Q_RANK + MLA_KV_RANK
    w_kr = jnp.pad(w_in[:, o_kr:], ((0, 0), (MLA_NOPE, LANES - MLA_QK)))
    w_all = jnp.concatenate([w_in[:, :o_kr], w_kr], axis=1).astype(BF16)
    z_lru, q_lat, kv_lat, k_rope = _norm_proj(h, g_mix, w_all, (2 * c, MLA_Q_RANK, MLA_KV_RANK, LANES))
    lru = _rg_lru(z_lru.reshape(bsz, s, 2 * c), conv_w, conv_b, wa, ba, wx, bx, lam)
    q, k, v = _mla_prep(q_lat, kv_lat, k_rope, bsz, s, q_norm, kv_norm, w_uq, w_ukv, qn_q, qn_k)
    mla = _mla_attention(q, k, v)
    wo = w_out.astype(BF16)
    return _mix_mlp(h, [[lru.reshape(bsz * s, c)], [mla.reshape(bsz * s, -1)]], [wo[:c], wo[c:]],
                    g_mlp, w1, w2, layer)


def _nsa_layer(h, bsz, s, layer, g_mix, g_mlp, w1, w2, w_in, pos_k, w1_k, w2_k, pos_v, w1_v, w2_v,
               q_norm, k_norm, rel_bias, w_out):
    nq = NSA_HEADS * NSA_DK
    kvw = NSA_GROUPS * NSA_DK
    w_gate = _pad_cols(w_in[:, nq + 6 * kvw:], LANES)
    w_all = jnp.concatenate([w_in[:, :nq + 6 * kvw], w_gate], axis=1).astype(BF16)
    z_q, z_kc, z_vc, z_kv, z_gate = _norm_proj(h, g_mix, w_all, (nq, kvw, kvw, 4 * kvw, LANES))

    q, ks, kw, vs, vw = _nsa_prep(z_q, z_kv, bsz, s, q_norm, k_norm)
    ck, cv = _compress(z_kc, z_vc, bsz, s, pos_k, w1_k, w2_k, pos_v, w1_v, w2_v, k_norm[0])

    order = np.array(HEAD_ORDER)
    expand = np.zeros((3, LANES, NSA_HEADS * NSA_DK), np.float32)
    for br in range(3):
        for hd in range(NSA_HEADS):
            expand[br, 3 * hd + br, hd * NSA_DK:(hd + 1) * NSA_DK] = 1.0

    slot_heads = (np.arange(NSA_GROUPS)[:, None] * NSA_HPG + order[None, :]).reshape(-1)
    tab_slots = rel_bias.T[slot_heads]
    nh = s // CMP_STRIDE
    nc = (s - CMP_BLOCK) // CMP_STRIDE + 1
    cmp_bias = _bias_tiles(tab_slots, _cmp_bias_ids(s))
    sel_bias = _bias_tiles(tab_slots, _toeplitz_bucket_ids(None)[0])
    win_ids, n_win_tiles = _toeplitz_bucket_ids(WINDOW)
    win_bias = _bias_tiles(tab_slots, win_ids)

    nsel = s // SLC_BLOCK
    cstart = np.arange(nh) * CMP_STRIDE
    sstart = np.arange(nsel) * SLC_BLOCK
    ov = np.clip(np.minimum(cstart[:, None] + CMP_BLOCK, sstart[None, :] + SLC_BLOCK)
                 - np.maximum(cstart[:, None], sstart[None, :]), 0, None).astype(np.float32) / CMP_BLOCK
    ov[nc:] = 0.0
    ov_p = np.zeros((nh, LANES), np.float32)
    ov_p[:, HALF:HALF + nsel] = ov

    o_c, q_sel = _cmp_attention(q, ck, cv, cmp_bias, jnp.asarray(np.concatenate([ov_p, ov_p])), s)
    o_s = _nsa_selected(q_sel, ks, vs, sel_bias, s)
    window_steps = (n_win_tiles + 1) // (NSA_TK // NSA_TQ)
    o_w = _nsa_window(q, kw, vw, win_bias, s, window_steps)

    flat = lambda a: a.reshape(bsz * s, -1)
    return _mix_mlp(h, [[flat(o_c), flat(o_s), flat(o_w)]], [w_out.astype(BF16)],
                    g_mlp, w1, w2, layer,
                    gate=(z_gate, jnp.asarray(np.concatenate([expand, expand], axis=1), BF16)))


def kernel(x, rel_bias, norm_mix, norm_mlp, mlp_w1, mlp_w2, hy_w_in, lru_conv_w, lru_conv_b, lru_wa, lru_ba, lru_wx, lru_bx, lru_lambda, mla_q_norm, mla_kv_norm, mla_w_uq, mla_w_ukv, mla_qn_q, mla_qn_k, hy_w_out, nsa_w_in, nsa_cmp_pos_k, nsa_cmp_w1_k, nsa_cmp_w2_k, nsa_cmp_pos_v, nsa_cmp_w1_v, nsa_cmp_w2_v, nsa_q_norm, nsa_k_norm, nsa_w_out):
    bsz, s, d = x.shape
    depth = norm_mix.shape[0]
    h = x.reshape(bsz * s, d)
    w1_all, w2_all = mlp_w1.astype(BF16), mlp_w2.astype(BF16)
    for layer in range(depth):
        if layer % 2 == 0:
            e = layer // 2
            h = _hybrid_layer(h, bsz, s, layer, norm_mix[layer], norm_mlp[layer], w1_all, w2_all,
                              hy_w_in[e], lru_conv_w[e], lru_conv_b[e], lru_wa[e], lru_ba[e], lru_wx[e],
                              lru_bx[e], lru_lambda[e], mla_q_norm[e], mla_kv_norm[e], mla_w_uq[e],
                              mla_w_ukv[e], mla_qn_q[e], mla_qn_k[e], hy_w_out[e])
        else:
            o = layer // 2
            h = _nsa_layer(h, bsz, s, layer, norm_mix[layer], norm_mlp[layer], w1_all, w2_all,
                           nsa_w_in[o], nsa_cmp_pos_k[o], nsa_cmp_w1_k[o], nsa_cmp_w2_k[o],
                           nsa_cmp_pos_v[o], nsa_cmp_w1_v[o], nsa_cmp_w2_v[o], nsa_q_norm[o],
                           nsa_k_norm[o], rel_bias, nsa_w_out[o])
    return h.reshape(bsz, s, d)
```

```python
import functools
import math

import numpy as np
import jax
import jax.numpy as jnp
from jax import lax
from jax.experimental import pallas as pl
from jax.experimental.pallas import tpu as pltpu

F32 = jnp.float32
BF16 = jnp.bfloat16

NORM_EPS = 1e-6
MASK_VALUE = -1e30
FORCE_SCORE = 1e4
REMOVED_SCORE = -3e38

NUM_BUCKETS = 32
MAX_DISTANCE = 1024
MASKED_ID = NUM_BUCKETS
EXCLUDED_ID = NUM_BUCKETS + 1
LOG2E = math.log2(math.e)

LRU_WIDTH = 512
LRU_BLOCKS = 8
CONV_WIDTH = 4
LRU_C = 8.0

MLA_HEADS = 8
MLA_NOPE = 64
MLA_ROPE = 32
MLA_QK = MLA_NOPE + MLA_ROPE
MLA_V = 64
MLA_Q_RANK = 384
MLA_KV_RANK = 256
ROPE_THETA = 10000.0

NSA_HEADS = 16
NSA_GROUPS = 2
NSA_HPG = NSA_HEADS // NSA_GROUPS
NSA_DK = 64
CMP_BLOCK = 32
CMP_STRIDE = 16
CMP_HIDDEN = 256
SLC_BLOCK = 64
SLC_TOPN = 8
WINDOW = 512

LANES = 128
SUBLANES = 8
HALF = LANES // 2
NSA_TQ = 128
NSA_TK = 256
SEL_TILES = 2
MLA_TR = 128
MLA_TK = 512
HEAD_ORDER = (0, 2, 4, 6, 1, 3, 5, 7)
V7X_VMEM_BYTES = 64 * 1024 * 1024
VMEM_LIMIT = V7X_VMEM_BYTES * 7 // 8


def _cparams(sem):
    return pltpu.CompilerParams(dimension_semantics=sem, vmem_limit_bytes=VMEM_LIMIT)


def _rms(x, g, n=None):
    n = x.shape[-1] if n is None else n
    ms = jnp.sum(x * x, axis=-1, keepdims=True) * (1.0 / n)
    return x * lax.rsqrt(ms + NORM_EPS) * g


def _rms_lanes(x, g, n):
    y = x * x
    hi = y.astype(BF16)
    parts = jnp.concatenate([hi, (y - hi.astype(F32)).astype(BF16)], axis=1)
    ss = _dot(parts, jnp.ones((2 * LANES, LANES), BF16))
    return x * lax.rsqrt(ss * (1.0 / n) + NORM_EPS) * g


def _rms_half_lanes(x, g):
    y = x * x
    hi = y.astype(BF16)
    parts = jnp.concatenate([hi, (y - hi.astype(F32)).astype(BF16)], axis=1)
    row_half = lax.broadcasted_iota(jnp.int32, (2 * LANES, LANES), 0) % LANES // HALF
    col_half = lax.broadcasted_iota(jnp.int32, (2 * LANES, LANES), 1) // HALF
    ss = _dot(parts, jnp.where(row_half == col_half, 1.0, 0.0).astype(BF16))
    return x * lax.rsqrt(ss * (1.0 / HALF) + NORM_EPS) * g


def _gelu_tanh(x):
    return 0.5 * x * (1.0 + jnp.tanh(math.sqrt(2.0 / math.pi) * (x + 0.044715 * (x * x * x))))


def _sigmoid(x):
    return 1.0 / (1.0 + jnp.exp(-x))


def _dot(a, b):
    return jnp.dot(a, b, preferred_element_type=F32)


def _dot_t(a, b):
    return lax.dot_general(a, b, (((1,), (1,)), ((), ())), preferred_element_type=F32)


def _causal_item_stream(n_tiles, last_chunk, n_chains, scores, values, write_out, s_bufs, m_ref, acc_ref):
    m_ref[...] = jnp.full_like(m_ref, MASK_VALUE)
    acc_ref[...] = jnp.zeros_like(acc_ref)
    n_items = sum(last_chunk(t) + 1 for t in range(n_tiles))

    def advance(t, kc):
        wrap = kc >= last_chunk(t)
        return jnp.where(wrap, t + 1, t), jnp.where(wrap, 0, kc + 1)

    def stage_scores(t, kc, buf):
        valid = t < n_tiles
        scores(jnp.minimum(t, n_tiles - 1), jnp.where(valid, kc, 0), valid, buf)

    def stage_update(t, kc, buf, maybe_padding):
        tile_done = kc >= last_chunk(t)
        kc = jnp.where(t < n_tiles, kc, 0) if maybe_padding else kc
        accs = []
        for c in range(n_chains):
            m_old = m_ref[c]
            m_new = jnp.maximum(m_old, jnp.max(buf[c], axis=-1, keepdims=True))
            alpha = jnp.exp2(m_old - m_new)
            s = buf[c]
            p = jnp.exp2(s - jnp.concatenate([m_new] * (s.shape[1] // LANES), axis=1))
            acc = acc_ref[c] * alpha + _dot(p.astype(BF16), values(c, kc))
            m_ref[c] = jnp.where(tile_done, MASK_VALUE, m_new)
            acc_ref[c] = jnp.where(tile_done, 0.0, acc)
            accs.append(acc)
        if maybe_padding:
            pl.when(t < n_tiles)(lambda: write_out(t, accs))
        else:
            write_out(t, accs)

    start = (jnp.int32(0), jnp.int32(0))
    stage_scores(*start, s_bufs[0])

    def two_items(_, item):
        item1 = advance(*item)
        stage_scores(*item1, s_bufs[1])
        stage_update(*item, s_bufs[0], False)
        item2 = advance(*item1)
        stage_scores(*item2, s_bufs[0])
        stage_update(*item1, s_bufs[1], n_items % 2 == 1)
        return item2

    lax.fori_loop(0, (n_items + 1) // 2, two_items, start)


def _normalized_pair(acc_even, acc_odd):
    lo = lax.broadcasted_iota(jnp.int32, acc_even.shape, 1) < HALF
    return jnp.where(lo, acc_even / acc_even[:, LANES - 1:LANES], acc_odd / acc_odd[:, 0:1])


def _with_ones_column(v, odd):
    lane = lax.broadcasted_iota(jnp.int32, v.shape, 1)
    return jnp.where(lane == (0 if odd else LANES - 1), 1.0, v)


def _norm_proj_kernel(x_ref, g_ref, w_ref, *refs, groups, folds):
    n_out = len(folds)
    out_refs, fold_refs = refs[:n_out], list(refs[n_out:])
    xn = _rms(x_ref[...], g_ref[...]).astype(BF16)
    off = 0
    k = 0
    for group in groups:
        width = sum(group)
        z = _dot(xn, w_ref[:, off:off + width])
        off += width
        col = 0
        for n in group:
            piece = z[:, col:col + n]
            col += n
            if folds[k] == 1:
                out_refs[k][...] = piece
            else:
                scratch = fold_refs.pop(0)
                scratch[...] = piece
                rows = piece.shape[0] // folds[k]
                for j in range(folds[k]):
                    out_refs[k][:, j * n:(j + 1) * n] = scratch[pl.ds(j, rows, stride=folds[k]), :]
            k += 1


def _norm_proj(x, g, w, groups, folds=None, tm=512):
    t, d = x.shape
    n = w.shape[1]
    splits = [s for group in groups for s in group]
    folds = tuple(folds) if folds is not None else (1,) * len(splits)
    assert sum(splits) == n and t % tm == 0
    return pl.pallas_call(
        functools.partial(_norm_proj_kernel, groups=groups, folds=folds),
        grid=(t // tm,),
        in_specs=[pl.BlockSpec((tm, d), lambda i: (i, 0)),
                  pl.BlockSpec((1, d), lambda i: (0, 0)),
                  pl.BlockSpec((d, n), lambda i: (0, 0))],
        out_specs=[pl.BlockSpec((tm // f, s * f), lambda i: (i, 0)) for s, f in zip(splits, folds)],
        out_shape=[jax.ShapeDtypeStruct((t // f, s * f), F32) for s, f in zip(splits, folds)],
        scratch_shapes=[pltpu.VMEM((tm, s), F32) for s, f in zip(splits, folds) if f > 1],
        compiler_params=_cparams(("parallel",)),
        name="norm_proj",
    )(x, g.reshape(1, d), w)


def _mix_mlp_kernel(*refs, group_sizes, gated):
    n_a = sum(group_sizes)
    n_g = len(group_sizes)
    h_ref = refs[0]
    a_refs = refs[1:1 + n_a]
    wo_refs = refs[1 + n_a:1 + n_a + n_g]
    rest = refs[1 + n_a + n_g:]
    if gated:
        zg_ref, e_ref = rest[:2]
        rest = rest[2:]
    g_ref, w1_ref, w2_ref, out_ref, hres, xn, acc = rest
    f = pl.program_id(1)

    @pl.when(f == 0)
    def _():
        h1 = h_ref[...]
        if gated:
            gate = _sigmoid(zg_ref[...])
            g_hi = gate.astype(BF16)
            g_parts = jnp.concatenate([g_hi, (gate - g_hi.astype(F32)).astype(BF16)], axis=1)
        k = 0
        for gi, gs in enumerate(group_sizes):
            a = None
            for j in range(gs):
                aj = a_refs[k + j][...].astype(F32)
                if gated:
                    aj = aj * _dot(g_parts, e_ref[j])
                a = aj if a is None else a + aj
            k += gs
            h1 = h1 + _dot(a.astype(BF16), wo_refs[gi][...])
        hres[...] = h1
        xn[...] = _rms(h1, g_ref[...]).astype(BF16)
        acc[...] = jnp.zeros_like(acc)

    u = jnp.maximum(_dot(xn[...], w1_ref[...]), 0.0)
    acc[...] += _dot((u * u).astype(BF16), w2_ref[...])

    @pl.when(f == pl.num_programs(1) - 1)
    def _():
        out_ref[...] = hres[...] + acc[...]


def _mix_mlp(h, groups, wos, g, w1, w2, layer, gate=None, tm=512, tf=2048):
    t, d = h.shape
    ff = w1.shape[2]
    group_sizes = tuple(len(gr) for gr in groups)
    a_list = [a for gr in groups for a in gr]
    extra = [] if gate is None else list(gate)
    in_specs = [pl.BlockSpec((tm, d), lambda i, f: (i, 0))]
    in_specs += [pl.BlockSpec((tm, a.shape[1]), lambda i, f: (i, 0)) for a in a_list]
    in_specs += [pl.BlockSpec(w.shape, lambda i, f: (0, 0)) for w in wos]
    if gate is not None:
        in_specs += [pl.BlockSpec((tm, LANES), lambda i, f: (i, 0)),
                     pl.BlockSpec(gate[1].shape, lambda i, f: (0, 0, 0))]
    in_specs += [pl.BlockSpec((1, d), lambda i, f: (0, 0)),
                 pl.BlockSpec((None, d, tf), lambda i, f: (layer, 0, f)),
                 pl.BlockSpec((None, tf, d), lambda i, f: (layer, f, 0))]
    return pl.pallas_call(
        functools.partial(_mix_mlp_kernel, group_sizes=group_sizes, gated=gate is not None),
        grid=(t // tm, ff // tf),
        in_specs=in_specs,
        out_specs=pl.BlockSpec((tm, d), lambda i, f: (i, 0)),
        out_shape=jax.ShapeDtypeStruct((t, d), F32),
        scratch_shapes=[pltpu.VMEM((tm, d), F32), pltpu.VMEM((tm, d), BF16),
                        pltpu.VMEM((tm, d), F32)],
        compiler_params=_cparams(("parallel", "arbitrary")),
        name="mix_mlp",
    )(h, *a_list, *wos, *extra, g.reshape(1, d), w1, w2)


def _lru_kernel(xr_ref, xg_ref, cw_ref, cb_ref, wa_ref, ba_ref, wx_ref, bx_ref, lam_ref,
                o_ref, tail_ref, hc_ref, *, tc):
    c = pl.program_id(1)

    @pl.when(c == 0)
    def _():
        tail_ref[...] = jnp.zeros_like(tail_ref)
        hc_ref[...] = jnp.zeros_like(hc_ref)

    x = xr_ref[0]
    width = x.shape[1]
    row = lax.broadcasted_iota(jnp.int32, (tc, width), 0)
    tail_row = lax.broadcasted_iota(jnp.int32, (SUBLANES, width), 0)
    tail = tail_ref[...]
    cw = cw_ref[...]
    y = x * cw[CONV_WIDTH - 1:CONV_WIDTH]
    for s in range(1, CONV_WIDTH):
        sh = pltpu.roll(x, s, 0)
        head = jnp.where(tail_row < s, pltpu.roll(tail, s, 0), sh[0:SUBLANES])
        sh = jnp.concatenate([head, sh[SUBLANES:]], axis=0)
        y = y + sh * cw[CONV_WIDTH - 1 - s:CONV_WIDTH - s]
    y = y + cb_ref[...]
    tail_ref[...] = x[tc - SUBLANES:tc]

    yb = y.astype(BF16)
    r = _sigmoid(_dot(yb, wa_ref[...]) + ba_ref[...])
    i = _sigmoid(_dot(yb, wx_ref[...]) + bx_ref[...])
    nl = -lam_ref[...]
    softplus = jnp.maximum(nl, 0.0) + jnp.log(1.0 + jnp.exp(-jnp.abs(nl)))
    log_a = (-LRU_C) * r * softplus
    a = jnp.exp(log_a)
    mult = jnp.sqrt(1.0 - jnp.exp(2.0 * log_a))
    mult = jnp.where((row == 0) & (c == 0), 1.0, mult)
    b = mult * (i * y)

    d = 1
    while d < tc:
        keep = row >= d
        a_sh = jnp.where(keep, pltpu.roll(a, d, 0), 1.0)
        b_sh = jnp.where(keep, pltpu.roll(b, d, 0), 0.0)
        b = a * b_sh + b
        a = a * a_sh
        d *= 2
    h = b + a * hc_ref[...]
    hc_ref[...] = h[tc - 1:tc]
    o_ref[0] = (h * _gelu_tanh(xg_ref[0])).astype(o_ref.dtype)


def _block_diag(w):
    n, d, e = w.shape
    eye = jnp.eye(n, dtype=w.dtype)
    return (w[:, :, None, :] * eye[:, None, :, None]).reshape(n * d, n * e)


def _rg_lru(z_lru, conv_w, conv_b, wa, ba, wx, bx, lam, tc=256):
    bsz, s, _ = z_lru.shape
    c = LRU_WIDTH
    tc = min(tc, s)
    vec = lambda v: pl.BlockSpec(v, lambda b, i: (0, 0))
    return pl.pallas_call(
        functools.partial(_lru_kernel, tc=tc),
        grid=(bsz, s // tc),
        in_specs=[pl.BlockSpec((1, tc, c), lambda b, i: (b, i, 0)),
                  pl.BlockSpec((1, tc, c), lambda b, i: (b, i, 1)),
                  vec((CONV_WIDTH, c)), vec((1, c)), vec((c, c)), vec((1, c)),
                  vec((c, c)), vec((1, c)), vec((1, c))],
        out_specs=pl.BlockSpec((1, tc, c), lambda b, i: (b, i, 0)),
        out_shape=jax.ShapeDtypeStruct((bsz, s, c), BF16),
        scratch_shapes=[pltpu.VMEM((SUBLANES, c), F32), pltpu.VMEM((1, c), F32)],
        compiler_params=_cparams(("parallel", "arbitrary")),
        name="rg_lru",
    )(z_lru, z_lru, conv_w, conv_b.reshape(1, c), _block_diag(wa).astype(BF16), ba.reshape(1, c),
      _block_diag(wx).astype(BF16), bx.reshape(1, c), lam.reshape(1, c))


def _rope(x, cos, sin_lo, sin_hi):
    return (x * cos + pltpu.roll(x, LANES - MLA_ROPE // 2, 1) * sin_lo
            + pltpu.roll(x, MLA_ROPE // 2, 1) * sin_hi)


def _mla_prep_kernel(ql_ref, kvl_ref, kr_ref, gq_ref, gkv_ref, wq_ref, wk_ref, wv_ref,
                     nq_ref, nk_ref, cos_ref, s1_ref, s2_ref, q_out, k_out, v_out):
    qn = _rms(ql_ref[...], gq_ref[...]).astype(BF16)
    kvn = _rms(kvl_ref[...], gkv_ref[...]).astype(BF16)
    kr = kr_ref[...]
    cos, s1, s2 = cos_ref[...], s1_ref[...], s2_ref[...]
    scale = MLA_QK ** -0.5 * LOG2E
    for h in range(MLA_HEADS):
        sl = slice(h * LANES, (h + 1) * LANES)
        qh = _rms_lanes(_dot(qn, wq_ref[:, sl]), nq_ref[...], MLA_QK)
        q_out[0, h] = (_rope(qh, cos, s1, s2) * scale).astype(BF16)
        kh = _rms_lanes(_dot(kvn, wk_ref[:, sl]) + kr, nk_ref[...], MLA_QK)
        k_out[0, h] = _rope(kh, cos, s1, s2).astype(BF16)
        v_out[0, h] = _with_ones_column(_dot(kvn, wv_ref[:, sl]), h % 2 == 1).astype(BF16)


def _rope_tables(s):
    half = MLA_ROPE // 2
    freqs = ROPE_THETA ** (-jnp.arange(half, dtype=F32) / half)
    ang = jnp.arange(s, dtype=F32)[:, None] * freqs[None, :]
    cos, sin = jnp.cos(ang), jnp.sin(ang)
    z = lambda n: jnp.zeros((s, n), F32)
    cos_t = jnp.concatenate([jnp.ones((s, MLA_NOPE), F32), cos, cos, z(LANES - MLA_QK)], axis=1)
    s_lo = jnp.concatenate([z(MLA_NOPE), -sin, z(LANES - MLA_NOPE - half)], axis=1)
    s_hi = jnp.concatenate([z(MLA_NOPE + half), sin, z(LANES - MLA_QK)], axis=1)
    return cos_t, s_lo, s_hi


def _pad_cols(w, n):
    return jnp.pad(w, ((0, 0),) * (w.ndim - 1) + ((0, n - w.shape[-1]),))


def _mla_prep(q_lat, kv_lat, k_rope, bsz, s, q_norm, kv_norm, w_uq, w_ukv, qn_q, qn_k, tm=512):
    tm = min(tm, s)
    nt = s // tm
    hl = MLA_HEADS * LANES
    wq = _pad_cols(w_uq.reshape(MLA_Q_RANK, MLA_HEADS, MLA_QK), LANES).reshape(MLA_Q_RANK, hl)
    wkv = w_ukv.reshape(MLA_KV_RANK, MLA_HEADS, MLA_NOPE + MLA_V)
    wk = _pad_cols(wkv[..., :MLA_NOPE], LANES).reshape(MLA_KV_RANK, hl)
    wv = wkv[..., MLA_NOPE:]
    zero = jnp.zeros_like(wv)
    even = (jnp.arange(MLA_HEADS) % 2 == 0)[None, :, None]
    wv = jnp.concatenate([jnp.where(even, wv, zero), jnp.where(even, zero, wv)], axis=-1)
    wv = wv.reshape(MLA_KV_RANK, hl)
    cos_t, s_lo, s_hi = _rope_tables(s)
    row = lambda n: pl.BlockSpec((tm, n), lambda b, i: (b * nt + i, 0))
    full = lambda a: pl.BlockSpec(a.shape, lambda b, i: (0, 0))
    tab = pl.BlockSpec((tm, LANES), lambda b, i: (i, 0))
    head_out = pl.BlockSpec((1, MLA_HEADS, tm, LANES), lambda b, i: (b, 0, i, 0))
    args = (q_lat, kv_lat, k_rope, q_norm.reshape(1, -1), kv_norm.reshape(1, -1),
            wq.astype(BF16), wk.astype(BF16), wv.astype(BF16),
            _pad_cols(qn_q.reshape(1, -1), LANES), _pad_cols(qn_k.reshape(1, -1), LANES),
            cos_t, s_lo, s_hi)
    in_specs = [row(MLA_Q_RANK), row(MLA_KV_RANK), row(LANES)] + [full(a) for a in args[3:10]] + [tab] * 3
    shp = jax.ShapeDtypeStruct((bsz, MLA_HEADS, s, LANES), BF16)
    return pl.pallas_call(
        _mla_prep_kernel,
        grid=(bsz, nt),
        in_specs=in_specs,
        out_specs=[head_out] * 3,
        out_shape=[shp] * 3,
        compiler_params=_cparams(("parallel", "parallel")),
        name="mla_prep",
    )(*args)


def _mla_attn_kernel(q_ref, k_ref, v_ref, mask_ref, o_ref, s0_ref, s1_ref, m_ref, acc_ref, *, tq):
    n_rc = tq // MLA_TR
    ratio = MLA_TK // MLA_TR
    n_tiles = mask_ref.shape[0]

    def key_rows(kc):
        return pl.ds(pl.multiple_of(kc * MLA_TK, MLA_TK), MLA_TK)

    def scores(qi, kc, valid, buf):
        rows = pl.ds(pl.multiple_of(qi * tq, tq), tq)
        for j in range(2):
            s = _dot_t(q_ref[0, j, rows, :], k_ref[0, j, key_rows(kc), :])
            for rc in range(n_rc):
                offset = qi * n_rc + rc - ratio * kc
                tile = jnp.where(valid, jnp.clip(offset + 1, 0, n_tiles - 1), 0)
                buf[j * n_rc + rc] = s[rc * MLA_TR:(rc + 1) * MLA_TR] + mask_ref[tile]

    def values(c, kc):
        return v_ref[0, c // n_rc, key_rows(kc), :]

    def write_out(qi, accs):
        for rc in range(n_rc):
            rows = pl.ds(pl.multiple_of(qi * tq + rc * MLA_TR, MLA_TR), MLA_TR)
            o_ref[0, rows, :] = _normalized_pair(accs[rc], accs[n_rc + rc]).astype(o_ref.dtype)

    _causal_item_stream(q_ref.shape[2] // tq, lambda qi: (qi * tq + tq - 1) // MLA_TK, 2 * n_rc,
                        scores, values, write_out, (s0_ref, s1_ref), m_ref, acc_ref)


def _mla_causal_tiles():
    i = np.arange(MLA_TR)[:, None]
    j = np.arange(MLA_TK)[None, :]
    tiles = [np.full((MLA_TR, MLA_TK), -np.inf, np.float32)]
    for d in range(MLA_TK // MLA_TR):
        tiles.append(np.where(j <= i + d * MLA_TR, 0.0, -np.inf).astype(np.float32))
    tiles.append(np.zeros((MLA_TR, MLA_TK), np.float32))
    return np.stack(tiles)


def _mla_attention(q, k, v, tq=512):
    bsz, nh, s, _ = q.shape
    tq = min(tq, s)
    assert tq % MLA_TK == 0 and tq % MLA_TR == 0
    n_chains = 2 * (tq // MLA_TR)
    mask = _mla_causal_tiles()
    seq_spec = pl.BlockSpec((1, 2, s, LANES), lambda b, hp: (b, hp, 0, 0))
    return pl.pallas_call(
        functools.partial(_mla_attn_kernel, tq=tq),
        grid=(bsz, nh // 2),
        in_specs=[seq_spec, seq_spec, seq_spec, pl.BlockSpec(mask.shape, lambda b, hp: (0, 0, 0))],
        out_specs=pl.BlockSpec((1, s, LANES), lambda b, hp: (b, 0, hp)),
        out_shape=jax.ShapeDtypeStruct((bsz, s, nh * MLA_V), BF16),
        scratch_shapes=[pltpu.VMEM((n_chains, MLA_TR, MLA_TK), F32), pltpu.VMEM((n_chains, MLA_TR, MLA_TK), F32),
                        pltpu.VMEM((n_chains, MLA_TR, LANES), F32), pltpu.VMEM((n_chains, MLA_TR, LANES), F32)],
        compiler_params=_cparams(("parallel", "parallel")),
        name="mla_attn",
    )(q, k, v, jnp.asarray(mask))


def _t5_bucket_np(dist):
    n = np.maximum(dist, 0)
    max_exact = NUM_BUCKETS // 2
    nf = np.maximum(n, 1).astype(np.float32)
    large = max_exact + (np.log(nf / np.float32(max_exact)) / np.float32(math.log(MAX_DISTANCE / max_exact))
                         * np.float32(NUM_BUCKETS - max_exact)).astype(np.int32)
    large = np.minimum(large, NUM_BUCKETS - 1)
    return np.where(n < max_exact, n, large).astype(np.int32)


def _bias_tiles_kernel(tab_ref, idx_ref, o_ref, *, buckets):
    slot = pl.program_id(0) * NSA_HPG + pl.program_id(1)
    for i, present in enumerate(buckets):
        idx = idx_ref[i]
        acc = jnp.where(idx == EXCLUDED_ID, -jnp.inf, MASK_VALUE).astype(F32)
        for b in present:
            acc = jnp.where(idx == b, tab_ref[slot, b] * LOG2E, acc)
        o_ref[0, i, 0] = acc


def _bias_tiles(tab_slots, idx):
    n, r, c = idx.shape
    buckets = tuple(tuple(int(b) for b in np.unique(tile) if b < NUM_BUCKETS) for tile in idx)
    return pl.pallas_call(
        functools.partial(_bias_tiles_kernel, buckets=buckets),
        grid=(NSA_GROUPS, NSA_HPG),
        in_specs=[pl.BlockSpec(memory_space=pltpu.SMEM),
                  pl.BlockSpec((n, r, c), lambda g, p: (0, 0, 0))],
        out_specs=pl.BlockSpec((1, n, 1, r, c), lambda g, p: (g, 0, p, 0, 0)),
        out_shape=jax.ShapeDtypeStruct((NSA_GROUPS, n, NSA_HPG, r, c), F32),
        compiler_params=_cparams(("parallel", "parallel")),
        name="bias_tiles",
    )(tab_slots, jnp.asarray(idx))


def _cmp_bias_ids(s):
    nqt = s // NSA_TQ
    nh = s // CMP_STRIDE
    per_tile = NSA_TQ // CMP_STRIDE
    i = np.arange(NSA_TQ)[:, None]
    c = np.arange(nh)[None, :]
    dist = i - (CMP_BLOCK - 1) - CMP_STRIDE * (c - per_tile * (nqt - 1))
    ids = np.where(dist >= 0, _t5_bucket_np(dist), MASKED_ID).astype(np.int32)
    return ids[None]


def _toeplitz_bucket_ids(window):
    i = np.arange(NSA_TQ)[:, None]
    j = np.arange(NSA_TK)[None, :]
    tiles = []
    d = 0
    while True:
        dist = d * NSA_TQ + i - j
        ids = _t5_bucket_np(dist)
        ok = dist >= 0
        if window is not None:
            ok &= dist < window
        ids = np.where(ok, ids, MASKED_ID).astype(np.int32)
        tiles.append(ids)
        if window is None and (ids == NUM_BUCKETS - 1).all():
            break
        if window is not None and not ok.any():
            tiles.pop()
            break
        d += 1
    n_real = len(tiles)
    tiles.append(np.full((NSA_TQ, NSA_TK), EXCLUDED_ID, np.int32))
    return np.stack(tiles), n_real


def _split_halves(x):
    lo = lax.broadcasted_iota(jnp.int32, x.shape, 1) < HALF
    return jnp.where(lo, x, 0.0), pltpu.roll(jnp.where(lo, 0.0, x), HALF, 1)


def _value_layouts(x):
    lo = lax.broadcasted_iota(jnp.int32, x.shape, 1) < HALF
    g0e = jnp.where(lo, x, 0.0)
    g1o = jnp.where(lo, 0.0, x)
    pair = lambda e, o: (_with_ones_column(e, False), _with_ones_column(o, True))
    return pair(g0e, pltpu.roll(g0e, HALF, 1)), pair(pltpu.roll(g1o, HALF, 1), g1o)


def _nsa_prep_kernel(zq_ref, ks_ref, vs_ref, kw_ref, vw_ref, gq_ref, gks_ref, gkw_ref,
                     q_out, ks_out, kw_out, vs_out, vw_out, *, tm):
    i = pl.program_id(1)
    scale = NSA_DK ** -0.5 * LOG2E
    for j in range(NSA_HEADS // 2):
        e, o = _split_halves(_rms_half_lanes(zq_ref[:, j * LANES:(j + 1) * LANES], gq_ref[...]) * scale)
        g = (2 * j) // NSA_HPG
        pe = (2 * j - NSA_HPG * g) // 2
        q_out[0, NSA_HPG * g + pe] = e.astype(BF16)
        q_out[0, NSA_HPG * g + NSA_HPG // 2 + pe] = o.astype(BF16)

    lane = lax.broadcasted_iota(jnp.int32, (tm, LANES), 1)
    pos = i * tm + lax.broadcasted_iota(jnp.int32, (tm, LANES), 0)
    block_tag = jnp.where(lane - HALF == pos // SLC_BLOCK, MASK_VALUE, 0.0)
    for g, x in enumerate(_split_halves(_rms_half_lanes(ks_ref[...], gks_ref[...]))):
        ks_out[0, g] = (x + block_tag).astype(BF16)
    for g, x in enumerate(_split_halves(_rms_half_lanes(kw_ref[...], gkw_ref[...]))):
        kw_out[0, g] = x.astype(BF16)
    for ref, out in ((vs_ref, vs_out), (vw_ref, vw_out)):
        for g, (ve, vo) in enumerate(_value_layouts(ref[...])):
            out[0, g, 0] = ve.astype(BF16)
            out[0, g, 1] = vo.astype(BF16)


def _nsa_prep(z_q, z_kv, bsz, s, q_norm, k_norm, tm=512):
    tm = min(tm, s)
    nt = s // tm
    row = lambda n, c: pl.BlockSpec((tm, n), lambda b, i: (b * nt + i, c))
    gain = pl.BlockSpec((1, LANES), lambda b, i: (0, 0))
    kspec = pl.BlockSpec((1, NSA_GROUPS, tm, LANES), lambda b, i: (b, 0, i, 0))
    vspec = pl.BlockSpec((1, NSA_GROUPS, 2, tm, LANES), lambda b, i: (b, 0, 0, i, 0))
    kshape = jax.ShapeDtypeStruct((bsz, NSA_GROUPS, s, LANES), BF16)
    vshape = jax.ShapeDtypeStruct((bsz, NSA_GROUPS, 2, s, LANES), BF16)
    pad = lambda v: jnp.tile(v.reshape(1, -1), (1, LANES // NSA_DK))
    return pl.pallas_call(
        functools.partial(_nsa_prep_kernel, tm=tm),
        grid=(bsz, nt),
        in_specs=[row(NSA_HEADS * NSA_DK, 0), row(LANES, 0), row(LANES, 1), row(LANES, 2), row(LANES, 3),
                  gain, gain, gain],
        out_specs=[pl.BlockSpec((1, NSA_HEADS, tm, LANES), lambda b, i: (b, 0, i, 0)),
                   kspec, kspec, vspec, vspec],
        out_shape=[jax.ShapeDtypeStruct((bsz, NSA_HEADS, s, LANES), BF16), kshape, kshape, vshape, vshape],
        compiler_params=_cparams(("parallel", "parallel")),
        name="nsa_prep",
    )(z_q, z_kv, z_kv, z_kv, z_kv, pad(q_norm), pad(k_norm[1]), pad(k_norm[2]))


def _compress_kernel(xk_ref, xv_ref, pk_ref, pv_ref, w1k_ref, w1v_ref, w2k_ref, w2v_ref, gk_ref,
                     ck_out, cv_out):
    def mlp(x, p_ref, w1_ref, w2_ref):
        nh = x.shape[0]
        top = _dot((x + p_ref[0:1]).astype(BF16), w1_ref[0])
        bot = _dot((x + p_ref[1:2]).astype(BF16), w1_ref[1])
        hid = _gelu_tanh(top + pltpu.roll(bot, nh - 1, 0))
        return _dot(hid.astype(BF16), w2_ref[...])

    nh = xk_ref.shape[1]
    yk = mlp(xk_ref[0], pk_ref, w1k_ref, w2k_ref)
    for g, x in enumerate(_split_halves(yk)):
        ck = _rms(x, gk_ref[...], NSA_DK)
        ck_out[0, g, 0:nh] = ck
        ck_out[0, g, nh:2 * nh] = ck
    yv = mlp(xv_ref[0], pv_ref, w1v_ref, w2v_ref)
    for g, layouts in enumerate(_value_layouts(yv)):
        for parity, v in enumerate(layouts):
            cv_out[0, g, parity, 0:nh] = v
            cv_out[0, g, parity, nh:2 * nh] = v


def _compress_weights(pos, w1, w2):
    half = CMP_BLOCK // 2
    g = NSA_GROUPS
    eye = jnp.eye(g, dtype=F32)
    w1 = w1.reshape(2, half, NSA_DK, CMP_HIDDEN)
    w1 = w1[:, :, None, :, None, :] * eye[None, None, :, None, :, None]
    w1 = w1.reshape(2, half * g * NSA_DK, g * CMP_HIDDEN).astype(BF16)
    p = jnp.broadcast_to(pos.reshape(2, half, 1, NSA_DK), (2, half, g, NSA_DK)).reshape(2, half * g * NSA_DK)
    w2 = (w2[None, :, None, :] * eye[:, None, :, None]).reshape(g * CMP_HIDDEN, g * NSA_DK).astype(BF16)
    return p, w1, w2


def _compress(z_kc, z_vc, bsz, s, pos_k, w1_k, w2_k, pos_v, w1_v, w2_v, k_norm0):
    nh = s // CMP_STRIDE
    feat = CMP_STRIDE * LANES
    pk, w1k, w2k = _compress_weights(pos_k, w1_k, w2_k)
    pv, w1v, w2v = _compress_weights(pos_v, w1_v, w2_v)
    full = lambda a: pl.BlockSpec(a.shape, lambda b: (0,) * a.ndim)
    xspec = pl.BlockSpec((1, nh, feat), lambda b: (b, 0, 0))
    gk = _pad_cols(k_norm0.reshape(1, -1), LANES)
    return pl.pallas_call(
        _compress_kernel,
        grid=(bsz,),
        in_specs=[xspec, xspec, full(pk), full(pv), full(w1k), full(w1v), full(w2k), full(w2v), full(gk)],
        out_specs=[pl.BlockSpec((1, NSA_GROUPS, 2 * nh, LANES), lambda b: (b, 0, 0, 0)),
                   pl.BlockSpec((1, NSA_GROUPS, 2, 2 * nh, LANES), lambda b: (b, 0, 0, 0, 0))],
        out_shape=[jax.ShapeDtypeStruct((bsz, NSA_GROUPS, 2 * nh, LANES), F32),
                   jax.ShapeDtypeStruct((bsz, NSA_GROUPS, 2, 2 * nh, LANES), F32)],
        compiler_params=_cparams(("parallel",)),
        name="nsa_compress",
    )(z_kc.reshape(bsz, nh, feat), z_vc.reshape(bsz, nh, feat), pk, pv, w1k, w1v, w2k, w2v, gk)


def _cmp_attn_kernel(q_ref, ck_ref, cv_ref, bias_ref, ov_ref, o_ref, q2_ref, s0_ref, s1_ref,
                     rank0_ref, rank1_ref, *, tq, nsel):
    nqt = q_ref.shape[2] // tq
    assert nqt % 2 == 0
    half = NSA_HPG // 2
    nh = bias_ref.shape[4]
    per_tile = tq // CMP_STRIDE
    c_last = per_tile * (nqt - 1)
    n_blk = LANES - HALF

    def block_window(t):
        start = lax.rem(per_tile * t + (nh - c_last % nh), nh)
        return pl.ds(pl.multiple_of(start, per_tile), nh)

    def query_rows(t):
        return pl.ds(pl.multiple_of(t * tq, tq), tq)

    def stage_scores(t, buf):
        t = jnp.minimum(t, nqt - 1)
        col = lax.broadcasted_iota(jnp.int32, (nh, LANES), 0)
        lane_k = lax.broadcasted_iota(jnp.int32, (nh, LANES), 1)
        wrapped = (lane_k == HALF) & (col < c_last - per_tile * t)
        ck = jnp.where(wrapped, MASK_VALUE, ck_ref[0, 0, block_window(t), :]).astype(BF16)
        lane_q = lax.broadcasted_iota(jnp.int32, (NSA_HPG, tq, LANES), 2)
        q = q_ref[0, :, query_rows(t), :] + jnp.where(lane_q == HALF, 1.0, 0.0).astype(BF16)
        s = _dot_t(q.reshape(NSA_HPG * tq, LANES), ck)
        buf[...] = s.reshape(NSA_HPG, tq, nh) + bias_ref[0, 0]

    def stage_select(t, buf, rank_ref):
        cols = block_window(t)
        cv = [cv_ref[0, 0, parity, cols, :].astype(BF16) for parity in range(2)]
        ov = ov_ref[cols, :].astype(BF16)
        row_ok = t * tq + lax.broadcasted_iota(jnp.int32, (tq, nh), 0) >= CMP_BLOCK - 1
        outs = []
        psum = None
        for h in range(NSA_HPG):
            s = buf[h]
            e = jnp.exp2(s - jnp.max(s, axis=-1, keepdims=True))
            p = jnp.where(row_ok, e / jnp.sum(e, axis=-1, keepdims=True), 0.0)
            outs.append(_dot(p.astype(BF16), cv[h // half]))
            psum = p if psum is None else psum + p
        lane_lo = lax.broadcasted_iota(jnp.int32, (tq, LANES), 1) < HALF
        for j in range(half):
            o = jnp.where(lane_lo, outs[j], outs[half + j])
            o_ref[0, query_rows(t), j * LANES:(j + 1) * LANES] = o.astype(o_ref.dtype)

        hi = psum.astype(BF16)
        lo = (psum - hi.astype(F32)).astype(BF16)
        imp = _dot(hi, ov) + _dot(lo, ov)

        blk = lax.broadcasted_iota(jnp.int32, (n_blk, tq), 0)
        qpos = t * tq + lax.broadcasted_iota(jnp.int32, (n_blk, tq), 1)
        cur = qpos // SLC_BLOCK
        valid = blk * SLC_BLOCK <= qpos
        forced = (blk == 0) | (blk == cur) | (blk == cur - 1)
        score = jnp.where(valid & forced, FORCE_SCORE, jnp.where(valid, imp.T[HALF:], -1.0))
        score = jnp.where(blk < nsel, score, REMOVED_SCORE)
        rank_ref[...] = score
        rank = jnp.zeros((n_blk, tq), F32)
        for j in range(n_blk):
            sj = jnp.broadcast_to(rank_ref[j:j + 1, :], (n_blk, tq))
            beats = (sj > score) | ((sj == score) & (blk > j))
            rank = rank + jnp.where(beats, 1.0, 0.0)
        unselected = jnp.where(rank < min(SLC_TOPN, nsel), 0.0, 1.0)
        unselected = jnp.concatenate([jnp.zeros((HALF, tq), F32), unselected], axis=0).T.astype(BF16)
        for h in range(NSA_HPG):
            q2_ref[0, h, query_rows(t), :] = q_ref[0, h, query_rows(t), :] + unselected

    stage_scores(0, s0_ref)

    def two_tiles(j, carry):
        t = 2 * j
        stage_scores(t + 1, s1_ref)
        stage_select(t, s0_ref, rank0_ref)
        stage_scores(t + 2, s0_ref)
        stage_select(t + 1, s1_ref, rank1_ref)
        return carry

    lax.fori_loop(0, nqt // 2, two_tiles, 0)


def _cmp_attention(q, ck, cv, bias, ov, s):
    bsz = q.shape[0]
    tq = NSA_TQ
    nh = s // CMP_STRIDE
    nsel = s // SLC_BLOCK
    s_buf = pltpu.VMEM((NSA_HPG, tq, nh), F32)
    rank_buf = pltpu.VMEM((LANES - HALF, tq), F32)
    return pl.pallas_call(
        functools.partial(_cmp_attn_kernel, tq=tq, nsel=nsel),
        grid=(NSA_GROUPS, bsz),
        in_specs=[pl.BlockSpec((1, NSA_HPG, s, LANES), lambda g, b: (b, g, 0, 0), pipeline_mode=pl.Buffered(1)),
                  pl.BlockSpec((1, 1, 2 * nh, LANES), lambda g, b: (b, g, 0, 0)),
                  pl.BlockSpec((1, 1, 2, 2 * nh, LANES), lambda g, b: (b, g, 0, 0, 0)),
                  pl.BlockSpec((1, 1, NSA_HPG, tq, nh), lambda g, b: (g, 0, 0, 0, 0)),
                  pl.BlockSpec((2 * nh, LANES), lambda g, b: (0, 0))],
        out_specs=[pl.BlockSpec((1, s, NSA_HPG * NSA_DK), lambda g, b: (b, 0, g)),
                   pl.BlockSpec((1, NSA_HPG, s, LANES), lambda g, b: (b, g, 0, 0))],
        out_shape=[jax.ShapeDtypeStruct((bsz, s, NSA_HEADS * NSA_DK), BF16),
                   jax.ShapeDtypeStruct(q.shape, BF16)],
        scratch_shapes=[s_buf, s_buf, rank_buf, rank_buf],
        compiler_params=_cparams(("parallel", "parallel")),
        name="nsa_cmp_attn",
    )(q, ck, cv, bias, ov)


def _nsa_window_kernel(q_ref, k_ref, v_ref, bias_ref, o_ref, s0_ref, s1_ref, *, tq, window_steps):
    nqt = q_ref.shape[2] // tq
    assert nqt % 2 == 0
    ratio = NSA_TK // tq
    n_tiles = bias_ref.shape[1]
    half = NSA_HPG // 2

    def key_chunk(t, i):
        return t // ratio - (window_steps - 1) + i

    def key_rows(t, i):
        return pl.ds(pl.multiple_of(jnp.maximum(key_chunk(t, i), 0) * NSA_TK, NSA_TK), NSA_TK)

    def stage_scores(t, buf):
        t = jnp.minimum(t, nqt - 1)
        q = q_ref[0, :, pl.ds(pl.multiple_of(t * tq, tq), tq), :].reshape(NSA_HPG * tq, LANES)
        for i in range(window_steps):
            kc = key_chunk(t, i)
            tile = jnp.where(kc < 0, n_tiles - 1, t - ratio * kc)
            s = _dot_t(q, k_ref[0, 0, key_rows(t, i), :]).reshape(NSA_HPG, tq, NSA_TK)
            buf[:, :, i * NSA_TK:(i + 1) * NSA_TK] = s + bias_ref[0, tile]

    def stage_out(t, buf):
        accs = []
        for h in range(NSA_HPG):
            s = buf[h]
            p = jnp.exp2(s - jnp.max(s, axis=-1, keepdims=True)).astype(BF16)
            acc = None
            for i in range(window_steps):
                pv = _dot(p[:, i * NSA_TK:(i + 1) * NSA_TK], v_ref[0, 0, h // half, key_rows(t, i), :])
                acc = pv if acc is None else acc + pv
            accs.append(acc)
        rows = pl.ds(pl.multiple_of(t * tq, tq), tq)
        for j in range(half):
            o = _normalized_pair(accs[j], accs[half + j])
            o_ref[0, rows, j * LANES:(j + 1) * LANES] = o.astype(o_ref.dtype)

    stage_scores(0, s0_ref)

    def two_tiles(j, carry):
        t = 2 * j
        stage_scores(t + 1, s1_ref)
        stage_out(t, s0_ref)
        stage_scores(t + 2, s0_ref)
        stage_out(t + 1, s1_ref)
        return carry

    lax.fori_loop(0, nqt // 2, two_tiles, 0)


def _nsa_selected_kernel(q_ref, k_ref, v_ref, bias_ref, o_ref, s0_ref, s1_ref, m_ref, acc_ref, *, tq):
    nqt = q_ref.shape[2] // tq
    ratio = NSA_TK // tq
    n_tiles = bias_ref.shape[1]
    half = NSA_HPG // 2
    chunk = SEL_TILES * NSA_TK
    per = chunk // tq

    def key_rows(kc):
        return pl.ds(pl.multiple_of(kc * chunk, chunk), chunk)

    def scores(t, kc, valid, buf):
        q = q_ref[0, :, pl.ds(pl.multiple_of(t * tq, tq), tq), :].reshape(NSA_HPG * tq, LANES)
        s = _dot_t(q, k_ref[0, 0, key_rows(kc), :]).reshape(NSA_HPG, tq, chunk)
        for i in range(SEL_TILES):
            offset = t - ratio * (SEL_TILES * kc + i)
            tile = jnp.where(valid & (offset >= 0), jnp.minimum(offset, n_tiles - 2), n_tiles - 1)
            cols = slice(i * NSA_TK, (i + 1) * NSA_TK)
            buf[:, :, cols] = s[:, :, cols] + bias_ref[0, tile]

    def values(h, kc):
        return v_ref[0, 0, h // half, key_rows(kc), :]

    def write_out(t, accs):
        rows = pl.ds(pl.multiple_of(t * tq, tq), tq)
        for j in range(half):
            o = _normalized_pair(accs[j], accs[half + j])
            o_ref[0, rows, j * LANES:(j + 1) * LANES] = o.astype(o_ref.dtype)

    _causal_item_stream(nqt, lambda t: t // per, NSA_HPG, scores, values, write_out,
                        (s0_ref, s1_ref), m_ref, acc_ref)


def _nsa_selected(q, k, v, bias, s):
    bsz = q.shape[0]
    tq = NSA_TQ
    n_off = bias.shape[1]
    assert s % (SEL_TILES * NSA_TK) == 0
    s_buf = pltpu.VMEM((NSA_HPG, tq, SEL_TILES * NSA_TK), F32)
    once = pl.Buffered(1)
    return pl.pallas_call(
        functools.partial(_nsa_selected_kernel, tq=tq),
        grid=(NSA_GROUPS, bsz),
        in_specs=[pl.BlockSpec((1, NSA_HPG, s, LANES), lambda g, b: (b, g, 0, 0), pipeline_mode=once),
                  pl.BlockSpec((1, 1, s, LANES), lambda g, b: (b, g, 0, 0)),
                  pl.BlockSpec((1, 1, 2, s, LANES), lambda g, b: (b, g, 0, 0, 0)),
                  pl.BlockSpec((1, n_off, NSA_HPG, tq, NSA_TK), lambda g, b: (g, 0, 0, 0, 0), pipeline_mode=once)],
        out_specs=pl.BlockSpec((1, s, NSA_HPG * NSA_DK), lambda g, b: (b, 0, g)),
        out_shape=jax.ShapeDtypeStruct((bsz, s, NSA_HEADS * NSA_DK), BF16),
        scratch_shapes=[s_buf, s_buf, pltpu.VMEM((NSA_HPG, tq, LANES), F32), pltpu.VMEM((NSA_HPG, tq, LANES), F32)],
        compiler_params=_cparams(("parallel", "parallel")),
        name="nsa_flash_sel",
    )(q, k, v, bias)


def _nsa_window(q, k, v, bias, s, window_steps):
    bsz = q.shape[0]
    tq = NSA_TQ
    n_off = bias.shape[1]
    s_buf = pltpu.VMEM((NSA_HPG, tq, window_steps * NSA_TK), F32)
    once = pl.Buffered(1)
    return pl.pallas_call(
        functools.partial(_nsa_window_kernel, tq=tq, window_steps=window_steps),
        grid=(NSA_GROUPS, bsz),
        in_specs=[pl.BlockSpec((1, NSA_HPG, s, LANES), lambda g, b: (b, g, 0, 0), pipeline_mode=once),
                  pl.BlockSpec((1, 1, s, LANES), lambda g, b: (b, g, 0, 0)),
                  pl.BlockSpec((1, 1, 2, s, LANES), lambda g, b: (b, g, 0, 0, 0)),
                  pl.BlockSpec((1, n_off, NSA_HPG, tq, NSA_TK), lambda g, b: (g, 0, 0, 0, 0), pipeline_mode=once)],
        out_specs=pl.BlockSpec((1, s, NSA_HPG * NSA_DK), lambda g, b: (b, 0, g)),
        out_shape=jax.ShapeDtypeStruct((bsz, s, NSA_HEADS * NSA_DK), BF16),
        scratch_shapes=[s_buf, s_buf],
        compiler_params=_cparams(("parallel", "parallel")),
        name="nsa_flash_win",
    )(q, k, v, bias)


def _hybrid_layer(h, bsz, s, layer, g_mix, g_mlp, w1, w2, w_in, conv_w, conv_b, wa, ba, wx, bx, lam,
                  q_norm, kv_norm, w_uq, w_ukv, qn_q, qn_k, w_out):
    d = h.shape[1]
    c = LRU_WIDTH
    o_kv = 2 * c + MLA_Q_RANK
    o_kr = o_kv + MLA_KV_RANK
    w_kr = jnp.pad(w_in[:, o_kr:], ((0, 0), (MLA_NOPE, LANES - MLA_QK)))
    w_all = jnp.concatenate([w_in[:, :o_kv], w_kr, w_in[:, o_kv:o_kr]], axis=1).astype(BF16)
    z_lru, q_lat, k_rope, kv_lat = _norm_proj(h, g_mix, w_all,
                                              ((2 * c,), (MLA_Q_RANK, LANES), (MLA_KV_RANK,)))
    lru = _rg_lru(z_lru.reshape(bsz, s, 2 * c), conv_w, conv_b, wa, ba, wx, bx, lam)
    q, k, v = _mla_prep(q_lat, kv_lat, k_rope, bsz, s, q_norm, kv_norm, w_uq, w_ukv, qn_q, qn_k)
    mla = _mla_attention(q, k, v)
    wo = w_out.astype(BF16)
    return _mix_mlp(h, [[lru.reshape(bsz * s, c)], [mla.reshape(bsz * s, -1)]], [wo[:c], wo[c:]],
                    g_mlp, w1, w2, layer)


def _nsa_layer(h, bsz, s, layer, g_mix, g_mlp, w1, w2, w_in, pos_k, w1_k, w2_k, pos_v, w1_v, w2_v,
               q_norm, k_norm, rel_bias, w_out):
    nq = NSA_HEADS * NSA_DK
    kvw = NSA_GROUPS * NSA_DK
    w_gate = _pad_cols(w_in[:, nq + 6 * kvw:], LANES)
    w_all = jnp.concatenate([w_in[:, :nq + 6 * kvw], w_gate], axis=1).astype(BF16)
    z_q, z_kc, z_vc, z_kv, z_gate = _norm_proj(h, g_mix, w_all, ((nq,), (kvw, kvw), (4 * kvw,), (LANES,)),
                                               folds=(1, CMP_STRIDE, CMP_STRIDE, 1, 1))

    q, ks, kw, vs, vw = _nsa_prep(z_q, z_kv, bsz, s, q_norm, k_norm)
    ck, cv = _compress(z_kc, z_vc, bsz, s, pos_k, w1_k, w2_k, pos_v, w1_v, w2_v, k_norm[0])

    order = np.array(HEAD_ORDER)
    expand = np.zeros((3, LANES, NSA_HEADS * NSA_DK), np.float32)
    for br in range(3):
        for hd in range(NSA_HEADS):
            expand[br, 3 * hd + br, hd * NSA_DK:(hd + 1) * NSA_DK] = 1.0

    slot_heads = (np.arange(NSA_GROUPS)[:, None] * NSA_HPG + order[None, :]).reshape(-1)
    tab_slots = rel_bias.T[slot_heads]
    nh = s // CMP_STRIDE
    nc = (s - CMP_BLOCK) // CMP_STRIDE + 1
    cmp_bias = _bias_tiles(tab_slots, _cmp_bias_ids(s))
    sel_bias = _bias_tiles(tab_slots, _toeplitz_bucket_ids(None)[0])
    win_ids, n_win_tiles = _toeplitz_bucket_ids(WINDOW)
    win_bias = _bias_tiles(tab_slots, win_ids)

    nsel = s // SLC_BLOCK
    cstart = np.arange(nh) * CMP_STRIDE
    sstart = np.arange(nsel) * SLC_BLOCK
    ov = np.clip(np.minimum(cstart[:, None] + CMP_BLOCK, sstart[None, :] + SLC_BLOCK)
                 - np.maximum(cstart[:, None], sstart[None, :]), 0, None).astype(np.float32) / CMP_BLOCK
    ov[nc:] = 0.0
    ov_p = np.zeros((nh, LANES), np.float32)
    ov_p[:, HALF:HALF + nsel] = ov

    o_c, q_sel = _cmp_attention(q, ck, cv, cmp_bias, jnp.asarray(np.concatenate([ov_p, ov_p])), s)
    o_s = _nsa_selected(q_sel, ks, vs, sel_bias, s)
    window_steps = (n_win_tiles + 1) // (NSA_TK // NSA_TQ)
    o_w = _nsa_window(q, kw, vw, win_bias, s, window_steps)

    flat = lambda a: a.reshape(bsz * s, -1)
    return _mix_mlp(h, [[flat(o_c), flat(o_s), flat(o_w)]], [w_out.astype(BF16)],
                    g_mlp, w1, w2, layer,
                    gate=(z_gate, jnp.asarray(np.concatenate([expand, expand], axis=1), BF16)))


def kernel(x, rel_bias, norm_mix, norm_mlp, mlp_w1, mlp_w2, hy_w_in, lru_conv_w, lru_conv_b, lru_wa, lru_ba, lru_wx, lru_bx, lru_lambda, mla_q_norm, mla_kv_norm, mla_w_uq, mla_w_ukv, mla_qn_q, mla_qn_k, hy_w_out, nsa_w_in, nsa_cmp_pos_k, nsa_cmp_w1_k, nsa_cmp_w2_k, nsa_cmp_pos_v, nsa_cmp_w1_v, nsa_cmp_w2_v, nsa_q_norm, nsa_k_norm, nsa_w_out):
    bsz, s, d = x.shape
    depth = norm_mix.shape[0]
    h = x.reshape(bsz * s, d)
    w1_all, w2_all = mlp_w1.astype(BF16), mlp_w2.astype(BF16)
    for layer in range(depth):
        if layer % 2 == 0:
            e = layer // 2
            h = _hybrid_layer(h, bsz, s, layer, norm_mix[layer], norm_mlp[layer], w1_all, w2_all,
                              hy_w_in[e], lru_conv_w[e], lru_conv_b[e], lru_wa[e], lru_ba[e], lru_wx[e],
                              lru_bx[e], lru_lambda[e], mla_q_norm[e], mla_kv_norm[e], mla_w_uq[e],
                              mla_w_ukv[e], mla_qn_q[e], mla_qn_k[e], hy_w_out[e])
        else:
            o = layer // 2
            h = _nsa_layer(h, bsz, s, layer, norm_mix[layer], norm_mlp[layer], w1_all, w2_all,
                           nsa_w_in[o], nsa_cmp_pos_k[o], nsa_cmp_w1_k[o], nsa_cmp_w2_k[o],
                           nsa_cmp_pos_v[o], nsa_cmp_w1_v[o], nsa_cmp_w2_v[o], nsa_q_norm[o],
                           nsa_k_norm[o], rel_bias, nsa_w_out[o])
    return h.reshape(bsz, s, d)
```

```python
import functools
import math

import numpy as np
import jax
import jax.numpy as jnp
from jax import lax
from jax.experimental import pallas as pl
from jax.experimental.pallas import tpu as pltpu

F32 = jnp.float32
BF16 = jnp.bfloat16

NORM_EPS = 1e-6
MASK_VALUE = -1e30
FORCE_SCORE = 1e4
REMOVED_SCORE = -3e38

NUM_BUCKETS = 32
MAX_DISTANCE = 1024
MASKED_ID = NUM_BUCKETS
EXCLUDED_ID = NUM_BUCKETS + 1
LOG2E = math.log2(math.e)

LRU_WIDTH = 512
LRU_BLOCKS = 8
CONV_WIDTH = 4
LRU_C = 8.0

MLA_HEADS = 8
MLA_NOPE = 64
MLA_ROPE = 32
MLA_QK = MLA_NOPE + MLA_ROPE
MLA_V = 64
MLA_Q_RANK = 384
MLA_KV_RANK = 256
ROPE_THETA = 10000.0

NSA_HEADS = 16
NSA_GROUPS = 2
NSA_HPG = NSA_HEADS // NSA_GROUPS
NSA_DK = 64
CMP_BLOCK = 32
CMP_STRIDE = 16
CMP_HIDDEN = 256
SLC_BLOCK = 64
SLC_TOPN = 8
WINDOW = 512

LANES = 128
SUBLANES = 8
HALF = LANES // 2
NSA_TQ = 128
NSA_TK = 256
SEL_TILES = 2
MLA_TR = 128
MLA_TK = 512
HEAD_ORDER = (0, 2, 4, 6, 1, 3, 5, 7)
V7X_VMEM_BYTES = 64 * 1024 * 1024
VMEM_LIMIT = V7X_VMEM_BYTES * 7 // 8


def _cparams(sem):
    return pltpu.CompilerParams(dimension_semantics=sem, vmem_limit_bytes=VMEM_LIMIT)


def _rms(x, g, n=None):
    n = x.shape[-1] if n is None else n
    ms = jnp.sum(x * x, axis=-1, keepdims=True) * (1.0 / n)
    return x * lax.rsqrt(ms + NORM_EPS) * g


def _rms_lanes(x, g, n):
    y = x * x
    hi = y.astype(BF16)
    parts = jnp.concatenate([hi, (y - hi.astype(F32)).astype(BF16)], axis=1)
    ss = _dot(parts, jnp.ones((2 * LANES, LANES), BF16))
    return x * lax.rsqrt(ss * (1.0 / n) + NORM_EPS) * g


def _rms_half_lanes(x, g):
    y = x * x
    hi = y.astype(BF16)
    parts = jnp.concatenate([hi, (y - hi.astype(F32)).astype(BF16)], axis=1)
    row_half = lax.broadcasted_iota(jnp.int32, (2 * LANES, LANES), 0) % LANES // HALF
    col_half = lax.broadcasted_iota(jnp.int32, (2 * LANES, LANES), 1) // HALF
    ss = _dot(parts, jnp.where(row_half == col_half, 1.0, 0.0).astype(BF16))
    return x * lax.rsqrt(ss * (1.0 / HALF) + NORM_EPS) * g


def _gelu_tanh(x):
    return 0.5 * x * (1.0 + jnp.tanh(math.sqrt(2.0 / math.pi) * (x + 0.044715 * (x * x * x))))


def _sigmoid(x):
    return 1.0 / (1.0 + jnp.exp(-x))


def _dot(a, b):
    return jnp.dot(a, b, preferred_element_type=F32)


def _dot_t(a, b):
    return lax.dot_general(a, b, (((1,), (1,)), ((), ())), preferred_element_type=F32)


def _causal_item_stream(n_tiles, last_chunk, n_chains, scores, values, write_out, s_bufs, m_ref, acc_ref):
    m_ref[...] = jnp.full_like(m_ref, MASK_VALUE)
    acc_ref[...] = jnp.zeros_like(acc_ref)
    n_items = sum(last_chunk(t) + 1 for t in range(n_tiles))

    def advance(t, kc):
        wrap = kc >= last_chunk(t)
        return jnp.where(wrap, t + 1, t), jnp.where(wrap, 0, kc + 1)

    def stage_scores(t, kc, buf):
        valid = t < n_tiles
        scores(jnp.minimum(t, n_tiles - 1), jnp.where(valid, kc, 0), valid, buf)

    def stage_update(t, kc, buf, maybe_padding):
        tile_done = kc >= last_chunk(t)
        kc = jnp.where(t < n_tiles, kc, 0) if maybe_padding else kc
        accs = []
        for c in range(n_chains):
            m_old = m_ref[c]
            m_new = jnp.maximum(m_old, jnp.max(buf[c], axis=-1, keepdims=True))
            alpha = jnp.exp2(m_old - m_new)
            s = buf[c]
            p = jnp.exp2(s - jnp.concatenate([m_new] * (s.shape[1] // LANES), axis=1))
            acc = acc_ref[c] * alpha + _dot(p.astype(BF16), values(c, kc))
            m_ref[c] = jnp.where(tile_done, MASK_VALUE, m_new)
            acc_ref[c] = jnp.where(tile_done, 0.0, acc)
            accs.append(acc)
        if maybe_padding:
            pl.when(t < n_tiles)(lambda: write_out(t, accs))
        else:
            write_out(t, accs)

    start = (jnp.int32(0), jnp.int32(0))
    stage_scores(*start, s_bufs[0])

    def two_items(_, item):
        item1 = advance(*item)
        stage_scores(*item1, s_bufs[1])
        stage_update(*item, s_bufs[0], False)
        item2 = advance(*item1)
        stage_scores(*item2, s_bufs[0])
        stage_update(*item1, s_bufs[1], n_items % 2 == 1)
        return item2

    lax.fori_loop(0, (n_items + 1) // 2, two_items, start)


def _normalized_pair(acc_even, acc_odd):
    lo = lax.broadcasted_iota(jnp.int32, acc_even.shape, 1) < HALF
    return jnp.where(lo, acc_even / acc_even[:, LANES - 1:LANES], acc_odd / acc_odd[:, 0:1])


def _with_ones_column(v, odd):
    lane = lax.broadcasted_iota(jnp.int32, v.shape, 1)
    return jnp.where(lane == (0 if odd else LANES - 1), 1.0, v)


def _norm_proj_kernel(x_ref, g_ref, w_ref, *refs, groups, folds):
    n_out = len(folds)
    out_refs, fold_refs = refs[:n_out], list(refs[n_out:])
    xn = _rms(x_ref[...], g_ref[...]).astype(BF16)
    off = 0
    k = 0
    for group in groups:
        width = sum(group)
        z = _dot(xn, w_ref[:, off:off + width])
        off += width
        col = 0
        for n in group:
            piece = z[:, col:col + n]
            col += n
            if folds[k] == 1:
                out_refs[k][...] = piece
            else:
                scratch = fold_refs.pop(0)
                scratch[...] = piece
                rows = piece.shape[0] // folds[k]
                for j in range(folds[k]):
                    out_refs[k][:, j * n:(j + 1) * n] = scratch[pl.ds(j, rows, stride=folds[k]), :]
            k += 1


def _norm_proj(x, g, w, groups, folds=None, tm=512):
    t, d = x.shape
    n = w.shape[1]
    splits = [s for group in groups for s in group]
    folds = tuple(folds) if folds is not None else (1,) * len(splits)
    assert sum(splits) == n and t % tm == 0
    return pl.pallas_call(
        functools.partial(_norm_proj_kernel, groups=groups, folds=folds),
        grid=(t // tm,),
        in_specs=[pl.BlockSpec((tm, d), lambda i: (i, 0)),
                  pl.BlockSpec((1, d), lambda i: (0, 0)),
                  pl.BlockSpec((d, n), lambda i: (0, 0))],
        out_specs=[pl.BlockSpec((tm // f, s * f), lambda i: (i, 0)) for s, f in zip(splits, folds)],
        out_shape=[jax.ShapeDtypeStruct((t // f, s * f), F32) for s, f in zip(splits, folds)],
        scratch_shapes=[pltpu.VMEM((tm, s), F32) for s, f in zip(splits, folds) if f > 1],
        compiler_params=_cparams(("parallel",)),
        name="norm_proj",
    )(x, g.reshape(1, d), w)


def _mix_mlp_kernel(*refs, group_sizes, gated):
    n_a = sum(group_sizes)
    n_g = len(group_sizes)
    h_ref = refs[0]
    a_refs = refs[1:1 + n_a]
    wo_refs = refs[1 + n_a:1 + n_a + n_g]
    rest = refs[1 + n_a + n_g:]
    if gated:
        zg_ref, e_ref = rest[:2]
        rest = rest[2:]
    g_ref, w1_ref, w2_ref, out_ref, hres, xn, acc = rest
    f = pl.program_id(1)

    @pl.when(f == 0)
    def _():
        h1 = h_ref[...]
        if gated:
            gate = _sigmoid(zg_ref[...])
            g_hi = gate.astype(BF16)
            g_parts = jnp.concatenate([g_hi, (gate - g_hi.astype(F32)).astype(BF16)], axis=1)
        k = 0
        for gi, gs in enumerate(group_sizes):
            a = None
            for j in range(gs):
                aj = a_refs[k + j][...].astype(F32)
                if gated:
                    aj = aj * _dot(g_parts, e_ref[j])
                a = aj if a is None else a + aj
            k += gs
            h1 = h1 + _dot(a.astype(BF16), wo_refs[gi][...])
        hres[...] = h1
        xn[...] = _rms(h1, g_ref[...]).astype(BF16)
        acc[...] = jnp.zeros_like(acc)

    u = jnp.maximum(_dot(xn[...], w1_ref[...]), 0.0)
    acc[...] += _dot((u * u).astype(BF16), w2_ref[...])

    @pl.when(f == pl.num_programs(1) - 1)
    def _():
        out_ref[...] = hres[...] + acc[...]


def _mix_mlp(h, groups, wos, g, w1, w2, layer, gate=None, tm=512, tf=2048):
    t, d = h.shape
    ff = w1.shape[2]
    group_sizes = tuple(len(gr) for gr in groups)
    a_list = [a for gr in groups for a in gr]
    extra = [] if gate is None else list(gate)
    in_specs = [pl.BlockSpec((tm, d), lambda i, f: (i, 0))]
    in_specs += [pl.BlockSpec((tm, a.shape[1]), lambda i, f: (i, 0)) for a in a_list]
    in_specs += [pl.BlockSpec(w.shape, lambda i, f: (0, 0)) for w in wos]
    if gate is not None:
        in_specs += [pl.BlockSpec((tm, LANES), lambda i, f: (i, 0)),
                     pl.BlockSpec(gate[1].shape, lambda i, f: (0, 0, 0))]
    in_specs += [pl.BlockSpec((1, d), lambda i, f: (0, 0)),
                 pl.BlockSpec((None, d, tf), lambda i, f: (layer, 0, f)),
                 pl.BlockSpec((None, tf, d), lambda i, f: (layer, f, 0))]
    return pl.pallas_call(
        functools.partial(_mix_mlp_kernel, group_sizes=group_sizes, gated=gate is not None),
        grid=(t // tm, ff // tf),
        in_specs=in_specs,
        out_specs=pl.BlockSpec((tm, d), lambda i, f: (i, 0)),
        out_shape=jax.ShapeDtypeStruct((t, d), F32),
        scratch_shapes=[pltpu.VMEM((tm, d), F32), pltpu.VMEM((tm, d), BF16),
                        pltpu.VMEM((tm, d), F32)],
        compiler_params=_cparams(("parallel", "arbitrary")),
        name="mix_mlp",
    )(h, *a_list, *wos, *extra, g.reshape(1, d), w1, w2)


def _lru_kernel(xr_ref, xg_ref, cw_ref, cb_ref, wa_ref, ba_ref, wx_ref, bx_ref, lam_ref,
                o_ref, tail_ref, hc_ref, *, tc):
    c = pl.program_id(1)

    @pl.when(c == 0)
    def _():
        tail_ref[...] = jnp.zeros_like(tail_ref)
        hc_ref[...] = jnp.zeros_like(hc_ref)

    x = xr_ref[0]
    width = x.shape[1]
    row = lax.broadcasted_iota(jnp.int32, (tc, width), 0)
    tail_row = lax.broadcasted_iota(jnp.int32, (SUBLANES, width), 0)
    tail = tail_ref[...]
    cw = cw_ref[...]
    y = x * cw[CONV_WIDTH - 1:CONV_WIDTH]
    for s in range(1, CONV_WIDTH):
        sh = pltpu.roll(x, s, 0)
        head = jnp.where(tail_row < s, pltpu.roll(tail, s, 0), sh[0:SUBLANES])
        sh = jnp.concatenate([head, sh[SUBLANES:]], axis=0)
        y = y + sh * cw[CONV_WIDTH - 1 - s:CONV_WIDTH - s]
    y = y + cb_ref[...]
    tail_ref[...] = x[tc - SUBLANES:tc]

    yb = y.astype(BF16)
    r = _sigmoid(_dot(yb, wa_ref[...]) + ba_ref[...])
    i = _sigmoid(_dot(yb, wx_ref[...]) + bx_ref[...])
    nl = -lam_ref[...]
    softplus = jnp.maximum(nl, 0.0) + jnp.log(1.0 + jnp.exp(-jnp.abs(nl)))
    log_a = (-LRU_C) * r * softplus
    a = jnp.exp(log_a)
    mult = jnp.sqrt(1.0 - jnp.exp(2.0 * log_a))
    mult = jnp.where((row == 0) & (c == 0), 1.0, mult)
    b = mult * (i * y)

    d = 1
    while d < tc:
        keep = row >= d
        a_sh = jnp.where(keep, pltpu.roll(a, d, 0), 1.0)
        b_sh = jnp.where(keep, pltpu.roll(b, d, 0), 0.0)
        b = a * b_sh + b
        a = a * a_sh
        d *= 2
    h = b + a * hc_ref[...]
    hc_ref[...] = h[tc - 1:tc]
    o_ref[0] = (h * _gelu_tanh(xg_ref[0])).astype(o_ref.dtype)


def _block_diag(w):
    n, d, e = w.shape
    eye = jnp.eye(n, dtype=w.dtype)
    return (w[:, :, None, :] * eye[:, None, :, None]).reshape(n * d, n * e)


def _rg_lru(z_lru, conv_w, conv_b, wa, ba, wx, bx, lam, tc=256):
    bsz, s, _ = z_lru.shape
    c = LRU_WIDTH
    tc = min(tc, s)
    vec = lambda v: pl.BlockSpec(v, lambda b, i: (0, 0))
    return pl.pallas_call(
        functools.partial(_lru_kernel, tc=tc),
        grid=(bsz, s // tc),
        in_specs=[pl.BlockSpec((1, tc, c), lambda b, i: (b, i, 0)),
                  pl.BlockSpec((1, tc, c), lambda b, i: (b, i, 1)),
                  vec((CONV_WIDTH, c)), vec((1, c)), vec((c, c)), vec((1, c)),
                  vec((c, c)), vec((1, c)), vec((1, c))],
        out_specs=pl.BlockSpec((1, tc, c), lambda b, i: (b, i, 0)),
        out_shape=jax.ShapeDtypeStruct((bsz, s, c), BF16),
        scratch_shapes=[pltpu.VMEM((SUBLANES, c), F32), pltpu.VMEM((1, c), F32)],
        compiler_params=_cparams(("parallel", "arbitrary")),
        name="rg_lru",
    )(z_lru, z_lru, conv_w, conv_b.reshape(1, c), _block_diag(wa).astype(BF16), ba.reshape(1, c),
      _block_diag(wx).astype(BF16), bx.reshape(1, c), lam.reshape(1, c))


def _rope(x, cos, sin_lo, sin_hi):
    return (x * cos + pltpu.roll(x, LANES - MLA_ROPE // 2, 1) * sin_lo
            + pltpu.roll(x, MLA_ROPE // 2, 1) * sin_hi)


def _mla_prep_kernel(ql_ref, kvl_ref, kr_ref, gq_ref, gkv_ref, wq_ref, wk_ref, wv_ref,
                     nq_ref, nk_ref, cos_ref, s1_ref, s2_ref, q_out, k_out, v_out):
    qn = _rms(ql_ref[...], gq_ref[...]).astype(BF16)
    kvn = _rms(kvl_ref[...], gkv_ref[...]).astype(BF16)
    kr = kr_ref[...]
    cos, s1, s2 = cos_ref[...], s1_ref[...], s2_ref[...]
    scale = MLA_QK ** -0.5 * LOG2E
    for h in range(MLA_HEADS):
        sl = slice(h * LANES, (h + 1) * LANES)
        qh = _rms_lanes(_dot(qn, wq_ref[:, sl]), nq_ref[...], MLA_QK)
        q_out[0, h] = (_rope(qh, cos, s1, s2) * scale).astype(BF16)
        kh = _rms_lanes(_dot(kvn, wk_ref[:, sl]) + kr, nk_ref[...], MLA_QK)
        k_out[0, h] = _rope(kh, cos, s1, s2).astype(BF16)
        v_out[0, h] = _with_ones_column(_dot(kvn, wv_ref[:, sl]), h % 2 == 1).astype(BF16)


def _rope_tables(s):
    half = MLA_ROPE // 2
    freqs = ROPE_THETA ** (-jnp.arange(half, dtype=F32) / half)
    ang = jnp.arange(s, dtype=F32)[:, None] * freqs[None, :]
    cos, sin = jnp.cos(ang), jnp.sin(ang)
    z = lambda n: jnp.zeros((s, n), F32)
    cos_t = jnp.concatenate([jnp.ones((s, MLA_NOPE), F32), cos, cos, z(LANES - MLA_QK)], axis=1)
    s_lo = jnp.concatenate([z(MLA_NOPE), -sin, z(LANES - MLA_NOPE - half)], axis=1)
    s_hi = jnp.concatenate([z(MLA_NOPE + half), sin, z(LANES - MLA_QK)], axis=1)
    return cos_t, s_lo, s_hi


def _pad_cols(w, n):
    return jnp.pad(w, ((0, 0),) * (w.ndim - 1) + ((0, n - w.shape[-1]),))


def _mla_prep(q_lat, kv_lat, k_rope, bsz, s, q_norm, kv_norm, w_uq, w_ukv, qn_q, qn_k, tm=512):
    tm = min(tm, s)
    nt = s // tm
    hl = MLA_HEADS * LANES
    wq = _pad_cols(w_uq.reshape(MLA_Q_RANK, MLA_HEADS, MLA_QK), LANES).reshape(MLA_Q_RANK, hl)
    wkv = w_ukv.reshape(MLA_KV_RANK, MLA_HEADS, MLA_NOPE + MLA_V)
    wk = _pad_cols(wkv[..., :MLA_NOPE], LANES).reshape(MLA_KV_RANK, hl)
    wv = wkv[..., MLA_NOPE:]
    zero = jnp.zeros_like(wv)
    even = (jnp.arange(MLA_HEADS) % 2 == 0)[None, :, None]
    wv = jnp.concatenate([jnp.where(even, wv, zero), jnp.where(even, zero, wv)], axis=-1)
    wv = wv.reshape(MLA_KV_RANK, hl)
    cos_t, s_lo, s_hi = _rope_tables(s)
    row = lambda n: pl.BlockSpec((tm, n), lambda b, i: (b * nt + i, 0))
    full = lambda a: pl.BlockSpec(a.shape, lambda b, i: (0, 0))
    tab = pl.BlockSpec((tm, LANES), lambda b, i: (i, 0))
    head_out = pl.BlockSpec((1, MLA_HEADS, tm, LANES), lambda b, i: (b, 0, i, 0))
    args = (q_lat, kv_lat, k_rope, q_norm.reshape(1, -1), kv_norm.reshape(1, -1),
            wq.astype(BF16), wk.astype(BF16), wv.astype(BF16),
            _pad_cols(qn_q.reshape(1, -1), LANES), _pad_cols(qn_k.reshape(1, -1), LANES),
            cos_t, s_lo, s_hi)
    in_specs = [row(MLA_Q_RANK), row(MLA_KV_RANK), row(LANES)] + [full(a) for a in args[3:10]] + [tab] * 3
    shp = jax.ShapeDtypeStruct((bsz, MLA_HEADS, s, LANES), BF16)
    return pl.pallas_call(
        _mla_prep_kernel,
        grid=(bsz, nt),
        in_specs=in_specs,
        out_specs=[head_out] * 3,
        out_shape=[shp] * 3,
        compiler_params=_cparams(("parallel", "parallel")),
        name="mla_prep",
    )(*args)


def _mla_attn_kernel(q_ref, k_ref, v_ref, mask_ref, o_ref, s0_ref, s1_ref, m_ref, acc_ref, *, tq):
    n_rc = tq // MLA_TR
    ratio = MLA_TK // MLA_TR
    n_tiles = mask_ref.shape[0]

    def key_rows(kc):
        return pl.ds(pl.multiple_of(kc * MLA_TK, MLA_TK), MLA_TK)

    def scores(qi, kc, valid, buf):
        rows = pl.ds(pl.multiple_of(qi * tq, tq), tq)
        for j in range(2):
            s = _dot_t(q_ref[0, j, rows, :], k_ref[0, j, key_rows(kc), :])
            for rc in range(n_rc):
                offset = qi * n_rc + rc - ratio * kc
                tile = jnp.where(valid, jnp.clip(offset + 1, 0, n_tiles - 1), 0)
                buf[j * n_rc + rc] = s[rc * MLA_TR:(rc + 1) * MLA_TR] + mask_ref[tile]

    def values(c, kc):
        return v_ref[0, c // n_rc, key_rows(kc), :]

    def write_out(qi, accs):
        for rc in range(n_rc):
            rows = pl.ds(pl.multiple_of(qi * tq + rc * MLA_TR, MLA_TR), MLA_TR)
            o_ref[0, rows, :] = _normalized_pair(accs[rc], accs[n_rc + rc]).astype(o_ref.dtype)

    _causal_item_stream(q_ref.shape[2] // tq, lambda qi: (qi * tq + tq - 1) // MLA_TK, 2 * n_rc,
                        scores, values, write_out, (s0_ref, s1_ref), m_ref, acc_ref)


def _mla_causal_tiles():
    i = np.arange(MLA_TR)[:, None]
    j = np.arange(MLA_TK)[None, :]
    tiles = [np.full((MLA_TR, MLA_TK), -np.inf, np.float32)]
    for d in range(MLA_TK // MLA_TR):
        tiles.append(np.where(j <= i + d * MLA_TR, 0.0, -np.inf).astype(np.float32))
    tiles.append(np.zeros((MLA_TR, MLA_TK), np.float32))
    return np.stack(tiles)


def _mla_attention(q, k, v, tq=512):
    bsz, nh, s, _ = q.shape
    tq = min(tq, s)
    assert tq % MLA_TK == 0 and tq % MLA_TR == 0
    n_chains = 2 * (tq // MLA_TR)
    mask = _mla_causal_tiles()
    seq_spec = pl.BlockSpec((1, 2, s, LANES), lambda b, hp: (b, hp, 0, 0))
    return pl.pallas_call(
        functools.partial(_mla_attn_kernel, tq=tq),
        grid=(bsz, nh // 2),
        in_specs=[seq_spec, seq_spec, seq_spec, pl.BlockSpec(mask.shape, lambda b, hp: (0, 0, 0))],
        out_specs=pl.BlockSpec((1, s, LANES), lambda b, hp: (b, 0, hp)),
        out_shape=jax.ShapeDtypeStruct((bsz, s, nh * MLA_V), BF16),
        scratch_shapes=[pltpu.VMEM((n_chains, MLA_TR, MLA_TK), F32), pltpu.VMEM((n_chains, MLA_TR, MLA_TK), F32),
                        pltpu.VMEM((n_chains, MLA_TR, LANES), F32), pltpu.VMEM((n_chains, MLA_TR, LANES), F32)],
        compiler_params=_cparams(("parallel", "parallel")),
        name="mla_attn",
    )(q, k, v, jnp.asarray(mask))


def _t5_bucket_np(dist):
    n = np.maximum(dist, 0)
    max_exact = NUM_BUCKETS // 2
    nf = np.maximum(n, 1).astype(np.float32)
    large = max_exact + (np.log(nf / np.float32(max_exact)) / np.float32(math.log(MAX_DISTANCE / max_exact))
                         * np.float32(NUM_BUCKETS - max_exact)).astype(np.int32)
    large = np.minimum(large, NUM_BUCKETS - 1)
    return np.where(n < max_exact, n, large).astype(np.int32)


def _bias_tiles_kernel(tab_ref, idx_ref, o_ref, *, buckets):
    slot = pl.program_id(0) * NSA_HPG + pl.program_id(1)
    for i, present in enumerate(buckets):
        idx = idx_ref[i]
        acc = jnp.where(idx == EXCLUDED_ID, -jnp.inf, MASK_VALUE).astype(F32)
        for b in present:
            acc = jnp.where(idx == b, tab_ref[slot, b] * LOG2E, acc)
        o_ref[0, i, 0] = acc


def _bias_tiles(tab_slots, idx):
    n, r, c = idx.shape
    buckets = tuple(tuple(int(b) for b in np.unique(tile) if b < NUM_BUCKETS) for tile in idx)
    return pl.pallas_call(
        functools.partial(_bias_tiles_kernel, buckets=buckets),
        grid=(NSA_GROUPS, NSA_HPG),
        in_specs=[pl.BlockSpec(memory_space=pltpu.SMEM),
                  pl.BlockSpec((n, r, c), lambda g, p: (0, 0, 0))],
        out_specs=pl.BlockSpec((1, n, 1, r, c), lambda g, p: (g, 0, p, 0, 0)),
        out_shape=jax.ShapeDtypeStruct((NSA_GROUPS, n, NSA_HPG, r, c), F32),
        compiler_params=_cparams(("parallel", "parallel")),
        name="bias_tiles",
    )(tab_slots, jnp.asarray(idx))


def _cmp_bias_ids(s):
    nqt = s // NSA_TQ
    nh = s // CMP_STRIDE
    per_tile = NSA_TQ // CMP_STRIDE
    i = np.arange(NSA_TQ)[:, None]
    c = np.arange(nh)[None, :]
    dist = i - (CMP_BLOCK - 1) - CMP_STRIDE * (c - per_tile * (nqt - 1))
    ids = np.where(dist >= 0, _t5_bucket_np(dist), MASKED_ID).astype(np.int32)
    return ids[None]


def _toeplitz_bucket_ids(window):
    i = np.arange(NSA_TQ)[:, None]
    j = np.arange(NSA_TK)[None, :]
    tiles = []
    d = 0
    while True:
        dist = d * NSA_TQ + i - j
        ids = _t5_bucket_np(dist)
        ok = dist >= 0
        if window is not None:
            ok &= dist < window
        ids = np.where(ok, ids, MASKED_ID).astype(np.int32)
        tiles.append(ids)
        if window is None and (ids == NUM_BUCKETS - 1).all():
            break
        if window is not None and not ok.any():
            tiles.pop()
            break
        d += 1
    n_real = len(tiles)
    tiles.append(np.full((NSA_TQ, NSA_TK), EXCLUDED_ID, np.int32))
    return np.stack(tiles), n_real


def _split_halves(x):
    lo = lax.broadcasted_iota(jnp.int32, x.shape, 1) < HALF
    return jnp.where(lo, x, 0.0), pltpu.roll(jnp.where(lo, 0.0, x), HALF, 1)


def _value_layouts(x):
    lo = lax.broadcasted_iota(jnp.int32, x.shape, 1) < HALF
    g0e = jnp.where(lo, x, 0.0)
    g1o = jnp.where(lo, 0.0, x)
    pair = lambda e, o: (_with_ones_column(e, False), _with_ones_column(o, True))
    return pair(g0e, pltpu.roll(g0e, HALF, 1)), pair(pltpu.roll(g1o, HALF, 1), g1o)


def _nsa_prep_kernel(zq_ref, ks_ref, vs_ref, kw_ref, vw_ref, gq_ref, gks_ref, gkw_ref,
                     q_out, ks_out, kw_out, vs_out, vw_out, *, tm):
    i = pl.program_id(1)
    scale = NSA_DK ** -0.5 * LOG2E
    for j in range(NSA_HEADS // 2):
        e, o = _split_halves(_rms_half_lanes(zq_ref[:, j * LANES:(j + 1) * LANES], gq_ref[...]) * scale)
        g = (2 * j) // NSA_HPG
        pe = (2 * j - NSA_HPG * g) // 2
        q_out[0, NSA_HPG * g + pe] = e.astype(BF16)
        q_out[0, NSA_HPG * g + NSA_HPG // 2 + pe] = o.astype(BF16)

    lane = lax.broadcasted_iota(jnp.int32, (tm, LANES), 1)
    pos = i * tm + lax.broadcasted_iota(jnp.int32, (tm, LANES), 0)
    block_tag = jnp.where(lane - HALF == pos // SLC_BLOCK, MASK_VALUE, 0.0)
    for g, x in enumerate(_split_halves(_rms_half_lanes(ks_ref[...], gks_ref[...]))):
        ks_out[0, g] = (x + block_tag).astype(BF16)
    for g, x in enumerate(_split_halves(_rms_half_lanes(kw_ref[...], gkw_ref[...]))):
        kw_out[0, g] = x.astype(BF16)
    for ref, out in ((vs_ref, vs_out), (vw_ref, vw_out)):
        for g, (ve, vo) in enumerate(_value_layouts(ref[...])):
            out[0, g, 0] = ve.astype(BF16)
            out[0, g, 1] = vo.astype(BF16)


def _nsa_prep(z_q, z_kv, bsz, s, q_norm, k_norm, tm=512):
    tm = min(tm, s)
    nt = s // tm
    row = lambda n, c: pl.BlockSpec((tm, n), lambda b, i: (b * nt + i, c))
    gain = pl.BlockSpec((1, LANES), lambda b, i: (0, 0))
    kspec = pl.BlockSpec((1, NSA_GROUPS, tm, LANES), lambda b, i: (b, 0, i, 0))
    vspec = pl.BlockSpec((1, NSA_GROUPS, 2, tm, LANES), lambda b, i: (b, 0, 0, i, 0))
    kshape = jax.ShapeDtypeStruct((bsz, NSA_GROUPS, s, LANES), BF16)
    vshape = jax.ShapeDtypeStruct((bsz, NSA_GROUPS, 2, s, LANES), BF16)
    pad = lambda v: jnp.tile(v.reshape(1, -1), (1, LANES // NSA_DK))
    return pl.pallas_call(
        functools.partial(_nsa_prep_kernel, tm=tm),
        grid=(bsz, nt),
        in_specs=[row(NSA_HEADS * NSA_DK, 0), row(LANES, 0), row(LANES, 1), row(LANES, 2), row(LANES, 3),
                  gain, gain, gain],
        out_specs=[pl.BlockSpec((1, NSA_HEADS, tm, LANES), lambda b, i: (b, 0, i, 0)),
                   kspec, kspec, vspec, vspec],
        out_shape=[jax.ShapeDtypeStruct((bsz, NSA_HEADS, s, LANES), BF16), kshape, kshape, vshape, vshape],
        compiler_params=_cparams(("parallel", "parallel")),
        name="nsa_prep",
    )(z_q, z_kv, z_kv, z_kv, z_kv, pad(q_norm), pad(k_norm[1]), pad(k_norm[2]))


def _compress_kernel(xk_ref, xv_ref, pk_ref, pv_ref, w1k_ref, w1v_ref, w2k_ref, w2v_ref, gk_ref,
                     ck_out, cv_out):
    def mlp(x, p_ref, w1_ref, w2_ref):
        nh = x.shape[0]
        top = _dot((x + p_ref[0:1]).astype(BF16), w1_ref[0])
        bot = _dot((x + p_ref[1:2]).astype(BF16), w1_ref[1])
        hid = _gelu_tanh(top + pltpu.roll(bot, nh - 1, 0))
        return _dot(hid.astype(BF16), w2_ref[...])

    nh = xk_ref.shape[1]
    yk = mlp(xk_ref[0], pk_ref, w1k_ref, w2k_ref)
    for g, x in enumerate(_split_halves(yk)):
        ck = _rms(x, gk_ref[...], NSA_DK)
        ck_out[0, g, 0:nh] = ck
        ck_out[0, g, nh:2 * nh] = ck
    yv = mlp(xv_ref[0], pv_ref, w1v_ref, w2v_ref)
    for g, layouts in enumerate(_value_layouts(yv)):
        for parity, v in enumerate(layouts):
            cv_out[0, g, parity, 0:nh] = v
            cv_out[0, g, parity, nh:2 * nh] = v


def _compress_weights(pos, w1, w2):
    half = CMP_BLOCK // 2
    g = NSA_GROUPS
    eye = jnp.eye(g, dtype=F32)
    w1 = w1.astype(BF16).reshape(2, half, NSA_DK, CMP_HIDDEN)
    w1 = jnp.stack([jnp.pad(w1, ((0, 0), (0, 0), (0, 0), (gi * CMP_HIDDEN, (g - 1 - gi) * CMP_HIDDEN)))
                    for gi in range(g)], axis=2)
    w1 = w1.reshape(2, half * g * NSA_DK, g * CMP_HIDDEN)
    p = jnp.broadcast_to(pos.reshape(2, half, 1, NSA_DK), (2, half, g, NSA_DK)).reshape(2, half * g * NSA_DK)
    w2 = (w2[None, :, None, :] * eye[:, None, :, None]).reshape(g * CMP_HIDDEN, g * NSA_DK).astype(BF16)
    return p, w1, w2


def _compress(z_kc, z_vc, bsz, s, pos_k, w1_k, w2_k, pos_v, w1_v, w2_v, k_norm0):
    nh = s // CMP_STRIDE
    feat = CMP_STRIDE * LANES
    pk, w1k, w2k = _compress_weights(pos_k, w1_k, w2_k)
    pv, w1v, w2v = _compress_weights(pos_v, w1_v, w2_v)
    full = lambda a: pl.BlockSpec(a.shape, lambda b: (0,) * a.ndim)
    xspec = pl.BlockSpec((1, nh, feat), lambda b: (b, 0, 0))
    gk = _pad_cols(k_norm0.reshape(1, -1), LANES)
    return pl.pallas_call(
        _compress_kernel,
        grid=(bsz,),
        in_specs=[xspec, xspec, full(pk), full(pv), full(w1k), full(w1v), full(w2k), full(w2v), full(gk)],
        out_specs=[pl.BlockSpec((1, NSA_GROUPS, 2 * nh, LANES), lambda b: (b, 0, 0, 0)),
                   pl.BlockSpec((1, NSA_GROUPS, 2, 2 * nh, LANES), lambda b: (b, 0, 0, 0, 0))],
        out_shape=[jax.ShapeDtypeStruct((bsz, NSA_GROUPS, 2 * nh, LANES), F32),
                   jax.ShapeDtypeStruct((bsz, NSA_GROUPS, 2, 2 * nh, LANES), F32)],
        compiler_params=_cparams(("parallel",)),
        name="nsa_compress",
    )(z_kc.reshape(bsz, nh, feat), z_vc.reshape(bsz, nh, feat), pk, pv, w1k, w1v, w2k, w2v, gk)


def _cmp_attn_kernel(q_ref, ck_ref, cv_ref, bias_ref, ov_ref, o_ref, q2_ref, s0_ref, s1_ref,
                     rank0_ref, rank1_ref, *, tq, nsel):
    nqt = q_ref.shape[2] // tq
    assert nqt % 2 == 0
    half = NSA_HPG // 2
    nh = bias_ref.shape[4]
    per_tile = tq // CMP_STRIDE
    c_last = per_tile * (nqt - 1)
    n_blk = LANES - HALF

    def block_window(t):
        start = lax.rem(per_tile * t + (nh - c_last % nh), nh)
        return pl.ds(pl.multiple_of(start, per_tile), nh)

    def query_rows(t):
        return pl.ds(pl.multiple_of(t * tq, tq), tq)

    def stage_scores(t, buf):
        t = jnp.minimum(t, nqt - 1)
        col = lax.broadcasted_iota(jnp.int32, (nh, LANES), 0)
        lane_k = lax.broadcasted_iota(jnp.int32, (nh, LANES), 1)
        wrapped = (lane_k == HALF) & (col < c_last - per_tile * t)
        ck = jnp.where(wrapped, MASK_VALUE, ck_ref[0, 0, block_window(t), :]).astype(BF16)
        lane_q = lax.broadcasted_iota(jnp.int32, (NSA_HPG, tq, LANES), 2)
        q = q_ref[0, :, query_rows(t), :] + jnp.where(lane_q == HALF, 1.0, 0.0).astype(BF16)
        s = _dot_t(q.reshape(NSA_HPG * tq, LANES), ck)
        buf[...] = s.reshape(NSA_HPG, tq, nh) + bias_ref[0, 0]

    def stage_select(t, buf, rank_ref):
        cols = block_window(t)
        cv = [cv_ref[0, 0, parity, cols, :].astype(BF16) for parity in range(2)]
        ov = ov_ref[cols, :].astype(BF16)
        row_ok = t * tq + lax.broadcasted_iota(jnp.int32, (tq, nh), 0) >= CMP_BLOCK - 1
        outs = []
        psum = None
        for h in range(NSA_HPG):
            s = buf[h]
            e = jnp.exp2(s - jnp.max(s, axis=-1, keepdims=True))
            p = jnp.where(row_ok, e / jnp.sum(e, axis=-1, keepdims=True), 0.0)
            outs.append(_dot(p.astype(BF16), cv[h // half]))
            psum = p if psum is None else psum + p
        lane_lo = lax.broadcasted_iota(jnp.int32, (tq, LANES), 1) < HALF
        for j in range(half):
            o = jnp.where(lane_lo, outs[j], outs[half + j])
            o_ref[0, query_rows(t), j * LANES:(j + 1) * LANES] = o.astype(o_ref.dtype)

        hi = psum.astype(BF16)
        lo = (psum - hi.astype(F32)).astype(BF16)
        imp = _dot(hi, ov) + _dot(lo, ov)

        blk = lax.broadcasted_iota(jnp.int32, (n_blk, tq), 0)
        qpos = t * tq + lax.broadcasted_iota(jnp.int32, (n_blk, tq), 1)
        cur = qpos // SLC_BLOCK
        valid = blk * SLC_BLOCK <= qpos
        forced = (blk == 0) | (blk == cur) | (blk == cur - 1)
        score = jnp.where(valid & forced, FORCE_SCORE, jnp.where(valid, imp.T[HALF:], -1.0))
        score = jnp.where(blk < nsel, score, REMOVED_SCORE)
        rank_ref[...] = score
        rank = jnp.zeros((n_blk, tq), F32)
        for j in range(n_blk):
            sj = jnp.broadcast_to(rank_ref[j:j + 1, :], (n_blk, tq))
            beats = (sj > score) | ((sj == score) & (blk > j))
            rank = rank + jnp.where(beats, 1.0, 0.0)
        unselected = jnp.where(rank < min(SLC_TOPN, nsel), 0.0, 1.0)
        unselected = jnp.concatenate([jnp.zeros((HALF, tq), F32), unselected], axis=0).T.astype(BF16)
        for h in range(NSA_HPG):
            q2_ref[0, h, query_rows(t), :] = q_ref[0, h, query_rows(t), :] + unselected

    stage_scores(0, s0_ref)

    def two_tiles(j, carry):
        t = 2 * j
        stage_scores(t + 1, s1_ref)
        stage_select(t, s0_ref, rank0_ref)
        stage_scores(t + 2, s0_ref)
        stage_select(t + 1, s1_ref, rank1_ref)
        return carry

    lax.fori_loop(0, nqt // 2, two_tiles, 0)


def _cmp_attention(q, ck, cv, bias, ov, s):
    bsz = q.shape[0]
    tq = NSA_TQ
    nh = s // CMP_STRIDE
    nsel = s // SLC_BLOCK
    s_buf = pltpu.VMEM((NSA_HPG, tq, nh), F32)
    rank_buf = pltpu.VMEM((LANES - HALF, tq), F32)
    return pl.pallas_call(
        functools.partial(_cmp_attn_kernel, tq=tq, nsel=nsel),
        grid=(NSA_GROUPS, bsz),
        in_specs=[pl.BlockSpec((1, NSA_HPG, s, LANES), lambda g, b: (b, g, 0, 0), pipeline_mode=pl.Buffered(1)),
                  pl.BlockSpec((1, 1, 2 * nh, LANES), lambda g, b: (b, g, 0, 0)),
                  pl.BlockSpec((1, 1, 2, 2 * nh, LANES), lambda g, b: (b, g, 0, 0, 0)),
                  pl.BlockSpec((1, 1, NSA_HPG, tq, nh), lambda g, b: (g, 0, 0, 0, 0)),
                  pl.BlockSpec((2 * nh, LANES), lambda g, b: (0, 0))],
        out_specs=[pl.BlockSpec((1, s, NSA_HPG * NSA_DK), lambda g, b: (b, 0, g)),
                   pl.BlockSpec((1, NSA_HPG, s, LANES), lambda g, b: (b, g, 0, 0))],
        out_shape=[jax.ShapeDtypeStruct((bsz, s, NSA_HEADS * NSA_DK), BF16),
                   jax.ShapeDtypeStruct(q.shape, BF16)],
        scratch_shapes=[s_buf, s_buf, rank_buf, rank_buf],
        compiler_params=_cparams(("parallel", "parallel")),
        name="nsa_cmp_attn",
    )(q, ck, cv, bias, ov)


def _nsa_window_kernel(q_ref, k_ref, v_ref, bias_ref, o_ref, s0_ref, s1_ref, *, tq, window_steps):
    nqt = q_ref.shape[2] // tq
    assert nqt % 2 == 0
    ratio = NSA_TK // tq
    n_tiles = bias_ref.shape[1]
    half = NSA_HPG // 2

    def key_chunk(t, i):
        return t // ratio - (window_steps - 1) + i

    def key_rows(t, i):
        return pl.ds(pl.multiple_of(jnp.maximum(key_chunk(t, i), 0) * NSA_TK, NSA_TK), NSA_TK)

    def stage_scores(t, buf):
        t = jnp.minimum(t, nqt - 1)
        q = q_ref[0, :, pl.ds(pl.multiple_of(t * tq, tq), tq), :].reshape(NSA_HPG * tq, LANES)
        for i in range(window_steps):
            kc = key_chunk(t, i)
            tile = jnp.where(kc < 0, n_tiles - 1, t - ratio * kc)
            s = _dot_t(q, k_ref[0, 0, key_rows(t, i), :]).reshape(NSA_HPG, tq, NSA_TK)
            buf[:, :, i * NSA_TK:(i + 1) * NSA_TK] = s + bias_ref[0, tile]

    def stage_out(t, buf):
        accs = []
        for h in range(NSA_HPG):
            s = buf[h]
            p = jnp.exp2(s - jnp.max(s, axis=-1, keepdims=True)).astype(BF16)
            acc = None
            for i in range(window_steps):
                pv = _dot(p[:, i * NSA_TK:(i + 1) * NSA_TK], v_ref[0, 0, h // half, key_rows(t, i), :])
                acc = pv if acc is None else acc + pv
            accs.append(acc)
        rows = pl.ds(pl.multiple_of(t * tq, tq), tq)
        for j in range(half):
            o = _normalized_pair(accs[j], accs[half + j])
            o_ref[0, rows, j * LANES:(j + 1) * LANES] = o.astype(o_ref.dtype)

    stage_scores(0, s0_ref)

    def two_tiles(j, carry):
        t = 2 * j
        stage_scores(t + 1, s1_ref)
        stage_out(t, s0_ref)
        stage_scores(t + 2, s0_ref)
        stage_out(t + 1, s1_ref)
        return carry

    lax.fori_loop(0, nqt // 2, two_tiles, 0)


def _nsa_selected_kernel(q_ref, k_ref, v_ref, bias_ref, o_ref, s0_ref, s1_ref, m_ref, acc_ref, *, tq):
    nqt = q_ref.shape[2] // tq
    ratio = NSA_TK // tq
    n_tiles = bias_ref.shape[1]
    half = NSA_HPG // 2
    chunk = SEL_TILES * NSA_TK
    per = chunk // tq

    def key_rows(kc):
        return pl.ds(pl.multiple_of(kc * chunk, chunk), chunk)

    def scores(t, kc, valid, buf):
        q = q_ref[0, :, pl.ds(pl.multiple_of(t * tq, tq), tq), :].reshape(NSA_HPG * tq, LANES)
        s = _dot_t(q, k_ref[0, 0, key_rows(kc), :]).reshape(NSA_HPG, tq, chunk)
        for i in range(SEL_TILES):
            offset = t - ratio * (SEL_TILES * kc + i)
            tile = jnp.where(valid & (offset >= 0), jnp.minimum(offset, n_tiles - 2), n_tiles - 1)
            cols = slice(i * NSA_TK, (i + 1) * NSA_TK)
            buf[:, :, cols] = s[:, :, cols] + bias_ref[0, tile]

    def values(h, kc):
        return v_ref[0, 0, h // half, key_rows(kc), :]

    def write_out(t, accs):
        rows = pl.ds(pl.multiple_of(t * tq, tq), tq)
        for j in range(half):
            o = _normalized_pair(accs[j], accs[half + j])
            o_ref[0, rows, j * LANES:(j + 1) * LANES] = o.astype(o_ref.dtype)

    _causal_item_stream(nqt, lambda t: t // per, NSA_HPG, scores, values, write_out,
                        (s0_ref, s1_ref), m_ref, acc_ref)


def _nsa_selected(q, k, v, bias, s):
    bsz = q.shape[0]
    tq = NSA_TQ
    n_off = bias.shape[1]
    assert s % (SEL_TILES * NSA_TK) == 0
    s_buf = pltpu.VMEM((NSA_HPG, tq, SEL_TILES * NSA_TK), F32)
    once = pl.Buffered(1)
    return pl.pallas_call(
        functools.partial(_nsa_selected_kernel, tq=tq),
        grid=(NSA_GROUPS, bsz),
        in_specs=[pl.BlockSpec((1, NSA_HPG, s, LANES), lambda g, b: (b, g, 0, 0), pipeline_mode=once),
                  pl.BlockSpec((1, 1, s, LANES), lambda g, b: (b, g, 0, 0)),
                  pl.BlockSpec((1, 1, 2, s, LANES), lambda g, b: (b, g, 0, 0, 0)),
                  pl.BlockSpec((1, n_off, NSA_HPG, tq, NSA_TK), lambda g, b: (g, 0, 0, 0, 0), pipeline_mode=once)],
        out_specs=pl.BlockSpec((1, s, NSA_HPG * NSA_DK), lambda g, b: (b, 0, g)),
        out_shape=jax.ShapeDtypeStruct((bsz, s, NSA_HEADS * NSA_DK), BF16),
        scratch_shapes=[s_buf, s_buf, pltpu.VMEM((NSA_HPG, tq, LANES), F32), pltpu.VMEM((NSA_HPG, tq, LANES), F32)],
        compiler_params=_cparams(("parallel", "parallel")),
        name="nsa_flash_sel",
    )(q, k, v, bias)


def _nsa_window(q, k, v, bias, s, window_steps):
    bsz = q.shape[0]
    tq = NSA_TQ
    n_off = bias.shape[1]
    s_buf = pltpu.VMEM((NSA_HPG, tq, window_steps * NSA_TK), F32)
    once = pl.Buffered(1)
    return pl.pallas_call(
        functools.partial(_nsa_window_kernel, tq=tq, window_steps=window_steps),
        grid=(NSA_GROUPS, bsz),
        in_specs=[pl.BlockSpec((1, NSA_HPG, s, LANES), lambda g, b: (b, g, 0, 0), pipeline_mode=once),
                  pl.BlockSpec((1, 1, s, LANES), lambda g, b: (b, g, 0, 0)),
                  pl.BlockSpec((1, 1, 2, s, LANES), lambda g, b: (b, g, 0, 0, 0)),
                  pl.BlockSpec((1, n_off, NSA_HPG, tq, NSA_TK), lambda g, b: (g, 0, 0, 0, 0), pipeline_mode=once)],
        out_specs=pl.BlockSpec((1, s, NSA_HPG * NSA_DK), lambda g, b: (b, 0, g)),
        out_shape=jax.ShapeDtypeStruct((bsz, s, NSA_HEADS * NSA_DK), BF16),
        scratch_shapes=[s_buf, s_buf],
        compiler_params=_cparams(("parallel", "parallel")),
        name="nsa_flash_win",
    )(q, k, v, bias)


def _hybrid_layer(h, bsz, s, layer, g_mix, g_mlp, w1, w2, w_in, conv_w, conv_b, wa, ba, wx, bx, lam,
                  q_norm, kv_norm, w_uq, w_ukv, qn_q, qn_k, w_out):
    d = h.shape[1]
    c = LRU_WIDTH
    o_kv = 2 * c + MLA_Q_RANK
    o_kr = o_kv + MLA_KV_RANK
    w_kr = jnp.pad(w_in[:, o_kr:], ((0, 0), (MLA_NOPE, LANES - MLA_QK)))
    w_all = jnp.concatenate([w_in[:, :o_kv], w_kr, w_in[:, o_kv:o_kr]], axis=1).astype(BF16)
    z_lru, q_lat, k_rope, kv_lat = _norm_proj(h, g_mix, w_all,
                                              ((2 * c,), (MLA_Q_RANK, LANES), (MLA_KV_RANK,)))
    lru = _rg_lru(z_lru.reshape(bsz, s, 2 * c), conv_w, conv_b, wa, ba, wx, bx, lam)
    q, k, v = _mla_prep(q_lat, kv_lat, k_rope, bsz, s, q_norm, kv_norm, w_uq, w_ukv, qn_q, qn_k)
    mla = _mla_attention(q, k, v)
    wo = w_out.astype(BF16)
    return _mix_mlp(h, [[lru.reshape(bsz * s, c)], [mla.reshape(bsz * s, -1)]], [wo[:c], wo[c:]],
                    g_mlp, w1, w2, layer)


def _nsa_layer(h, bsz, s, layer, g_mix, g_mlp, w1, w2, w_in, pos_k, w1_k, w2_k, pos_v, w1_v, w2_v,
               q_norm, k_norm, rel_bias, w_out):
    nq = NSA_HEADS * NSA_DK
    kvw = NSA_GROUPS * NSA_DK
    w_gate = _pad_cols(w_in[:, nq + 6 * kvw:], LANES)
    w_all = jnp.concatenate([w_in[:, :nq + 6 * kvw], w_gate], axis=1).astype(BF16)
    z_q, z_kc, z_vc, z_kv, z_gate = _norm_proj(h, g_mix, w_all, ((nq,), (kvw, kvw), (4 * kvw,), (LANES,)),
                                               folds=(1, CMP_STRIDE, CMP_STRIDE, 1, 1))

    q, ks, kw, vs, vw = _nsa_prep(z_q, z_kv, bsz, s, q_norm, k_norm)
    ck, cv = _compress(z_kc, z_vc, bsz, s, pos_k, w1_k, w2_k, pos_v, w1_v, w2_v, k_norm[0])

    order = np.array(HEAD_ORDER)
    expand = np.zeros((3, LANES, NSA_HEADS * NSA_DK), np.float32)
    for br in range(3):
        for hd in range(NSA_HEADS):
            expand[br, 3 * hd + br, hd * NSA_DK:(hd + 1) * NSA_DK] = 1.0

    slot_heads = (np.arange(NSA_GROUPS)[:, None] * NSA_HPG + order[None, :]).reshape(-1)
    tab_slots = rel_bias.T[slot_heads]
    nh = s // CMP_STRIDE
    nc = (s - CMP_BLOCK) // CMP_STRIDE + 1
    cmp_bias = _bias_tiles(tab_slots, _cmp_bias_ids(s))
    sel_bias = _bias_tiles(tab_slots, _toeplitz_bucket_ids(None)[0])
    win_ids, n_win_tiles = _toeplitz_bucket_ids(WINDOW)
    win_bias = _bias_tiles(tab_slots, win_ids)

    nsel = s // SLC_BLOCK
    cstart = np.arange(nh) * CMP_STRIDE
    sstart = np.arange(nsel) * SLC_BLOCK
    ov = np.clip(np.minimum(cstart[:, None] + CMP_BLOCK, sstart[None, :] + SLC_BLOCK)
                 - np.maximum(cstart[:, None], sstart[None, :]), 0, None).astype(np.float32) / CMP_BLOCK
    ov[nc:] = 0.0
    ov_p = np.zeros((nh, LANES), np.float32)
    ov_p[:, HALF:HALF + nsel] = ov

    o_c, q_sel = _cmp_attention(q, ck, cv, cmp_bias, jnp.asarray(np.concatenate([ov_p, ov_p])), s)
    o_s = _nsa_selected(q_sel, ks, vs, sel_bias, s)
    window_steps = (n_win_tiles + 1) // (NSA_TK // NSA_TQ)
    o_w = _nsa_window(q, kw, vw, win_bias, s, window_steps)

    flat = lambda a: a.reshape(bsz * s, -1)
    return _mix_mlp(h, [[flat(o_c), flat(o_s), flat(o_w)]], [w_out.astype(BF16)],
                    g_mlp, w1, w2, layer,
                    gate=(z_gate, jnp.asarray(np.concatenate([expand, expand], axis=1), BF16)))


def kernel(x, rel_bias, norm_mix, norm_mlp, mlp_w1, mlp_w2, hy_w_in, lru_conv_w, lru_conv_b, lru_wa, lru_ba, lru_wx, lru_bx, lru_lambda, mla_q_norm, mla_kv_norm, mla_w_uq, mla_w_ukv, mla_qn_q, mla_qn_k, hy_w_out, nsa_w_in, nsa_cmp_pos_k, nsa_cmp_w1_k, nsa_cmp_w2_k, nsa_cmp_pos_v, nsa_cmp_w1_v, nsa_cmp_w2_v, nsa_q_norm, nsa_k_norm, nsa_w_out):
    bsz, s, d = x.shape
    depth = norm_mix.shape[0]
    h = x.reshape(bsz * s, d)
    w1_all, w2_all = mlp_w1.astype(BF16), mlp_w2.astype(BF16)
    for layer in range(depth):
        if layer % 2 == 0:
            e = layer // 2
            h = _hybrid_layer(h, bsz, s, layer, norm_mix[layer], norm_mlp[layer], w1_all, w2_all,
                              hy_w_in[e], lru_conv_w[e], lru_conv_b[e], lru_wa[e], lru_ba[e], lru_wx[e],
                              lru_bx[e], lru_lambda[e], mla_q_norm[e], mla_kv_norm[e], mla_w_uq[e],
                              mla_w_ukv[e], mla_qn_q[e], mla_qn_k[e], hy_w_out[e])
        else:
            o = layer // 2
            h = _nsa_layer(h, bsz, s, layer, norm_mix[layer], norm_mlp[layer], w1_all, w2_all,
                           nsa_w_in[o], nsa_cmp_pos_k[o], nsa_cmp_w1_k[o], nsa_cmp_w2_k[o],
                           nsa_cmp_pos_v[o], nsa_cmp_w1_v[o], nsa_cmp_w2_v[o], nsa_q_norm[o],
                           nsa_k_norm[o], rel_bias, nsa_w_out[o])
    return h.reshape(bsz, s, d)
```

```python
import functools
import math

import numpy as np
import jax
import jax.numpy as jnp
from jax import lax
from jax.experimental import pallas as pl
from jax.experimental.pallas import tpu as pltpu

F32 = jnp.float32
BF16 = jnp.bfloat16

NORM_EPS = 1e-6
MASK_VALUE = -1e30
FORCE_SCORE = 1e4
REMOVED_SCORE = -3e38

NUM_BUCKETS = 32
MAX_DISTANCE = 1024
MASKED_ID = NUM_BUCKETS
EXCLUDED_ID = NUM_BUCKETS + 1
LOG2E = math.log2(math.e)

LRU_WIDTH = 512
LRU_BLOCKS = 8
CONV_WIDTH = 4
LRU_C = 8.0

MLA_HEADS = 8
MLA_NOPE = 64
MLA_ROPE = 32
MLA_QK = MLA_NOPE + MLA_ROPE
MLA_V = 64
MLA_Q_RANK = 384
MLA_KV_RANK = 256
ROPE_THETA = 10000.0

NSA_HEADS = 16
NSA_GROUPS = 2
NSA_HPG = NSA_HEADS // NSA_GROUPS
NSA_DK = 64
CMP_BLOCK = 32
CMP_STRIDE = 16
CMP_HIDDEN = 256
SLC_BLOCK = 64
SLC_TOPN = 8
WINDOW = 512

LANES = 128
SUBLANES = 8
HALF = LANES // 2
NSA_TQ = 128
NSA_TK = 256
SEL_TILES = 2
MLA_TR = 128
MLA_TK = 512
HEAD_ORDER = (0, 2, 4, 6, 1, 3, 5, 7)
V7X_VMEM_BYTES = 64 * 1024 * 1024
VMEM_LIMIT = V7X_VMEM_BYTES * 7 // 8


def _cparams(sem):
    return pltpu.CompilerParams(dimension_semantics=sem, vmem_limit_bytes=VMEM_LIMIT)


def _rms(x, g, n=None):
    n = x.shape[-1] if n is None else n
    ms = jnp.sum(x * x, axis=-1, keepdims=True) * (1.0 / n)
    return x * lax.rsqrt(ms + NORM_EPS) * g


def _rms_lanes(x, g, n):
    y = x * x
    hi = y.astype(BF16)
    parts = jnp.concatenate([hi, (y - hi.astype(F32)).astype(BF16)], axis=1)
    ss = _dot(parts, jnp.ones((2 * LANES, LANES), BF16))
    return x * lax.rsqrt(ss * (1.0 / n) + NORM_EPS) * g


def _rms_half_lanes(x, g):
    y = x * x
    hi = y.astype(BF16)
    parts = jnp.concatenate([hi, (y - hi.astype(F32)).astype(BF16)], axis=1)
    row_half = lax.broadcasted_iota(jnp.int32, (2 * LANES, LANES), 0) % LANES // HALF
    col_half = lax.broadcasted_iota(jnp.int32, (2 * LANES, LANES), 1) // HALF
    ss = _dot(parts, jnp.where(row_half == col_half, 1.0, 0.0).astype(BF16))
    return x * lax.rsqrt(ss * (1.0 / HALF) + NORM_EPS) * g


def _gelu_tanh(x):
    return 0.5 * x * (1.0 + jnp.tanh(math.sqrt(2.0 / math.pi) * (x + 0.044715 * (x * x * x))))


def _sigmoid(x):
    return 1.0 / (1.0 + jnp.exp(-x))


def _dot(a, b):
    return jnp.dot(a, b, preferred_element_type=F32)


def _dot_t(a, b):
    return lax.dot_general(a, b, (((1,), (1,)), ((), ())), preferred_element_type=F32)


def _causal_item_stream(n_tiles, last_chunk, n_chains, scores, values, write_out, s_bufs, m_ref, acc_ref):
    m_ref[...] = jnp.full_like(m_ref, MASK_VALUE)
    acc_ref[...] = jnp.zeros_like(acc_ref)
    n_items = sum(last_chunk(t) + 1 for t in range(n_tiles))

    def advance(t, kc):
        wrap = kc >= last_chunk(t)
        return jnp.where(wrap, t + 1, t), jnp.where(wrap, 0, kc + 1)

    def stage_scores(t, kc, buf):
        valid = t < n_tiles
        scores(jnp.minimum(t, n_tiles - 1), jnp.where(valid, kc, 0), valid, buf)

    def stage_update(t, kc, buf, maybe_padding):
        kc = jnp.where(t < n_tiles, kc, 0) if maybe_padding else kc
        tile_start = kc == 0
        accs = []
        for c in range(n_chains):
            m_old = jnp.where(tile_start, MASK_VALUE, m_ref[c])
            m_new = jnp.maximum(m_old, jnp.max(buf[c], axis=-1, keepdims=True))
            alpha = jnp.exp2(m_old - m_new)
            s = buf[c]
            p = jnp.exp2(s - jnp.concatenate([m_new] * (s.shape[1] // LANES), axis=1))
            acc = acc_ref[c] * alpha + _dot(p.astype(BF16), values(c, kc))
            m_ref[c] = m_new
            acc_ref[c] = acc
            accs.append(acc)
        if maybe_padding:
            pl.when(t < n_tiles)(lambda: write_out(t, accs))
        else:
            write_out(t, accs)

    start = (jnp.int32(0), jnp.int32(0))
    stage_scores(*start, s_bufs[0])

    def two_items(_, item):
        item1 = advance(*item)
        stage_scores(*item1, s_bufs[1])
        stage_update(*item, s_bufs[0], False)
        item2 = advance(*item1)
        stage_scores(*item2, s_bufs[0])
        stage_update(*item1, s_bufs[1], n_items % 2 == 1)
        return item2

    lax.fori_loop(0, (n_items + 1) // 2, two_items, start)


def _normalized_pair(acc_even, acc_odd):
    lo = lax.broadcasted_iota(jnp.int32, acc_even.shape, 1) < HALF
    return jnp.where(lo, acc_even / acc_even[:, LANES - 1:LANES], acc_odd / acc_odd[:, 0:1])


def _with_ones_column(v, odd):
    lane = lax.broadcasted_iota(jnp.int32, v.shape, 1)
    return jnp.where(lane == (0 if odd else LANES - 1), 1.0, v)


def _norm_proj_kernel(x_ref, g_ref, w_ref, *refs, groups, folds):
    n_out = len(folds)
    out_refs, fold_refs = refs[:n_out], list(refs[n_out:])
    xn = _rms(x_ref[...], g_ref[...]).astype(BF16)
    off = 0
    k = 0
    for group in groups:
        width = sum(group)
        z = _dot(xn, w_ref[:, off:off + width])
        off += width
        col = 0
        for n in group:
            piece = z[:, col:col + n]
            col += n
            if folds[k] == 1:
                out_refs[k][...] = piece
            else:
                scratch = fold_refs.pop(0)
                scratch[...] = piece
                rows = piece.shape[0] // folds[k]
                for j in range(folds[k]):
                    out_refs[k][:, j * n:(j + 1) * n] = scratch[pl.ds(j, rows, stride=folds[k]), :]
            k += 1


def _norm_proj(x, g, w, groups, folds=None, tm=512):
    t, d = x.shape
    n = w.shape[1]
    splits = [s for group in groups for s in group]
    folds = tuple(folds) if folds is not None else (1,) * len(splits)
    assert sum(splits) == n and t % tm == 0
    return pl.pallas_call(
        functools.partial(_norm_proj_kernel, groups=groups, folds=folds),
        grid=(t // tm,),
        in_specs=[pl.BlockSpec((tm, d), lambda i: (i, 0)),
                  pl.BlockSpec((1, d), lambda i: (0, 0)),
                  pl.BlockSpec((d, n), lambda i: (0, 0))],
        out_specs=[pl.BlockSpec((tm // f, s * f), lambda i: (i, 0)) for s, f in zip(splits, folds)],
        out_shape=[jax.ShapeDtypeStruct((t // f, s * f), F32) for s, f in zip(splits, folds)],
        scratch_shapes=[pltpu.VMEM((tm, s), F32) for s, f in zip(splits, folds) if f > 1],
        compiler_params=_cparams(("parallel",)),
        name="norm_proj",
    )(x, g.reshape(1, d), w)


def _mix_mlp_kernel(*refs, group_sizes, gated):
    n_a = sum(group_sizes)
    n_g = len(group_sizes)
    h_ref = refs[0]
    a_refs = refs[1:1 + n_a]
    wo_refs = refs[1 + n_a:1 + n_a + n_g]
    rest = refs[1 + n_a + n_g:]
    if gated:
        zg_ref, e_ref = rest[:2]
        rest = rest[2:]
    g_ref, w1_ref, w2_ref, out_ref, hres, xn, acc = rest
    f = pl.program_id(1)

    @pl.when(f == 0)
    def _():
        h1 = h_ref[...]
        if gated:
            gate = _sigmoid(zg_ref[...])
            g_hi = gate.astype(BF16)
            g_parts = jnp.concatenate([g_hi, (gate - g_hi.astype(F32)).astype(BF16)], axis=1)
        k = 0
        for gi, gs in enumerate(group_sizes):
            a = None
            for j in range(gs):
                aj = a_refs[k + j][...].astype(F32)
                if gated:
                    aj = aj * _dot(g_parts, e_ref[j])
                a = aj if a is None else a + aj
            k += gs
            h1 = h1 + _dot(a.astype(BF16), wo_refs[gi][...])
        hres[...] = h1
        xn[...] = _rms(h1, g_ref[...]).astype(BF16)
        acc[...] = jnp.zeros_like(acc)

    u = jnp.maximum(_dot(xn[...], w1_ref[...]), 0.0)
    acc[...] += _dot((u * u).astype(BF16), w2_ref[...])

    @pl.when(f == pl.num_programs(1) - 1)
    def _():
        out_ref[...] = hres[...] + acc[...]


def _mix_mlp(h, groups, wos, g, w1, w2, layer, gate=None, tm=512, tf=2048):
    t, d = h.shape
    ff = w1.shape[2]
    group_sizes = tuple(len(gr) for gr in groups)
    a_list = [a for gr in groups for a in gr]
    extra = [] if gate is None else list(gate)
    in_specs = [pl.BlockSpec((tm, d), lambda i, f: (i, 0))]
    in_specs += [pl.BlockSpec((tm, a.shape[1]), lambda i, f: (i, 0)) for a in a_list]
    in_specs += [pl.BlockSpec(w.shape, lambda i, f: (0, 0)) for w in wos]
    if gate is not None:
        in_specs += [pl.BlockSpec((tm, LANES), lambda i, f: (i, 0)),
                     pl.BlockSpec(gate[1].shape, lambda i, f: (0, 0, 0))]
    in_specs += [pl.BlockSpec((1, d), lambda i, f: (0, 0)),
                 pl.BlockSpec((None, d, tf), lambda i, f: (layer, 0, f)),
                 pl.BlockSpec((None, tf, d), lambda i, f: (layer, f, 0))]
    return pl.pallas_call(
        functools.partial(_mix_mlp_kernel, group_sizes=group_sizes, gated=gate is not None),
        grid=(t // tm, ff // tf),
        in_specs=in_specs,
        out_specs=pl.BlockSpec((tm, d), lambda i, f: (i, 0)),
        out_shape=jax.ShapeDtypeStruct((t, d), F32),
        scratch_shapes=[pltpu.VMEM((tm, d), F32), pltpu.VMEM((tm, d), BF16),
                        pltpu.VMEM((tm, d), F32)],
        compiler_params=_cparams(("parallel", "arbitrary")),
        name="mix_mlp",
    )(h, *a_list, *wos, *extra, g.reshape(1, d), w1, w2)


def _lru_kernel(xr_ref, xg_ref, cw_ref, cb_ref, wa_ref, ba_ref, wx_ref, bx_ref, lam_ref,
                o_ref, tail_ref, hc_ref, *, tc):
    c = pl.program_id(1)

    @pl.when(c == 0)
    def _():
        tail_ref[...] = jnp.zeros_like(tail_ref)
        hc_ref[...] = jnp.zeros_like(hc_ref)

    x = xr_ref[0]
    width = x.shape[1]
    row = lax.broadcasted_iota(jnp.int32, (tc, width), 0)
    tail_row = lax.broadcasted_iota(jnp.int32, (SUBLANES, width), 0)
    tail = tail_ref[...]
    cw = cw_ref[...]
    y = x * cw[CONV_WIDTH - 1:CONV_WIDTH]
    for s in range(1, CONV_WIDTH):
        sh = pltpu.roll(x, s, 0)
        head = jnp.where(tail_row < s, pltpu.roll(tail, s, 0), sh[0:SUBLANES])
        sh = jnp.concatenate([head, sh[SUBLANES:]], axis=0)
        y = y + sh * cw[CONV_WIDTH - 1 - s:CONV_WIDTH - s]
    y = y + cb_ref[...]
    tail_ref[...] = x[tc - SUBLANES:tc]

    yb = y.astype(BF16)
    r = _sigmoid(_dot(yb, wa_ref[...]) + ba_ref[...])
    i = _sigmoid(_dot(yb, wx_ref[...]) + bx_ref[...])
    nl = -lam_ref[...]
    softplus = jnp.maximum(nl, 0.0) + jnp.log(1.0 + jnp.exp(-jnp.abs(nl)))
    log_a = (-LRU_C) * r * softplus
    a = jnp.exp(log_a)
    mult = jnp.sqrt(1.0 - jnp.exp(2.0 * log_a))
    mult = jnp.where((row == 0) & (c == 0), 1.0, mult)
    b = mult * (i * y)

    d = 1
    while d < tc:
        keep = row >= d
        a_sh = jnp.where(keep, pltpu.roll(a, d, 0), 1.0)
        b_sh = jnp.where(keep, pltpu.roll(b, d, 0), 0.0)
        b = a * b_sh + b
        a = a * a_sh
        d *= 2
    h = b + a * hc_ref[...]
    hc_ref[...] = h[tc - 1:tc]
    o_ref[0] = (h * _gelu_tanh(xg_ref[0])).astype(o_ref.dtype)


def _block_diag(w):
    n, d, e = w.shape
    eye = jnp.eye(n, dtype=w.dtype)
    return (w[:, :, None, :] * eye[:, None, :, None]).reshape(n * d, n * e)


def _rg_lru(z_lru, conv_w, conv_b, wa, ba, wx, bx, lam, tc=256):
    bsz, s, _ = z_lru.shape
    c = LRU_WIDTH
    tc = min(tc, s)
    vec = lambda v: pl.BlockSpec(v, lambda b, i: (0, 0))
    return pl.pallas_call(
        functools.partial(_lru_kernel, tc=tc),
        grid=(bsz, s // tc),
        in_specs=[pl.BlockSpec((1, tc, c), lambda b, i: (b, i, 0)),
                  pl.BlockSpec((1, tc, c), lambda b, i: (b, i, 1)),
                  vec((CONV_WIDTH, c)), vec((1, c)), vec((c, c)), vec((1, c)),
                  vec((c, c)), vec((1, c)), vec((1, c))],
        out_specs=pl.BlockSpec((1, tc, c), lambda b, i: (b, i, 0)),
        out_shape=jax.ShapeDtypeStruct((bsz, s, c), BF16),
        scratch_shapes=[pltpu.VMEM((SUBLANES, c), F32), pltpu.VMEM((1, c), F32)],
        compiler_params=_cparams(("parallel", "arbitrary")),
        name="rg_lru",
    )(z_lru, z_lru, conv_w, conv_b.reshape(1, c), _block_diag(wa).astype(BF16), ba.reshape(1, c),
      _block_diag(wx).astype(BF16), bx.reshape(1, c), lam.reshape(1, c))


def _rope(x, cos, sin_lo, sin_hi):
    return (x * cos + pltpu.roll(x, LANES - MLA_ROPE // 2, 1) * sin_lo
            + pltpu.roll(x, MLA_ROPE // 2, 1) * sin_hi)


def _mla_prep_kernel(ql_ref, kvl_ref, kr_ref, gq_ref, gkv_ref, wq_ref, wk_ref, wv_ref,
                     nq_ref, nk_ref, cos_ref, s1_ref, s2_ref, q_out, k_out, v_out):
    qn = _rms(ql_ref[...], gq_ref[...]).astype(BF16)
    kvn = _rms(kvl_ref[...], gkv_ref[...]).astype(BF16)
    kr = kr_ref[...]
    cos, s1, s2 = cos_ref[...], s1_ref[...], s2_ref[...]
    scale = MLA_QK ** -0.5 * LOG2E
    for h in range(MLA_HEADS):
        sl = slice(h * LANES, (h + 1) * LANES)
        qh = _rms_lanes(_dot(qn, wq_ref[:, sl]), nq_ref[...], MLA_QK)
        q_out[0, h] = (_rope(qh, cos, s1, s2) * scale).astype(BF16)
        kh = _rms_lanes(_dot(kvn, wk_ref[:, sl]) + kr, nk_ref[...], MLA_QK)
        k_out[0, h] = _rope(kh, cos, s1, s2).astype(BF16)
        v_out[0, h] = _with_ones_column(_dot(kvn, wv_ref[:, sl]), h % 2 == 1).astype(BF16)


def _rope_tables(s):
    half = MLA_ROPE // 2
    freqs = ROPE_THETA ** (-jnp.arange(half, dtype=F32) / half)
    ang = jnp.arange(s, dtype=F32)[:, None] * freqs[None, :]
    cos, sin = jnp.cos(ang), jnp.sin(ang)
    z = lambda n: jnp.zeros((s, n), F32)
    cos_t = jnp.concatenate([jnp.ones((s, MLA_NOPE), F32), cos, cos, z(LANES - MLA_QK)], axis=1)
    s_lo = jnp.concatenate([z(MLA_NOPE), -sin, z(LANES - MLA_NOPE - half)], axis=1)
    s_hi = jnp.concatenate([z(MLA_NOPE + half), sin, z(LANES - MLA_QK)], axis=1)
    return cos_t, s_lo, s_hi


def _pad_cols(w, n):
    return jnp.pad(w, ((0, 0),) * (w.ndim - 1) + ((0, n - w.shape[-1]),))


def _mla_prep(q_lat, kv_lat, k_rope, bsz, s, q_norm, kv_norm, w_uq, w_ukv, qn_q, qn_k, tm=512):
    tm = min(tm, s)
    nt = s // tm
    hl = MLA_HEADS * LANES
    wq = _pad_cols(w_uq.reshape(MLA_Q_RANK, MLA_HEADS, MLA_QK), LANES).reshape(MLA_Q_RANK, hl)
    wkv = w_ukv.reshape(MLA_KV_RANK, MLA_HEADS, MLA_NOPE + MLA_V)
    wk = _pad_cols(wkv[..., :MLA_NOPE], LANES).reshape(MLA_KV_RANK, hl)
    wv = wkv[..., MLA_NOPE:]
    zero = jnp.zeros_like(wv)
    even = (jnp.arange(MLA_HEADS) % 2 == 0)[None, :, None]
    wv = jnp.concatenate([jnp.where(even, wv, zero), jnp.where(even, zero, wv)], axis=-1)
    wv = wv.reshape(MLA_KV_RANK, hl)
    cos_t, s_lo, s_hi = _rope_tables(s)
    row = lambda n: pl.BlockSpec((tm, n), lambda b, i: (b * nt + i, 0))
    full = lambda a: pl.BlockSpec(a.shape, lambda b, i: (0, 0))
    tab = pl.BlockSpec((tm, LANES), lambda b, i: (i, 0))
    head_out = pl.BlockSpec((1, MLA_HEADS, tm, LANES), lambda b, i: (b, 0, i, 0))
    args = (q_lat, kv_lat, k_rope, q_norm.reshape(1, -1), kv_norm.reshape(1, -1),
            wq.astype(BF16), wk.astype(BF16), wv.astype(BF16),
            _pad_cols(qn_q.reshape(1, -1), LANES), _pad_cols(qn_k.reshape(1, -1), LANES),
            cos_t, s_lo, s_hi)
    in_specs = [row(MLA_Q_RANK), row(MLA_KV_RANK), row(LANES)] + [full(a) for a in args[3:10]] + [tab] * 3
    shp = jax.ShapeDtypeStruct((bsz, MLA_HEADS, s, LANES), BF16)
    return pl.pallas_call(
        _mla_prep_kernel,
        grid=(bsz, nt),
        in_specs=in_specs,
        out_specs=[head_out] * 3,
        out_shape=[shp] * 3,
        compiler_params=_cparams(("parallel", "parallel")),
        name="mla_prep",
    )(*args)


def _mla_attn_kernel(q_ref, k_ref, v_ref, mask_ref, o_ref, s0_ref, s1_ref, m_ref, acc_ref, *, tq):
    n_rc = tq // MLA_TR
    ratio = MLA_TK // MLA_TR
    n_tiles = mask_ref.shape[0]

    def key_rows(kc):
        return pl.ds(pl.multiple_of(kc * MLA_TK, MLA_TK), MLA_TK)

    def scores(qi, kc, valid, buf):
        rows = pl.ds(pl.multiple_of(qi * tq, tq), tq)
        for j in range(2):
            s = _dot_t(q_ref[0, j, rows, :], k_ref[0, j, key_rows(kc), :])
            for rc in range(n_rc):
                offset = qi * n_rc + rc - ratio * kc
                tile = jnp.where(valid, jnp.clip(offset + 1, 0, n_tiles - 1), 0)
                buf[j * n_rc + rc] = s[rc * MLA_TR:(rc + 1) * MLA_TR] + mask_ref[tile]

    def values(c, kc):
        return v_ref[0, c // n_rc, key_rows(kc), :]

    def write_out(qi, accs):
        for rc in range(n_rc):
            rows = pl.ds(pl.multiple_of(qi * tq + rc * MLA_TR, MLA_TR), MLA_TR)
            o_ref[0, rows, :] = _normalized_pair(accs[rc], accs[n_rc + rc]).astype(o_ref.dtype)

    _causal_item_stream(q_ref.shape[2] // tq, lambda qi: (qi * tq + tq - 1) // MLA_TK, 2 * n_rc,
                        scores, values, write_out, (s0_ref, s1_ref), m_ref, acc_ref)


def _mla_causal_tiles():
    i = np.arange(MLA_TR)[:, None]
    j = np.arange(MLA_TK)[None, :]
    tiles = [np.full((MLA_TR, MLA_TK), -np.inf, np.float32)]
    for d in range(MLA_TK // MLA_TR):
        tiles.append(np.where(j <= i + d * MLA_TR, 0.0, -np.inf).astype(np.float32))
    tiles.append(np.zeros((MLA_TR, MLA_TK), np.float32))
    return np.stack(tiles)


def _mla_attention(q, k, v, tq=512):
    bsz, nh, s, _ = q.shape
    tq = min(tq, s)
    assert tq % MLA_TK == 0 and tq % MLA_TR == 0
    n_chains = 2 * (tq // MLA_TR)
    mask = _mla_causal_tiles()
    seq_spec = pl.BlockSpec((1, 2, s, LANES), lambda b, hp: (b, hp, 0, 0))
    return pl.pallas_call(
        functools.partial(_mla_attn_kernel, tq=tq),
        grid=(bsz, nh // 2),
        in_specs=[seq_spec, seq_spec, seq_spec, pl.BlockSpec(mask.shape, lambda b, hp: (0, 0, 0))],
        out_specs=pl.BlockSpec((1, s, LANES), lambda b, hp: (b, 0, hp)),
        out_shape=jax.ShapeDtypeStruct((bsz, s, nh * MLA_V), BF16),
        scratch_shapes=[pltpu.VMEM((n_chains, MLA_TR, MLA_TK), F32), pltpu.VMEM((n_chains, MLA_TR, MLA_TK), F32),
                        pltpu.VMEM((n_chains, MLA_TR, LANES), F32), pltpu.VMEM((n_chains, MLA_TR, LANES), F32)],
        compiler_params=_cparams(("parallel", "parallel")),
        name="mla_attn",
    )(q, k, v, jnp.asarray(mask))


def _t5_bucket_np(dist):
    n = np.maximum(dist, 0)
    max_exact = NUM_BUCKETS // 2
    nf = np.maximum(n, 1).astype(np.float32)
    large = max_exact + (np.log(nf / np.float32(max_exact)) / np.float32(math.log(MAX_DISTANCE / max_exact))
                         * np.float32(NUM_BUCKETS - max_exact)).astype(np.int32)
    large = np.minimum(large, NUM_BUCKETS - 1)
    return np.where(n < max_exact, n, large).astype(np.int32)


def _bias_tiles_kernel(tab_ref, idx_ref, o_ref, *, buckets):
    slot = pl.program_id(0) * NSA_HPG + pl.program_id(1)
    for i, present in enumerate(buckets):
        idx = idx_ref[i]
        acc = jnp.where(idx == EXCLUDED_ID, -jnp.inf, MASK_VALUE).astype(F32)
        for b in present:
            acc = jnp.where(idx == b, tab_ref[slot, b] * LOG2E, acc)
        o_ref[0, i, 0] = acc


def _bias_tiles(tab_slots, idx):
    n, r, c = idx.shape
    buckets = tuple(tuple(int(b) for b in np.unique(tile) if b < NUM_BUCKETS) for tile in idx)
    return pl.pallas_call(
        functools.partial(_bias_tiles_kernel, buckets=buckets),
        grid=(NSA_GROUPS, NSA_HPG),
        in_specs=[pl.BlockSpec(memory_space=pltpu.SMEM),
                  pl.BlockSpec((n, r, c), lambda g, p: (0, 0, 0))],
        out_specs=pl.BlockSpec((1, n, 1, r, c), lambda g, p: (g, 0, p, 0, 0)),
        out_shape=jax.ShapeDtypeStruct((NSA_GROUPS, n, NSA_HPG, r, c), F32),
        compiler_params=_cparams(("parallel", "parallel")),
        name="bias_tiles",
    )(tab_slots, jnp.asarray(idx))


def _cmp_bias_ids(s):
    nqt = s // NSA_TQ
    nh = s // CMP_STRIDE
    per_tile = NSA_TQ // CMP_STRIDE
    i = np.arange(NSA_TQ)[:, None]
    c = np.arange(nh)[None, :]
    dist = i - (CMP_BLOCK - 1) - CMP_STRIDE * (c - per_tile * (nqt - 1))
    ids = np.where(dist >= 0, _t5_bucket_np(dist), MASKED_ID).astype(np.int32)
    return ids[None]


def _toeplitz_bucket_ids(window):
    i = np.arange(NSA_TQ)[:, None]
    j = np.arange(NSA_TK)[None, :]
    tiles = []
    d = 0
    while True:
        dist = d * NSA_TQ + i - j
        ids = _t5_bucket_np(dist)
        ok = dist >= 0
        if window is not None:
            ok &= dist < window
        ids = np.where(ok, ids, MASKED_ID).astype(np.int32)
        tiles.append(ids)
        if window is None and (ids == NUM_BUCKETS - 1).all():
            break
        if window is not None and not ok.any():
            tiles.pop()
            break
        d += 1
    n_real = len(tiles)
    tiles.append(np.full((NSA_TQ, NSA_TK), EXCLUDED_ID, np.int32))
    return np.stack(tiles), n_real


def _split_halves(x):
    lo = lax.broadcasted_iota(jnp.int32, x.shape, 1) < HALF
    return jnp.where(lo, x, 0.0), pltpu.roll(jnp.where(lo, 0.0, x), HALF, 1)


def _value_layouts(x):
    lo = lax.broadcasted_iota(jnp.int32, x.shape, 1) < HALF
    g0e = jnp.where(lo, x, 0.0)
    g1o = jnp.where(lo, 0.0, x)
    pair = lambda e, o: (_with_ones_column(e, False), _with_ones_column(o, True))
    return pair(g0e, pltpu.roll(g0e, HALF, 1)), pair(pltpu.roll(g1o, HALF, 1), g1o)


def _nsa_prep_kernel(zq_ref, ks_ref, vs_ref, kw_ref, vw_ref, gq_ref, gks_ref, gkw_ref,
                     q_out, ks_out, kw_out, vs_out, vw_out, *, tm):
    i = pl.program_id(1)
    scale = NSA_DK ** -0.5 * LOG2E
    for j in range(NSA_HEADS // 2):
        e, o = _split_halves(_rms_half_lanes(zq_ref[:, j * LANES:(j + 1) * LANES], gq_ref[...]) * scale)
        g = (2 * j) // NSA_HPG
        pe = (2 * j - NSA_HPG * g) // 2
        q_out[0, NSA_HPG * g + pe] = e.astype(BF16)
        q_out[0, NSA_HPG * g + NSA_HPG // 2 + pe] = o.astype(BF16)

    lane = lax.broadcasted_iota(jnp.int32, (tm, LANES), 1)
    pos = i * tm + lax.broadcasted_iota(jnp.int32, (tm, LANES), 0)
    block_tag = jnp.where(lane - HALF == pos // SLC_BLOCK, MASK_VALUE, 0.0)
    for g, x in enumerate(_split_halves(_rms_half_lanes(ks_ref[...], gks_ref[...]))):
        ks_out[0, g] = (x + block_tag).astype(BF16)
    for g, x in enumerate(_split_halves(_rms_half_lanes(kw_ref[...], gkw_ref[...]))):
        kw_out[0, g] = x.astype(BF16)
    for ref, out in ((vs_ref, vs_out), (vw_ref, vw_out)):
        for g, (ve, vo) in enumerate(_value_layouts(ref[...])):
            out[0, g, 0] = ve.astype(BF16)
            out[0, g, 1] = vo.astype(BF16)


def _nsa_prep(z_q, z_kv, bsz, s, q_norm, k_norm, tm=512):
    tm = min(tm, s)
    nt = s // tm
    row = lambda n, c: pl.BlockSpec((tm, n), lambda b, i: (b * nt + i, c))
    gain = pl.BlockSpec((1, LANES), lambda b, i: (0, 0))
    kspec = pl.BlockSpec((1, NSA_GROUPS, tm, LANES), lambda b, i: (b, 0, i, 0))
    vspec = pl.BlockSpec((1, NSA_GROUPS, 2, tm, LANES), lambda b, i: (b, 0, 0, i, 0))
    kshape = jax.ShapeDtypeStruct((bsz, NSA_GROUPS, s, LANES), BF16)
    vshape = jax.ShapeDtypeStruct((bsz, NSA_GROUPS, 2, s, LANES), BF16)
    pad = lambda v: jnp.tile(v.reshape(1, -1), (1, LANES // NSA_DK))
    return pl.pallas_call(
        functools.partial(_nsa_prep_kernel, tm=tm),
        grid=(bsz, nt),
        in_specs=[row(NSA_HEADS * NSA_DK, 0), row(LANES, 0), row(LANES, 1), row(LANES, 2), row(LANES, 3),
                  gain, gain, gain],
        out_specs=[pl.BlockSpec((1, NSA_HEADS, tm, LANES), lambda b, i: (b, 0, i, 0)),
                   kspec, kspec, vspec, vspec],
        out_shape=[jax.ShapeDtypeStruct((bsz, NSA_HEADS, s, LANES), BF16), kshape, kshape, vshape, vshape],
        compiler_params=_cparams(("parallel", "parallel")),
        name="nsa_prep",
    )(z_q, z_kv, z_kv, z_kv, z_kv, pad(q_norm), pad(k_norm[1]), pad(k_norm[2]))


def _compress_kernel(xk_ref, xv_ref, pk_ref, pv_ref, w1k_ref, w1v_ref, w2k_ref, w2v_ref, gk_ref,
                     ck_out, cv_out):
    def mlp(x, p_ref, w1_ref, w2_ref):
        nh = x.shape[0]
        top = _dot((x + p_ref[0:1]).astype(BF16), w1_ref[0])
        bot = _dot((x + p_ref[1:2]).astype(BF16), w1_ref[1])
        hid = _gelu_tanh(top + pltpu.roll(bot, nh - 1, 0))
        return _dot(hid.astype(BF16), w2_ref[...])

    nh = xk_ref.shape[1]
    yk = mlp(xk_ref[0], pk_ref, w1k_ref, w2k_ref)
    for g, x in enumerate(_split_halves(yk)):
        ck = _rms(x, gk_ref[...], NSA_DK)
        ck_out[0, g, 0:nh] = ck
        ck_out[0, g, nh:2 * nh] = ck
    yv = mlp(xv_ref[0], pv_ref, w1v_ref, w2v_ref)
    for g, layouts in enumerate(_value_layouts(yv)):
        for parity, v in enumerate(layouts):
            cv_out[0, g, parity, 0:nh] = v
            cv_out[0, g, parity, nh:2 * nh] = v


def _compress_weights(pos, w1, w2):
    half = CMP_BLOCK // 2
    g = NSA_GROUPS
    eye = jnp.eye(g, dtype=F32)
    w1 = w1.astype(BF16).reshape(2, half, NSA_DK, CMP_HIDDEN)
    w1 = jnp.stack([jnp.pad(w1, ((0, 0), (0, 0), (0, 0), (gi * CMP_HIDDEN, (g - 1 - gi) * CMP_HIDDEN)))
                    for gi in range(g)], axis=2)
    w1 = w1.reshape(2, half * g * NSA_DK, g * CMP_HIDDEN)
    p = jnp.broadcast_to(pos.reshape(2, half, 1, NSA_DK), (2, half, g, NSA_DK)).reshape(2, half * g * NSA_DK)
    w2 = (w2[None, :, None, :] * eye[:, None, :, None]).reshape(g * CMP_HIDDEN, g * NSA_DK).astype(BF16)
    return p, w1, w2


def _compress(z_kc, z_vc, bsz, s, pos_k, w1_k, w2_k, pos_v, w1_v, w2_v, k_norm0):
    nh = s // CMP_STRIDE
    feat = CMP_STRIDE * LANES
    pk, w1k, w2k = _compress_weights(pos_k, w1_k, w2_k)
    pv, w1v, w2v = _compress_weights(pos_v, w1_v, w2_v)
    full = lambda a: pl.BlockSpec(a.shape, lambda b: (0,) * a.ndim)
    xspec = pl.BlockSpec((1, nh, feat), lambda b: (b, 0, 0))
    gk = _pad_cols(k_norm0.reshape(1, -1), LANES)
    return pl.pallas_call(
        _compress_kernel,
        grid=(bsz,),
        in_specs=[xspec, xspec, full(pk), full(pv), full(w1k), full(w1v), full(w2k), full(w2v), full(gk)],
        out_specs=[pl.BlockSpec((1, NSA_GROUPS, 2 * nh, LANES), lambda b: (b, 0, 0, 0)),
                   pl.BlockSpec((1, NSA_GROUPS, 2, 2 * nh, LANES), lambda b: (b, 0, 0, 0, 0))],
        out_shape=[jax.ShapeDtypeStruct((bsz, NSA_GROUPS, 2 * nh, LANES), F32),
                   jax.ShapeDtypeStruct((bsz, NSA_GROUPS, 2, 2 * nh, LANES), F32)],
        compiler_params=_cparams(("parallel",)),
        name="nsa_compress",
    )(z_kc.reshape(bsz, nh, feat), z_vc.reshape(bsz, nh, feat), pk, pv, w1k, w1v, w2k, w2v, gk)


def _cmp_attn_kernel(q_ref, ck_ref, cv_ref, bias_ref, ov_ref, o_ref, q2_ref, s0_ref, s1_ref,
                     rank0_ref, rank1_ref, *, tq, nsel):
    nqt = q_ref.shape[2] // tq
    assert nqt % 2 == 0
    half = NSA_HPG // 2
    nh = bias_ref.shape[4]
    per_tile = tq // CMP_STRIDE
    c_last = per_tile * (nqt - 1)
    n_blk = LANES - HALF

    def block_window(t):
        start = lax.rem(per_tile * t + (nh - c_last % nh), nh)
        return pl.ds(pl.multiple_of(start, per_tile), nh)

    def query_rows(t):
        return pl.ds(pl.multiple_of(t * tq, tq), tq)

    def stage_scores(t, buf):
        t = jnp.minimum(t, nqt - 1)
        col = lax.broadcasted_iota(jnp.int32, (nh, LANES), 0)
        lane_k = lax.broadcasted_iota(jnp.int32, (nh, LANES), 1)
        wrapped = (lane_k == HALF) & (col < c_last - per_tile * t)
        ck = jnp.where(wrapped, MASK_VALUE, ck_ref[0, 0, block_window(t), :]).astype(BF16)
        lane_q = lax.broadcasted_iota(jnp.int32, (NSA_HPG, tq, LANES), 2)
        q = q_ref[0, :, query_rows(t), :] + jnp.where(lane_q == HALF, 1.0, 0.0).astype(BF16)
        s = _dot_t(q.reshape(NSA_HPG * tq, LANES), ck)
        buf[...] = s.reshape(NSA_HPG, tq, nh) + bias_ref[0, 0]

    def stage_select(t, buf, rank_ref):
        cols = block_window(t)
        cv = [cv_ref[0, 0, parity, cols, :].astype(BF16) for parity in range(2)]
        ov = ov_ref[cols, :].astype(BF16)
        row_ok = t * tq + lax.broadcasted_iota(jnp.int32, (tq, nh), 0) >= CMP_BLOCK - 1
        outs = []
        psum = None
        for h in range(NSA_HPG):
            s = buf[h]
            e = jnp.exp2(s - jnp.max(s, axis=-1, keepdims=True))
            p = jnp.where(row_ok, e / jnp.sum(e, axis=-1, keepdims=True), 0.0)
            outs.append(_dot(p.astype(BF16), cv[h // half]))
            psum = p if psum is None else psum + p
        lane_lo = lax.broadcasted_iota(jnp.int32, (tq, LANES), 1) < HALF
        for j in range(half):
            o = jnp.where(lane_lo, outs[j], outs[half + j])
            o_ref[0, query_rows(t), j * LANES:(j + 1) * LANES] = o.astype(o_ref.dtype)

        hi = psum.astype(BF16)
        lo = (psum - hi.astype(F32)).astype(BF16)
        imp = _dot(hi, ov) + _dot(lo, ov)

        blk = lax.broadcasted_iota(jnp.int32, (n_blk, tq), 0)
        qpos = t * tq + lax.broadcasted_iota(jnp.int32, (n_blk, tq), 1)
        cur = qpos // SLC_BLOCK
        valid = blk * SLC_BLOCK <= qpos
        forced = (blk == 0) | (blk == cur) | (blk == cur - 1)
        score = jnp.where(valid & forced, FORCE_SCORE, jnp.where(valid, imp.T[HALF:], -1.0))
        score = jnp.where(blk < nsel, score, REMOVED_SCORE)
        rank_ref[...] = score
        rank = jnp.zeros((n_blk, tq), F32)
        for j in range(n_blk):
            sj = jnp.broadcast_to(rank_ref[j:j + 1, :], (n_blk, tq))
            beats = (sj > score) | ((sj == score) & (blk > j))
            rank = rank + jnp.where(beats, 1.0, 0.0)
        unselected = jnp.where(rank < min(SLC_TOPN, nsel), 0.0, 1.0)
        unselected = jnp.concatenate([jnp.zeros((HALF, tq), F32), unselected], axis=0).T.astype(BF16)
        for h in range(NSA_HPG):
            q2_ref[0, h, query_rows(t), :] = q_ref[0, h, query_rows(t), :] + unselected

    stage_scores(0, s0_ref)

    def two_tiles(j, carry):
        t = 2 * j
        stage_scores(t + 1, s1_ref)
        stage_select(t, s0_ref, rank0_ref)
        stage_scores(t + 2, s0_ref)
        stage_select(t + 1, s1_ref, rank1_ref)
        return carry

    lax.fori_loop(0, nqt // 2, two_tiles, 0)


def _cmp_attention(q, ck, cv, bias, ov, s):
    bsz = q.shape[0]
    tq = NSA_TQ
    nh = s // CMP_STRIDE
    nsel = s // SLC_BLOCK
    s_buf = pltpu.VMEM((NSA_HPG, tq, nh), F32)
    rank_buf = pltpu.VMEM((LANES - HALF, tq), F32)
    return pl.pallas_call(
        functools.partial(_cmp_attn_kernel, tq=tq, nsel=nsel),
        grid=(NSA_GROUPS, bsz),
        in_specs=[pl.BlockSpec((1, NSA_HPG, s, LANES), lambda g, b: (b, g, 0, 0), pipeline_mode=pl.Buffered(1)),
                  pl.BlockSpec((1, 1, 2 * nh, LANES), lambda g, b: (b, g, 0, 0)),
                  pl.BlockSpec((1, 1, 2, 2 * nh, LANES), lambda g, b: (b, g, 0, 0, 0)),
                  pl.BlockSpec((1, 1, NSA_HPG, tq, nh), lambda g, b: (g, 0, 0, 0, 0)),
                  pl.BlockSpec((2 * nh, LANES), lambda g, b: (0, 0))],
        out_specs=[pl.BlockSpec((1, s, NSA_HPG * NSA_DK), lambda g, b: (b, 0, g)),
                   pl.BlockSpec((1, NSA_HPG, s, LANES), lambda g, b: (b, g, 0, 0))],
        out_shape=[jax.ShapeDtypeStruct((bsz, s, NSA_HEADS * NSA_DK), BF16),
                   jax.ShapeDtypeStruct(q.shape, BF16)],
        scratch_shapes=[s_buf, s_buf, rank_buf, rank_buf],
        compiler_params=_cparams(("parallel", "parallel")),
        name="nsa_cmp_attn",
    )(q, ck, cv, bias, ov)


def _nsa_window_kernel(q_ref, k_ref, v_ref, bias_ref, o_ref, s0_ref, s1_ref, *, tq, window_steps):
    nqt = q_ref.shape[2] // tq
    assert nqt % 2 == 0
    ratio = NSA_TK // tq
    n_tiles = bias_ref.shape[1]
    half = NSA_HPG // 2

    def key_chunk(t, i):
        return t // ratio - (window_steps - 1) + i

    def key_rows(t, i):
        return pl.ds(pl.multiple_of(jnp.maximum(key_chunk(t, i), 0) * NSA_TK, NSA_TK), NSA_TK)

    def stage_scores(t, buf):
        t = jnp.minimum(t, nqt - 1)
        q = q_ref[0, :, pl.ds(pl.multiple_of(t * tq, tq), tq), :].reshape(NSA_HPG * tq, LANES)
        for i in range(window_steps):
            kc = key_chunk(t, i)
            tile = jnp.where(kc < 0, n_tiles - 1, t - ratio * kc)
            s = _dot_t(q, k_ref[0, 0, key_rows(t, i), :]).reshape(NSA_HPG, tq, NSA_TK)
            buf[:, :, i * NSA_TK:(i + 1) * NSA_TK] = s + bias_ref[0, tile]

    def stage_out(t, buf):
        accs = []
        for h in range(NSA_HPG):
            s = buf[h]
            p = jnp.exp2(s - jnp.max(s, axis=-1, keepdims=True)).astype(BF16)
            acc = None
            for i in range(window_steps):
                pv = _dot(p[:, i * NSA_TK:(i + 1) * NSA_TK], v_ref[0, 0, h // half, key_rows(t, i), :])
                acc = pv if acc is None else acc + pv
            accs.append(acc)
        rows = pl.ds(pl.multiple_of(t * tq, tq), tq)
        for j in range(half):
            o = _normalized_pair(accs[j], accs[half + j])
            o_ref[0, rows, j * LANES:(j + 1) * LANES] = o.astype(o_ref.dtype)

    stage_scores(0, s0_ref)

    def two_tiles(j, carry):
        t = 2 * j
        stage_scores(t + 1, s1_ref)
        stage_out(t, s0_ref)
        stage_scores(t + 2, s0_ref)
        stage_out(t + 1, s1_ref)
        return carry

    lax.fori_loop(0, nqt // 2, two_tiles, 0)


def _nsa_selected_kernel(q_ref, k_ref, v_ref, bias_ref, o_ref, s0_ref, s1_ref, m_ref, acc_ref, *, tq):
    nqt = q_ref.shape[2] // tq
    ratio = NSA_TK // tq
    n_tiles = bias_ref.shape[1]
    half = NSA_HPG // 2
    chunk = SEL_TILES * NSA_TK
    per = chunk // tq

    def key_rows(kc):
        return pl.ds(pl.multiple_of(kc * chunk, chunk), chunk)

    def scores(t, kc, valid, buf):
        q = q_ref[0, :, pl.ds(pl.multiple_of(t * tq, tq), tq), :].reshape(NSA_HPG * tq, LANES)
        s = _dot_t(q, k_ref[0, 0, key_rows(kc), :]).reshape(NSA_HPG, tq, chunk)
        for i in range(SEL_TILES):
            offset = t - ratio * (SEL_TILES * kc + i)
            tile = jnp.where(valid & (offset >= 0), jnp.minimum(offset, n_tiles - 2), n_tiles - 1)
            cols = slice(i * NSA_TK, (i + 1) * NSA_TK)
            buf[:, :, cols] = s[:, :, cols] + bias_ref[0, tile]

    def values(h, kc):
        return v_ref[0, 0, h // half, key_rows(kc), :]

    def write_out(t, accs):
        rows = pl.ds(pl.multiple_of(t * tq, tq), tq)
        for j in range(half):
            o = _normalized_pair(accs[j], accs[half + j])
            o_ref[0, rows, j * LANES:(j + 1) * LANES] = o.astype(o_ref.dtype)

    _causal_item_stream(nqt, lambda t: t // per, NSA_HPG, scores, values, write_out,
                        (s0_ref, s1_ref), m_ref, acc_ref)


def _nsa_selected(q, k, v, bias, s):
    bsz = q.shape[0]
    tq = NSA_TQ
    n_off = bias.shape[1]
    assert s % (SEL_TILES * NSA_TK) == 0
    s_buf = pltpu.VMEM((NSA_HPG, tq, SEL_TILES * NSA_TK), F32)
    once = pl.Buffered(1)
    return pl.pallas_call(
        functools.partial(_nsa_selected_kernel, tq=tq),
        grid=(NSA_GROUPS, bsz),
        in_specs=[pl.BlockSpec((1, NSA_HPG, s, LANES), lambda g, b: (b, g, 0, 0), pipeline_mode=once),
                  pl.BlockSpec((1, 1, s, LANES), lambda g, b: (b, g, 0, 0)),
                  pl.BlockSpec((1, 1, 2, s, LANES), lambda g, b: (b, g, 0, 0, 0)),
                  pl.BlockSpec((1, n_off, NSA_HPG, tq, NSA_TK), lambda g, b: (g, 0, 0, 0, 0), pipeline_mode=once)],
        out_specs=pl.BlockSpec((1, s, NSA_HPG * NSA_DK), lambda g, b: (b, 0, g)),
        out_shape=jax.ShapeDtypeStruct((bsz, s, NSA_HEADS * NSA_DK), BF16),
        scratch_shapes=[s_buf, s_buf, pltpu.VMEM((NSA_HPG, tq, LANES), F32), pltpu.VMEM((NSA_HPG, tq, LANES), F32)],
        compiler_params=_cparams(("parallel", "parallel")),
        name="nsa_flash_sel",
    )(q, k, v, bias)


def _nsa_window(q, k, v, bias, s, window_steps):
    bsz = q.shape[0]
    tq = NSA_TQ
    n_off = bias.shape[1]
    s_buf = pltpu.VMEM((NSA_HPG, tq, window_steps * NSA_TK), F32)
    once = pl.Buffered(1)
    return pl.pallas_call(
        functools.partial(_nsa_window_kernel, tq=tq, window_steps=window_steps),
        grid=(NSA_GROUPS, bsz),
        in_specs=[pl.BlockSpec((1, NSA_HPG, s, LANES), lambda g, b: (b, g, 0, 0), pipeline_mode=once),
                  pl.BlockSpec((1, 1, s, LANES), lambda g, b: (b, g, 0, 0)),
                  pl.BlockSpec((1, 1, 2, s, LANES), lambda g, b: (b, g, 0, 0, 0)),
                  pl.BlockSpec((1, n_off, NSA_HPG, tq, NSA_TK), lambda g, b: (g, 0, 0, 0, 0), pipeline_mode=once)],
        out_specs=pl.BlockSpec((1, s, NSA_HPG * NSA_DK), lambda g, b: (b, 0, g)),
        out_shape=jax.ShapeDtypeStruct((bsz, s, NSA_HEADS * NSA_DK), BF16),
        scratch_shapes=[s_buf, s_buf],
        compiler_params=_cparams(("parallel", "parallel")),
        name="nsa_flash_win",
    )(q, k, v, bias)


def _hybrid_layer(h, bsz, s, layer, g_mix, g_mlp, w1, w2, w_in, conv_w, conv_b, wa, ba, wx, bx, lam,
                  q_norm, kv_norm, w_uq, w_ukv, qn_q, qn_k, w_out):
    d = h.shape[1]
    c = LRU_WIDTH
    o_kv = 2 * c + MLA_Q_RANK
    o_kr = o_kv + MLA_KV_RANK
    w_kr = jnp.pad(w_in[:, o_kr:], ((0, 0), (MLA_NOPE, LANES - MLA_QK)))
    w_all = jnp.concatenate([w_in[:, :o_kv], w_kr, w_in[:, o_kv:o_kr]], axis=1).astype(BF16)
    z_lru, q_lat, k_rope, kv_lat = _norm_proj(h, g_mix, w_all,
                                              ((2 * c,), (MLA_Q_RANK, LANES), (MLA_KV_RANK,)))
    lru = _rg_lru(z_lru.reshape(bsz, s, 2 * c), conv_w, conv_b, wa, ba, wx, bx, lam)
    q, k, v = _mla_prep(q_lat, kv_lat, k_rope, bsz, s, q_norm, kv_norm, w_uq, w_ukv, qn_q, qn_k)
    mla = _mla_attention(q, k, v)
    wo = w_out.astype(BF16)
    return _mix_mlp(h, [[lru.reshape(bsz * s, c)], [mla.reshape(bsz * s, -1)]], [wo[:c], wo[c:]],
                    g_mlp, w1, w2, layer)


def _nsa_layer(h, bsz, s, layer, g_mix, g_mlp, w1, w2, w_in, pos_k, w1_k, w2_k, pos_v, w1_v, w2_v,
               q_norm, k_norm, rel_bias, w_out):
    nq = NSA_HEADS * NSA_DK
    kvw = NSA_GROUPS * NSA_DK
    w_gate = _pad_cols(w_in[:, nq + 6 * kvw:], LANES)
    w_all = jnp.concatenate([w_in[:, :nq + 6 * kvw], w_gate], axis=1).astype(BF16)
    z_q, z_kc, z_vc, z_kv, z_gate = _norm_proj(h, g_mix, w_all, ((nq,), (kvw, kvw), (4 * kvw,), (LANES,)),
                                               folds=(1, CMP_STRIDE, CMP_STRIDE, 1, 1))

    q, ks, kw, vs, vw = _nsa_prep(z_q, z_kv, bsz, s, q_norm, k_norm)
    ck, cv = _compress(z_kc, z_vc, bsz, s, pos_k, w1_k, w2_k, pos_v, w1_v, w2_v, k_norm[0])

    order = np.array(HEAD_ORDER)
    expand = np.zeros((3, LANES, NSA_HEADS * NSA_DK), np.float32)
    for br in range(3):
        for hd in range(NSA_HEADS):
            expand[br, 3 * hd + br, hd * NSA_DK:(hd + 1) * NSA_DK] = 1.0

    slot_heads = (np.arange(NSA_GROUPS)[:, None] * NSA_HPG + order[None, :]).reshape(-1)
    tab_slots = rel_bias.T[slot_heads]
    nh = s // CMP_STRIDE
    nc = (s - CMP_BLOCK) // CMP_STRIDE + 1
    cmp_bias = _bias_tiles(tab_slots, _cmp_bias_ids(s))
    sel_bias = _bias_tiles(tab_slots, _toeplitz_bucket_ids(None)[0])
    win_ids, n_win_tiles = _toeplitz_bucket_ids(WINDOW)
    win_bias = _bias_tiles(tab_slots, win_ids)

    nsel = s // SLC_BLOCK
    cstart = np.arange(nh) * CMP_STRIDE
    sstart = np.arange(nsel) * SLC_BLOCK
    ov = np.clip(np.minimum(cstart[:, None] + CMP_BLOCK, sstart[None, :] + SLC_BLOCK)
                 - np.maximum(cstart[:, None], sstart[None, :]), 0, None).astype(np.float32) / CMP_BLOCK
    ov[nc:] = 0.0
    ov_p = np.zeros((nh, LANES), np.float32)
    ov_p[:, HALF:HALF + nsel] = ov

    o_c, q_sel = _cmp_attention(q, ck, cv, cmp_bias, jnp.asarray(np.concatenate([ov_p, ov_p])), s)
    o_s = _nsa_selected(q_sel, ks, vs, sel_bias, s)
    window_steps = (n_win_tiles + 1) // (NSA_TK // NSA_TQ)
    o_w = _nsa_window(q, kw, vw, win_bias, s, window_steps)

    flat = lambda a: a.reshape(bsz * s, -1)
    return _mix_mlp(h, [[flat(o_c), flat(o_s), flat(o_w)]], [w_out.astype(BF16)],
                    g_mlp, w1, w2, layer,
                    gate=(z_gate, jnp.asarray(np.concatenate([expand, expand], axis=1), BF16)))


def kernel(x, rel_bias, norm_mix, norm_mlp, mlp_w1, mlp_w2, hy_w_in, lru_conv_w, lru_conv_b, lru_wa, lru_ba, lru_wx, lru_bx, lru_lambda, mla_q_norm, mla_kv_norm, mla_w_uq, mla_w_ukv, mla_qn_q, mla_qn_k, hy_w_out, nsa_w_in, nsa_cmp_pos_k, nsa_cmp_w1_k, nsa_cmp_w2_k, nsa_cmp_pos_v, nsa_cmp_w1_v, nsa_cmp_w2_v, nsa_q_norm, nsa_k_norm, nsa_w_out):
    bsz, s, d = x.shape
    depth = norm_mix.shape[0]
    h = x.reshape(bsz * s, d)
    w1_all, w2_all = mlp_w1.astype(BF16), mlp_w2.astype(BF16)
    for layer in range(depth):
        if layer % 2 == 0:
            e = layer // 2
            h = _hybrid_layer(h, bsz, s, layer, norm_mix[layer], norm_mlp[layer], w1_all, w2_all,
                              hy_w_in[e], lru_conv_w[e], lru_conv_b[e], lru_wa[e], lru_ba[e], lru_wx[e],
                              lru_bx[e], lru_lambda[e], mla_q_norm[e], mla_kv_norm[e], mla_w_uq[e],
                              mla_w_ukv[e], mla_qn_q[e], mla_qn_k[e], hy_w_out[e])
        else:
            o = layer // 2
            h = _nsa_layer(h, bsz, s, layer, norm_mix[layer], norm_mlp[layer], w1_all, w2_all,
                           nsa_w_in[o], nsa_cmp_pos_k[o], nsa_cmp_w1_k[o], nsa_cmp_w2_k[o],
                           nsa_cmp_pos_v[o], nsa_cmp_w1_v[o], nsa_cmp_w2_v[o], nsa_q_norm[o],
                           nsa_k_norm[o], rel_bias, nsa_w_out[o])
    return h.reshape(bsz, s, d)
```

```python
import functools
import math

import numpy as np
import jax
import jax.numpy as jnp
from jax import lax
from jax.experimental import pallas as pl
from jax.experimental.pallas import tpu as pltpu

F32 = jnp.float32
BF16 = jnp.bfloat16

NORM_EPS = 1e-6
MASK_VALUE = -1e30
FORCE_SCORE = 1e4
REMOVED_SCORE = -3e38

NUM_BUCKETS = 32
MAX_DISTANCE = 1024
MASKED_ID = NUM_BUCKETS
EXCLUDED_ID = NUM_BUCKETS + 1
LOG2E = math.log2(math.e)

LRU_WIDTH = 512
LRU_BLOCKS = 8
CONV_WIDTH = 4
LRU_C = 8.0

MLA_HEADS = 8
MLA_NOPE = 64
MLA_ROPE = 32
MLA_QK = MLA_NOPE + MLA_ROPE
MLA_V = 64
MLA_Q_RANK = 384
MLA_KV_RANK = 256
ROPE_THETA = 10000.0

NSA_HEADS = 16
NSA_GROUPS = 2
NSA_HPG = NSA_HEADS // NSA_GROUPS
NSA_DK = 64
CMP_BLOCK = 32
CMP_STRIDE = 16
CMP_HIDDEN = 256
SLC_BLOCK = 64
SLC_TOPN = 8
WINDOW = 512

LANES = 128
SUBLANES = 8
HALF = LANES // 2
NSA_TQ = 128
NSA_TK = 256
SEL_TILES = 2
MLA_TR = 128
MLA_TK = 512
HEAD_ORDER = (0, 2, 4, 6, 1, 3, 5, 7)
V7X_VMEM_BYTES = 64 * 1024 * 1024
VMEM_LIMIT = V7X_VMEM_BYTES * 7 // 8


def _cparams(sem):
    return pltpu.CompilerParams(dimension_semantics=sem, vmem_limit_bytes=VMEM_LIMIT)


def _rms(x, g, n=None):
    n = x.shape[-1] if n is None else n
    ms = jnp.sum(x * x, axis=-1, keepdims=True) * (1.0 / n)
    return x * lax.rsqrt(ms + NORM_EPS) * g


def _rms_lanes(x, g, n):
    y = x * x
    hi = y.astype(BF16)
    parts = jnp.concatenate([hi, (y - hi.astype(F32)).astype(BF16)], axis=1)
    ss = _dot(parts, jnp.ones((2 * LANES, LANES), BF16))
    return x * lax.rsqrt(ss * (1.0 / n) + NORM_EPS) * g


def _rms_half_lanes(x, g):
    y = x * x
    hi = y.astype(BF16)
    parts = jnp.concatenate([hi, (y - hi.astype(F32)).astype(BF16)], axis=1)
    row_half = lax.broadcasted_iota(jnp.int32, (2 * LANES, LANES), 0) % LANES // HALF
    col_half = lax.broadcasted_iota(jnp.int32, (2 * LANES, LANES), 1) // HALF
    ss = _dot(parts, jnp.where(row_half == col_half, 1.0, 0.0).astype(BF16))
    return x * lax.rsqrt(ss * (1.0 / HALF) + NORM_EPS) * g


def _gelu_tanh(x):
    return 0.5 * x * (1.0 + jnp.tanh(math.sqrt(2.0 / math.pi) * (x + 0.044715 * (x * x * x))))


def _sigmoid(x):
    return 1.0 / (1.0 + jnp.exp(-x))


def _dot(a, b):
    return jnp.dot(a, b, preferred_element_type=F32)


def _dot_t(a, b):
    return lax.dot_general(a, b, (((1,), (1,)), ((), ())), preferred_element_type=F32)


def _causal_item_stream(n_tiles, last_chunk, n_chains, scores, values, write_out, s_bufs, m_ref, acc_ref):
    m_ref[...] = jnp.full_like(m_ref, MASK_VALUE)
    acc_ref[...] = jnp.zeros_like(acc_ref)
    n_items = sum(last_chunk(t) + 1 for t in range(n_tiles))

    def advance(t, kc):
        wrap = kc >= last_chunk(t)
        return jnp.where(wrap, t + 1, t), jnp.where(wrap, 0, kc + 1)

    def stage_scores(t, kc, buf):
        valid = t < n_tiles
        scores(jnp.minimum(t, n_tiles - 1), jnp.where(valid, kc, 0), valid, buf)

    def stage_update(t, kc, buf, maybe_padding):
        kc = jnp.where(t < n_tiles, kc, 0) if maybe_padding else kc
        tile_start = kc == 0
        accs = []
        for c in range(n_chains):
            m_old = jnp.where(tile_start, MASK_VALUE, m_ref[c])
            m_new = jnp.maximum(m_old, jnp.max(buf[c], axis=-1, keepdims=True))
            alpha = jnp.exp2(m_old - m_new)
            s = buf[c]
            p = jnp.exp2(s - jnp.concatenate([m_new] * (s.shape[1] // LANES), axis=1))
            acc = acc_ref[c] * alpha + _dot(p.astype(BF16), values(c, kc))
            m_ref[c] = m_new
            acc_ref[c] = acc
            accs.append(acc)
        if maybe_padding:
            pl.when(t < n_tiles)(lambda: write_out(t, accs))
        else:
            write_out(t, accs)

    start = (jnp.int32(0), jnp.int32(0))
    stage_scores(*start, s_bufs[0])

    def two_items(_, item):
        item1 = advance(*item)
        stage_scores(*item1, s_bufs[1])
        stage_update(*item, s_bufs[0], False)
        item2 = advance(*item1)
        stage_scores(*item2, s_bufs[0])
        stage_update(*item1, s_bufs[1], n_items % 2 == 1)
        return item2

    lax.fori_loop(0, (n_items + 1) // 2, two_items, start)


def _normalized_pair(acc_even, acc_odd):
    lo = lax.broadcasted_iota(jnp.int32, acc_even.shape, 1) < HALF
    return jnp.where(lo, acc_even / acc_even[:, LANES - 1:LANES], acc_odd / acc_odd[:, 0:1])


def _with_ones_column(v, odd):
    lane = lax.broadcasted_iota(jnp.int32, v.shape, 1)
    return jnp.where(lane == (0 if odd else LANES - 1), 1.0, v)


def _norm_proj_kernel(x_ref, g_ref, w_ref, *refs, groups, folds):
    n_out = len(folds)
    out_refs, fold_refs = refs[:n_out], list(refs[n_out:])
    xn = _rms(x_ref[...], g_ref[...]).astype(BF16)
    off = 0
    k = 0
    for group in groups:
        width = sum(group)
        z = _dot(xn, w_ref[:, off:off + width])
        off += width
        col = 0
        for n in group:
            piece = z[:, col:col + n]
            col += n
            if folds[k] == 1:
                out_refs[k][...] = piece
            else:
                scratch = fold_refs.pop(0)
                scratch[...] = piece
                rows = piece.shape[0] // folds[k]
                for j in range(folds[k]):
                    out_refs[k][:, j * n:(j + 1) * n] = scratch[pl.ds(j, rows, stride=folds[k]), :]
            k += 1


def _norm_proj(x, g, w, groups, folds=None, tm=512):
    t, d = x.shape
    n = w.shape[1]
    splits = [s for group in groups for s in group]
    folds = tuple(folds) if folds is not None else (1,) * len(splits)
    assert sum(splits) == n and t % tm == 0
    return pl.pallas_call(
        functools.partial(_norm_proj_kernel, groups=groups, folds=folds),
        grid=(t // tm,),
        in_specs=[pl.BlockSpec((tm, d), lambda i: (i, 0)),
                  pl.BlockSpec((1, d), lambda i: (0, 0)),
                  pl.BlockSpec((d, n), lambda i: (0, 0))],
        out_specs=[pl.BlockSpec((tm // f, s * f), lambda i: (i, 0)) for s, f in zip(splits, folds)],
        out_shape=[jax.ShapeDtypeStruct((t // f, s * f), F32) for s, f in zip(splits, folds)],
        scratch_shapes=[pltpu.VMEM((tm, s), F32) for s, f in zip(splits, folds) if f > 1],
        compiler_params=_cparams(("parallel",)),
        name="norm_proj",
    )(x, g.reshape(1, d), w)


def _mix_mlp_kernel(*refs, group_sizes, gated):
    n_a = sum(group_sizes)
    n_g = len(group_sizes)
    h_ref = refs[0]
    a_refs = refs[1:1 + n_a]
    wo_refs = refs[1 + n_a:1 + n_a + n_g]
    rest = refs[1 + n_a + n_g:]
    if gated:
        zg_ref, e_ref = rest[:2]
        rest = rest[2:]
    g_ref, w1_ref, w2_ref, out_ref, hres, xn, acc = rest
    f = pl.program_id(1)

    @pl.when(f == 0)
    def _():
        h1 = h_ref[...]
        if gated:
            gate = _sigmoid(zg_ref[...])
            g_hi = gate.astype(BF16)
            g_parts = jnp.concatenate([g_hi, (gate - g_hi.astype(F32)).astype(BF16)], axis=1)
        k = 0
        for gi, gs in enumerate(group_sizes):
            a = None
            for j in range(gs):
                aj = a_refs[k + j][...].astype(F32)
                if gated:
                    aj = aj * _dot(g_parts, e_ref[j])
                a = aj if a is None else a + aj
            k += gs
            h1 = h1 + _dot(a.astype(BF16), wo_refs[gi][...])
        hres[...] = h1
        xn[...] = _rms(h1, g_ref[...]).astype(BF16)
        acc[...] = jnp.zeros_like(acc)

    u = jnp.maximum(_dot(xn[...], w1_ref[...]), 0.0)
    acc[...] += _dot((u * u).astype(BF16), w2_ref[...])

    @pl.when(f == pl.num_programs(1) - 1)
    def _():
        out_ref[...] = hres[...] + acc[...]


def _mix_mlp(h, groups, wos, g, w1, w2, layer, gate=None, tm=512, tf=4096):
    t, d = h.shape
    ff = w1.shape[2]
    group_sizes = tuple(len(gr) for gr in groups)
    a_list = [a for gr in groups for a in gr]
    extra = [] if gate is None else list(gate)
    in_specs = [pl.BlockSpec((tm, d), lambda i, f: (i, 0))]
    in_specs += [pl.BlockSpec((tm, a.shape[1]), lambda i, f: (i, 0)) for a in a_list]
    in_specs += [pl.BlockSpec(w.shape, lambda i, f: (0, 0)) for w in wos]
    if gate is not None:
        in_specs += [pl.BlockSpec((tm, LANES), lambda i, f: (i, 0)),
                     pl.BlockSpec(gate[1].shape, lambda i, f: (0, 0, 0))]
    once = pl.Buffered(1) if tf == ff else None
    in_specs += [pl.BlockSpec((1, d), lambda i, f: (0, 0)),
                 pl.BlockSpec((None, d, tf), lambda i, f: (layer, 0, f), pipeline_mode=once),
                 pl.BlockSpec((None, tf, d), lambda i, f: (layer, f, 0), pipeline_mode=once)]
    return pl.pallas_call(
        functools.partial(_mix_mlp_kernel, group_sizes=group_sizes, gated=gate is not None),
        grid=(t // tm, ff // tf),
        in_specs=in_specs,
        out_specs=pl.BlockSpec((tm, d), lambda i, f: (i, 0)),
        out_shape=jax.ShapeDtypeStruct((t, d), F32),
        scratch_shapes=[pltpu.VMEM((tm, d), F32), pltpu.VMEM((tm, d), BF16),
                        pltpu.VMEM((tm, d), F32)],
        compiler_params=_cparams(("parallel", "arbitrary")),
        name="mix_mlp",
    )(h, *a_list, *wos, *extra, g.reshape(1, d), w1, w2)


def _lru_kernel(xr_ref, xg_ref, cw_ref, cb_ref, wa_ref, ba_ref, wx_ref, bx_ref, lam_ref,
                o_ref, tail_ref, hc_ref, *, tc):
    c = pl.program_id(1)

    @pl.when(c == 0)
    def _():
        tail_ref[...] = jnp.zeros_like(tail_ref)
        hc_ref[...] = jnp.zeros_like(hc_ref)

    x = xr_ref[0]
    width = x.shape[1]
    row = lax.broadcasted_iota(jnp.int32, (tc, width), 0)
    tail_row = lax.broadcasted_iota(jnp.int32, (SUBLANES, width), 0)
    tail = tail_ref[...]
    cw = cw_ref[...]
    y = x * cw[CONV_WIDTH - 1:CONV_WIDTH]
    for s in range(1, CONV_WIDTH):
        sh = pltpu.roll(x, s, 0)
        head = jnp.where(tail_row < s, pltpu.roll(tail, s, 0), sh[0:SUBLANES])
        sh = jnp.concatenate([head, sh[SUBLANES:]], axis=0)
        y = y + sh * cw[CONV_WIDTH - 1 - s:CONV_WIDTH - s]
    y = y + cb_ref[...]
    tail_ref[...] = x[tc - SUBLANES:tc]

    yb = y.astype(BF16)
    r = _sigmoid(_dot(yb, wa_ref[...]) + ba_ref[...])
    i = _sigmoid(_dot(yb, wx_ref[...]) + bx_ref[...])
    nl = -lam_ref[...]
    softplus = jnp.maximum(nl, 0.0) + jnp.log(1.0 + jnp.exp(-jnp.abs(nl)))
    log_a = (-LRU_C) * r * softplus
    a = jnp.exp(log_a)
    mult = jnp.sqrt(1.0 - jnp.exp(2.0 * log_a))
    mult = jnp.where((row == 0) & (c == 0), 1.0, mult)
    b = mult * (i * y)

    d = 1
    while d < tc:
        keep = row >= d
        a_sh = jnp.where(keep, pltpu.roll(a, d, 0), 1.0)
        b_sh = jnp.where(keep, pltpu.roll(b, d, 0), 0.0)
        b = a * b_sh + b
        a = a * a_sh
        d *= 2
    h = b + a * hc_ref[...]
    hc_ref[...] = h[tc - 1:tc]
    o_ref[0] = (h * _gelu_tanh(xg_ref[0])).astype(o_ref.dtype)


def _block_diag(w):
    n, d, e = w.shape
    eye = jnp.eye(n, dtype=w.dtype)
    return (w[:, :, None, :] * eye[:, None, :, None]).reshape(n * d, n * e)


def _rg_lru(z_lru, conv_w, conv_b, wa, ba, wx, bx, lam, tc=256):
    bsz, s, _ = z_lru.shape
    c = LRU_WIDTH
    tc = min(tc, s)
    vec = lambda v: pl.BlockSpec(v, lambda b, i: (0, 0))
    return pl.pallas_call(
        functools.partial(_lru_kernel, tc=tc),
        grid=(bsz, s // tc),
        in_specs=[pl.BlockSpec((1, tc, c), lambda b, i: (b, i, 0)),
                  pl.BlockSpec((1, tc, c), lambda b, i: (b, i, 1)),
                  vec((CONV_WIDTH, c)), vec((1, c)), vec((c, c)), vec((1, c)),
                  vec((c, c)), vec((1, c)), vec((1, c))],
        out_specs=pl.BlockSpec((1, tc, c), lambda b, i: (b, i, 0)),
        out_shape=jax.ShapeDtypeStruct((bsz, s, c), BF16),
        scratch_shapes=[pltpu.VMEM((SUBLANES, c), F32), pltpu.VMEM((1, c), F32)],
        compiler_params=_cparams(("parallel", "arbitrary")),
        name="rg_lru",
    )(z_lru, z_lru, conv_w, conv_b.reshape(1, c), _block_diag(wa).astype(BF16), ba.reshape(1, c),
      _block_diag(wx).astype(BF16), bx.reshape(1, c), lam.reshape(1, c))


def _rope(x, cos, sin_lo, sin_hi):
    return (x * cos + pltpu.roll(x, LANES - MLA_ROPE // 2, 1) * sin_lo
            + pltpu.roll(x, MLA_ROPE // 2, 1) * sin_hi)


def _mla_prep_kernel(ql_ref, kvl_ref, kr_ref, gq_ref, gkv_ref, wq_ref, wk_ref, wv_ref,
                     nq_ref, nk_ref, cos_ref, s1_ref, s2_ref, q_out, k_out, v_out):
    qn = _rms(ql_ref[...], gq_ref[...]).astype(BF16)
    kvn = _rms(kvl_ref[...], gkv_ref[...]).astype(BF16)
    kr = kr_ref[...]
    cos, s1, s2 = cos_ref[...], s1_ref[...], s2_ref[...]
    scale = MLA_QK ** -0.5 * LOG2E
    for h in range(MLA_HEADS):
        sl = slice(h * LANES, (h + 1) * LANES)
        qh = _rms_lanes(_dot(qn, wq_ref[:, sl]), nq_ref[...], MLA_QK)
        q_out[0, h] = (_rope(qh, cos, s1, s2) * scale).astype(BF16)
        kh = _rms_lanes(_dot(kvn, wk_ref[:, sl]) + kr, nk_ref[...], MLA_QK)
        k_out[0, h] = _rope(kh, cos, s1, s2).astype(BF16)
        v_out[0, h] = _with_ones_column(_dot(kvn, wv_ref[:, sl]), h % 2 == 1).astype(BF16)


def _rope_tables(s):
    half = MLA_ROPE // 2
    freqs = ROPE_THETA ** (-jnp.arange(half, dtype=F32) / half)
    ang = jnp.arange(s, dtype=F32)[:, None] * freqs[None, :]
    cos, sin = jnp.cos(ang), jnp.sin(ang)
    z = lambda n: jnp.zeros((s, n), F32)
    cos_t = jnp.concatenate([jnp.ones((s, MLA_NOPE), F32), cos, cos, z(LANES - MLA_QK)], axis=1)
    s_lo = jnp.concatenate([z(MLA_NOPE), -sin, z(LANES - MLA_NOPE - half)], axis=1)
    s_hi = jnp.concatenate([z(MLA_NOPE + half), sin, z(LANES - MLA_QK)], axis=1)
    return cos_t, s_lo, s_hi


def _pad_cols(w, n):
    return jnp.pad(w, ((0, 0),) * (w.ndim - 1) + ((0, n - w.shape[-1]),))


def _mla_prep(q_lat, kv_lat, k_rope, bsz, s, q_norm, kv_norm, w_uq, w_ukv, qn_q, qn_k, tm=512):
    tm = min(tm, s)
    nt = s // tm
    hl = MLA_HEADS * LANES
    wq = _pad_cols(w_uq.reshape(MLA_Q_RANK, MLA_HEADS, MLA_QK), LANES).reshape(MLA_Q_RANK, hl)
    wkv = w_ukv.reshape(MLA_KV_RANK, MLA_HEADS, MLA_NOPE + MLA_V)
    wk = _pad_cols(wkv[..., :MLA_NOPE], LANES).reshape(MLA_KV_RANK, hl)
    wv = wkv[..., MLA_NOPE:]
    zero = jnp.zeros_like(wv)
    even = (jnp.arange(MLA_HEADS) % 2 == 0)[None, :, None]
    wv = jnp.concatenate([jnp.where(even, wv, zero), jnp.where(even, zero, wv)], axis=-1)
    wv = wv.reshape(MLA_KV_RANK, hl)
    cos_t, s_lo, s_hi = _rope_tables(s)
    row = lambda n: pl.BlockSpec((tm, n), lambda b, i: (b * nt + i, 0))
    full = lambda a: pl.BlockSpec(a.shape, lambda b, i: (0, 0))
    tab = pl.BlockSpec((tm, LANES), lambda b, i: (i, 0))
    head_out = pl.BlockSpec((1, MLA_HEADS, tm, LANES), lambda b, i: (b, 0, i, 0))
    args = (q_lat, kv_lat, k_rope, q_norm.reshape(1, -1), kv_norm.reshape(1, -1),
            wq.astype(BF16), wk.astype(BF16), wv.astype(BF16),
            _pad_cols(qn_q.reshape(1, -1), LANES), _pad_cols(qn_k.reshape(1, -1), LANES),
            cos_t, s_lo, s_hi)
    in_specs = [row(MLA_Q_RANK), row(MLA_KV_RANK), row(LANES)] + [full(a) for a in args[3:10]] + [tab] * 3
    shp = jax.ShapeDtypeStruct((bsz, MLA_HEADS, s, LANES), BF16)
    return pl.pallas_call(
        _mla_prep_kernel,
        grid=(bsz, nt),
        in_specs=in_specs,
        out_specs=[head_out] * 3,
        out_shape=[shp] * 3,
        compiler_params=_cparams(("parallel", "parallel")),
        name="mla_prep",
    )(*args)


def _mla_attn_kernel(q_ref, k_ref, v_ref, mask_ref, o_ref, s0_ref, s1_ref, m_ref, acc_ref, *, tq):
    n_rc = tq // MLA_TR
    ratio = MLA_TK // MLA_TR
    n_tiles = mask_ref.shape[0]

    def key_rows(kc):
        return pl.ds(pl.multiple_of(kc * MLA_TK, MLA_TK), MLA_TK)

    def scores(qi, kc, valid, buf):
        rows = pl.ds(pl.multiple_of(qi * tq, tq), tq)
        for j in range(2):
            s = _dot_t(q_ref[0, j, rows, :], k_ref[0, j, key_rows(kc), :])
            for rc in range(n_rc):
                offset = qi * n_rc + rc - ratio * kc
                tile = jnp.where(valid, jnp.clip(offset + 1, 0, n_tiles - 1), 0)
                buf[j * n_rc + rc] = s[rc * MLA_TR:(rc + 1) * MLA_TR] + mask_ref[tile]

    def values(c, kc):
        return v_ref[0, c // n_rc, key_rows(kc), :]

    def write_out(qi, accs):
        for rc in range(n_rc):
            rows = pl.ds(pl.multiple_of(qi * tq + rc * MLA_TR, MLA_TR), MLA_TR)
            o_ref[0, rows, :] = _normalized_pair(accs[rc], accs[n_rc + rc]).astype(o_ref.dtype)

    _causal_item_stream(q_ref.shape[2] // tq, lambda qi: (qi * tq + tq - 1) // MLA_TK, 2 * n_rc,
                        scores, values, write_out, (s0_ref, s1_ref), m_ref, acc_ref)


def _mla_causal_tiles():
    i = np.arange(MLA_TR)[:, None]
    j = np.arange(MLA_TK)[None, :]
    tiles = [np.full((MLA_TR, MLA_TK), -np.inf, np.float32)]
    for d in range(MLA_TK // MLA_TR):
        tiles.append(np.where(j <= i + d * MLA_TR, 0.0, -np.inf).astype(np.float32))
    tiles.append(np.zeros((MLA_TR, MLA_TK), np.float32))
    return np.stack(tiles)


def _mla_attention(q, k, v, tq=512):
    bsz, nh, s, _ = q.shape
    tq = min(tq, s)
    assert tq % MLA_TK == 0 and tq % MLA_TR == 0
    n_chains = 2 * (tq // MLA_TR)
    mask = _mla_causal_tiles()
    seq_spec = pl.BlockSpec((1, 2, s, LANES), lambda b, hp: (b, hp, 0, 0))
    return pl.pallas_call(
        functools.partial(_mla_attn_kernel, tq=tq),
        grid=(bsz, nh // 2),
        in_specs=[seq_spec, seq_spec, seq_spec, pl.BlockSpec(mask.shape, lambda b, hp: (0, 0, 0))],
        out_specs=pl.BlockSpec((1, s, LANES), lambda b, hp: (b, 0, hp)),
        out_shape=jax.ShapeDtypeStruct((bsz, s, nh * MLA_V), BF16),
        scratch_shapes=[pltpu.VMEM((n_chains, MLA_TR, MLA_TK), F32), pltpu.VMEM((n_chains, MLA_TR, MLA_TK), F32),
                        pltpu.VMEM((n_chains, MLA_TR, LANES), F32), pltpu.VMEM((n_chains, MLA_TR, LANES), F32)],
        compiler_params=_cparams(("parallel", "parallel")),
        name="mla_attn",
    )(q, k, v, jnp.asarray(mask))


def _t5_bucket_np(dist):
    n = np.maximum(dist, 0)
    max_exact = NUM_BUCKETS // 2
    nf = np.maximum(n, 1).astype(np.float32)
    large = max_exact + (np.log(nf / np.float32(max_exact)) / np.float32(math.log(MAX_DISTANCE / max_exact))
                         * np.float32(NUM_BUCKETS - max_exact)).astype(np.int32)
    large = np.minimum(large, NUM_BUCKETS - 1)
    return np.where(n < max_exact, n, large).astype(np.int32)


def _bias_tiles_kernel(tab_ref, idx_ref, o_ref, *, buckets):
    slot = pl.program_id(0) * NSA_HPG + pl.program_id(1)
    for i, present in enumerate(buckets):
        idx = idx_ref[i]
        acc = jnp.where(idx == EXCLUDED_ID, -jnp.inf, MASK_VALUE).astype(F32)
        for b in present:
            acc = jnp.where(idx == b, tab_ref[slot, b] * LOG2E, acc)
        o_ref[0, i, 0] = acc


def _bias_tiles(tab_slots, idx):
    n, r, c = idx.shape
    buckets = tuple(tuple(int(b) for b in np.unique(tile) if b < NUM_BUCKETS) for tile in idx)
    return pl.pallas_call(
        functools.partial(_bias_tiles_kernel, buckets=buckets),
        grid=(NSA_GROUPS, NSA_HPG),
        in_specs=[pl.BlockSpec(memory_space=pltpu.SMEM),
                  pl.BlockSpec((n, r, c), lambda g, p: (0, 0, 0))],
        out_specs=pl.BlockSpec((1, n, 1, r, c), lambda g, p: (g, 0, p, 0, 0)),
        out_shape=jax.ShapeDtypeStruct((NSA_GROUPS, n, NSA_HPG, r, c), F32),
        compiler_params=_cparams(("parallel", "parallel")),
        name="bias_tiles",
    )(tab_slots, jnp.asarray(idx))


def _cmp_bias_ids(s):
    nqt = s // NSA_TQ
    nh = s // CMP_STRIDE
    per_tile = NSA_TQ // CMP_STRIDE
    i = np.arange(NSA_TQ)[:, None]
    c = np.arange(nh)[None, :]
    dist = i - (CMP_BLOCK - 1) - CMP_STRIDE * (c - per_tile * (nqt - 1))
    ids = np.where(dist >= 0, _t5_bucket_np(dist), MASKED_ID).astype(np.int32)
    return ids[None]


def _toeplitz_bucket_ids(window):
    i = np.arange(NSA_TQ)[:, None]
    j = np.arange(NSA_TK)[None, :]
    tiles = []
    d = 0
    while True:
        dist = d * NSA_TQ + i - j
        ids = _t5_bucket_np(dist)
        ok = dist >= 0
        if window is not None:
            ok &= dist < window
        ids = np.where(ok, ids, MASKED_ID).astype(np.int32)
        tiles.append(ids)
        if window is None and (ids == NUM_BUCKETS - 1).all():
            break
        if window is not None and not ok.any():
            tiles.pop()
            break
        d += 1
    n_real = len(tiles)
    tiles.append(np.full((NSA_TQ, NSA_TK), EXCLUDED_ID, np.int32))
    return np.stack(tiles), n_real


def _split_halves(x):
    lo = lax.broadcasted_iota(jnp.int32, x.shape, 1) < HALF
    return jnp.where(lo, x, 0.0), pltpu.roll(jnp.where(lo, 0.0, x), HALF, 1)


def _value_layouts(x):
    lo = lax.broadcasted_iota(jnp.int32, x.shape, 1) < HALF
    g0e = jnp.where(lo, x, 0.0)
    g1o = jnp.where(lo, 0.0, x)
    pair = lambda e, o: (_with_ones_column(e, False), _with_ones_column(o, True))
    return pair(g0e, pltpu.roll(g0e, HALF, 1)), pair(pltpu.roll(g1o, HALF, 1), g1o)


def _nsa_prep_kernel(zq_ref, ks_ref, vs_ref, kw_ref, vw_ref, gq_ref, gks_ref, gkw_ref,
                     q_out, ks_out, kw_out, vs_out, vw_out, *, tm):
    i = pl.program_id(1)
    scale = NSA_DK ** -0.5 * LOG2E
    for j in range(NSA_HEADS // 2):
        e, o = _split_halves(_rms_half_lanes(zq_ref[:, j * LANES:(j + 1) * LANES], gq_ref[...]) * scale)
        g = (2 * j) // NSA_HPG
        pe = (2 * j - NSA_HPG * g) // 2
        q_out[0, NSA_HPG * g + pe] = e.astype(BF16)
        q_out[0, NSA_HPG * g + NSA_HPG // 2 + pe] = o.astype(BF16)

    lane = lax.broadcasted_iota(jnp.int32, (tm, LANES), 1)
    pos = i * tm + lax.broadcasted_iota(jnp.int32, (tm, LANES), 0)
    block_tag = jnp.where(lane - HALF == pos // SLC_BLOCK, MASK_VALUE, 0.0)
    for g, x in enumerate(_split_halves(_rms_half_lanes(ks_ref[...], gks_ref[...]))):
        ks_out[0, g] = (x + block_tag).astype(BF16)
    for g, x in enumerate(_split_halves(_rms_half_lanes(kw_ref[...], gkw_ref[...]))):
        kw_out[0, g] = x.astype(BF16)
    for ref, out in ((vs_ref, vs_out), (vw_ref, vw_out)):
        for g, (ve, vo) in enumerate(_value_layouts(ref[...])):
            out[0, g, 0] = ve.astype(BF16)
            out[0, g, 1] = vo.astype(BF16)


def _nsa_prep(z_q, z_kv, bsz, s, q_norm, k_norm, tm=512):
    tm = min(tm, s)
    nt = s // tm
    row = lambda n, c: pl.BlockSpec((tm, n), lambda b, i: (b * nt + i, c))
    gain = pl.BlockSpec((1, LANES), lambda b, i: (0, 0))
    kspec = pl.BlockSpec((1, NSA_GROUPS, tm, LANES), lambda b, i: (b, 0, i, 0))
    vspec = pl.BlockSpec((1, NSA_GROUPS, 2, tm, LANES), lambda b, i: (b, 0, 0, i, 0))
    kshape = jax.ShapeDtypeStruct((bsz, NSA_GROUPS, s, LANES), BF16)
    vshape = jax.ShapeDtypeStruct((bsz, NSA_GROUPS, 2, s, LANES), BF16)
    pad = lambda v: jnp.tile(v.reshape(1, -1), (1, LANES // NSA_DK))
    return pl.pallas_call(
        functools.partial(_nsa_prep_kernel, tm=tm),
        grid=(bsz, nt),
        in_specs=[row(NSA_HEADS * NSA_DK, 0), row(LANES, 0), row(LANES, 1), row(LANES, 2), row(LANES, 3),
                  gain, gain, gain],
        out_specs=[pl.BlockSpec((1, NSA_HEADS, tm, LANES), lambda b, i: (b, 0, i, 0)),
                   kspec, kspec, vspec, vspec],
        out_shape=[jax.ShapeDtypeStruct((bsz, NSA_HEADS, s, LANES), BF16), kshape, kshape, vshape, vshape],
        compiler_params=_cparams(("parallel", "parallel")),
        name="nsa_prep",
    )(z_q, z_kv, z_kv, z_kv, z_kv, pad(q_norm), pad(k_norm[1]), pad(k_norm[2]))


def _compress_kernel(xk_ref, xv_ref, pk_ref, pv_ref, w1k_ref, w1v_ref, w2k_ref, w2v_ref, gk_ref,
                     ck_out, cv_out):
    def mlp(x, p_ref, w1_ref, w2_ref):
        nh = x.shape[0]
        top = _dot((x + p_ref[0:1]).astype(BF16), w1_ref[0])
        bot = _dot((x + p_ref[1:2]).astype(BF16), w1_ref[1])
        hid = _gelu_tanh(top + pltpu.roll(bot, nh - 1, 0))
        return _dot(hid.astype(BF16), w2_ref[...])

    nh = xk_ref.shape[1]
    yk = mlp(xk_ref[0], pk_ref, w1k_ref, w2k_ref)
    for g, x in enumerate(_split_halves(yk)):
        ck = _rms(x, gk_ref[...], NSA_DK)
        ck_out[0, g, 0:nh] = ck
        ck_out[0, g, nh:2 * nh] = ck
    yv = mlp(xv_ref[0], pv_ref, w1v_ref, w2v_ref)
    for g, layouts in enumerate(_value_layouts(yv)):
        for parity, v in enumerate(layouts):
            cv_out[0, g, parity, 0:nh] = v
            cv_out[0, g, parity, nh:2 * nh] = v


def _compress_weights(pos, w1, w2):
    half = CMP_BLOCK // 2
    g = NSA_GROUPS
    eye = jnp.eye(g, dtype=F32)
    w1 = w1.astype(BF16).reshape(2, half, NSA_DK, CMP_HIDDEN)
    w1 = jnp.stack([jnp.pad(w1, ((0, 0), (0, 0), (0, 0), (gi * CMP_HIDDEN, (g - 1 - gi) * CMP_HIDDEN)))
                    for gi in range(g)], axis=2)
    w1 = w1.reshape(2, half * g * NSA_DK, g * CMP_HIDDEN)
    p = jnp.broadcast_to(pos.reshape(2, half, 1, NSA_DK), (2, half, g, NSA_DK)).reshape(2, half * g * NSA_DK)
    w2 = (w2[None, :, None, :] * eye[:, None, :, None]).reshape(g * CMP_HIDDEN, g * NSA_DK).astype(BF16)
    return p, w1, w2


def _compress(z_kc, z_vc, bsz, s, pos_k, w1_k, w2_k, pos_v, w1_v, w2_v, k_norm0):
    nh = s // CMP_STRIDE
    feat = CMP_STRIDE * LANES
    pk, w1k, w2k = _compress_weights(pos_k, w1_k, w2_k)
    pv, w1v, w2v = _compress_weights(pos_v, w1_v, w2_v)
    full = lambda a: pl.BlockSpec(a.shape, lambda b: (0,) * a.ndim)
    xspec = pl.BlockSpec((1, nh, feat), lambda b: (b, 0, 0))
    gk = _pad_cols(k_norm0.reshape(1, -1), LANES)
    return pl.pallas_call(
        _compress_kernel,
        grid=(bsz,),
        in_specs=[xspec, xspec, full(pk), full(pv), full(w1k), full(w1v), full(w2k), full(w2v), full(gk)],
        out_specs=[pl.BlockSpec((1, NSA_GROUPS, 2 * nh, LANES), lambda b: (b, 0, 0, 0)),
                   pl.BlockSpec((1, NSA_GROUPS, 2, 2 * nh, LANES), lambda b: (b, 0, 0, 0, 0))],
        out_shape=[jax.ShapeDtypeStruct((bsz, NSA_GROUPS, 2 * nh, LANES), F32),
                   jax.ShapeDtypeStruct((bsz, NSA_GROUPS, 2, 2 * nh, LANES), F32)],
        compiler_params=_cparams(("parallel",)),
        name="nsa_compress",
    )(z_kc.reshape(bsz, nh, feat), z_vc.reshape(bsz, nh, feat), pk, pv, w1k, w1v, w2k, w2v, gk)


def _cmp_attn_kernel(q_ref, ck_ref, cv_ref, bias_ref, ov_ref, o_ref, q2_ref, s0_ref, s1_ref,
                     rank0_ref, rank1_ref, *, tq, nsel):
    nqt = q_ref.shape[2] // tq
    assert nqt % 2 == 0
    half = NSA_HPG // 2
    nh = bias_ref.shape[4]
    per_tile = tq // CMP_STRIDE
    c_last = per_tile * (nqt - 1)
    n_blk = LANES - HALF

    def block_window(t):
        start = lax.rem(per_tile * t + (nh - c_last % nh), nh)
        return pl.ds(pl.multiple_of(start, per_tile), nh)

    def query_rows(t):
        return pl.ds(pl.multiple_of(t * tq, tq), tq)

    def stage_scores(t, buf):
        t = jnp.minimum(t, nqt - 1)
        col = lax.broadcasted_iota(jnp.int32, (nh, LANES), 0)
        lane_k = lax.broadcasted_iota(jnp.int32, (nh, LANES), 1)
        wrapped = (lane_k == HALF) & (col < c_last - per_tile * t)
        ck = jnp.where(wrapped, MASK_VALUE, ck_ref[0, 0, block_window(t), :]).astype(BF16)
        lane_q = lax.broadcasted_iota(jnp.int32, (NSA_HPG, tq, LANES), 2)
        q = q_ref[0, :, query_rows(t), :] + jnp.where(lane_q == HALF, 1.0, 0.0).astype(BF16)
        s = _dot_t(q.reshape(NSA_HPG * tq, LANES), ck)
        buf[...] = s.reshape(NSA_HPG, tq, nh) + bias_ref[0, 0]

    def stage_select(t, buf, rank_ref):
        cols = block_window(t)
        cv = [cv_ref[0, 0, parity, cols, :].astype(BF16) for parity in range(2)]
        ov = ov_ref[cols, :].astype(BF16)
        row_ok = t * tq + lax.broadcasted_iota(jnp.int32, (tq, nh), 0) >= CMP_BLOCK - 1
        outs = []
        psum = None
        for h in range(NSA_HPG):
            s = buf[h]
            e = jnp.exp2(s - jnp.max(s, axis=-1, keepdims=True))
            p = jnp.where(row_ok, e / jnp.sum(e, axis=-1, keepdims=True), 0.0)
            outs.append(_dot(p.astype(BF16), cv[h // half]))
            psum = p if psum is None else psum + p
        lane_lo = lax.broadcasted_iota(jnp.int32, (tq, LANES), 1) < HALF
        for j in range(half):
            o = jnp.where(lane_lo, outs[j], outs[half + j])
            o_ref[0, query_rows(t), j * LANES:(j + 1) * LANES] = o.astype(o_ref.dtype)

        hi = psum.astype(BF16)
        lo = (psum - hi.astype(F32)).astype(BF16)
        imp = _dot(hi, ov) + _dot(lo, ov)

        blk = lax.broadcasted_iota(jnp.int32, (n_blk, tq), 0)
        qpos = t * tq + lax.broadcasted_iota(jnp.int32, (n_blk, tq), 1)
        cur = qpos // SLC_BLOCK
        valid = blk * SLC_BLOCK <= qpos
        forced = (blk == 0) | (blk == cur) | (blk == cur - 1)
        score = jnp.where(valid & forced, FORCE_SCORE, jnp.where(valid, imp.T[HALF:], -1.0))
        score = jnp.where(blk < nsel, score, REMOVED_SCORE)
        rank_ref[...] = score
        rank = jnp.zeros((n_blk, tq), F32)
        for j in range(n_blk):
            sj = jnp.broadcast_to(rank_ref[j:j + 1, :], (n_blk, tq))
            beats = (sj > score) | ((sj == score) & (blk > j))
            rank = rank + jnp.where(beats, 1.0, 0.0)
        unselected = jnp.where(rank < min(SLC_TOPN, nsel), 0.0, 1.0)
        unselected = jnp.concatenate([jnp.zeros((HALF, tq), F32), unselected], axis=0).T.astype(BF16)
        for h in range(NSA_HPG):
            q2_ref[0, h, query_rows(t), :] = q_ref[0, h, query_rows(t), :] + unselected

    stage_scores(0, s0_ref)

    def two_tiles(j, carry):
        t = 2 * j
        stage_scores(t + 1, s1_ref)
        stage_select(t, s0_ref, rank0_ref)
        stage_scores(t + 2, s0_ref)
        stage_select(t + 1, s1_ref, rank1_ref)
        return carry

    lax.fori_loop(0, nqt // 2, two_tiles, 0)


def _cmp_attention(q, ck, cv, bias, ov, s):
    bsz = q.shape[0]
    tq = NSA_TQ
    nh = s // CMP_STRIDE
    nsel = s // SLC_BLOCK
    s_buf = pltpu.VMEM((NSA_HPG, tq, nh), F32)
    rank_buf = pltpu.VMEM((LANES - HALF, tq), F32)
    return pl.pallas_call(
        functools.partial(_cmp_attn_kernel, tq=tq, nsel=nsel),
        grid=(NSA_GROUPS, bsz),
        in_specs=[pl.BlockSpec((1, NSA_HPG, s, LANES), lambda g, b: (b, g, 0, 0), pipeline_mode=pl.Buffered(1)),
                  pl.BlockSpec((1, 1, 2 * nh, LANES), lambda g, b: (b, g, 0, 0)),
                  pl.BlockSpec((1, 1, 2, 2 * nh, LANES), lambda g, b: (b, g, 0, 0, 0)),
                  pl.BlockSpec((1, 1, NSA_HPG, tq, nh), lambda g, b: (g, 0, 0, 0, 0)),
                  pl.BlockSpec((2 * nh, LANES), lambda g, b: (0, 0))],
        out_specs=[pl.BlockSpec((1, s, NSA_HPG * NSA_DK), lambda g, b: (b, 0, g)),
                   pl.BlockSpec((1, NSA_HPG, s, LANES), lambda g, b: (b, g, 0, 0))],
        out_shape=[jax.ShapeDtypeStruct((bsz, s, NSA_HEADS * NSA_DK), BF16),
                   jax.ShapeDtypeStruct(q.shape, BF16)],
        scratch_shapes=[s_buf, s_buf, rank_buf, rank_buf],
        compiler_params=_cparams(("parallel", "parallel")),
        name="nsa_cmp_attn",
    )(q, ck, cv, bias, ov)


def _nsa_window_kernel(q_ref, k_ref, v_ref, bias_ref, o_ref, s0_ref, s1_ref, *, tq, window_steps):
    nqt = q_ref.shape[2] // tq
    assert nqt % 2 == 0
    ratio = NSA_TK // tq
    n_tiles = bias_ref.shape[1]
    half = NSA_HPG // 2

    def key_chunk(t, i):
        return t // ratio - (window_steps - 1) + i

    def key_rows(t, i):
        return pl.ds(pl.multiple_of(jnp.maximum(key_chunk(t, i), 0) * NSA_TK, NSA_TK), NSA_TK)

    def stage_scores(t, buf):
        t = jnp.minimum(t, nqt - 1)
        q = q_ref[0, :, pl.ds(pl.multiple_of(t * tq, tq), tq), :].reshape(NSA_HPG * tq, LANES)
        for i in range(window_steps):
            kc = key_chunk(t, i)
            tile = jnp.where(kc < 0, n_tiles - 1, t - ratio * kc)
            s = _dot_t(q, k_ref[0, 0, key_rows(t, i), :]).reshape(NSA_HPG, tq, NSA_TK)
            buf[:, :, i * NSA_TK:(i + 1) * NSA_TK] = s + bias_ref[0, tile]

    def stage_out(t, buf):
        accs = []
        for h in range(NSA_HPG):
            s = buf[h]
            p = jnp.exp2(s - jnp.max(s, axis=-1, keepdims=True)).astype(BF16)
            acc = None
            for i in range(window_steps):
                pv = _dot(p[:, i * NSA_TK:(i + 1) * NSA_TK], v_ref[0, 0, h // half, key_rows(t, i), :])
                acc = pv if acc is None else acc + pv
            accs.append(acc)
        rows = pl.ds(pl.multiple_of(t * tq, tq), tq)
        for j in range(half):
            o = _normalized_pair(accs[j], accs[half + j])
            o_ref[0, rows, j * LANES:(j + 1) * LANES] = o.astype(o_ref.dtype)

    stage_scores(0, s0_ref)

    def two_tiles(j, carry):
        t = 2 * j
        stage_scores(t + 1, s1_ref)
        stage_out(t, s0_ref)
        stage_scores(t + 2, s0_ref)
        stage_out(t + 1, s1_ref)
        return carry

    lax.fori_loop(0, nqt // 2, two_tiles, 0)


def _nsa_selected_kernel(q_ref, k_ref, v_ref, bias_ref, o_ref, s0_ref, s1_ref, m_ref, acc_ref, *, tq):
    nqt = q_ref.shape[2] // tq
    ratio = NSA_TK // tq
    n_tiles = bias_ref.shape[1]
    half = NSA_HPG // 2
    chunk = SEL_TILES * NSA_TK
    per = chunk // tq

    def key_rows(kc):
        return pl.ds(pl.multiple_of(kc * chunk, chunk), chunk)

    def scores(t, kc, valid, buf):
        q = q_ref[0, :, pl.ds(pl.multiple_of(t * tq, tq), tq), :].reshape(NSA_HPG * tq, LANES)
        s = _dot_t(q, k_ref[0, 0, key_rows(kc), :]).reshape(NSA_HPG, tq, chunk)
        for i in range(SEL_TILES):
            offset = t - ratio * (SEL_TILES * kc + i)
            tile = jnp.where(valid & (offset >= 0), jnp.minimum(offset, n_tiles - 2), n_tiles - 1)
            cols = slice(i * NSA_TK, (i + 1) * NSA_TK)
            buf[:, :, cols] = s[:, :, cols] + bias_ref[0, tile]

    def values(h, kc):
        return v_ref[0, 0, h // half, key_rows(kc), :]

    def write_out(t, accs):
        rows = pl.ds(pl.multiple_of(t * tq, tq), tq)
        for j in range(half):
            o = _normalized_pair(accs[j], accs[half + j])
            o_ref[0, rows, j * LANES:(j + 1) * LANES] = o.astype(o_ref.dtype)

    _causal_item_stream(nqt, lambda t: t // per, NSA_HPG, scores, values, write_out,
                        (s0_ref, s1_ref), m_ref, acc_ref)


def _nsa_selected(q, k, v, bias, s):
    bsz = q.shape[0]
    tq = NSA_TQ
    n_off = bias.shape[1]
    assert s % (SEL_TILES * NSA_TK) == 0
    s_buf = pltpu.VMEM((NSA_HPG, tq, SEL_TILES * NSA_TK), F32)
    once = pl.Buffered(1)
    return pl.pallas_call(
        functools.partial(_nsa_selected_kernel, tq=tq),
        grid=(NSA_GROUPS, bsz),
        in_specs=[pl.BlockSpec((1, NSA_HPG, s, LANES), lambda g, b: (b, g, 0, 0), pipeline_mode=once),
                  pl.BlockSpec((1, 1, s, LANES), lambda g, b: (b, g, 0, 0)),
                  pl.BlockSpec((1, 1, 2, s, LANES), lambda g, b: (b, g, 0, 0, 0)),
                  pl.BlockSpec((1, n_off, NSA_HPG, tq, NSA_TK), lambda g, b: (g, 0, 0, 0, 0), pipeline_mode=once)],
        out_specs=pl.BlockSpec((1, s, NSA_HPG * NSA_DK), lambda g, b: (b, 0, g)),
        out_shape=jax.ShapeDtypeStruct((bsz, s, NSA_HEADS * NSA_DK), BF16),
        scratch_shapes=[s_buf, s_buf, pltpu.VMEM((NSA_HPG, tq, LANES), F32), pltpu.VMEM((NSA_HPG, tq, LANES), F32)],
        compiler_params=_cparams(("parallel", "parallel")),
        name="nsa_flash_sel",
    )(q, k, v, bias)


def _nsa_window(q, k, v, bias, s, window_steps):
    bsz = q.shape[0]
    tq = NSA_TQ
    n_off = bias.shape[1]
    s_buf = pltpu.VMEM((NSA_HPG, tq, window_steps * NSA_TK), F32)
    once = pl.Buffered(1)
    return pl.pallas_call(
        functools.partial(_nsa_window_kernel, tq=tq, window_steps=window_steps),
        grid=(NSA_GROUPS, bsz),
        in_specs=[pl.BlockSpec((1, NSA_HPG, s, LANES), lambda g, b: (b, g, 0, 0), pipeline_mode=once),
                  pl.BlockSpec((1, 1, s, LANES), lambda g, b: (b, g, 0, 0)),
                  pl.BlockSpec((1, 1, 2, s, LANES), lambda g, b: (b, g, 0, 0, 0)),
                  pl.BlockSpec((1, n_off, NSA_HPG, tq, NSA_TK), lambda g, b: (g, 0, 0, 0, 0), pipeline_mode=once)],
        out_specs=pl.BlockSpec((1, s, NSA_HPG * NSA_DK), lambda g, b: (b, 0, g)),
        out_shape=jax.ShapeDtypeStruct((bsz, s, NSA_HEADS * NSA_DK), BF16),
        scratch_shapes=[s_buf, s_buf],
        compiler_params=_cparams(("parallel", "parallel")),
        name="nsa_flash_win",
    )(q, k, v, bias)


def _hybrid_layer(h, bsz, s, layer, g_mix, g_mlp, w1, w2, w_in, conv_w, conv_b, wa, ba, wx, bx, lam,
                  q_norm, kv_norm, w_uq, w_ukv, qn_q, qn_k, w_out):
    d = h.shape[1]
    c = LRU_WIDTH
    o_kv = 2 * c + MLA_Q_RANK
    o_kr = o_kv + MLA_KV_RANK
    w_kr = jnp.pad(w_in[:, o_kr:], ((0, 0), (MLA_NOPE, LANES - MLA_QK)))
    w_all = jnp.concatenate([w_in[:, :o_kv], w_kr, w_in[:, o_kv:o_kr]], axis=1).astype(BF16)
    z_lru, q_lat, k_rope, kv_lat = _norm_proj(h, g_mix, w_all,
                                              ((2 * c,), (MLA_Q_RANK, LANES), (MLA_KV_RANK,)))
    lru = _rg_lru(z_lru.reshape(bsz, s, 2 * c), conv_w, conv_b, wa, ba, wx, bx, lam)
    q, k, v = _mla_prep(q_lat, kv_lat, k_rope, bsz, s, q_norm, kv_norm, w_uq, w_ukv, qn_q, qn_k)
    mla = _mla_attention(q, k, v)
    wo = w_out.astype(BF16)
    return _mix_mlp(h, [[lru.reshape(bsz * s, c)], [mla.reshape(bsz * s, -1)]], [wo[:c], wo[c:]],
                    g_mlp, w1, w2, layer)


def _nsa_layer(h, bsz, s, layer, g_mix, g_mlp, w1, w2, w_in, pos_k, w1_k, w2_k, pos_v, w1_v, w2_v,
               q_norm, k_norm, rel_bias, w_out):
    nq = NSA_HEADS * NSA_DK
    kvw = NSA_GROUPS * NSA_DK
    w_gate = _pad_cols(w_in[:, nq + 6 * kvw:], LANES)
    w_all = jnp.concatenate([w_in[:, :nq + 6 * kvw], w_gate], axis=1).astype(BF16)
    z_q, z_kc, z_vc, z_kv, z_gate = _norm_proj(h, g_mix, w_all, ((nq,), (kvw, kvw), (4 * kvw,), (LANES,)),
                                               folds=(1, CMP_STRIDE, CMP_STRIDE, 1, 1))

    q, ks, kw, vs, vw = _nsa_prep(z_q, z_kv, bsz, s, q_norm, k_norm)
    ck, cv = _compress(z_kc, z_vc, bsz, s, pos_k, w1_k, w2_k, pos_v, w1_v, w2_v, k_norm[0])

    order = np.array(HEAD_ORDER)
    expand = np.zeros((3, LANES, NSA_HEADS * NSA_DK), np.float32)
    for br in range(3):
        for hd in range(NSA_HEADS):
            expand[br, 3 * hd + br, hd * NSA_DK:(hd + 1) * NSA_DK] = 1.0

    slot_heads = (np.arange(NSA_GROUPS)[:, None] * NSA_HPG + order[None, :]).reshape(-1)
    tab_slots = rel_bias.T[slot_heads]
    nh = s // CMP_STRIDE
    nc = (s - CMP_BLOCK) // CMP_STRIDE + 1
    cmp_bias = _bias_tiles(tab_slots, _cmp_bias_ids(s))
    sel_bias = _bias_tiles(tab_slots, _toeplitz_bucket_ids(None)[0])
    win_ids, n_win_tiles = _toeplitz_bucket_ids(WINDOW)
    win_bias = _bias_tiles(tab_slots, win_ids)

    nsel = s // SLC_BLOCK
    cstart = np.arange(nh) * CMP_STRIDE
    sstart = np.arange(nsel) * SLC_BLOCK
    ov = np.clip(np.minimum(cstart[:, None] + CMP_BLOCK, sstart[None, :] + SLC_BLOCK)
                 - np.maximum(cstart[:, None], sstart[None, :]), 0, None).astype(np.float32) / CMP_BLOCK
    ov[nc:] = 0.0
    ov_p = np.zeros((nh, LANES), np.float32)
    ov_p[:, HALF:HALF + nsel] = ov

    o_c, q_sel = _cmp_attention(q, ck, cv, cmp_bias, jnp.asarray(np.concatenate([ov_p, ov_p])), s)
    o_s = _nsa_selected(q_sel, ks, vs, sel_bias, s)
    window_steps = (n_win_tiles + 1) // (NSA_TK // NSA_TQ)
    o_w = _nsa_window(q, kw, vw, win_bias, s, window_steps)

    flat = lambda a: a.reshape(bsz * s, -1)
    return _mix_mlp(h, [[flat(o_c), flat(o_s), flat(o_w)]], [w_out.astype(BF16)],
                    g_mlp, w1, w2, layer,
                    gate=(z_gate, jnp.asarray(np.concatenate([expand, expand], axis=1), BF16)))


def kernel(x, rel_bias, norm_mix, norm_mlp, mlp_w1, mlp_w2, hy_w_in, lru_conv_w, lru_conv_b, lru_wa, lru_ba, lru_wx, lru_bx, lru_lambda, mla_q_norm, mla_kv_norm, mla_w_uq, mla_w_ukv, mla_qn_q, mla_qn_k, hy_w_out, nsa_w_in, nsa_cmp_pos_k, nsa_cmp_w1_k, nsa_cmp_w2_k, nsa_cmp_pos_v, nsa_cmp_w1_v, nsa_cmp_w2_v, nsa_q_norm, nsa_k_norm, nsa_w_out):
    bsz, s, d = x.shape
    depth = norm_mix.shape[0]
    h = x.reshape(bsz * s, d)
    w1_all, w2_all = mlp_w1.astype(BF16), mlp_w2.astype(BF16)
    for layer in range(depth):
        if layer % 2 == 0:
            e = layer // 2
            h = _hybrid_layer(h, bsz, s, layer, norm_mix[layer], norm_mlp[layer], w1_all, w2_all,
                              hy_w_in[e], lru_conv_w[e], lru_conv_b[e], lru_wa[e], lru_ba[e], lru_wx[e],
                              lru_bx[e], lru_lambda[e], mla_q_norm[e], mla_kv_norm[e], mla_w_uq[e],
                              mla_w_ukv[e], mla_qn_q[e], mla_qn_k[e], hy_w_out[e])
        else:
            o = layer // 2
            h = _nsa_layer(h, bsz, s, layer, norm_mix[layer], norm_mlp[layer], w1_all, w2_all,
                           nsa_w_in[o], nsa_cmp_pos_k[o], nsa_cmp_w1_k[o], nsa_cmp_w2_k[o],
                           nsa_cmp_pos_v[o], nsa_cmp_w1_v[o], nsa_cmp_w2_v[o], nsa_q_norm[o],
                           nsa_k_norm[o], rel_bias, nsa_w_out[o])
    return h.reshape(bsz, s, d)
```
